```python
import math
import jax, jax.numpy as jnp
from jax import lax
import numpy as np

D_MODEL = 1024
BATCH = 2
SEQ = 16384
DEPTH = 2

HEAD_DIM = 64
BLOCK = 128
EPS = 1e-6
NEG_INF = -1e30
SWA_HEADS = 8
SWA_KV_HEADS = 2
SWA_WINDOW = 128
MLA_HEADS = 8
MLA_Q_RANK = 256
MLA_KV_RANK = 128
MLA_NOPE_DIM = 64
MLA_ROPE_DIM = 32
MLA_V_DIM = 64
MLA_QK_DIM = MLA_NOPE_DIM + MLA_ROPE_DIM
ROPE_THETA = 10000.0
DIFF_HEADS = 8
DIFF_DIM = 64
MEM_TOKENS = 256
MEM_HEADS = 4
MEM_HEAD_DIM = 128
D_FF = -(-8 * D_MODEL // (3 * 256)) * 256

SWA_Q_W = SWA_HEADS * HEAD_DIM
SWA_KV_W = SWA_KV_HEADS * HEAD_DIM
EVEN_SPLITS = (SWA_Q_W, SWA_Q_W + SWA_KV_W, SWA_Q_W + 2 * SWA_KV_W,
               SWA_Q_W + 2 * SWA_KV_W + MLA_Q_RANK,
               SWA_Q_W + 2 * SWA_KV_W + MLA_Q_RANK + MLA_KV_RANK)
IN_EVEN = SWA_Q_W + 2 * SWA_KV_W + MLA_Q_RANK + MLA_KV_RANK + MLA_ROPE_DIM
MIX_EVEN = SWA_HEADS * HEAD_DIM + MLA_HEADS * MLA_V_DIM
DIFF_W = DIFF_HEADS * 2 * DIFF_DIM
N_EVEN = (DEPTH + 1) // 2
N_ODD = DEPTH // 2

kernel_name = "hybrid_swa_mla_diffattn_mem_swiglu"


def rms_norm(x, g):
    xf = x.astype(jnp.float32)
    y = xf * lax.rsqrt(jnp.mean(xf * xf, axis=-1, keepdims=True) + EPS)
    return (y * g.astype(jnp.float32)).astype(x.dtype)


def alibi_slopes(n):
    return jnp.asarray([2.0 ** (-8.0 * (i + 1) / n) for i in range(n)], jnp.float32)


def rope(x, positions):
    half = x.shape[-1] // 2
    inv = ROPE_THETA ** (-jnp.arange(half, dtype=jnp.float32) / half)
    ang = positions.astype(jnp.float32)[:, :, None, None] * inv
    cos, sin = jnp.cos(ang), jnp.sin(ang)
    xf = x.astype(jnp.float32)
    x1, x2 = xf[..., :half], xf[..., half:]
    return jnp.concatenate([x1 * cos - x2 * sin, x2 * cos + x1 * sin], axis=-1).astype(x.dtype)


def _query_blocks(q):
    B, S = q.shape[:2]
    return q.reshape(B, S // BLOCK, BLOCK, *q.shape[2:]).swapaxes(0, 1)


def _merge_blocks(o):
    nb, B = o.shape[:2]
    return o.swapaxes(0, 1).reshape(B, nb * BLOCK, *o.shape[3:])


def _causal_probs(qblk, k, n, slopes):
    S = k.shape[1]
    s = jnp.einsum('bqhd,bkhd->bhqk', qblk, k, preferred_element_type=jnp.float32) * (qblk.shape[-1] ** -0.5)
    dist = (n * BLOCK + jnp.arange(BLOCK))[:, None] - jnp.arange(S)[None, :]
    if slopes is not None:
        s = s - slopes[None, :, None, None] * dist.astype(jnp.float32)
    s = jnp.where(dist >= 0, s, NEG_INF)
    return jax.nn.softmax(s, axis=-1)


def swa_sink_attention(q, k, v, sinks, slopes):
    B, S, H, D = q.shape
    Hk = k.shape[2]
    G = H // Hk
    nb = S // BLOCK
    qb = q.reshape(B, nb, BLOCK, Hk, G, D)
    pad = jnp.zeros((B, BLOCK, Hk, D), k.dtype)
    kp = jnp.concatenate([pad, k], axis=1).reshape(B, nb + 1, BLOCK, Hk, D)
    vp = jnp.concatenate([pad.astype(v.dtype), v], axis=1).reshape(B, nb + 1, BLOCK, Hk, D)
    kb = jnp.concatenate([kp[:, :-1], kp[:, 1:]], axis=2)
    vb = jnp.concatenate([vp[:, :-1], vp[:, 1:]], axis=2)
    s = jnp.einsum('bnqhgd,bnkhd->bnhgqk', qb, kb, preferred_element_type=jnp.float32) * (D ** -0.5)
    dist = jnp.arange(BLOCK)[:, None] - jnp.arange(2 * BLOCK)[None, :] + BLOCK
    src = jnp.arange(nb)[:, None, None] * BLOCK - BLOCK + jnp.arange(2 * BLOCK)[None, None, :]
    valid = ((dist >= 0) & (dist < SWA_WINDOW))[None] & (src >= 0)
    s = s - slopes.reshape(Hk, G)[:, :, None, None] * dist.astype(jnp.float32)
    s = jnp.where(valid[None, :, None, None], s, NEG_INF)
    sink = jnp.broadcast_to(sinks.astype(jnp.float32).reshape(Hk, G)[None, None, :, :, None, None],
                            s.shape[:-1] + (1,))
    p = jax.nn.softmax(jnp.concatenate([s, sink], axis=-1), axis=-1)[..., :-1]
    o = jnp.einsum('bnhgqk,bnkhd->bnqhgd', p.astype(v.dtype), vb)
    return o.reshape(B, S, H * D)


def mla_attention(q, k, v):
    def body(args):
        qblk, n = args
        p = _causal_probs(qblk, k, n, None)
        return jnp.einsum('bhqk,bkhd->bqhd', p.astype(v.dtype), v)
    o = _merge_blocks(lax.map(body, (_query_blocks(q), jnp.arange(q.shape[1] // BLOCK))))
    return o.reshape(q.shape[0], q.shape[1], -1)


def diff_attention(q1, q2, k1, k2, v, lam, slopes):
    def body(args):
        q1b, q2b, n = args
        a = _causal_probs(q1b, k1, n, slopes) - lam * _causal_probs(q2b, k2, n, slopes)
        return jnp.einsum('bhqk,bkhd->bqhd', a.astype(v.dtype), v)
    return _merge_blocks(lax.map(body, (_query_blocks(q1), _query_blocks(q2), jnp.arange(q1.shape[1] // BLOCK))))


def even_mixer(h, positions, w_in, swa_q_gain, swa_k_gain, sinks, q_latent_norm, kv_latent_norm,
               w_uq, w_ukv, mla_q_gain, mla_k_gain, w_out):
    B, S, _ = h.shape
    z = h @ w_in
    qa, ka, va, cq, ckv, kr = jnp.split(z, EVEN_SPLITS, axis=-1)
    qa = rms_norm(qa.reshape(B, S, SWA_HEADS, HEAD_DIM), swa_q_gain)
    ka = rms_norm(ka.reshape(B, S, SWA_KV_HEADS, HEAD_DIM), swa_k_gain)
    va = va.reshape(B, S, SWA_KV_HEADS, HEAD_DIM)
    out_a = swa_sink_attention(qa, ka, va, sinks, alibi_slopes(SWA_HEADS))
    q_full = (rms_norm(cq, q_latent_norm) @ w_uq).reshape(B, S, MLA_HEADS, MLA_QK_DIM)
    kv = (rms_norm(ckv, kv_latent_norm) @ w_ukv).reshape(B, S, MLA_HEADS, MLA_NOPE_DIM + MLA_V_DIM)
    k_full = jnp.concatenate([kv[..., :MLA_NOPE_DIM],
                              jnp.broadcast_to(kr[:, :, None, :], (B, S, MLA_HEADS, MLA_ROPE_DIM))], axis=-1)
    vb = kv[..., MLA_NOPE_DIM:]
    q_full = rms_norm(q_full, mla_q_gain)
    k_full = rms_norm(k_full, mla_k_gain)
    qb = jnp.concatenate([q_full[..., :MLA_NOPE_DIM], rope(q_full[..., MLA_NOPE_DIM:], positions)], axis=-1)
    kb = jnp.concatenate([k_full[..., :MLA_NOPE_DIM], rope(k_full[..., MLA_NOPE_DIM:], positions)], axis=-1)
    out_b = mla_attention(qb, kb, vb)
    return jnp.concatenate([out_a, out_b], axis=-1) @ w_out


def odd_mixer(h, w_qkv, q_gain, k_gain, lambdas, subln, w_out, lambda_init):
    B, S, _ = h.shape
    q, k, v = jnp.split(h @ w_qkv, 3, axis=-1)
    q = rms_norm(q.reshape(B, S, DIFF_HEADS, 2, DIFF_DIM), q_gain)
    k = rms_norm(k.reshape(B, S, DIFF_HEADS, 2, DIFF_DIM), k_gain)
    v = v.reshape(B, S, DIFF_HEADS, 2 * DIFF_DIM)
    lf = lambdas.astype(jnp.float32)
    lam = jnp.exp(jnp.sum(lf[0] * lf[1])) - jnp.exp(jnp.sum(lf[2] * lf[3])) + lambda_init
    o = diff_attention(q[:, :, :, 0], q[:, :, :, 1], k[:, :, :, 0], k[:, :, :, 1], v, lam,
                       alibi_slopes(DIFF_HEADS))
    o = rms_norm(o, subln) * (1.0 - lambda_init)
    return o.reshape(B, S, DIFF_W) @ w_out


def mem_attention(h, m, w_q, w_kv, q_gain, k_gain, w_out):
    B, S, _ = h.shape
    q = rms_norm((h @ w_q).reshape(B, S, MEM_HEADS, MEM_HEAD_DIM), q_gain)
    k, v = jnp.split(m @ w_kv, 2, axis=-1)
    k = rms_norm(k.reshape(B, MEM_TOKENS, MEM_HEADS, MEM_HEAD_DIM), k_gain)
    v = v.reshape(B, MEM_TOKENS, MEM_HEADS, MEM_HEAD_DIM)
    s = jnp.einsum('bshd,bmhd->bhsm', q, k, preferred_element_type=jnp.float32) * (MEM_HEAD_DIM ** -0.5)
    p = jax.nn.softmax(s, axis=-1)
    o = jnp.einsum('bhsm,bmhd->bshd', p.astype(v.dtype), v)
    return o.reshape(B, S, MEM_HEADS * MEM_HEAD_DIM) @ w_out


def swiglu(h, w_gate, w_up, w_down):
    return (jax.nn.silu(h @ w_gate) * (h @ w_up)) @ w_down


def setup_inputs(seed: int = 0) -> dict:
    key = jax.random.key(seed)
    ks = iter(jax.random.split(key, 48))

    def w(shape, fan_in):
        return jax.random.normal(next(ks), shape, jnp.float32) * (fan_in ** -0.5)

    def gain(shape):
        return 1.0 + 0.02 * jax.random.normal(next(ks), shape, jnp.float32)

    x = jax.random.normal(next(ks), (BATCH, SEQ, D_MODEL), jnp.float32)
    mem = jax.random.normal(next(ks), (BATCH, MEM_TOKENS, D_MODEL), jnp.float32)
    offset = jax.random.randint(next(ks), (BATCH, 1), 0, 1024, dtype=jnp.int32)
    positions = (offset + jnp.arange(SEQ, dtype=jnp.int32)[None, :]).astype(jnp.int32)
    return {
        "x": x,
        "mem": mem,
        "positions": positions,
        "mix_norm": gain((DEPTH, D_MODEL)),
        "ev_w_in": w((N_EVEN, D_MODEL, IN_EVEN), D_MODEL),
        "ev_swa_q_gain": gain((N_EVEN, HEAD_DIM)),
        "ev_swa_k_gain": gain((N_EVEN, HEAD_DIM)),
        "ev_sinks": 0.5 * jax.random.normal(next(ks), (N_EVEN, SWA_HEADS), jnp.float32),
        "ev_q_latent_norm": gain((N_EVEN, MLA_Q_RANK)),
        "ev_kv_latent_norm": gain((N_EVEN, MLA_KV_RANK)),
        "ev_w_uq": w((N_EVEN, MLA_Q_RANK, MLA_HEADS * MLA_QK_DIM), MLA_Q_RANK),
        "ev_w_ukv": w((N_EVEN, MLA_KV_RANK, MLA_HEADS * (MLA_NOPE_DIM + MLA_V_DIM)), MLA_KV_RANK),
        "ev_mla_q_gain": gain((N_EVEN, MLA_QK_DIM)),
        "ev_mla_k_gain": gain((N_EVEN, MLA_QK_DIM)),
        "ev_w_out": w((N_EVEN, MIX_EVEN, D_MODEL), MIX_EVEN),
        "od_w_qkv": w((N_ODD, D_MODEL, 3 * DIFF_W), D_MODEL),
        "od_q_gain": gain((N_ODD, DIFF_DIM)),
        "od_k_gain": gain((N_ODD, DIFF_DIM)),
        "od_lambda": 0.1 * jax.random.normal(next(ks), (N_ODD, 4, DIFF_DIM), jnp.float32),
        "od_subln": gain((N_ODD, 2 * DIFF_DIM)),
        "od_w_out": w((N_ODD, DIFF_W, D_MODEL), DIFF_W),
        "mem_q_norm": gain((DEPTH, D_MODEL)),
        "mem_kv_norm": gain((DEPTH, D_MODEL)),
        "mem_w_q": w((DEPTH, D_MODEL, MEM_HEADS * MEM_HEAD_DIM), D_MODEL),
        "mem_w_kv": w((DEPTH, D_MODEL, 2 * MEM_HEADS * MEM_HEAD_DIM), D_MODEL),
        "mem_q_gain": gain((DEPTH, MEM_HEAD_DIM)),
        "mem_k_gain": gain((DEPTH, MEM_HEAD_DIM)),
        "mem_w_out": w((DEPTH, MEM_HEADS * MEM_HEAD_DIM, D_MODEL), MEM_HEADS * MEM_HEAD_DIM),
        "ffn_norm": gain((DEPTH, D_MODEL)),
        "ffn_w_gate": w((DEPTH, D_MODEL, D_FF), D_MODEL),
        "ffn_w_up": w((DEPTH, D_MODEL, D_FF), D_MODEL),
        "ffn_w_down": w((DEPTH, D_FF, D_MODEL), D_FF),
    }


def reference(x, mem, positions, mix_norm, ev_w_in, ev_swa_q_gain, ev_swa_k_gain, ev_sinks,
              ev_q_latent_norm, ev_kv_latent_norm, ev_w_uq, ev_w_ukv, ev_mla_q_gain, ev_mla_k_gain,
              ev_w_out, od_w_qkv, od_q_gain, od_k_gain, od_lambda, od_subln, od_w_out,
              mem_q_norm, mem_kv_norm, mem_w_q, mem_w_kv, mem_q_gain, mem_k_gain, mem_w_out,
              ffn_norm, ffn_w_gate, ffn_w_up, ffn_w_down):
    for l in range(DEPTH):
        h = rms_norm(x, mix_norm[l])
        if l % 2 == 0:
            e = l // 2
            y = even_mixer(h, positions, ev_w_in[e], ev_swa_q_gain[e], ev_swa_k_gain[e], ev_sinks[e],
                           ev_q_latent_norm[e], ev_kv_latent_norm[e], ev_w_uq[e], ev_w_ukv[e],
                           ev_mla_q_gain[e], ev_mla_k_gain[e], ev_w_out[e])
        else:
            o = l // 2
            lambda_init = 0.8 - 0.6 * math.exp(-0.3 * l)
            y = odd_mixer(h, od_w_qkv[o], od_q_gain[o], od_k_gain[o], od_lambda[o], od_subln[o],
                          od_w_out[o], lambda_init)
        x = x + y
        x = x + mem_attention(rms_norm(x, mem_q_norm[l]), rms_norm(mem, mem_kv_norm[l]), mem_w_q[l],
                              mem_w_kv[l], mem_q_gain[l], mem_k_gain[l], mem_w_out[l])
        x = x + swiglu(rms_norm(x, ffn_norm[l]), ffn_w_gate[l], ffn_w_up[l], ffn_w_down[l])
    return x
```

```python
import functools
import math

import numpy as np
import jax
import jax.numpy as jnp
from jax import lax
from jax.experimental import pallas as pl
from jax.experimental.pallas import tpu as pltpu

F32 = jnp.float32
BF16 = jnp.bfloat16

EPS = 1e-6
MASKED = -1e30
LOG2E = 1.4426950408889634
ROPE_THETA = 10000.0

LANES = 128
HEAD_DIM = 64
SWA_HEADS = 8
SWA_KV_HEADS = 2
SWA_WINDOW = 128
MLA_HEADS = 8
MLA_NOPE = 64
MLA_ROPE = 32
MLA_QK = MLA_NOPE + MLA_ROPE
MLA_V = 64
DIFF_HEADS = 8
DIFF_DIM = 64
MEM_HEADS = 4
MEM_HEAD_DIM = 128

VMEM_LIMIT = 56 * 1024 * 1024


def _params(sem):
    return pltpu.CompilerParams(dimension_semantics=sem, vmem_limit_bytes=VMEM_LIMIT)


def _const_spec(shape):
    nd = len(shape)
    return pl.BlockSpec(shape, lambda *_: (0,) * nd, pipeline_mode=pl.Buffered(1))


def _rms(x, g):
    ms = jnp.mean(x * x, axis=-1, keepdims=True)
    return x * lax.rsqrt(ms + EPS) * g


def _dot(a, b):
    return jnp.dot(a, b, preferred_element_type=F32)


def _dot_nt(a, b):
    return lax.dot_general(a, b, (((1,), (1,)), ((), ())), preferred_element_type=F32)


def _group_sumsq(x, ones_blk):
    x2 = x * x
    hi = x2.astype(BF16)
    lo = (x2 - hi.astype(F32)).astype(BF16)
    return _dot(hi, ones_blk) + _dot(lo, ones_blk)


def _group_norm64(x, ones_blk, gain):
    return x * lax.rsqrt(_group_sumsq(x, ones_blk) * (1.0 / HEAD_DIM) + EPS) * gain


def _even_proj_kernel(x_ref, pos_ref, gmix_ref, w1_ref, ones_ref, gq_ref, gk_ref, qln_ref, kvln_ref,
                      wuq_ref, wuk_ref, wuv_ref, gmq_ref, gmk_ref, inv_ref,
                      qa_ref, ka_ref, va_ref, qm_ref, km_ref, vm_ref):
    x = x_ref[0]
    h = _rms(x, gmix_ref[...]).astype(BF16)
    z = _dot(h, w1_ref[...])
    ones = ones_ref[...]
    qa = _group_norm64(z[:, 0:512], ones, gq_ref[...]) * (HEAD_DIM ** -0.5 * LOG2E)
    qa_ref[0] = qa.astype(BF16)
    ka = _group_norm64(z[:, 512:640], ones[0:128, 0:128], gk_ref[...])
    ka_ref[0] = ka.astype(BF16)
    va_ref[0] = z[:, 640:768].astype(BF16)

    cqn = _rms(z[:, 768:1024], qln_ref[...]).astype(BF16)
    ckvn = _rms(z[:, 1024:1152], kvln_ref[...]).astype(BF16)
    kr = z[:, 1152:1280]
    qf = _dot(cqn, wuq_ref[...])
    kn = _dot(ckvn, wuk_ref[...])
    vm_ref[0] = _dot(ckvn, wuv_ref[...]).astype(BF16)

    ang = pos_ref[0].astype(F32) * inv_ref[...]
    lane = lax.broadcasted_iota(jnp.int32, ang.shape, 1)
    cos = jnp.cos(ang)
    sin = jnp.sin(ang)
    half = MLA_ROPE // 2
    cos_t = jnp.where(lane < MLA_NOPE, 1.0, cos)
    sin_lo = jnp.where((lane >= MLA_NOPE) & (lane < MLA_NOPE + half), -sin, 0.0)
    sin_hi = jnp.where((lane >= MLA_NOPE + half) & (lane < MLA_QK), sin, 0.0)

    def norm_rope(xh, gain):
        r = lax.rsqrt(jnp.sum(xh * xh, axis=-1, keepdims=True) * (1.0 / MLA_QK) + EPS)
        xn = xh * r * gain
        return (xn * cos_t + pltpu.roll(xn, LANES - half, 1) * sin_lo + pltpu.roll(xn, half, 1) * sin_hi)

    gmq = gmq_ref[...]
    gmk = gmk_ref[...]
    for hd in range(MLA_HEADS):
        sl = slice(LANES * hd, LANES * (hd + 1))
        qm_ref[0, :, sl] = (norm_rope(qf[:, sl], gmq) * (MLA_QK ** -0.5 * LOG2E)).astype(BF16)
        km_ref[0, :, sl] = norm_rope(kn[:, sl] + kr, gmk).astype(BF16)


def _even_proj(x, pos3, gmix, w1, ones, gq, gk, qln, kvln, wuq, wuk, wuv, gmq, gmk, inv, ts):
    B, S, D = x.shape
    grid = (B, S // ts)
    tok = lambda c: pl.BlockSpec((1, ts, c), lambda b, i: (b, i, 0))
    consts = [gmix, w1, ones, gq, gk, qln, kvln, wuq, wuk, wuv, gmq, gmk, inv]
    out_cols = [512, 128, 128, 1024, 1024, 512]
    return pl.pallas_call(
        _even_proj_kernel,
        grid=grid,
        in_specs=[tok(D), tok(1)] + [_const_spec(c.shape) for c in consts],
        out_specs=[tok(c) for c in out_cols],
        out_shape=[jax.ShapeDtypeStruct((B, S, c), BF16) for c in out_cols],
        compiler_params=_params(("parallel", "parallel")),
        name="even_proj",
    )(x, pos3, *consts)


def _swa_kernel(sink_ref, q_ref, k_ref, v_ref, o_ref, *, tq):
    qi = pl.program_id(1)
    win = tq + SWA_WINDOW
    wstart = pl.multiple_of(jnp.maximum(qi * tq - SWA_WINDOW, 0), SWA_WINDOW)
    kw = k_ref[0, pl.ds(wstart, win), :]
    vw = v_ref[0, pl.ds(wstart, win), :]
    row = lax.broadcasted_iota(jnp.int32, (tq, win), 0)
    col = lax.broadcasted_iota(jnp.int32, (tq, win), 1)
    dist = (qi * tq + row) - (wstart + col)
    valid = (dist >= 0) & (dist < SWA_WINDOW)
    distf = dist.astype(F32)
    lane = lax.broadcasted_iota(jnp.int32, (tq, LANES), 1)
    group = SWA_HEADS // SWA_KV_HEADS
    for cb in range(SWA_HEADS // 2):
        qq = q_ref[0, :, LANES * cb:LANES * (cb + 1)].astype(F32)
        halves = []
        for e in range(2):
            hd = 2 * cb + e
            hk = hd // group
            xq = qq if e == hk else pltpu.roll(qq, HEAD_DIM, 1)
            in_kv = (lane >= HEAD_DIM * hk) & (lane < HEAD_DIM * (hk + 1))
            xq = jnp.where(in_kv, xq, 0.0).astype(BF16)
            s = _dot_nt(xq, kw) - (2.0 ** (-8.0 * (hd + 1) / SWA_HEADS) * LOG2E) * distf
            s = jnp.where(valid, s, MASKED)
            sink = sink_ref[hd] * LOG2E
            m = jnp.maximum(jnp.max(s, axis=-1, keepdims=True), sink)
            p = jnp.exp2(s - m)
            l = jnp.sum(p, axis=-1, keepdims=True) + jnp.exp2(sink - m)
            o = _dot(p.astype(BF16), vw) / l
            halves.append(o if e == hk else pltpu.roll(o, HEAD_DIM, 1))
        o_ref[0, :, LANES * cb:LANES * (cb + 1)] = jnp.where(lane < HEAD_DIM, halves[0], halves[1]).astype(BF16)


def _swa_attention(sinks, qa, ka, va, tq):
    B, S, _ = qa.shape
    return pl.pallas_call(
        functools.partial(_swa_kernel, tq=tq),
        grid=(B, S // tq),
        in_specs=[
            pl.BlockSpec(memory_space=pltpu.SMEM),
            pl.BlockSpec((1, tq, 512), lambda b, i: (b, i, 0)),
            pl.BlockSpec((1, S, LANES), lambda b, i: (b, 0, 0)),
            pl.BlockSpec((1, S, LANES), lambda b, i: (b, 0, 0)),
        ],
        out_specs=pl.BlockSpec((1, tq, 512), lambda b, i: (b, i, 0)),
        out_shape=jax.ShapeDtypeStruct((B, S, 512), BF16),
        compiler_params=_params(("parallel", "arbitrary")),
        name="swa_attention",
    )(sinks, qa, ka, va)


def _mla_kernel(q_ref, k_ref, v_ref, o_ref, *, tq):
    qi = pl.program_id(2)
    row = lax.broadcasted_iota(jnp.int32, (tq, tq), 0)
    col = lax.broadcasted_iota(jnp.int32, (tq, tq), 1)
    causal = row >= col
    results = []
    for e in range(2):
        q = q_ref[0, :, LANES * e:LANES * (e + 1)]

        def step(j, carry, masked):
            m, l, acc = carry
            ks = pl.multiple_of(j * tq, tq)
            k = k_ref[0, pl.ds(ks, tq), LANES * e:LANES * (e + 1)]
            v = v_ref[0, pl.ds(ks, tq), :]
            s = _dot_nt(q, k)
            if masked:
                s = jnp.where(causal, s, MASKED)
            m_new = jnp.maximum(m, jnp.max(s, axis=-1, keepdims=True))
            alpha = jnp.exp2(m - m_new)
            p = jnp.exp2(s - m_new)
            l = alpha * l + jnp.sum(p, axis=-1, keepdims=True)
            acc = alpha * acc + _dot(p.astype(BF16), v)
            return m_new, l, acc

        init = (jnp.full((tq, 1), MASKED, F32), jnp.zeros((tq, 1), F32), jnp.zeros((tq, LANES), F32))
        carry = lax.fori_loop(0, qi, lambda j, c: step(j, c, False), init)
        m, l, acc = step(qi, carry, True)
        results.append(acc / l)
    lane = lax.broadcasted_iota(jnp.int32, (tq, LANES), 1)
    o_ref[0] = jnp.where(lane < MLA_V, results[0], results[1]).astype(BF16)


def _mla_attention(qm, km, vm, tq):
    B, S, _ = qm.shape
    return pl.pallas_call(
        functools.partial(_mla_kernel, tq=tq),
        grid=(B, MLA_HEADS // 2, S // tq),
        in_specs=[
            pl.BlockSpec((1, tq, 2 * LANES), lambda b, h, i: (b, i, h)),
            pl.BlockSpec((1, S, 2 * LANES), lambda b, h, i: (b, 0, h)),
            pl.BlockSpec((1, S, LANES), lambda b, h, i: (b, 0, h)),
        ],
        out_specs=pl.BlockSpec((1, tq, LANES), lambda b, h, i: (b, i, h)),
        out_shape=jax.ShapeDtypeStruct((B, S, MLA_HEADS * MLA_V), BF16),
        compiler_params=_params(("parallel", "parallel", "arbitrary")),
        name="mla_attention",
    )(qm, km, vm)


def _odd_proj_kernel(x_ref, gmix_ref, w_ref, ones_ref, gq_ref, gk_ref, q_ref, k_ref, v_ref):
    x = x_ref[0]
    h = _rms(x, gmix_ref[...]).astype(BF16)
    z = _dot(h, w_ref[...])
    ones = ones_ref[...]
    width = DIFF_HEADS * 2 * DIFF_DIM
    for c in range(width // 512):
        sl = slice(512 * c, 512 * (c + 1))
        q = _group_norm64(z[:, sl], ones, gq_ref[...]) * (DIFF_DIM ** -0.5 * LOG2E)
        q_ref[0, :, sl] = q.astype(BF16)
        k = _group_norm64(z[:, width + 512 * c:width + 512 * (c + 1)], ones, gk_ref[...])
        k_ref[0, :, sl] = k.astype(BF16)
    v_ref[0] = z[:, 2 * width:3 * width].astype(BF16)


def _odd_proj(x, gmix, w, ones, gq, gk, ts):
    B, S, D = x.shape
    tok = pl.BlockSpec((1, ts, D), lambda b, i: (b, i, 0))
    consts = [gmix, w, ones, gq, gk]
    return pl.pallas_call(
        _odd_proj_kernel,
        grid=(B, S // ts),
        in_specs=[tok] + [_const_spec(c.shape) for c in consts],
        out_specs=[tok, tok, tok],
        out_shape=[jax.ShapeDtypeStruct((B, S, D), BF16)] * 3,
        compiler_params=_params(("parallel", "parallel")),
        name="odd_proj",
    )(x, *consts)


def _diff_kernel(slope_ref, q_ref, k_ref, v_ref, lam_ref, subln_ref, o_ref, *, tq, lambda_init):
    hd = pl.program_id(1)
    qi = pl.program_id(2)
    slope = slope_ref[hd] * LOG2E
    row = lax.broadcasted_iota(jnp.int32, (tq, tq), 0)
    col = lax.broadcasted_iota(jnp.int32, (tq, tq), 1)
    causal = row >= col
    bias = slope * (col - row).astype(F32)
    lane = lax.broadcasted_iota(jnp.int32, (tq, LANES), 1)
    q = q_ref[0]
    q_parts = (jnp.where(lane < DIFF_DIM, q, jnp.zeros_like(q)), jnp.where(lane >= DIFF_DIM, q, jnp.zeros_like(q)))

    def step(j, carry, masked):
        ks = pl.multiple_of(j * tq, tq)
        k = k_ref[0, pl.ds(ks, tq), :]
        v = v_ref[0, pl.ds(ks, tq), :]
        offset = slope * ((j - qi) * tq).astype(F32)
        new = []
        for c in range(2):
            m, l, acc = carry[c]
            s = _dot_nt(q_parts[c], k) + bias
            if masked:
                s = jnp.where(causal, s, MASKED)
            m_new = jnp.maximum(m, jnp.max(s, axis=-1, keepdims=True) + offset)
            alpha = jnp.exp2(m - m_new)
            p = jnp.exp2(s - (m_new - offset))
            l = alpha * l + jnp.sum(p, axis=-1, keepdims=True)
            acc = alpha * acc + _dot(p.astype(BF16), v)
            new.append((m_new, l, acc))
        return tuple(new)

    one = (jnp.full((tq, 1), MASKED, F32), jnp.zeros((tq, 1), F32), jnp.zeros((tq, LANES), F32))
    carry = lax.fori_loop(0, qi, lambda j, c: step(j, c, False), (one, one))
    (m1, l1, a1), (m2, l2, a2) = step(qi, carry, True)
    lf = lam_ref[...]
    lam = (jnp.exp(jnp.sum(lf[0:1] * lf[1:2], axis=-1, keepdims=True))
           - jnp.exp(jnp.sum(lf[2:3] * lf[3:4], axis=-1, keepdims=True)) + lambda_init)
    o = a1 / l1 - lam * (a2 / l2)
    o = _rms(o, subln_ref[...]) * (1.0 - lambda_init)
    o_ref[0] = o.astype(BF16)


def _diff_attention(slopes, q, k, v, lambdas, subln, tq, lambda_init):
    B, S, D = q.shape
    return pl.pallas_call(
        functools.partial(_diff_kernel, tq=tq, lambda_init=lambda_init),
        grid=(B, DIFF_HEADS, S // tq),
        in_specs=[
            pl.BlockSpec(memory_space=pltpu.SMEM),
            pl.BlockSpec((1, tq, LANES), lambda b, h, i: (b, i, h)),
            pl.BlockSpec((1, S, LANES), lambda b, h, i: (b, 0, h)),
            pl.BlockSpec((1, S, LANES), lambda b, h, i: (b, 0, h)),
            pl.BlockSpec(lambdas.shape, lambda b, h, i: (0, 0)),
            pl.BlockSpec(subln.shape, lambda b, h, i: (0, 0)),
        ],
        out_specs=pl.BlockSpec((1, tq, LANES), lambda b, h, i: (b, i, h)),
        out_shape=jax.ShapeDtypeStruct((B, S, D), BF16),
        compiler_params=_params(("parallel", "parallel", "arbitrary")),
        name="diff_attention",
    )(slopes, q, k, v, lambdas, subln)


def _mem_kv_kernel(mem_ref, norm_ref, w_ref, gain_ref, k_ref, v_ref):
    h = _rms(mem_ref[0], norm_ref[0]).astype(BF16)
    z = _dot(h, w_ref[0])
    width = MEM_HEADS * MEM_HEAD_DIM
    gain = gain_ref[0]
    for hd in range(MEM_HEADS):
        sl = slice(MEM_HEAD_DIM * hd, MEM_HEAD_DIM * (hd + 1))
        k_ref[0, 0, :, sl] = _rms(z[:, sl], gain).astype(BF16)
    v_ref[0, 0] = z[:, width:2 * width].astype(BF16)


def _mem_kv(mem, norms, w_kv, gains):
    B, M, D = mem.shape
    L = norms.shape[0]
    width = MEM_HEADS * MEM_HEAD_DIM
    out = pl.BlockSpec((1, 1, M, width), lambda l, b: (l, b, 0, 0))
    return pl.pallas_call(
        _mem_kv_kernel,
        grid=(L, B),
        in_specs=[
            pl.BlockSpec((1, M, D), lambda l, b: (b, 0, 0)),
            pl.BlockSpec((1, 1, D), lambda l, b: (l, 0, 0)),
            pl.BlockSpec((1, D, 2 * width), lambda l, b: (l, 0, 0)),
            pl.BlockSpec((1, 1, MEM_HEAD_DIM), lambda l, b: (l, 0, 0)),
        ],
        out_specs=[out, out],
        out_shape=[jax.ShapeDtypeStruct((L, B, M, width), BF16)] * 2,
        compiler_params=_params(("parallel", "parallel")),
        name="mem_kv",
    )(mem, norms, w_kv, gains)


def _post_kernel(*refs, n_attn):
    x_ref = refs[0]
    attn_refs = refs[1:1 + n_attn]
    wo_refs = refs[1 + n_attn:1 + 2 * n_attn]
    (mqn_ref, wq_ref, qg_ref, mk_ref, mv_ref, wmo_ref, fn_ref, wg_ref, wu_ref, wd_ref, o_ref) = refs[1 + 2 * n_attn:]
    x = x_ref[0]
    for a_ref, w_ref in zip(attn_refs, wo_refs):
        x = x + _dot(a_ref[0], w_ref[...])

    q = _dot(_rms(x, mqn_ref[...]).astype(BF16), wq_ref[...])
    heads = []
    for hd in range(MEM_HEADS):
        sl = slice(MEM_HEAD_DIM * hd, MEM_HEAD_DIM * (hd + 1))
        qh = (_rms(q[:, sl], qg_ref[...]) * (MEM_HEAD_DIM ** -0.5 * LOG2E)).astype(BF16)
        s = _dot_nt(qh, mk_ref[0, :, sl])
        p = jnp.exp2(s - jnp.max(s, axis=-1, keepdims=True))
        l = jnp.sum(p, axis=-1, keepdims=True)
        heads.append((_dot(p.astype(BF16), mv_ref[0, :, sl]) / l).astype(BF16))
    x = x + _dot(jnp.concatenate(heads, axis=-1), wmo_ref[...])

    hf = _rms(x, fn_ref[...]).astype(BF16)
    g = _dot(hf, wg_ref[...])
    u = _dot(hf, wu_ref[...])
    act = (g * jax.nn.sigmoid(g) * u).astype(BF16)
    o_ref[0] = x + _dot(act, wd_ref[...])


def _post_block(x, attns, wos, mqn, wq, qg, mk, mv, wmo, fn, wg, wu, wd, ts):
    B, S, D = x.shape
    n = len(attns)
    tok = lambda c: pl.BlockSpec((1, ts, c), lambda b, i: (b, i, 0))
    memspec = pl.BlockSpec((1,) + mk.shape[1:], lambda b, i: (b, 0, 0))
    consts_a = list(wos) + [mqn, wq, qg]
    consts_b = [wmo, fn, wg, wu, wd]
    return pl.pallas_call(
        functools.partial(_post_kernel, n_attn=n),
        grid=(B, S // ts),
        in_specs=([tok(D)] + [tok(a.shape[-1]) for a in attns] + [_const_spec(c.shape) for c in consts_a]
                  + [memspec, memspec] + [_const_spec(c.shape) for c in consts_b]),
        out_specs=tok(D),
        out_shape=jax.ShapeDtypeStruct((B, S, D), F32),
        compiler_params=_params(("parallel", "parallel")),
        name="post_block",
    )(x, *attns, *consts_a, mk, mv, *consts_b)


def _row(v):
    return v.reshape(1, -1).astype(F32)


def _pad_cols(w, lo, total):
    return jnp.pad(w, ((0, 0), (lo, total - lo - w.shape[1])))


def _even_weights(w_in, w_uq, w_ukv, q_gain, k_gain):
    kr = _pad_cols(w_in[:, 1152:1184], MLA_NOPE, LANES)
    w1 = jnp.concatenate([w_in[:, :1152], kr], axis=1).astype(BF16)
    wuq = jnp.concatenate(
        [_pad_cols(w_uq[:, MLA_QK * h:MLA_QK * (h + 1)], 0, LANES) for h in range(MLA_HEADS)], axis=1).astype(BF16)
    kv_w = MLA_NOPE + MLA_V
    wuk = jnp.concatenate(
        [_pad_cols(w_ukv[:, kv_w * h:kv_w * h + MLA_NOPE], 0, LANES) for h in range(MLA_HEADS)], axis=1).astype(BF16)
    wuv = jnp.concatenate(
        [w_ukv[:, kv_w * h + MLA_NOPE:kv_w * (h + 1)] for h in range(MLA_HEADS)], axis=1).astype(BF16)
    gmq = _row(jnp.pad(q_gain, (0, LANES - MLA_QK)))
    gmk = _row(jnp.pad(k_gain, (0, LANES - MLA_QK)))
    return w1, wuq, wuk, wuv, gmq, gmk


def _rope_lane_freqs():
    half = MLA_ROPE // 2
    inv = np.zeros((1, LANES), np.float32)
    freqs = (np.float32(ROPE_THETA) ** (-np.arange(half, dtype=np.float32) / np.float32(half))).astype(np.float32)
    inv[0, MLA_NOPE:MLA_NOPE + half] = freqs
    inv[0, MLA_NOPE + half:MLA_QK] = freqs
    return jnp.asarray(inv)


def _ones_blocks(n):
    g = np.arange(n) // HEAD_DIM
    return jnp.asarray((g[:, None] == g[None, :]).astype(np.float32), dtype=BF16)


def kernel(x, mem, positions, mix_norm, ev_w_in, ev_swa_q_gain, ev_swa_k_gain, ev_sinks, ev_q_latent_norm,
           ev_kv_latent_norm, ev_w_uq, ev_w_ukv, ev_mla_q_gain, ev_mla_k_gain, ev_w_out, od_w_qkv, od_q_gain,
           od_k_gain, od_lambda, od_subln, od_w_out, mem_q_norm, mem_kv_norm, mem_w_q, mem_w_kv, mem_q_gain,
           mem_k_gain, mem_w_out, ffn_norm, ffn_w_gate, ffn_w_up, ffn_w_down):
    B, S, D = x.shape
    depth = mix_norm.shape[0]
    ts_proj = min(512, S)
    ts_post = min(256, S)
    tq_swa = min(256, S - SWA_WINDOW)
    tq_flash = min(512, S)

    ones = _ones_blocks(512)
    pos3 = positions.reshape(B, S, 1)
    inv = _rope_lane_freqs()
    mem_k, mem_v = _mem_kv(mem, mem_kv_norm.reshape(depth, 1, D), mem_w_kv.astype(BF16),
                           mem_k_gain.reshape(depth, 1, MEM_HEAD_DIM))
    diff_slopes = jnp.asarray([2.0 ** (-8.0 * (i + 1) / DIFF_HEADS) for i in range(DIFF_HEADS)], F32)

    for l in range(depth):
        if l % 2 == 0:
            e = l // 2
            w1, wuq, wuk, wuv, gmq, gmk = _even_weights(ev_w_in[e], ev_w_uq[e], ev_w_ukv[e], ev_mla_q_gain[e],
                                                        ev_mla_k_gain[e])
            qa, ka, va, qm, km, vm = _even_proj(
                x, pos3, _row(mix_norm[l]), w1, ones, _row(jnp.tile(ev_swa_q_gain[e], SWA_HEADS)),
                _row(jnp.tile(ev_swa_k_gain[e], SWA_KV_HEADS)), _row(ev_q_latent_norm[e]),
                _row(ev_kv_latent_norm[e]), wuq, wuk, wuv, gmq, gmk, inv, ts_proj)
            out_a = _swa_attention(ev_sinks[e].astype(F32), qa, ka, va, tq_swa)
            out_b = _mla_attention(qm, km, vm, tq_flash)
            wo = ev_w_out[e].astype(BF16)
            attns = [out_a, out_b]
            wos = [wo[:512], wo[512:]]
        else:
            o = l // 2
            lambda_init = 0.8 - 0.6 * math.exp(-0.3 * l)
            n_grp = DIFF_HEADS * 2
            q, k, v = _odd_proj(x, _row(mix_norm[l]), od_w_qkv[o].astype(BF16), ones,
                                _row(jnp.tile(od_q_gain[o], n_grp // 2)), _row(jnp.tile(od_k_gain[o], n_grp // 2)),
                                ts_proj)
            out_d = _diff_attention(diff_slopes, q, k, v, od_lambda[o].astype(F32), _row(od_subln[o]), tq_flash,
                                    lambda_init)
            attns = [out_d]
            wos = [od_w_out[o].astype(BF16)]
        x = _post_block(x, attns, wos, _row(mem_q_norm[l]), mem_w_q[l].astype(BF16), _row(mem_q_gain[l]),
                        mem_k[l], mem_v[l], mem_w_out[l].astype(BF16), _row(ffn_norm[l]),
                        ffn_w_gate[l].astype(BF16), ffn_w_up[l].astype(BF16), ffn_w_down[l].astype(BF16), ts_post)
    return x
```

```python
import functools
import math

import numpy as np
import jax
import jax.numpy as jnp
from jax import lax
from jax.experimental import pallas as pl
from jax.experimental.pallas import tpu as pltpu

F32 = jnp.float32
BF16 = jnp.bfloat16

EPS = 1e-6
MASKED = -1e30
LOG2E = 1.4426950408889634
ROPE_THETA = 10000.0

LANES = 128
HEAD_DIM = 64
SWA_HEADS = 8
SWA_KV_HEADS = 2
SWA_WINDOW = 128
MLA_HEADS = 8
MLA_NOPE = 64
MLA_ROPE = 32
MLA_QK = MLA_NOPE + MLA_ROPE
MLA_V = 64
DIFF_HEADS = 8
DIFF_DIM = 64
MEM_HEADS = 4
MEM_HEAD_DIM = 128

VMEM_LIMIT = 56 * 1024 * 1024

UNDERFLOW_GUARD = 2.0 ** -80
N_SPLIT = 3


def _params(sem):
    return pltpu.CompilerParams(dimension_semantics=sem, vmem_limit_bytes=VMEM_LIMIT)


def _const_spec(shape):
    nd = len(shape)
    return pl.BlockSpec(shape, lambda *_: (0,) * nd, pipeline_mode=pl.Buffered(1))


def _rms(x, g):
    ms = jnp.mean(x * x, axis=-1, keepdims=True)
    return x * lax.rsqrt(ms + EPS) * g


def _dot(a, b):
    return jnp.dot(a, b, preferred_element_type=F32)


def _dot_nt(a, b):
    return lax.dot_general(a, b, (((1,), (1,)), ((), ())), preferred_element_type=F32)


def _group_sumsq(x, ones_blk):
    x2 = x * x
    hi = x2.astype(BF16)
    lo = (x2 - hi.astype(F32)).astype(BF16)
    return _dot(hi, ones_blk) + _dot(lo, ones_blk)


def _group_norm64(x, ones_blk, gain):
    return x * lax.rsqrt(_group_sumsq(x, ones_blk) * (1.0 / HEAD_DIM) + EPS) * gain


def _even_proj_kernel(x_ref, pos_ref, gmix_ref, w1_ref, ones_ref, gq_ref, gk_ref, qln_ref, kvln_ref,
                      wuq_ref, wuk_ref, wuv_ref, gmq_ref, gmk_ref, inv_ref,
                      qa_ref, ka_ref, va_ref, qm_ref, km_ref, vm_ref):
    x = x_ref[0]
    h = _rms(x, gmix_ref[...]).astype(BF16)
    z = _dot(h, w1_ref[...])
    ones = ones_ref[...]
    qa = _group_norm64(z[:, 0:512], ones, gq_ref[...]) * (HEAD_DIM ** -0.5 * LOG2E)
    qa_ref[0] = qa.astype(BF16)
    ka = _group_norm64(z[:, 512:640], ones[0:128, 0:128], gk_ref[...])
    ka_ref[0] = ka.astype(BF16)
    va_ref[0] = z[:, 640:768].astype(BF16)

    cqn = _rms(z[:, 768:1024], qln_ref[...]).astype(BF16)
    ckvn = _rms(z[:, 1024:1152], kvln_ref[...]).astype(BF16)
    kr = z[:, 1152:1280]
    qf = _dot(cqn, wuq_ref[...])
    kn = _dot(ckvn, wuk_ref[...])
    vm_ref[0] = _dot(ckvn, wuv_ref[...]).astype(BF16)

    ang = pos_ref[0].astype(F32) * inv_ref[...]
    lane = lax.broadcasted_iota(jnp.int32, ang.shape, 1)
    cos = jnp.cos(ang)
    sin = jnp.sin(ang)
    half = MLA_ROPE // 2
    cos_t = jnp.where(lane < MLA_NOPE, 1.0, cos)
    sin_lo = jnp.where((lane >= MLA_NOPE) & (lane < MLA_NOPE + half), -sin, 0.0)
    sin_hi = jnp.where((lane >= MLA_NOPE + half) & (lane < MLA_QK), sin, 0.0)

    def norm_rope(xh, gain):
        r = lax.rsqrt(jnp.sum(xh * xh, axis=-1, keepdims=True) * (1.0 / MLA_QK) + EPS)
        xn = xh * r * gain
        return (xn * cos_t + pltpu.roll(xn, LANES - half, 1) * sin_lo + pltpu.roll(xn, half, 1) * sin_hi)

    gmq = gmq_ref[...]
    gmk = gmk_ref[...]
    ones_lanes = (lane >= MLA_QK) & (lane < MLA_QK + N_SPLIT)
    for hd in range(MLA_HEADS):
        sl = slice(LANES * hd, LANES * (hd + 1))
        qm_ref[0, :, sl] = (norm_rope(qf[:, sl], gmq) * (MLA_QK ** -0.5 * LOG2E)).astype(BF16)
        km_ref[0, :, sl] = jnp.where(ones_lanes, 1.0, norm_rope(kn[:, sl] + kr, gmk)).astype(BF16)


def _even_proj(x, pos3, gmix, w1, ones, gq, gk, qln, kvln, wuq, wuk, wuv, gmq, gmk, inv, ts):
    B, S, D = x.shape
    grid = (B, S // ts)
    tok = lambda c: pl.BlockSpec((1, ts, c), lambda b, i: (b, i, 0))
    consts = [gmix, w1, ones, gq, gk, qln, kvln, wuq, wuk, wuv, gmq, gmk, inv]
    out_cols = [512, 128, 128, 1024, 1024, 512]
    return pl.pallas_call(
        _even_proj_kernel,
        grid=grid,
        in_specs=[tok(D), tok(1)] + [_const_spec(c.shape) for c in consts],
        out_specs=[tok(c) for c in out_cols],
        out_shape=[jax.ShapeDtypeStruct((B, S, c), BF16) for c in out_cols],
        compiler_params=_params(("parallel", "parallel")),
        name="even_proj",
    )(x, pos3, *consts)


def _swa_kernel(sink_ref, q_ref, k_ref, v_ref, o_ref, *, tq):
    qi = pl.program_id(1)
    win = tq + SWA_WINDOW
    wstart = pl.multiple_of(jnp.maximum(qi * tq - SWA_WINDOW, 0), SWA_WINDOW)
    kw = k_ref[0, pl.ds(wstart, win), :]
    vw = v_ref[0, pl.ds(wstart, win), :]
    row = lax.broadcasted_iota(jnp.int32, (tq, win), 0)
    col = lax.broadcasted_iota(jnp.int32, (tq, win), 1)
    dist = (qi * tq + row) - (wstart + col)
    valid = (dist >= 0) & (dist < SWA_WINDOW)
    distf = dist.astype(F32)
    lane = lax.broadcasted_iota(jnp.int32, (tq, LANES), 1)
    group = SWA_HEADS // SWA_KV_HEADS
    for cb in range(SWA_HEADS // 2):
        qq = q_ref[0, :, LANES * cb:LANES * (cb + 1)].astype(F32)
        halves = []
        for e in range(2):
            hd = 2 * cb + e
            hk = hd // group
            xq = qq if e == hk else pltpu.roll(qq, HEAD_DIM, 1)
            in_kv = (lane >= HEAD_DIM * hk) & (lane < HEAD_DIM * (hk + 1))
            xq = jnp.where(in_kv, xq, 0.0).astype(BF16)
            s = _dot_nt(xq, kw) - (2.0 ** (-8.0 * (hd + 1) / SWA_HEADS) * LOG2E) * distf
            s = jnp.where(valid, s, MASKED)
            sink = sink_ref[hd] * LOG2E
            m = jnp.maximum(jnp.max(s, axis=-1, keepdims=True), sink)
            p = jnp.exp2(s - m)
            l = jnp.sum(p, axis=-1, keepdims=True) + jnp.exp2(sink - m)
            o = _dot(p.astype(BF16), vw) / l
            halves.append(o if e == hk else pltpu.roll(o, HEAD_DIM, 1))
        o_ref[0, :, LANES * cb:LANES * (cb + 1)] = jnp.where(lane < HEAD_DIM, halves[0], halves[1]).astype(BF16)


def _swa_attention(sinks, qa, ka, va, tq):
    B, S, _ = qa.shape
    return pl.pallas_call(
        functools.partial(_swa_kernel, tq=tq),
        grid=(B, S // tq),
        in_specs=[
            pl.BlockSpec(memory_space=pltpu.SMEM),
            pl.BlockSpec((1, tq, 512), lambda b, i: (b, i, 0)),
            pl.BlockSpec((1, S, LANES), lambda b, i: (b, 0, 0)),
            pl.BlockSpec((1, S, LANES), lambda b, i: (b, 0, 0)),
        ],
        out_specs=pl.BlockSpec((1, tq, 512), lambda b, i: (b, i, 0)),
        out_shape=jax.ShapeDtypeStruct((B, S, 512), BF16),
        compiler_params=_params(("parallel", "arbitrary")),
        name="swa_attention",
    )(sinks, qa, ka, va)


def _split3(x):
    hi = x.astype(BF16).astype(F32)
    r = x - hi
    mid = r.astype(BF16).astype(F32)
    lo = (r - mid).astype(BF16).astype(F32)
    return hi, mid, lo


def _lane_tile_sum(p):
    acc = p[:, 0:LANES]
    for t in range(1, p.shape[1] // LANES):
        acc = acc + p[:, LANES * t:LANES * (t + 1)]
    return acc


def _max_sq_norm(k_ref, lanes, gsum, rows):
    n_chunks = k_ref.shape[1] // rows

    def body(c, mx):
        kc = k_ref[0, pl.ds(pl.multiple_of(c * rows, rows), rows), lanes].astype(F32)
        return jnp.maximum(mx, jnp.max(_dot((kc * kc).astype(BF16), gsum), axis=0, keepdims=True))

    return lax.fori_loop(0, n_chunks, body, jnp.zeros((1, LANES), F32))


def _causal_mask(tq):
    row = lax.broadcasted_iota(jnp.int32, (tq, tq), 0)
    col = lax.broadcasted_iota(jnp.int32, (tq, tq), 1)
    return row >= col


def _mla_running_max(q_ref, k_ref, v_ref, e, qi, tq):
    causal = _causal_mask(tq)
    q = q_ref[0, :, LANES * e:LANES * (e + 1)]

    def step(j, carry, masked):
        m, l, acc = carry
        ks = pl.multiple_of(j * tq, tq)
        k = k_ref[0, pl.ds(ks, tq), LANES * e:LANES * (e + 1)]
        v = v_ref[0, pl.ds(ks, tq), :]
        s = _dot_nt(q, k)
        if masked:
            s = jnp.where(causal, s, MASKED)
        m_new = jnp.maximum(m, jnp.max(s, axis=-1, keepdims=True))
        alpha = jnp.exp2(m - m_new)
        p = jnp.exp2(s - m_new)
        l = alpha * l + jnp.sum(p, axis=-1, keepdims=True)
        acc = alpha * acc + _dot(p.astype(BF16), v)
        return m_new, l, acc

    init = (jnp.full((tq, 1), MASKED, F32), jnp.zeros((tq, 1), F32), jnp.zeros((tq, LANES), F32))
    carry = lax.fori_loop(0, qi, lambda j, c: step(j, c, False), init)
    m, l, acc = step(qi, carry, True)
    return acc / l


def _mla_kernel(q_ref, k_ref, v_ref, o_ref, kmax_ref, *, tq):
    qi = pl.program_id(2)
    lane = lax.broadcasted_iota(jnp.int32, (tq, LANES), 1)
    gsum = (lax.broadcasted_iota(jnp.int32, (LANES, LANES), 0) < MLA_QK).astype(BF16)

    @pl.when(qi == 0)
    def _():
        for e in range(2):
            kmax_ref[e:e + 1, :] = _max_sq_norm(k_ref, slice(LANES * e, LANES * (e + 1)), gsum, tq)

    causal = _causal_mask(tq)
    outs = []
    sums = []
    for e in range(2):
        sl = slice(LANES * e, LANES * (e + 1))
        qf = q_ref[0, :, sl].astype(F32)
        bound = jnp.sqrt(_dot((qf * qf).astype(BF16), gsum) * kmax_ref[e:e + 1, :])
        hi, mid, lo = _split3(-bound)
        qa = jnp.where(lane == MLA_QK, hi, jnp.where(lane == MLA_QK + 1, mid,
                                                      jnp.where(lane == MLA_QK + 2, lo, qf))).astype(BF16)

        def step(j, carry, masked):
            lsum, acc = carry
            ks = pl.multiple_of(j * tq, tq)
            s = _dot_nt(qa, k_ref[0, pl.ds(ks, tq), sl])
            if masked:
                s = jnp.where(causal, s, MASKED)
            p = jnp.exp2(s)
            return lsum + _lane_tile_sum(p), acc + _dot(p.astype(BF16), v_ref[0, pl.ds(ks, tq), :])

        zero = jnp.zeros((tq, LANES), F32)
        carry = lax.fori_loop(0, qi, lambda j, c: step(j, c, False), (zero, zero))
        lsum, acc = step(qi, carry, True)
        l = jnp.sum(lsum, axis=-1, keepdims=True)
        outs.append(acc / l)
        sums.append(jnp.min(l))
    ok = jnp.minimum(sums[0], sums[1]) >= UNDERFLOW_GUARD

    @pl.when(ok)
    def _():
        o_ref[0] = jnp.where(lane < MLA_V, outs[0], outs[1]).astype(BF16)

    @pl.when(jnp.logical_not(ok))
    def _():
        slow = [_mla_running_max(q_ref, k_ref, v_ref, e, qi, tq) for e in range(2)]
        o_ref[0] = jnp.where(lane < MLA_V, slow[0], slow[1]).astype(BF16)


def _mla_attention(qm, km, vm, tq):
    B, S, _ = qm.shape
    return pl.pallas_call(
        functools.partial(_mla_kernel, tq=tq),
        grid=(B, MLA_HEADS // 2, S // tq),
        in_specs=[
            pl.BlockSpec((1, tq, 2 * LANES), lambda b, h, i: (b, i, h)),
            pl.BlockSpec((1, S, 2 * LANES), lambda b, h, i: (b, 0, h)),
            pl.BlockSpec((1, S, LANES), lambda b, h, i: (b, 0, h)),
        ],
        out_specs=pl.BlockSpec((1, tq, LANES), lambda b, h, i: (b, i, h)),
        out_shape=jax.ShapeDtypeStruct((B, S, MLA_HEADS * MLA_V), BF16),
        scratch_shapes=[pltpu.VMEM((8, LANES), F32)],
        compiler_params=_params(("parallel", "parallel", "arbitrary")),
        name="mla_attention",
    )(qm, km, vm)


def _odd_proj_kernel(x_ref, gmix_ref, w_ref, ones_ref, gq_ref, gk_ref, q_ref, k_ref, v_ref):
    x = x_ref[0]
    h = _rms(x, gmix_ref[...]).astype(BF16)
    z = _dot(h, w_ref[...])
    ones = ones_ref[...]
    width = DIFF_HEADS * 2 * DIFF_DIM
    for c in range(width // 512):
        sl = slice(512 * c, 512 * (c + 1))
        q = _group_norm64(z[:, sl], ones, gq_ref[...]) * (DIFF_DIM ** -0.5 * LOG2E)
        q_ref[0, :, sl] = q.astype(BF16)
        k = _group_norm64(z[:, width + 512 * c:width + 512 * (c + 1)], ones, gk_ref[...])
        k_ref[0, :, sl] = k.astype(BF16)
    v_ref[0] = z[:, 2 * width:3 * width].astype(BF16)


def _odd_proj(x, gmix, w, ones, gq, gk, ts):
    B, S, D = x.shape
    tok = pl.BlockSpec((1, ts, D), lambda b, i: (b, i, 0))
    consts = [gmix, w, ones, gq, gk]
    return pl.pallas_call(
        _odd_proj_kernel,
        grid=(B, S // ts),
        in_specs=[tok] + [_const_spec(c.shape) for c in consts],
        out_specs=[tok, tok, tok],
        out_shape=[jax.ShapeDtypeStruct((B, S, D), BF16)] * 3,
        compiler_params=_params(("parallel", "parallel")),
        name="odd_proj",
    )(x, *consts)


def _diff_q_parts(q, lane):
    zero = jnp.zeros_like(q)
    return jnp.where(lane < DIFF_DIM, q, zero), jnp.where(lane >= DIFF_DIM, q, zero)


def _diff_running_max(slope, q_ref, k_ref, v_ref, qi, tq):
    row = lax.broadcasted_iota(jnp.int32, (tq, tq), 0)
    col = lax.broadcasted_iota(jnp.int32, (tq, tq), 1)
    causal = row >= col
    bias = slope * (col - row).astype(F32)
    lane = lax.broadcasted_iota(jnp.int32, (tq, LANES), 1)
    q_parts = _diff_q_parts(q_ref[0], lane)

    def step(j, carry, masked):
        ks = pl.multiple_of(j * tq, tq)
        k = k_ref[0, pl.ds(ks, tq), :]
        v = v_ref[0, pl.ds(ks, tq), :]
        offset = slope * ((j - qi) * tq).astype(F32)
        new = []
        for c in range(2):
            m, l, acc = carry[c]
            s = _dot_nt(q_parts[c], k) + bias
            if masked:
                s = jnp.where(causal, s, MASKED)
            m_new = jnp.maximum(m, jnp.max(s, axis=-1, keepdims=True) + offset)
            alpha = jnp.exp2(m - m_new)
            p = jnp.exp2(s - (m_new - offset))
            l = alpha * l + jnp.sum(p, axis=-1, keepdims=True)
            acc = alpha * acc + _dot(p.astype(BF16), v)
            new.append((m_new, l, acc))
        return tuple(new)

    one = (jnp.full((tq, 1), MASKED, F32), jnp.zeros((tq, 1), F32), jnp.zeros((tq, LANES), F32))
    carry = lax.fori_loop(0, qi, lambda j, c: step(j, c, False), (one, one))
    (m1, l1, a1), (m2, l2, a2) = step(qi, carry, True)
    return a1 / l1, a2 / l2


def _diff_kernel(slope_ref, qx_ref, q_ref, k_ref, kx_ref, v_ref, lam_ref, subln_ref, o_ref, kmax_ref, *, tq,
                 lambda_init):
    hd = pl.program_id(1)
    qi = pl.program_id(2)
    lane = lax.broadcasted_iota(jnp.int32, (tq, LANES), 1)
    gi = lax.broadcasted_iota(jnp.int32, (LANES, LANES), 0)
    gsums = ((gi < DIFF_DIM).astype(BF16), (gi >= DIFF_DIM).astype(BF16))

    @pl.when(qi == 0)
    def _():
        for c in range(2):
            kmax_ref[c:c + 1, :] = _max_sq_norm(k_ref, slice(None), gsums[c], tq)

    causal = _causal_mask(tq)
    q = q_ref[0]
    qf = q.astype(F32)
    q2 = (qf * qf).astype(BF16)
    q_parts = _diff_q_parts(q, lane)
    qx_const = qx_ref[0]
    slope = qx_const[:, 3:4] + qx_const[:, 4:5] + qx_const[:, 5:6]
    rowpos = (qi * tq + lax.broadcasted_iota(jnp.int32, (tq, LANES), 0)).astype(F32)
    q_aug = []
    for c in range(2):
        bound = jnp.sqrt(_dot(q2, gsums[c]) * kmax_ref[c:c + 1, :])
        hi, mid, lo = _split3(-slope * rowpos - bound)
        qx = jnp.where(lane == 2 * N_SPLIT, hi, jnp.where(lane == 2 * N_SPLIT + 1, mid,
                                                           jnp.where(lane == 2 * N_SPLIT + 2, lo, qx_const)))
        q_aug.append(jnp.concatenate([q_parts[c], qx.astype(BF16)], axis=1))

    def step(j, carry, masked):
        ks = pl.multiple_of(j * tq, tq)
        k = jnp.concatenate([k_ref[0, pl.ds(ks, tq), :], kx_ref[pl.ds(ks, tq), :]], axis=1)
        v = v_ref[0, pl.ds(ks, tq), :]
        new = []
        for c in range(2):
            lsum, acc = carry[c]
            s = _dot_nt(q_aug[c], k)
            if masked:
                s = jnp.where(causal, s, MASKED)
            p = jnp.exp2(s)
            new.append((lsum + _lane_tile_sum(p), acc + _dot(p.astype(BF16), v)))
        return tuple(new)

    zero = jnp.zeros((tq, LANES), F32)
    carry = lax.fori_loop(0, qi, lambda j, c: step(j, c, False), ((zero, zero), (zero, zero)))
    (ls1, a1), (ls2, a2) = step(qi, carry, True)
    l1 = jnp.sum(ls1, axis=-1, keepdims=True)
    l2 = jnp.sum(ls2, axis=-1, keepdims=True)
    ok = jnp.minimum(jnp.min(l1), jnp.min(l2)) >= UNDERFLOW_GUARD

    lf = lam_ref[...]
    lam = (jnp.exp(jnp.sum(lf[0:1] * lf[1:2], axis=-1, keepdims=True))
           - jnp.exp(jnp.sum(lf[2:3] * lf[3:4], axis=-1, keepdims=True)) + lambda_init)

    def finish(o1, o2):
        o = _rms(o1 - lam * o2, subln_ref[...]) * (1.0 - lambda_init)
        o_ref[0] = o.astype(BF16)

    @pl.when(ok)
    def _():
        finish(a1 / l1, a2 / l2)

    @pl.when(jnp.logical_not(ok))
    def _():
        finish(*_diff_running_max(slope_ref[hd] * LOG2E, q_ref, k_ref, v_ref, qi, tq))


def _alibi_tables(slopes, S):
    s = jnp.asarray(slopes, F32).reshape(-1, 1) * LOG2E
    terms = jnp.concatenate(_split3(s), axis=1)
    qx = jnp.concatenate([terms * float(LANES), terms, jnp.zeros((terms.shape[0], LANES - 2 * N_SPLIT), F32)], axis=1)
    j = np.arange(S)
    kx = np.zeros((S, LANES), np.float32)
    kx[:, 0:N_SPLIT] = (j // LANES)[:, None]
    kx[:, N_SPLIT:2 * N_SPLIT] = (j % LANES)[:, None]
    kx[:, 2 * N_SPLIT:3 * N_SPLIT] = 1.0
    return qx.reshape(-1, 1, LANES), jnp.asarray(kx, dtype=BF16)


def _diff_attention(slopes, q, k, v, lambdas, subln, tq, lambda_init):
    B, S, D = q.shape
    qx, kx = _alibi_tables(slopes, S)
    return pl.pallas_call(
        functools.partial(_diff_kernel, tq=tq, lambda_init=lambda_init),
        grid=(B, DIFF_HEADS, S // tq),
        in_specs=[
            pl.BlockSpec(memory_space=pltpu.SMEM),
            pl.BlockSpec((1, 1, LANES), lambda b, h, i: (h, 0, 0)),
            pl.BlockSpec((1, tq, LANES), lambda b, h, i: (b, i, h)),
            pl.BlockSpec((1, S, LANES), lambda b, h, i: (b, 0, h)),
            _const_spec(kx.shape),
            pl.BlockSpec((1, S, LANES), lambda b, h, i: (b, 0, h)),
            pl.BlockSpec(lambdas.shape, lambda b, h, i: (0, 0)),
            pl.BlockSpec(subln.shape, lambda b, h, i: (0, 0)),
        ],
        out_specs=pl.BlockSpec((1, tq, LANES), lambda b, h, i: (b, i, h)),
        out_shape=jax.ShapeDtypeStruct((B, S, D), BF16),
        scratch_shapes=[pltpu.VMEM((8, LANES), F32)],
        compiler_params=_params(("parallel", "parallel", "arbitrary")),
        name="diff_attention",
    )(jnp.asarray(slopes, F32), qx, q, k, kx, v, lambdas, subln)


def _mem_kv_kernel(mem_ref, norm_ref, w_ref, gain_ref, k_ref, v_ref):
    h = _rms(mem_ref[0], norm_ref[0]).astype(BF16)
    z = _dot(h, w_ref[0])
    width = MEM_HEADS * MEM_HEAD_DIM
    gain = gain_ref[0]
    for hd in range(MEM_HEADS):
        sl = slice(MEM_HEAD_DIM * hd, MEM_HEAD_DIM * (hd + 1))
        k_ref[0, 0, :, sl] = _rms(z[:, sl], gain).astype(BF16)
    v_ref[0, 0] = z[:, width:2 * width].astype(BF16)


def _mem_kv(mem, norms, w_kv, gains):
    B, M, D = mem.shape
    L = norms.shape[0]
    width = MEM_HEADS * MEM_HEAD_DIM
    out = pl.BlockSpec((1, 1, M, width), lambda l, b: (l, b, 0, 0))
    return pl.pallas_call(
        _mem_kv_kernel,
        grid=(L, B),
        in_specs=[
            pl.BlockSpec((1, M, D), lambda l, b: (b, 0, 0)),
            pl.BlockSpec((1, 1, D), lambda l, b: (l, 0, 0)),
            pl.BlockSpec((1, D, 2 * width), lambda l, b: (l, 0, 0)),
            pl.BlockSpec((1, 1, MEM_HEAD_DIM), lambda l, b: (l, 0, 0)),
        ],
        out_specs=[out, out],
        out_shape=[jax.ShapeDtypeStruct((L, B, M, width), BF16)] * 2,
        compiler_params=_params(("parallel", "parallel")),
        name="mem_kv",
    )(mem, norms, w_kv, gains)


def _post_kernel(*refs, n_attn):
    x_ref = refs[0]
    attn_refs = refs[1:1 + n_attn]
    wo_refs = refs[1 + n_attn:1 + 2 * n_attn]
    (mqn_ref, wq_ref, qg_ref, mk_ref, mv_ref, wmo_ref, fn_ref, wg_ref, wu_ref, wd_ref, o_ref) = refs[1 + 2 * n_attn:]
    x = x_ref[0]
    for a_ref, w_ref in zip(attn_refs, wo_refs):
        x = x + _dot(a_ref[0], w_ref[...])

    q = _dot(_rms(x, mqn_ref[...]).astype(BF16), wq_ref[...])
    heads = []
    for hd in range(MEM_HEADS):
        sl = slice(MEM_HEAD_DIM * hd, MEM_HEAD_DIM * (hd + 1))
        qh = (_rms(q[:, sl], qg_ref[...]) * (MEM_HEAD_DIM ** -0.5 * LOG2E)).astype(BF16)
        s = _dot_nt(qh, mk_ref[0, :, sl])
        p = jnp.exp2(s - jnp.max(s, axis=-1, keepdims=True))
        l = jnp.sum(p, axis=-1, keepdims=True)
        heads.append((_dot(p.astype(BF16), mv_ref[0, :, sl]) / l).astype(BF16))
    x = x + _dot(jnp.concatenate(heads, axis=-1), wmo_ref[...])

    hf = _rms(x, fn_ref[...]).astype(BF16)
    g = _dot(hf, wg_ref[...])
    u = _dot(hf, wu_ref[...])
    act = (g * jax.nn.sigmoid(g) * u).astype(BF16)
    o_ref[0] = x + _dot(act, wd_ref[...])


def _post_block(x, attns, wos, mqn, wq, qg, mk, mv, wmo, fn, wg, wu, wd, ts):
    B, S, D = x.shape
    n = len(attns)
    tok = lambda c: pl.BlockSpec((1, ts, c), lambda b, i: (b, i, 0))
    memspec = pl.BlockSpec((1,) + mk.shape[1:], lambda b, i: (b, 0, 0))
    consts_a = list(wos) + [mqn, wq, qg]
    consts_b = [wmo, fn, wg, wu, wd]
    return pl.pallas_call(
        functools.partial(_post_kernel, n_attn=n),
        grid=(B, S // ts),
        in_specs=([tok(D)] + [tok(a.shape[-1]) for a in attns] + [_const_spec(c.shape) for c in consts_a]
                  + [memspec, memspec] + [_const_spec(c.shape) for c in consts_b]),
        out_specs=tok(D),
        out_shape=jax.ShapeDtypeStruct((B, S, D), F32),
        compiler_params=_params(("parallel", "parallel")),
        name="post_block",
    )(x, *attns, *consts_a, mk, mv, *consts_b)


def _row(v):
    return v.reshape(1, -1).astype(F32)


def _pad_cols(w, lo, total):
    return jnp.pad(w, ((0, 0), (lo, total - lo - w.shape[1])))


def _even_weights(w_in, w_uq, w_ukv, q_gain, k_gain):
    kr = _pad_cols(w_in[:, 1152:1184], MLA_NOPE, LANES)
    w1 = jnp.concatenate([w_in[:, :1152], kr], axis=1).astype(BF16)
    wuq = jnp.concatenate(
        [_pad_cols(w_uq[:, MLA_QK * h:MLA_QK * (h + 1)], 0, LANES) for h in range(MLA_HEADS)], axis=1).astype(BF16)
    kv_w = MLA_NOPE + MLA_V
    wuk = jnp.concatenate(
        [_pad_cols(w_ukv[:, kv_w * h:kv_w * h + MLA_NOPE], 0, LANES) for h in range(MLA_HEADS)], axis=1).astype(BF16)
    wuv = jnp.concatenate(
        [w_ukv[:, kv_w * h + MLA_NOPE:kv_w * (h + 1)] for h in range(MLA_HEADS)], axis=1).astype(BF16)
    gmq = _row(jnp.pad(q_gain, (0, LANES - MLA_QK)))
    gmk = _row(jnp.pad(k_gain, (0, LANES - MLA_QK)))
    return w1, wuq, wuk, wuv, gmq, gmk


def _rope_lane_freqs():
    half = MLA_ROPE // 2
    inv = np.zeros((1, LANES), np.float32)
    freqs = (np.float32(ROPE_THETA) ** (-np.arange(half, dtype=np.float32) / np.float32(half))).astype(np.float32)
    inv[0, MLA_NOPE:MLA_NOPE + half] = freqs
    inv[0, MLA_NOPE + half:MLA_QK] = freqs
    return jnp.asarray(inv)


def _ones_blocks(n):
    g = np.arange(n) // HEAD_DIM
    return jnp.asarray((g[:, None] == g[None, :]).astype(np.float32), dtype=BF16)


def kernel(x, mem, positions, mix_norm, ev_w_in, ev_swa_q_gain, ev_swa_k_gain, ev_sinks, ev_q_latent_norm,
           ev_kv_latent_norm, ev_w_uq, ev_w_ukv, ev_mla_q_gain, ev_mla_k_gain, ev_w_out, od_w_qkv, od_q_gain,
           od_k_gain, od_lambda, od_subln, od_w_out, mem_q_norm, mem_kv_norm, mem_w_q, mem_w_kv, mem_q_gain,
           mem_k_gain, mem_w_out, ffn_norm, ffn_w_gate, ffn_w_up, ffn_w_down):
    B, S, D = x.shape
    depth = mix_norm.shape[0]
    ts_proj = min(512, S)
    ts_post = min(256, S)
    tq_swa = min(256, S - SWA_WINDOW)
    tq_flash = min(512, S)

    ones = _ones_blocks(512)
    pos3 = positions.reshape(B, S, 1)
    inv = _rope_lane_freqs()
    mem_k, mem_v = _mem_kv(mem, mem_kv_norm.reshape(depth, 1, D), mem_w_kv.astype(BF16),
                           mem_k_gain.reshape(depth, 1, MEM_HEAD_DIM))
    diff_slopes = [2.0 ** (-8.0 * (i + 1) / DIFF_HEADS) for i in range(DIFF_HEADS)]

    for l in range(depth):
        if l % 2 == 0:
            e = l // 2
            w1, wuq, wuk, wuv, gmq, gmk = _even_weights(ev_w_in[e], ev_w_uq[e], ev_w_ukv[e], ev_mla_q_gain[e],
                                                        ev_mla_k_gain[e])
            qa, ka, va, qm, km, vm = _even_proj(
                x, pos3, _row(mix_norm[l]), w1, ones, _row(jnp.tile(ev_swa_q_gain[e], SWA_HEADS)),
                _row(jnp.tile(ev_swa_k_gain[e], SWA_KV_HEADS)), _row(ev_q_latent_norm[e]),
                _row(ev_kv_latent_norm[e]), wuq, wuk, wuv, gmq, gmk, inv, ts_proj)
            out_a = _swa_attention(ev_sinks[e].astype(F32), qa, ka, va, tq_swa)
            out_b = _mla_attention(qm, km, vm, tq_flash)
            wo = ev_w_out[e].astype(BF16)
            attns = [out_a, out_b]
            wos = [wo[:512], wo[512:]]
        else:
            o = l // 2
            lambda_init = 0.8 - 0.6 * math.exp(-0.3 * l)
            n_grp = DIFF_HEADS * 2
            q, k, v = _odd_proj(x, _row(mix_norm[l]), od_w_qkv[o].astype(BF16), ones,
                                _row(jnp.tile(od_q_gain[o], n_grp // 2)), _row(jnp.tile(od_k_gain[o], n_grp // 2)),
                                ts_proj)
            out_d = _diff_attention(diff_slopes, q, k, v, od_lambda[o].astype(F32), _row(od_subln[o]), tq_flash,
                                    lambda_init)
            attns = [out_d]
            wos = [od_w_out[o].astype(BF16)]
        x = _post_block(x, attns, wos, _row(mem_q_norm[l]), mem_w_q[l].astype(BF16), _row(mem_q_gain[l]),
                        mem_k[l], mem_v[l], mem_w_out[l].astype(BF16), _row(ffn_norm[l]),
                        ffn_w_gate[l].astype(BF16), ffn_w_up[l].astype(BF16), ffn_w_down[l].astype(BF16), ts_post)
    return x
```

```python
import functools
import math

import numpy as np
import jax
import jax.numpy as jnp
from jax import lax
from jax.experimental import pallas as pl
from jax.experimental.pallas import tpu as pltpu

F32 = jnp.float32
BF16 = jnp.bfloat16

EPS = 1e-6
MASKED = -1e30
LOG2E = 1.4426950408889634
ROPE_THETA = 10000.0

LANES = 128
HEAD_DIM = 64
SWA_HEADS = 8
SWA_KV_HEADS = 2
SWA_WINDOW = 128
MLA_HEADS = 8
MLA_NOPE = 64
MLA_ROPE = 32
MLA_QK = MLA_NOPE + MLA_ROPE
MLA_V = 64
DIFF_HEADS = 8
DIFF_DIM = 64
MEM_HEADS = 4
MEM_HEAD_DIM = 128

VMEM_LIMIT = 56 * 1024 * 1024

UNDERFLOW_GUARD = 2.0 ** -80
N_SPLIT = 3


def _params(sem):
    return pltpu.CompilerParams(dimension_semantics=sem, vmem_limit_bytes=VMEM_LIMIT)


def _const_spec(shape):
    nd = len(shape)
    return pl.BlockSpec(shape, lambda *_: (0,) * nd, pipeline_mode=pl.Buffered(1))


def _rms(x, g):
    ms = jnp.mean(x * x, axis=-1, keepdims=True)
    return x * lax.rsqrt(ms + EPS) * g


def _dot(a, b):
    return jnp.dot(a, b, preferred_element_type=F32)


def _dot_nt(a, b):
    return lax.dot_general(a, b, (((1,), (1,)), ((), ())), preferred_element_type=F32)


def _group_sumsq(x, ones_blk):
    x2 = x * x
    hi = x2.astype(BF16)
    lo = (x2 - hi.astype(F32)).astype(BF16)
    return _dot(hi, ones_blk) + _dot(lo, ones_blk)


def _group_norm64(x, ones_blk, gain):
    return x * lax.rsqrt(_group_sumsq(x, ones_blk) * (1.0 / HEAD_DIM) + EPS) * gain


def _even_proj_kernel(x_ref, pos_ref, gmix_ref, w1_ref, ones_ref, gq_ref, gk_ref, qln_ref, kvln_ref,
                      wuq_ref, wuk_ref, wuv_ref, gmq_ref, gmk_ref, inv_ref,
                      qa_ref, ka_ref, va_ref, qm_ref, km_ref, vm_ref):
    x = x_ref[0]
    h = _rms(x, gmix_ref[...]).astype(BF16)
    z = _dot(h, w1_ref[...])
    ones = ones_ref[...]
    qa = _group_norm64(z[:, 0:512], ones, gq_ref[...]) * (HEAD_DIM ** -0.5 * LOG2E)
    qa_ref[0] = qa.astype(BF16)
    ka = _group_norm64(z[:, 512:640], ones[0:128, 0:128], gk_ref[...])
    ka_ref[0] = ka.astype(BF16)
    va_ref[0] = z[:, 640:768].astype(BF16)

    cqn = _rms(z[:, 768:1024], qln_ref[...]).astype(BF16)
    ckvn = _rms(z[:, 1024:1152], kvln_ref[...]).astype(BF16)
    kr = z[:, 1152:1280]
    qf = _dot(cqn, wuq_ref[...])
    kn = _dot(ckvn, wuk_ref[...])
    vm_ref[0] = _dot(ckvn, wuv_ref[...]).astype(BF16)

    ang = pos_ref[0].astype(F32) * inv_ref[...]
    lane = lax.broadcasted_iota(jnp.int32, ang.shape, 1)
    cos = jnp.cos(ang)
    sin = jnp.sin(ang)
    half = MLA_ROPE // 2
    cos_t = jnp.where(lane < MLA_NOPE, 1.0, cos)
    sin_lo = jnp.where((lane >= MLA_NOPE) & (lane < MLA_NOPE + half), -sin, 0.0)
    sin_hi = jnp.where((lane >= MLA_NOPE + half) & (lane < MLA_QK), sin, 0.0)

    def norm_rope(xh, gain):
        r = lax.rsqrt(jnp.sum(xh * xh, axis=-1, keepdims=True) * (1.0 / MLA_QK) + EPS)
        xn = xh * r * gain
        return (xn * cos_t + pltpu.roll(xn, LANES - half, 1) * sin_lo + pltpu.roll(xn, half, 1) * sin_hi)

    gmq = gmq_ref[...]
    gmk = gmk_ref[...]
    ones_lanes = (lane >= MLA_QK) & (lane < MLA_QK + N_SPLIT)
    for hd in range(MLA_HEADS):
        sl = slice(LANES * hd, LANES * (hd + 1))
        qm_ref[0, :, sl] = (norm_rope(qf[:, sl], gmq) * (MLA_QK ** -0.5 * LOG2E)).astype(BF16)
        km_ref[0, :, sl] = jnp.where(ones_lanes, 1.0, norm_rope(kn[:, sl] + kr, gmk)).astype(BF16)


def _even_proj(x, pos3, gmix, w1, ones, gq, gk, qln, kvln, wuq, wuk, wuv, gmq, gmk, inv, ts):
    B, S, D = x.shape
    grid = (B, S // ts)
    tok = lambda c: pl.BlockSpec((1, ts, c), lambda b, i: (b, i, 0))
    consts = [gmix, w1, ones, gq, gk, qln, kvln, wuq, wuk, wuv, gmq, gmk, inv]
    out_cols = [512, 128, 128, 1024, 1024, 512]
    return pl.pallas_call(
        _even_proj_kernel,
        grid=grid,
        in_specs=[tok(D), tok(1)] + [_const_spec(c.shape) for c in consts],
        out_specs=[tok(c) for c in out_cols],
        out_shape=[jax.ShapeDtypeStruct((B, S, c), BF16) for c in out_cols],
        compiler_params=_params(("parallel", "parallel")),
        name="even_proj",
    )(x, pos3, *consts)


def _swa_kernel(sink_ref, q_ref, k_ref, v_ref, o_ref, *, tq):
    qi = pl.program_id(1)
    win = tq + SWA_WINDOW
    wstart = pl.multiple_of(jnp.maximum(qi * tq - SWA_WINDOW, 0), SWA_WINDOW)
    kw = k_ref[0, pl.ds(wstart, win), :]
    vw = v_ref[0, pl.ds(wstart, win), :]
    row = lax.broadcasted_iota(jnp.int32, (tq, win), 0)
    col = lax.broadcasted_iota(jnp.int32, (tq, win), 1)
    dist = (qi * tq + row) - (wstart + col)
    valid = (dist >= 0) & (dist < SWA_WINDOW)
    distf = dist.astype(F32)
    lane = lax.broadcasted_iota(jnp.int32, (tq, LANES), 1)
    group = SWA_HEADS // SWA_KV_HEADS
    for cb in range(SWA_HEADS // 2):
        qq = q_ref[0, :, LANES * cb:LANES * (cb + 1)].astype(F32)
        halves = []
        for e in range(2):
            hd = 2 * cb + e
            hk = hd // group
            xq = qq if e == hk else pltpu.roll(qq, HEAD_DIM, 1)
            in_kv = (lane >= HEAD_DIM * hk) & (lane < HEAD_DIM * (hk + 1))
            xq = jnp.where(in_kv, xq, 0.0).astype(BF16)
            s = _dot_nt(xq, kw) - (2.0 ** (-8.0 * (hd + 1) / SWA_HEADS) * LOG2E) * distf
            s = jnp.where(valid, s, MASKED)
            sink = sink_ref[hd] * LOG2E
            m = jnp.maximum(jnp.max(s, axis=-1, keepdims=True), sink)
            p = jnp.exp2(s - m)
            l = jnp.sum(p, axis=-1, keepdims=True) + jnp.exp2(sink - m)
            o = _dot(p.astype(BF16), vw) / l
            halves.append(o if e == hk else pltpu.roll(o, HEAD_DIM, 1))
        o_ref[0, :, LANES * cb:LANES * (cb + 1)] = jnp.where(lane < HEAD_DIM, halves[0], halves[1]).astype(BF16)


def _swa_attention(sinks, qa, ka, va, tq):
    B, S, _ = qa.shape
    return pl.pallas_call(
        functools.partial(_swa_kernel, tq=tq),
        grid=(B, S // tq),
        in_specs=[
            pl.BlockSpec(memory_space=pltpu.SMEM),
            pl.BlockSpec((1, tq, 512), lambda b, i: (b, i, 0)),
            pl.BlockSpec((1, S, LANES), lambda b, i: (b, 0, 0)),
            pl.BlockSpec((1, S, LANES), lambda b, i: (b, 0, 0)),
        ],
        out_specs=pl.BlockSpec((1, tq, 512), lambda b, i: (b, i, 0)),
        out_shape=jax.ShapeDtypeStruct((B, S, 512), BF16),
        compiler_params=_params(("parallel", "arbitrary")),
        name="swa_attention",
    )(sinks, qa, ka, va)


def _split3(x):
    hi = x.astype(BF16).astype(F32)
    r = x - hi
    mid = r.astype(BF16).astype(F32)
    lo = (r - mid).astype(BF16).astype(F32)
    return hi, mid, lo


def _lane_tile_sum(p):
    acc = p[:, 0:LANES]
    for t in range(1, p.shape[1] // LANES):
        acc = acc + p[:, LANES * t:LANES * (t + 1)]
    return acc


def _max_sq_norm(k_ref, lanes, gsum, rows):
    n_chunks = k_ref.shape[1] // rows

    def body(c, mx):
        kc = k_ref[0, pl.ds(pl.multiple_of(c * rows, rows), rows), lanes].astype(F32)
        return jnp.maximum(mx, jnp.max(_dot((kc * kc).astype(BF16), gsum), axis=0, keepdims=True))

    return lax.fori_loop(0, n_chunks, body, jnp.zeros((1, LANES), F32))


def _causal_mask(tq):
    row = lax.broadcasted_iota(jnp.int32, (tq, tq), 0)
    col = lax.broadcasted_iota(jnp.int32, (tq, tq), 1)
    return row >= col


def _mla_running_max(q_ref, k_ref, v_ref, e, qi, tq):
    causal = _causal_mask(tq)
    q = q_ref[0, :, LANES * e:LANES * (e + 1)]

    def step(j, carry, masked):
        m, l, acc = carry
        ks = pl.multiple_of(j * tq, tq)
        k = k_ref[0, pl.ds(ks, tq), LANES * e:LANES * (e + 1)]
        v = v_ref[0, pl.ds(ks, tq), :]
        s = _dot_nt(q, k)
        if masked:
            s = jnp.where(causal, s, MASKED)
        m_new = jnp.maximum(m, jnp.max(s, axis=-1, keepdims=True))
        alpha = jnp.exp2(m - m_new)
        p = jnp.exp2(s - m_new)
        l = alpha * l + jnp.sum(p, axis=-1, keepdims=True)
        acc = alpha * acc + _dot(p.astype(BF16), v)
        return m_new, l, acc

    init = (jnp.full((tq, 1), MASKED, F32), jnp.zeros((tq, 1), F32), jnp.zeros((tq, LANES), F32))
    carry = lax.fori_loop(0, qi, lambda j, c: step(j, c, False), init)
    m, l, acc = step(qi, carry, True)
    return acc / l


def _mla_kernel(q_ref, k_ref, v_ref, o_ref, kmax_ref, *, tq):
    qi = pl.program_id(2)
    lane = lax.broadcasted_iota(jnp.int32, (tq, LANES), 1)
    gsum = (lax.broadcasted_iota(jnp.int32, (LANES, LANES), 0) < MLA_QK).astype(BF16)

    @pl.when(qi == 0)
    def _():
        for e in range(2):
            kmax_ref[e:e + 1, :] = _max_sq_norm(k_ref, slice(LANES * e, LANES * (e + 1)), gsum, tq)

    causal = _causal_mask(tq)
    outs = []
    sums = []
    for e in range(2):
        sl = slice(LANES * e, LANES * (e + 1))
        qf = q_ref[0, :, sl].astype(F32)
        bound = jnp.sqrt(_dot((qf * qf).astype(BF16), gsum) * kmax_ref[e:e + 1, :])
        hi, mid, lo = _split3(-bound)
        qa = jnp.where(lane == MLA_QK, hi, jnp.where(lane == MLA_QK + 1, mid,
                                                      jnp.where(lane == MLA_QK + 2, lo, qf))).astype(BF16)

        def step(j, carry, masked):
            lsum, acc = carry
            ks = pl.multiple_of(j * tq, tq)
            s = _dot_nt(qa, k_ref[0, pl.ds(ks, tq), sl])
            if masked:
                s = jnp.where(causal, s, MASKED)
            p = jnp.exp2(s)
            return lsum + _lane_tile_sum(p), acc + _dot(p.astype(BF16), v_ref[0, pl.ds(ks, tq), :])

        zero = jnp.zeros((tq, LANES), F32)
        carry = lax.fori_loop(0, qi, lambda j, c: step(j, c, False), (zero, zero))
        lsum, acc = step(qi, carry, True)
        l = jnp.sum(lsum, axis=-1, keepdims=True)
        outs.append(acc / l)
        sums.append(jnp.min(l))
    ok = jnp.minimum(sums[0], sums[1]) >= UNDERFLOW_GUARD

    @pl.when(ok)
    def _():
        o_ref[0] = jnp.where(lane < MLA_V, outs[0], outs[1]).astype(BF16)

    @pl.when(jnp.logical_not(ok))
    def _():
        slow = [_mla_running_max(q_ref, k_ref, v_ref, e, qi, tq) for e in range(2)]
        o_ref[0] = jnp.where(lane < MLA_V, slow[0], slow[1]).astype(BF16)


def _mla_attention(qm, km, vm, tq):
    B, S, _ = qm.shape
    return pl.pallas_call(
        functools.partial(_mla_kernel, tq=tq),
        grid=(B, MLA_HEADS // 2, S // tq),
        in_specs=[
            pl.BlockSpec((1, tq, 2 * LANES), lambda b, h, i: (b, i, h)),
            pl.BlockSpec((1, S, 2 * LANES), lambda b, h, i: (b, 0, h)),
            pl.BlockSpec((1, S, LANES), lambda b, h, i: (b, 0, h)),
        ],
        out_specs=pl.BlockSpec((1, tq, LANES), lambda b, h, i: (b, i, h)),
        out_shape=jax.ShapeDtypeStruct((B, S, MLA_HEADS * MLA_V), BF16),
        scratch_shapes=[pltpu.VMEM((8, LANES), F32)],
        compiler_params=_params(("parallel", "parallel", "arbitrary")),
        name="mla_attention",
    )(qm, km, vm)


def _odd_proj_kernel(x_ref, gmix_ref, w_ref, ones_ref, gq_ref, gk_ref, q_ref, k_ref, v_ref):
    x = x_ref[0]
    h = _rms(x, gmix_ref[...]).astype(BF16)
    z = _dot(h, w_ref[...])
    ones = ones_ref[...]
    width = DIFF_HEADS * 2 * DIFF_DIM
    for c in range(width // 512):
        sl = slice(512 * c, 512 * (c + 1))
        q = _group_norm64(z[:, sl], ones, gq_ref[...]) * (DIFF_DIM ** -0.5 * LOG2E)
        q_ref[0, :, sl] = q.astype(BF16)
        k = _group_norm64(z[:, width + 512 * c:width + 512 * (c + 1)], ones, gk_ref[...])
        k_ref[0, :, sl] = k.astype(BF16)
    v_ref[0] = z[:, 2 * width:3 * width].astype(BF16)


def _odd_proj(x, gmix, w, ones, gq, gk, ts):
    B, S, D = x.shape
    tok = pl.BlockSpec((1, ts, D), lambda b, i: (b, i, 0))
    consts = [gmix, w, ones, gq, gk]
    return pl.pallas_call(
        _odd_proj_kernel,
        grid=(B, S // ts),
        in_specs=[tok] + [_const_spec(c.shape) for c in consts],
        out_specs=[tok, tok, tok],
        out_shape=[jax.ShapeDtypeStruct((B, S, D), BF16)] * 3,
        compiler_params=_params(("parallel", "parallel")),
        name="odd_proj",
    )(x, *consts)


def _diff_q_parts(q, lane):
    zero = jnp.zeros_like(q)
    return jnp.where(lane < DIFF_DIM, q, zero), jnp.where(lane >= DIFF_DIM, q, zero)


def _diff_running_max(slope, q_ref, k_ref, v_ref, qi, tq):
    row = lax.broadcasted_iota(jnp.int32, (tq, tq), 0)
    col = lax.broadcasted_iota(jnp.int32, (tq, tq), 1)
    causal = row >= col
    bias = slope * (col - row).astype(F32)
    lane = lax.broadcasted_iota(jnp.int32, (tq, LANES), 1)
    q_parts = _diff_q_parts(q_ref[0], lane)

    def step(j, carry, masked):
        ks = pl.multiple_of(j * tq, tq)
        k = k_ref[0, pl.ds(ks, tq), :]
        v = v_ref[0, pl.ds(ks, tq), :]
        offset = slope * ((j - qi) * tq).astype(F32)
        new = []
        for c in range(2):
            m, l, acc = carry[c]
            s = _dot_nt(q_parts[c], k) + bias
            if masked:
                s = jnp.where(causal, s, MASKED)
            m_new = jnp.maximum(m, jnp.max(s, axis=-1, keepdims=True) + offset)
            alpha = jnp.exp2(m - m_new)
            p = jnp.exp2(s - (m_new - offset))
            l = alpha * l + jnp.sum(p, axis=-1, keepdims=True)
            acc = alpha * acc + _dot(p.astype(BF16), v)
            new.append((m_new, l, acc))
        return tuple(new)

    one = (jnp.full((tq, 1), MASKED, F32), jnp.zeros((tq, 1), F32), jnp.zeros((tq, LANES), F32))
    carry = lax.fori_loop(0, qi, lambda j, c: step(j, c, False), (one, one))
    (m1, l1, a1), (m2, l2, a2) = step(qi, carry, True)
    return a1 / l1, a2 / l2


def _diff_kernel(slope_ref, qx_ref, q_ref, k_ref, kx_ref, v_ref, lam_ref, subln_ref, o_ref, kmax_ref, *, tq,
                 lambda_init):
    hd = pl.program_id(1)
    qi = pl.program_id(2)
    lane = lax.broadcasted_iota(jnp.int32, (tq, LANES), 1)
    gi = lax.broadcasted_iota(jnp.int32, (LANES, LANES), 0)
    gsums = ((gi < DIFF_DIM).astype(BF16), (gi >= DIFF_DIM).astype(BF16))

    @pl.when(qi == 0)
    def _():
        for c in range(2):
            kmax_ref[c:c + 1, :] = _max_sq_norm(k_ref, slice(None), gsums[c], tq)

    causal = _causal_mask(tq)
    q = q_ref[0]
    qf = q.astype(F32)
    q2 = (qf * qf).astype(BF16)
    q_parts = _diff_q_parts(q, lane)
    qx_const = qx_ref[0]
    slope = qx_const[:, 3:4] + qx_const[:, 4:5] + qx_const[:, 5:6]
    rowpos = (qi * tq + lax.broadcasted_iota(jnp.int32, (tq, LANES), 0)).astype(F32)
    q_aug = []
    for c in range(2):
        bound = jnp.sqrt(_dot(q2, gsums[c]) * kmax_ref[c:c + 1, :])
        hi, mid, lo = _split3(-slope * rowpos - bound)
        qx = jnp.where(lane == 2 * N_SPLIT, hi, jnp.where(lane == 2 * N_SPLIT + 1, mid,
                                                           jnp.where(lane == 2 * N_SPLIT + 2, lo, qx_const)))
        q_aug.append(jnp.concatenate([q_parts[c], qx.astype(BF16)], axis=1))

    def step(j, carry, masked):
        ks = pl.multiple_of(j * tq, tq)
        k = jnp.concatenate([k_ref[0, pl.ds(ks, tq), :], kx_ref[pl.ds(ks, tq), :]], axis=1)
        v = v_ref[0, pl.ds(ks, tq), :]
        new = []
        for c in range(2):
            lsum, acc = carry[c]
            s = _dot_nt(q_aug[c], k)
            if masked:
                s = jnp.where(causal, s, MASKED)
            p = jnp.exp2(s)
            new.append((lsum + _lane_tile_sum(p), acc + _dot(p.astype(BF16), v)))
        return tuple(new)

    zero = jnp.zeros((tq, LANES), F32)
    carry = lax.fori_loop(0, qi, lambda j, c: step(j, c, False), ((zero, zero), (zero, zero)))
    (ls1, a1), (ls2, a2) = step(qi, carry, True)
    l1 = jnp.sum(ls1, axis=-1, keepdims=True)
    l2 = jnp.sum(ls2, axis=-1, keepdims=True)
    ok = jnp.minimum(jnp.min(l1), jnp.min(l2)) >= UNDERFLOW_GUARD

    lf = lam_ref[...]
    lam = (jnp.exp(jnp.sum(lf[0:1] * lf[1:2], axis=-1, keepdims=True))
           - jnp.exp(jnp.sum(lf[2:3] * lf[3:4], axis=-1, keepdims=True)) + lambda_init)

    def finish(o1, o2):
        o = _rms(o1 - lam * o2, subln_ref[...]) * (1.0 - lambda_init)
        o_ref[0] = o.astype(BF16)

    @pl.when(ok)
    def _():
        finish(a1 / l1, a2 / l2)

    @pl.when(jnp.logical_not(ok))
    def _():
        finish(*_diff_running_max(slope_ref[hd] * LOG2E, q_ref, k_ref, v_ref, qi, tq))


def _alibi_tables(slopes, S):
    s = jnp.asarray(slopes, F32).reshape(-1, 1) * LOG2E
    terms = jnp.concatenate(_split3(s), axis=1)
    qx = jnp.concatenate([terms * float(LANES), terms, jnp.zeros((terms.shape[0], LANES - 2 * N_SPLIT), F32)], axis=1)
    j = np.arange(S)
    kx = np.zeros((S, LANES), np.float32)
    kx[:, 0:N_SPLIT] = (j // LANES)[:, None]
    kx[:, N_SPLIT:2 * N_SPLIT] = (j % LANES)[:, None]
    kx[:, 2 * N_SPLIT:3 * N_SPLIT] = 1.0
    return qx.reshape(-1, 1, LANES), jnp.asarray(kx, dtype=BF16)


def _diff_attention(slopes, q, k, v, lambdas, subln, tq, lambda_init):
    B, S, D = q.shape
    qx, kx = _alibi_tables(slopes, S)
    return pl.pallas_call(
        functools.partial(_diff_kernel, tq=tq, lambda_init=lambda_init),
        grid=(B, DIFF_HEADS, S // tq),
        in_specs=[
            pl.BlockSpec(memory_space=pltpu.SMEM),
            pl.BlockSpec((1, 1, LANES), lambda b, h, i: (h, 0, 0)),
            pl.BlockSpec((1, tq, LANES), lambda b, h, i: (b, i, h)),
            pl.BlockSpec((1, S, LANES), lambda b, h, i: (b, 0, h)),
            _const_spec(kx.shape),
            pl.BlockSpec((1, S, LANES), lambda b, h, i: (b, 0, h)),
            pl.BlockSpec(lambdas.shape, lambda b, h, i: (0, 0)),
            pl.BlockSpec(subln.shape, lambda b, h, i: (0, 0)),
        ],
        out_specs=pl.BlockSpec((1, tq, LANES), lambda b, h, i: (b, i, h)),
        out_shape=jax.ShapeDtypeStruct((B, S, D), BF16),
        scratch_shapes=[pltpu.VMEM((8, LANES), F32)],
        compiler_params=_params(("parallel", "parallel", "arbitrary")),
        name="diff_attention",
    )(jnp.asarray(slopes, F32), qx, q, k, kx, v, lambdas, subln)


def _mem_kv_kernel(mem_ref, norm_ref, w_ref, gain_ref, k_ref, v_ref):
    h = _rms(mem_ref[0], norm_ref[0]).astype(BF16)
    z = _dot(h, w_ref[0])
    width = MEM_HEADS * MEM_HEAD_DIM
    gain = gain_ref[0]
    for hd in range(MEM_HEADS):
        sl = slice(MEM_HEAD_DIM * hd, MEM_HEAD_DIM * (hd + 1))
        k_ref[0, 0, :, sl] = _rms(z[:, sl], gain).astype(BF16)
    v_ref[0, 0] = z[:, width:2 * width].astype(BF16)


def _mem_kv(mem, norms, w_kv, gains):
    B, M, D = mem.shape
    L = norms.shape[0]
    width = MEM_HEADS * MEM_HEAD_DIM
    out = pl.BlockSpec((1, 1, M, width), lambda l, b: (l, b, 0, 0))
    return pl.pallas_call(
        _mem_kv_kernel,
        grid=(L, B),
        in_specs=[
            pl.BlockSpec((1, M, D), lambda l, b: (b, 0, 0)),
            pl.BlockSpec((1, 1, D), lambda l, b: (l, 0, 0)),
            pl.BlockSpec((1, D, 2 * width), lambda l, b: (l, 0, 0)),
            pl.BlockSpec((1, 1, MEM_HEAD_DIM), lambda l, b: (l, 0, 0)),
        ],
        out_specs=[out, out],
        out_shape=[jax.ShapeDtypeStruct((L, B, M, width), BF16)] * 2,
        compiler_params=_params(("parallel", "parallel")),
        name="mem_kv",
    )(mem, norms, w_kv, gains)


def _post_kernel(*refs, n_attn):
    x_ref = refs[0]
    attn_refs = refs[1:1 + n_attn]
    wo_refs = refs[1 + n_attn:1 + 2 * n_attn]
    (mqn_ref, wq_ref, qg_ref, mk_ref, mv_ref, wmo_ref, fn_ref, wg_ref, wu_ref, wd_ref, o_ref) = refs[1 + 2 * n_attn:]
    x = x_ref[0]
    for a_ref, w_ref in zip(attn_refs, wo_refs):
        x = x + _dot(a_ref[0], w_ref[...])

    q = _dot(_rms(x, mqn_ref[...]).astype(BF16), wq_ref[...])
    heads = []
    for hd in range(MEM_HEADS):
        sl = slice(MEM_HEAD_DIM * hd, MEM_HEAD_DIM * (hd + 1))
        qh = (_rms(q[:, sl], qg_ref[...]) * (MEM_HEAD_DIM ** -0.5 * LOG2E)).astype(BF16)
        s = _dot_nt(qh, mk_ref[0, :, sl])
        p = jnp.exp2(s - jnp.max(s, axis=-1, keepdims=True))
        l = jnp.sum(p, axis=-1, keepdims=True)
        heads.append((_dot(p.astype(BF16), mv_ref[0, :, sl]) / l).astype(BF16))
    x = x + _dot(jnp.concatenate(heads, axis=-1), wmo_ref[...])

    hf = _rms(x, fn_ref[...]).astype(BF16)
    g = _dot(hf, wg_ref[...])
    u = _dot(hf, wu_ref[...])
    act = (g * jax.nn.sigmoid(g) * u).astype(BF16)
    o_ref[0] = x + _dot(act, wd_ref[...])


def _post_block(x, attns, wos, mqn, wq, qg, mk, mv, wmo, fn, wg, wu, wd, ts):
    B, S, D = x.shape
    n = len(attns)
    tok = lambda c: pl.BlockSpec((1, ts, c), lambda b, i: (b, i, 0))
    memspec = pl.BlockSpec((1,) + mk.shape[1:], lambda b, i: (b, 0, 0))
    consts_a = list(wos) + [mqn, wq, qg]
    consts_b = [wmo, fn, wg, wu, wd]
    return pl.pallas_call(
        functools.partial(_post_kernel, n_attn=n),
        grid=(B, S // ts),
        in_specs=([tok(D)] + [tok(a.shape[-1]) for a in attns] + [_const_spec(c.shape) for c in consts_a]
                  + [memspec, memspec] + [_const_spec(c.shape) for c in consts_b]),
        out_specs=tok(D),
        out_shape=jax.ShapeDtypeStruct((B, S, D), F32),
        compiler_params=_params(("parallel", "parallel")),
        name="post_block",
    )(x, *attns, *consts_a, mk, mv, *consts_b)


def _row(v):
    return v.reshape(1, -1).astype(F32)


def _pad_cols(w, lo, total):
    return jnp.pad(w, ((0, 0), (lo, total - lo - w.shape[1])))


def _even_weights(w_in, w_uq, w_ukv, q_gain, k_gain):
    kr = _pad_cols(w_in[:, 1152:1184], MLA_NOPE, LANES)
    w1 = jnp.concatenate([w_in[:, :1152], kr], axis=1).astype(BF16)
    wuq = jnp.concatenate(
        [_pad_cols(w_uq[:, MLA_QK * h:MLA_QK * (h + 1)], 0, LANES) for h in range(MLA_HEADS)], axis=1).astype(BF16)
    kv_w = MLA_NOPE + MLA_V
    wuk = jnp.concatenate(
        [_pad_cols(w_ukv[:, kv_w * h:kv_w * h + MLA_NOPE], 0, LANES) for h in range(MLA_HEADS)], axis=1).astype(BF16)
    wuv = jnp.concatenate(
        [w_ukv[:, kv_w * h + MLA_NOPE:kv_w * (h + 1)] for h in range(MLA_HEADS)], axis=1).astype(BF16)
    gmq = _row(jnp.pad(q_gain, (0, LANES - MLA_QK)))
    gmk = _row(jnp.pad(k_gain, (0, LANES - MLA_QK)))
    return w1, wuq, wuk, wuv, gmq, gmk


def _rope_lane_freqs():
    half = MLA_ROPE // 2
    inv = np.zeros((1, LANES), np.float32)
    freqs = (np.float32(ROPE_THETA) ** (-np.arange(half, dtype=np.float32) / np.float32(half))).astype(np.float32)
    inv[0, MLA_NOPE:MLA_NOPE + half] = freqs
    inv[0, MLA_NOPE + half:MLA_QK] = freqs
    return jnp.asarray(inv)


def _ones_blocks(n):
    g = np.arange(n) // HEAD_DIM
    return jnp.asarray((g[:, None] == g[None, :]).astype(np.float32), dtype=BF16)


def kernel(x, mem, positions, mix_norm, ev_w_in, ev_swa_q_gain, ev_swa_k_gain, ev_sinks, ev_q_latent_norm,
           ev_kv_latent_norm, ev_w_uq, ev_w_ukv, ev_mla_q_gain, ev_mla_k_gain, ev_w_out, od_w_qkv, od_q_gain,
           od_k_gain, od_lambda, od_subln, od_w_out, mem_q_norm, mem_kv_norm, mem_w_q, mem_w_kv, mem_q_gain,
           mem_k_gain, mem_w_out, ffn_norm, ffn_w_gate, ffn_w_up, ffn_w_down):
    B, S, D = x.shape
    depth = mix_norm.shape[0]
    ts_proj = min(512, S)
    ts_post = min(256, S)
    tq_swa = min(256, S - SWA_WINDOW)
    tq_flash = min(1024, S)

    ones = _ones_blocks(512)
    pos3 = positions.reshape(B, S, 1)
    inv = _rope_lane_freqs()
    mem_k, mem_v = _mem_kv(mem, mem_kv_norm.reshape(depth, 1, D), mem_w_kv.astype(BF16),
                           mem_k_gain.reshape(depth, 1, MEM_HEAD_DIM))
    diff_slopes = [2.0 ** (-8.0 * (i + 1) / DIFF_HEADS) for i in range(DIFF_HEADS)]

    for l in range(depth):
        if l % 2 == 0:
            e = l // 2
            w1, wuq, wuk, wuv, gmq, gmk = _even_weights(ev_w_in[e], ev_w_uq[e], ev_w_ukv[e], ev_mla_q_gain[e],
                                                        ev_mla_k_gain[e])
            qa, ka, va, qm, km, vm = _even_proj(
                x, pos3, _row(mix_norm[l]), w1, ones, _row(jnp.tile(ev_swa_q_gain[e], SWA_HEADS)),
                _row(jnp.tile(ev_swa_k_gain[e], SWA_KV_HEADS)), _row(ev_q_latent_norm[e]),
                _row(ev_kv_latent_norm[e]), wuq, wuk, wuv, gmq, gmk, inv, ts_proj)
            out_a = _swa_attention(ev_sinks[e].astype(F32), qa, ka, va, tq_swa)
            out_b = _mla_attention(qm, km, vm, tq_flash)
            wo = ev_w_out[e].astype(BF16)
            attns = [out_a, out_b]
            wos = [wo[:512], wo[512:]]
        else:
            o = l // 2
            lambda_init = 0.8 - 0.6 * math.exp(-0.3 * l)
            n_grp = DIFF_HEADS * 2
            q, k, v = _odd_proj(x, _row(mix_norm[l]), od_w_qkv[o].astype(BF16), ones,
                                _row(jnp.tile(od_q_gain[o], n_grp // 2)), _row(jnp.tile(od_k_gain[o], n_grp // 2)),
                                ts_proj)
            out_d = _diff_attention(diff_slopes, q, k, v, od_lambda[o].astype(F32), _row(od_subln[o]), tq_flash,
                                    lambda_init)
            attns = [out_d]
            wos = [od_w_out[o].astype(BF16)]
        x = _post_block(x, attns, wos, _row(mem_q_norm[l]), mem_w_q[l].astype(BF16), _row(mem_q_gain[l]),
                        mem_k[l], mem_v[l], mem_w_out[l].astype(BF16), _row(ffn_norm[l]),
                        ffn_w_gate[l].astype(BF16), ffn_w_up[l].astype(BF16), ffn_w_down[l].astype(BF16), ts_post)
    return x
```

```python
import functools
import math

import numpy as np
import jax
import jax.numpy as jnp
from jax import lax
from jax.experimental import pallas as pl
from jax.experimental.pallas import tpu as pltpu

F32 = jnp.float32
BF16 = jnp.bfloat16

EPS = 1e-6
MASKED = -1e30
LOG2E = 1.4426950408889634
ROPE_THETA = 10000.0

LANES = 128
SUBLANES = 8
HEAD_DIM = 64
SWA_HEADS = 8
SWA_KV_HEADS = 2
SWA_WINDOW = 128
MLA_HEADS = 8
MLA_NOPE = 64
MLA_ROPE = 32
MLA_QK = MLA_NOPE + MLA_ROPE
MLA_V = 64
DIFF_HEADS = 8
DIFF_DIM = 64
MEM_HEADS = 4
MEM_HEAD_DIM = 128

VMEM_LIMIT = 56 * 1024 * 1024

UNDERFLOW_GUARD = 2.0 ** -80
N_SPLIT = 3


def _params(sem):
    return pltpu.CompilerParams(dimension_semantics=sem, vmem_limit_bytes=VMEM_LIMIT)


def _const_spec(shape):
    nd = len(shape)
    return pl.BlockSpec(shape, lambda *_: (0,) * nd, pipeline_mode=pl.Buffered(1))


def _rms(x, g):
    ms = jnp.mean(x * x, axis=-1, keepdims=True)
    return x * lax.rsqrt(ms + EPS) * g


def _dot(a, b):
    return jnp.dot(a, b, preferred_element_type=F32)


def _dot_nt(a, b):
    return lax.dot_general(a, b, (((1,), (1,)), ((), ())), preferred_element_type=F32)


def _group_sumsq(x, ones_blk):
    x2 = x * x
    hi = x2.astype(BF16)
    lo = (x2 - hi.astype(F32)).astype(BF16)
    return _dot(hi, ones_blk) + _dot(lo, ones_blk)


def _group_norm64(x, ones_blk, gain):
    return x * lax.rsqrt(_group_sumsq(x, ones_blk) * (1.0 / HEAD_DIM) + EPS) * gain


def _even_proj_kernel(x_ref, pos_ref, gmix_ref, w1_ref, ones_ref, gq_ref, gk_ref, qln_ref, kvln_ref,
                      wuq_ref, wuk_ref, wuv_ref, gmq_ref, gmk_ref, inv_ref,
                      qa_ref, ka_ref, va_ref, qm_ref, km_ref, vm_ref):
    x = x_ref[0]
    h = _rms(x, gmix_ref[...]).astype(BF16)
    z = _dot(h, w1_ref[...])
    ones = ones_ref[...]
    qa = _group_norm64(z[:, 0:512], ones, gq_ref[...]) * (HEAD_DIM ** -0.5 * LOG2E)
    qa_ref[0] = qa.astype(BF16)
    ka = _group_norm64(z[:, 512:640], ones[0:128, 0:128], gk_ref[...])
    ka_ref[0] = ka.astype(BF16)
    va_ref[0] = z[:, 640:768].astype(BF16)

    cqn = _rms(z[:, 768:1024], qln_ref[...]).astype(BF16)
    ckvn = _rms(z[:, 1024:1152], kvln_ref[...]).astype(BF16)
    kr = z[:, 1152:1280]
    qf = _dot(cqn, wuq_ref[...])
    kn = _dot(ckvn, wuk_ref[...])
    vm_ref[0] = _dot(ckvn, wuv_ref[...]).astype(BF16)

    ang = pos_ref[0].astype(F32) * inv_ref[...]
    lane = lax.broadcasted_iota(jnp.int32, ang.shape, 1)
    cos = jnp.cos(ang)
    sin = jnp.sin(ang)
    half = MLA_ROPE // 2
    cos_t = jnp.where(lane < MLA_NOPE, 1.0, cos)
    sin_lo = jnp.where((lane >= MLA_NOPE) & (lane < MLA_NOPE + half), -sin, 0.0)
    sin_hi = jnp.where((lane >= MLA_NOPE + half) & (lane < MLA_QK), sin, 0.0)

    def norm_rope(xh, gain):
        r = lax.rsqrt(jnp.sum(xh * xh, axis=-1, keepdims=True) * (1.0 / MLA_QK) + EPS)
        xn = xh * r * gain
        return (xn * cos_t + pltpu.roll(xn, LANES - half, 1) * sin_lo + pltpu.roll(xn, half, 1) * sin_hi)

    gmq = gmq_ref[...]
    gmk = gmk_ref[...]
    ones_lanes = (lane >= MLA_QK) & (lane < MLA_QK + N_SPLIT)
    for hd in range(MLA_HEADS):
        sl = slice(LANES * hd, LANES * (hd + 1))
        qm_ref[0, :, sl] = (norm_rope(qf[:, sl], gmq) * (MLA_QK ** -0.5 * LOG2E)).astype(BF16)
        km_ref[0, :, sl] = jnp.where(ones_lanes, 1.0, norm_rope(kn[:, sl] + kr, gmk)).astype(BF16)


def _even_proj(x, pos3, gmix, w1, ones, gq, gk, qln, kvln, wuq, wuk, wuv, gmq, gmk, inv, ts):
    B, S, D = x.shape
    grid = (B, S // ts)
    tok = lambda c: pl.BlockSpec((1, ts, c), lambda b, i: (b, i, 0))
    consts = [gmix, w1, ones, gq, gk, qln, kvln, wuq, wuk, wuv, gmq, gmk, inv]
    out_cols = [512, 128, 128, 1024, 1024, 512]
    return pl.pallas_call(
        _even_proj_kernel,
        grid=grid,
        in_specs=[tok(D), tok(1)] + [_const_spec(c.shape) for c in consts],
        out_specs=[tok(c) for c in out_cols],
        out_shape=[jax.ShapeDtypeStruct((B, S, c), BF16) for c in out_cols],
        compiler_params=_params(("parallel", "parallel")),
        name="even_proj",
    )(x, pos3, *consts)


def _swa_kernel(sink_ref, q_ref, k_ref, v_ref, o_ref, *, tq):
    qi = pl.program_id(1)
    win = tq + SWA_WINDOW
    wstart = pl.multiple_of(jnp.maximum(qi * tq - SWA_WINDOW, 0), SWA_WINDOW)
    kw = k_ref[0, pl.ds(wstart, win), :]
    vw = v_ref[0, pl.ds(wstart, win), :]
    row = lax.broadcasted_iota(jnp.int32, (tq, win), 0)
    col = lax.broadcasted_iota(jnp.int32, (tq, win), 1)
    dist = (qi * tq + row) - (wstart + col)
    valid = (dist >= 0) & (dist < SWA_WINDOW)
    distf = dist.astype(F32)
    lane = lax.broadcasted_iota(jnp.int32, (tq, LANES), 1)
    group = SWA_HEADS // SWA_KV_HEADS
    for cb in range(SWA_HEADS // 2):
        qq = q_ref[0, :, LANES * cb:LANES * (cb + 1)].astype(F32)
        halves = []
        for e in range(2):
            hd = 2 * cb + e
            hk = hd // group
            xq = qq if e == hk else pltpu.roll(qq, HEAD_DIM, 1)
            in_kv = (lane >= HEAD_DIM * hk) & (lane < HEAD_DIM * (hk + 1))
            xq = jnp.where(in_kv, xq, 0.0).astype(BF16)
            s = _dot_nt(xq, kw) - (2.0 ** (-8.0 * (hd + 1) / SWA_HEADS) * LOG2E) * distf
            s = jnp.where(valid, s, MASKED)
            sink = sink_ref[hd] * LOG2E
            m = jnp.maximum(jnp.max(s, axis=-1, keepdims=True), sink)
            p = jnp.exp2(s - m)
            l = jnp.sum(p, axis=-1, keepdims=True) + jnp.exp2(sink - m)
            o = _dot(p.astype(BF16), vw) / l
            halves.append(o if e == hk else pltpu.roll(o, HEAD_DIM, 1))
        o_ref[0, :, LANES * cb:LANES * (cb + 1)] = jnp.where(lane < HEAD_DIM, halves[0], halves[1]).astype(BF16)


def _swa_attention(sinks, qa, ka, va, tq):
    B, S, _ = qa.shape
    return pl.pallas_call(
        functools.partial(_swa_kernel, tq=tq),
        grid=(B, S // tq),
        in_specs=[
            pl.BlockSpec(memory_space=pltpu.SMEM),
            pl.BlockSpec((1, tq, 512), lambda b, i: (b, i, 0)),
            pl.BlockSpec((1, S, LANES), lambda b, i: (b, 0, 0)),
            pl.BlockSpec((1, S, LANES), lambda b, i: (b, 0, 0)),
        ],
        out_specs=pl.BlockSpec((1, tq, 512), lambda b, i: (b, i, 0)),
        out_shape=jax.ShapeDtypeStruct((B, S, 512), BF16),
        compiler_params=_params(("parallel", "arbitrary")),
        name="swa_attention",
    )(sinks, qa, ka, va)


def _split3(x):
    hi = x.astype(BF16).astype(F32)
    r = x - hi
    mid = r.astype(BF16).astype(F32)
    lo = (r - mid).astype(BF16).astype(F32)
    return hi, mid, lo


def _lane_tile_sum(p):
    acc = p[:, 0:LANES]
    for t in range(1, p.shape[1] // LANES):
        acc = acc + p[:, LANES * t:LANES * (t + 1)]
    return acc


def _max_sq_norm(k_ref, lanes, gsum, rows):
    n_chunks = k_ref.shape[1] // rows

    def body(c, mx):
        kc = k_ref[0, pl.ds(pl.multiple_of(c * rows, rows), rows), lanes].astype(F32)
        return jnp.maximum(mx, jnp.max(_dot((kc * kc).astype(BF16), gsum), axis=0, keepdims=True))

    return lax.fori_loop(0, n_chunks, body, jnp.zeros((1, LANES), F32))


def _causal_mask(tq):
    row = lax.broadcasted_iota(jnp.int32, (tq, tq), 0)
    col = lax.broadcasted_iota(jnp.int32, (tq, tq), 1)
    return row >= col


def _mla_running_max(q_ref, k_ref, v_ref, e, qi, tq):
    causal = _causal_mask(tq)
    q = q_ref[0, :, LANES * e:LANES * (e + 1)]

    def step(j, carry, masked):
        m, l, acc = carry
        ks = pl.multiple_of(j * tq, tq)
        k = k_ref[0, pl.ds(ks, tq), LANES * e:LANES * (e + 1)]
        v = v_ref[0, pl.ds(ks, tq), :]
        s = _dot_nt(q, k)
        if masked:
            s = jnp.where(causal, s, MASKED)
        m_new = jnp.maximum(m, jnp.max(s, axis=-1, keepdims=True))
        alpha = jnp.exp2(m - m_new)
        p = jnp.exp2(s - m_new)
        l = alpha * l + jnp.sum(p, axis=-1, keepdims=True)
        acc = alpha * acc + _dot(p.astype(BF16), v)
        return m_new, l, acc

    init = (jnp.full((tq, 1), MASKED, F32), jnp.zeros((tq, 1), F32), jnp.zeros((tq, LANES), F32))
    carry = lax.fori_loop(0, qi, lambda j, c: step(j, c, False), init)
    m, l, acc = step(qi, carry, True)
    return acc / l


def _mla_kernel(q_ref, k_ref, v_ref, o_ref, kmax_ref, *, tq):
    qi = pl.program_id(2)
    lane = lax.broadcasted_iota(jnp.int32, (tq, LANES), 1)
    gsum = (lax.broadcasted_iota(jnp.int32, (LANES, LANES), 0) < MLA_QK).astype(BF16)

    @pl.when(qi == 0)
    def _():
        for e in range(2):
            kmax_ref[e:e + 1, :] = _max_sq_norm(k_ref, slice(LANES * e, LANES * (e + 1)), gsum, tq)

    causal = _causal_mask(tq)
    outs = []
    sums = []
    for e in range(2):
        sl = slice(LANES * e, LANES * (e + 1))
        qf = q_ref[0, :, sl].astype(F32)
        bound = jnp.sqrt(_dot((qf * qf).astype(BF16), gsum) * kmax_ref[e:e + 1, :])
        hi, mid, lo = _split3(-bound)
        qa = jnp.where(lane == MLA_QK, hi, jnp.where(lane == MLA_QK + 1, mid,
                                                      jnp.where(lane == MLA_QK + 2, lo, qf))).astype(BF16)

        def step(j, carry, masked):
            lsum, acc = carry
            ks = pl.multiple_of(j * tq, tq)
            s = _dot_nt(qa, k_ref[0, pl.ds(ks, tq), sl])
            if masked:
                s = jnp.where(causal, s, MASKED)
            p = jnp.exp2(s)
            return lsum + _lane_tile_sum(p), acc + _dot(p.astype(BF16), v_ref[0, pl.ds(ks, tq), :])

        zero = jnp.zeros((tq, LANES), F32)
        carry = lax.fori_loop(0, qi, lambda j, c: step(j, c, False), (zero, zero))
        lsum, acc = step(qi, carry, True)
        l = jnp.sum(lsum, axis=-1, keepdims=True)
        outs.append(acc / l)
        sums.append(jnp.min(l))
    ok = jnp.minimum(sums[0], sums[1]) >= UNDERFLOW_GUARD

    @pl.when(ok)
    def _():
        o_ref[0] = jnp.where(lane < MLA_V, outs[0], outs[1]).astype(BF16)

    @pl.when(jnp.logical_not(ok))
    def _():
        slow = [_mla_running_max(q_ref, k_ref, v_ref, e, qi, tq) for e in range(2)]
        o_ref[0] = jnp.where(lane < MLA_V, slow[0], slow[1]).astype(BF16)


def _mla_attention(qm, km, vm, tq):
    B, S, _ = qm.shape
    return pl.pallas_call(
        functools.partial(_mla_kernel, tq=tq),
        grid=(B, MLA_HEADS // 2, S // tq),
        in_specs=[
            pl.BlockSpec((1, tq, 2 * LANES), lambda b, h, i: (b, i, h)),
            pl.BlockSpec((1, S, 2 * LANES), lambda b, h, i: (b, 0, h)),
            pl.BlockSpec((1, S, LANES), lambda b, h, i: (b, 0, h)),
        ],
        out_specs=pl.BlockSpec((1, tq, LANES), lambda b, h, i: (b, i, h)),
        out_shape=jax.ShapeDtypeStruct((B, S, MLA_HEADS * MLA_V), BF16),
        scratch_shapes=[pltpu.VMEM((8, LANES), F32)],
        compiler_params=_params(("parallel", "parallel", "arbitrary")),
        name="mla_attention",
    )(qm, km, vm)


def _odd_proj_kernel(x_ref, gmix_ref, w_ref, wvt_ref, ones_ref, gq_ref, gk_ref, q_ref, k_ref, vt_ref):
    x = x_ref[0]
    h = _rms(x, gmix_ref[...]).astype(BF16)
    z = _dot(h, w_ref[...])
    ones = ones_ref[...]
    width = DIFF_HEADS * 2 * DIFF_DIM
    for c in range(width // 512):
        sl = slice(512 * c, 512 * (c + 1))
        q = _group_norm64(z[:, sl], ones, gq_ref[...]) * (DIFF_DIM ** -0.5 * LOG2E)
        q_ref[0, :, sl] = q.astype(BF16)
        k = _group_norm64(z[:, width + 512 * c:width + 512 * (c + 1)], ones, gk_ref[...])
        k_ref[0, :, sl] = k.astype(BF16)
    vt_ref[0] = _dot_nt(wvt_ref[...], h).astype(BF16)


def _odd_proj(x, gmix, w_qk, w_vt, ones, gq, gk, ts):
    B, S, D = x.shape
    tok = pl.BlockSpec((1, ts, D), lambda b, i: (b, i, 0))
    consts = [gmix, w_qk, w_vt, ones, gq, gk]
    return pl.pallas_call(
        _odd_proj_kernel,
        grid=(B, S // ts),
        in_specs=[tok] + [_const_spec(c.shape) for c in consts],
        out_specs=[tok, tok, pl.BlockSpec((1, D, ts), lambda b, i: (b, 0, i))],
        out_shape=[jax.ShapeDtypeStruct((B, S, D), BF16)] * 2 + [jax.ShapeDtypeStruct((B, D, S), BF16)],
        compiler_params=_params(("parallel", "parallel")),
        name="odd_proj",
    )(x, *consts)


def _diff_q_parts(q, lane):
    zero = jnp.zeros_like(q)
    return jnp.where(lane < DIFF_DIM, q, zero), jnp.where(lane >= DIFF_DIM, q, zero)


def _diff_running_max(slope, q_ref, k_ref, vt_ref, qi, tq):
    row = lax.broadcasted_iota(jnp.int32, (tq, tq), 0)
    col = lax.broadcasted_iota(jnp.int32, (tq, tq), 1)
    causal = row >= col
    bias = slope * (col - row).astype(F32)
    lane = lax.broadcasted_iota(jnp.int32, (tq, LANES), 1)
    q_parts = _diff_q_parts(q_ref[0], lane)

    def step(j, carry, masked):
        ks = pl.multiple_of(j * tq, tq)
        k = k_ref[0, pl.ds(ks, tq), :]
        vt = vt_ref[0, :, pl.ds(ks, tq)]
        offset = slope * ((j - qi) * tq).astype(F32)
        new = []
        for c in range(2):
            m, l, acc = carry[c]
            s = _dot_nt(q_parts[c], k) + bias
            if masked:
                s = jnp.where(causal, s, MASKED)
            m_new = jnp.maximum(m, jnp.max(s, axis=-1, keepdims=True) + offset)
            alpha = jnp.exp2(m - m_new)
            p = jnp.exp2(s - (m_new - offset))
            l = alpha * l + jnp.sum(p, axis=-1, keepdims=True)
            acc = alpha * acc + _dot_nt(p.astype(BF16), vt)
            new.append((m_new, l, acc))
        return tuple(new)

    one = (jnp.full((tq, 1), MASKED, F32), jnp.zeros((tq, 1), F32), jnp.zeros((tq, LANES), F32))
    carry = lax.fori_loop(0, qi, lambda j, c: step(j, c, False), (one, one))
    (m1, l1, a1), (m2, l2, a2) = step(qi, carry, True)
    return a1 / l1, a2 / l2


def _diff_kernel(slope_ref, qx_ref, q_ref, k_ref, kx_ref, vt_ref, lam_ref, subln_ref, o_ref, kmax_ref, *, tq,
                 lambda_init):
    hd = pl.program_id(1)
    qi = pl.program_id(2)
    lane = lax.broadcasted_iota(jnp.int32, (tq, LANES), 1)
    gi = lax.broadcasted_iota(jnp.int32, (LANES, LANES), 0)
    gsums = ((gi < DIFF_DIM).astype(BF16), (gi >= DIFF_DIM).astype(BF16))

    @pl.when(qi == 0)
    def _():
        for c in range(2):
            kmax_ref[c:c + 1, :] = _max_sq_norm(k_ref, slice(None), gsums[c], tq)

    key_row = lax.broadcasted_iota(jnp.int32, (tq, tq), 0)
    query_col = lax.broadcasted_iota(jnp.int32, (tq, tq), 1)
    causal_t = key_row <= query_col
    q = q_ref[0]
    qf = q.astype(F32)
    q2 = (qf * qf).astype(BF16)
    q_parts = _diff_q_parts(q, lane)
    qx_const = qx_ref[0]
    slope = qx_const[:, 3:4] + qx_const[:, 4:5] + qx_const[:, 5:6]
    rowpos = (qi * tq + lax.broadcasted_iota(jnp.int32, (tq, LANES), 0)).astype(F32)
    q_aug = []
    for c in range(2):
        bound = jnp.sqrt(_dot(q2, gsums[c]) * kmax_ref[c:c + 1, :])
        hi, mid, lo = _split3(-slope * rowpos - bound)
        qx = jnp.where(lane == 2 * N_SPLIT, hi, jnp.where(lane == 2 * N_SPLIT + 1, mid,
                                                           jnp.where(lane == 2 * N_SPLIT + 2, lo, qx_const)))
        q_aug.append(jnp.concatenate([q_parts[c], qx.astype(BF16)], axis=1))

    def step(j, carry, masked):
        ks = pl.multiple_of(j * tq, tq)
        k = jnp.concatenate([k_ref[0, pl.ds(ks, tq), :], kx_ref[pl.ds(ks, tq), :]], axis=1)
        vt = vt_ref[0, :, pl.ds(ks, tq)]
        new = []
        for c in range(2):
            lsum, acc = carry[c]
            st = _dot_nt(k, q_aug[c])
            if masked:
                st = jnp.where(causal_t, st, MASKED)
            pt = jnp.exp2(st)
            new.append((lsum + jnp.sum(pt.reshape(tq // SUBLANES, SUBLANES, tq), axis=0),
                        acc + _dot(vt, pt.astype(BF16))))
        return tuple(new)

    zero = (jnp.zeros((SUBLANES, tq), F32), jnp.zeros((LANES, tq), F32))
    carry = lax.fori_loop(0, qi, lambda j, c: step(j, c, False), (zero, zero))
    (ls1, a1), (ls2, a2) = step(qi, carry, True)
    l1 = jnp.sum(ls1, axis=0, keepdims=True)
    l2 = jnp.sum(ls2, axis=0, keepdims=True)
    ok = jnp.minimum(jnp.min(l1), jnp.min(l2)) >= UNDERFLOW_GUARD

    lf = lam_ref[...]
    lam = (jnp.exp(jnp.sum(lf[0:1] * lf[1:2], axis=-1, keepdims=True))
           - jnp.exp(jnp.sum(lf[2:3] * lf[3:4], axis=-1, keepdims=True)) + lambda_init)

    def finish(o1, o2):
        o = _rms(o1 - lam * o2, subln_ref[...]) * (1.0 - lambda_init)
        o_ref[0] = o.astype(BF16)

    @pl.when(ok)
    def _():
        finish((a1 / l1).T, (a2 / l2).T)

    @pl.when(jnp.logical_not(ok))
    def _():
        finish(*_diff_running_max(slope_ref[hd] * LOG2E, q_ref, k_ref, vt_ref, qi, tq))


def _alibi_tables(slopes, S):
    s = jnp.asarray(slopes, F32).reshape(-1, 1) * LOG2E
    terms = jnp.concatenate(_split3(s), axis=1)
    qx = jnp.concatenate([terms * float(LANES), terms, jnp.zeros((terms.shape[0], LANES - 2 * N_SPLIT), F32)], axis=1)
    j = np.arange(S)
    kx = np.zeros((S, LANES), np.float32)
    kx[:, 0:N_SPLIT] = (j // LANES)[:, None]
    kx[:, N_SPLIT:2 * N_SPLIT] = (j % LANES)[:, None]
    kx[:, 2 * N_SPLIT:3 * N_SPLIT] = 1.0
    return qx.reshape(-1, 1, LANES), jnp.asarray(kx, dtype=BF16)


def _diff_attention(slopes, q, k, vt, lambdas, subln, tq, lambda_init):
    B, S, D = q.shape
    qx, kx = _alibi_tables(slopes, S)
    return pl.pallas_call(
        functools.partial(_diff_kernel, tq=tq, lambda_init=lambda_init),
        grid=(B, DIFF_HEADS, S // tq),
        in_specs=[
            pl.BlockSpec(memory_space=pltpu.SMEM),
            pl.BlockSpec((1, 1, LANES), lambda b, h, i: (h, 0, 0)),
            pl.BlockSpec((1, tq, LANES), lambda b, h, i: (b, i, h)),
            pl.BlockSpec((1, S, LANES), lambda b, h, i: (b, 0, h)),
            _const_spec(kx.shape),
            pl.BlockSpec((1, LANES, S), lambda b, h, i: (b, h, 0)),
            pl.BlockSpec(lambdas.shape, lambda b, h, i: (0, 0)),
            pl.BlockSpec(subln.shape, lambda b, h, i: (0, 0)),
        ],
        out_specs=pl.BlockSpec((1, tq, LANES), lambda b, h, i: (b, i, h)),
        out_shape=jax.ShapeDtypeStruct((B, S, D), BF16),
        scratch_shapes=[pltpu.VMEM((8, LANES), F32)],
        compiler_params=_params(("parallel", "parallel", "arbitrary")),
        name="diff_attention",
    )(jnp.asarray(slopes, F32), qx, q, k, kx, vt, lambdas, subln)


def _mem_kv_kernel(mem_ref, norm_ref, w_ref, gain_ref, k_ref, v_ref):
    h = _rms(mem_ref[0], norm_ref[0]).astype(BF16)
    z = _dot(h, w_ref[0])
    width = MEM_HEADS * MEM_HEAD_DIM
    gain = gain_ref[0]
    for hd in range(MEM_HEADS):
        sl = slice(MEM_HEAD_DIM * hd, MEM_HEAD_DIM * (hd + 1))
        k_ref[0, 0, :, sl] = _rms(z[:, sl], gain).astype(BF16)
    v_ref[0, 0] = z[:, width:2 * width].astype(BF16)


def _mem_kv(mem, norms, w_kv, gains):
    B, M, D = mem.shape
    L = norms.shape[0]
    width = MEM_HEADS * MEM_HEAD_DIM
    out = pl.BlockSpec((1, 1, M, width), lambda l, b: (l, b, 0, 0))
    return pl.pallas_call(
        _mem_kv_kernel,
        grid=(L, B),
        in_specs=[
            pl.BlockSpec((1, M, D), lambda l, b: (b, 0, 0)),
            pl.BlockSpec((1, 1, D), lambda l, b: (l, 0, 0)),
            pl.BlockSpec((1, D, 2 * width), lambda l, b: (l, 0, 0)),
            pl.BlockSpec((1, 1, MEM_HEAD_DIM), lambda l, b: (l, 0, 0)),
        ],
        out_specs=[out, out],
        out_shape=[jax.ShapeDtypeStruct((L, B, M, width), BF16)] * 2,
        compiler_params=_params(("parallel", "parallel")),
        name="mem_kv",
    )(mem, norms, w_kv, gains)


def _post_kernel(*refs, n_attn):
    x_ref = refs[0]
    attn_refs = refs[1:1 + n_attn]
    wo_refs = refs[1 + n_attn:1 + 2 * n_attn]
    (mqn_ref, wq_ref, qg_ref, mk_ref, mv_ref, wmo_ref, fn_ref, wg_ref, wu_ref, wd_ref, o_ref) = refs[1 + 2 * n_attn:]
    x = x_ref[0]
    for a_ref, w_ref in zip(attn_refs, wo_refs):
        x = x + _dot(a_ref[0], w_ref[...])

    q = _dot(_rms(x, mqn_ref[...]).astype(BF16), wq_ref[...])
    heads = []
    for hd in range(MEM_HEADS):
        sl = slice(MEM_HEAD_DIM * hd, MEM_HEAD_DIM * (hd + 1))
        qh = (_rms(q[:, sl], qg_ref[...]) * (MEM_HEAD_DIM ** -0.5 * LOG2E)).astype(BF16)
        s = _dot_nt(qh, mk_ref[0, :, sl])
        p = jnp.exp2(s - jnp.max(s, axis=-1, keepdims=True))
        l = jnp.sum(p, axis=-1, keepdims=True)
        heads.append((_dot(p.astype(BF16), mv_ref[0, :, sl]) / l).astype(BF16))
    x = x + _dot(jnp.concatenate(heads, axis=-1), wmo_ref[...])

    hf = _rms(x, fn_ref[...]).astype(BF16)
    g = _dot(hf, wg_ref[...])
    u = _dot(hf, wu_ref[...])
    act = (g * jax.nn.sigmoid(g) * u).astype(BF16)
    o_ref[0] = x + _dot(act, wd_ref[...])


def _post_block(x, attns, wos, mqn, wq, qg, mk, mv, wmo, fn, wg, wu, wd, ts):
    B, S, D = x.shape
    n = len(attns)
    tok = lambda c: pl.BlockSpec((1, ts, c), lambda b, i: (b, i, 0))
    memspec = pl.BlockSpec((1,) + mk.shape[1:], lambda b, i: (b, 0, 0))
    consts_a = list(wos) + [mqn, wq, qg]
    consts_b = [wmo, fn, wg, wu, wd]
    return pl.pallas_call(
        functools.partial(_post_kernel, n_attn=n),
        grid=(B, S // ts),
        in_specs=([tok(D)] + [tok(a.shape[-1]) for a in attns] + [_const_spec(c.shape) for c in consts_a]
                  + [memspec, memspec] + [_const_spec(c.shape) for c in consts_b]),
        out_specs=tok(D),
        out_shape=jax.ShapeDtypeStruct((B, S, D), F32),
        compiler_params=_params(("parallel", "parallel")),
        name="post_block",
    )(x, *attns, *consts_a, mk, mv, *consts_b)


def _row(v):
    return v.reshape(1, -1).astype(F32)


def _pad_cols(w, lo, total):
    return jnp.pad(w, ((0, 0), (lo, total - lo - w.shape[1])))


def _even_weights(w_in, w_uq, w_ukv, q_gain, k_gain):
    kr = _pad_cols(w_in[:, 1152:1184], MLA_NOPE, LANES)
    w1 = jnp.concatenate([w_in[:, :1152], kr], axis=1).astype(BF16)
    wuq = jnp.concatenate(
        [_pad_cols(w_uq[:, MLA_QK * h:MLA_QK * (h + 1)], 0, LANES) for h in range(MLA_HEADS)], axis=1).astype(BF16)
    kv_w = MLA_NOPE + MLA_V
    wuk = jnp.concatenate(
        [_pad_cols(w_ukv[:, kv_w * h:kv_w * h + MLA_NOPE], 0, LANES) for h in range(MLA_HEADS)], axis=1).astype(BF16)
    wuv = jnp.concatenate(
        [w_ukv[:, kv_w * h + MLA_NOPE:kv_w * (h + 1)] for h in range(MLA_HEADS)], axis=1).astype(BF16)
    gmq = _row(jnp.pad(q_gain, (0, LANES - MLA_QK)))
    gmk = _row(jnp.pad(k_gain, (0, LANES - MLA_QK)))
    return w1, wuq, wuk, wuv, gmq, gmk


def _rope_lane_freqs():
    half = MLA_ROPE // 2
    inv = np.zeros((1, LANES), np.float32)
    freqs = (np.float32(ROPE_THETA) ** (-np.arange(half, dtype=np.float32) / np.float32(half))).astype(np.float32)
    inv[0, MLA_NOPE:MLA_NOPE + half] = freqs
    inv[0, MLA_NOPE + half:MLA_QK] = freqs
    return jnp.asarray(inv)


def _ones_blocks(n):
    g = np.arange(n) // HEAD_DIM
    return jnp.asarray((g[:, None] == g[None, :]).astype(np.float32), dtype=BF16)


def kernel(x, mem, positions, mix_norm, ev_w_in, ev_swa_q_gain, ev_swa_k_gain, ev_sinks, ev_q_latent_norm,
           ev_kv_latent_norm, ev_w_uq, ev_w_ukv, ev_mla_q_gain, ev_mla_k_gain, ev_w_out, od_w_qkv, od_q_gain,
           od_k_gain, od_lambda, od_subln, od_w_out, mem_q_norm, mem_kv_norm, mem_w_q, mem_w_kv, mem_q_gain,
           mem_k_gain, mem_w_out, ffn_norm, ffn_w_gate, ffn_w_up, ffn_w_down):
    B, S, D = x.shape
    depth = mix_norm.shape[0]
    ts_proj = min(512, S)
    ts_post = min(256, S)
    tq_swa = min(256, S - SWA_WINDOW)
    tq_flash = min(1024, S)

    ones = _ones_blocks(512)
    pos3 = positions.reshape(B, S, 1)
    inv = _rope_lane_freqs()
    mem_k, mem_v = _mem_kv(mem, mem_kv_norm.reshape(depth, 1, D), mem_w_kv.astype(BF16),
                           mem_k_gain.reshape(depth, 1, MEM_HEAD_DIM))
    diff_slopes = [2.0 ** (-8.0 * (i + 1) / DIFF_HEADS) for i in range(DIFF_HEADS)]

    for l in range(depth):
        if l % 2 == 0:
            e = l // 2
            w1, wuq, wuk, wuv, gmq, gmk = _even_weights(ev_w_in[e], ev_w_uq[e], ev_w_ukv[e], ev_mla_q_gain[e],
                                                        ev_mla_k_gain[e])
            qa, ka, va, qm, km, vm = _even_proj(
                x, pos3, _row(mix_norm[l]), w1, ones, _row(jnp.tile(ev_swa_q_gain[e], SWA_HEADS)),
                _row(jnp.tile(ev_swa_k_gain[e], SWA_KV_HEADS)), _row(ev_q_latent_norm[e]),
                _row(ev_kv_latent_norm[e]), wuq, wuk, wuv, gmq, gmk, inv, ts_proj)
            out_a = _swa_attention(ev_sinks[e].astype(F32), qa, ka, va, tq_swa)
            out_b = _mla_attention(qm, km, vm, tq_flash)
            wo = ev_w_out[e].astype(BF16)
            attns = [out_a, out_b]
            wos = [wo[:512], wo[512:]]
        else:
            o = l // 2
            lambda_init = 0.8 - 0.6 * math.exp(-0.3 * l)
            n_grp = DIFF_HEADS * 2
            w_qkv = od_w_qkv[o].astype(BF16)
            q, k, vt = _odd_proj(x, _row(mix_norm[l]), w_qkv[:, :2 * D], w_qkv[:, 2 * D:].T, ones,
                                 _row(jnp.tile(od_q_gain[o], n_grp // 2)), _row(jnp.tile(od_k_gain[o], n_grp // 2)),
                                 ts_proj)
            out_d = _diff_attention(diff_slopes, q, k, vt, od_lambda[o].astype(F32), _row(od_subln[o]), tq_flash,
                                    lambda_init)
            attns = [out_d]
            wos = [od_w_out[o].astype(BF16)]
        x = _post_block(x, attns, wos, _row(mem_q_norm[l]), mem_w_q[l].astype(BF16), _row(mem_q_gain[l]),
                        mem_k[l], mem_v[l], mem_w_out[l].astype(BF16), _row(ffn_norm[l]),
                        ffn_w_gate[l].astype(BF16), ffn_w_up[l].astype(BF16), ffn_w_down[l].astype(BF16), ts_post)
    return x
```

```python
import functools
import math

import numpy as np
import jax
import jax.numpy as jnp
from jax import lax
from jax.experimental import pallas as pl
from jax.experimental.pallas import tpu as pltpu

F32 = jnp.float32
BF16 = jnp.bfloat16

EPS = 1e-6
MASKED = -1e30
LOG2E = 1.4426950408889634
ROPE_THETA = 10000.0

LANES = 128
SUBLANES = 8
HEAD_DIM = 64
SWA_HEADS = 8
SWA_KV_HEADS = 2
SWA_WINDOW = 128
MLA_HEADS = 8
MLA_NOPE = 64
MLA_ROPE = 32
MLA_QK = MLA_NOPE + MLA_ROPE
MLA_V = 64
DIFF_HEADS = 8
DIFF_DIM = 64
MEM_HEADS = 4
MEM_HEAD_DIM = 128

VMEM_LIMIT = 56 * 1024 * 1024

UNDERFLOW_GUARD = 2.0 ** -80
N_SPLIT = 3


def _params(sem):
    return pltpu.CompilerParams(dimension_semantics=sem, vmem_limit_bytes=VMEM_LIMIT)


def _const_spec(shape):
    nd = len(shape)
    return pl.BlockSpec(shape, lambda *_: (0,) * nd, pipeline_mode=pl.Buffered(1))


def _rms(x, g):
    ms = jnp.mean(x * x, axis=-1, keepdims=True)
    return x * lax.rsqrt(ms + EPS) * g


def _dot(a, b):
    return jnp.dot(a, b, preferred_element_type=F32)


def _dot_nt(a, b):
    return lax.dot_general(a, b, (((1,), (1,)), ((), ())), preferred_element_type=F32)


def _group_sumsq(x, ones_blk):
    x2 = x * x
    hi = x2.astype(BF16)
    lo = (x2 - hi.astype(F32)).astype(BF16)
    return _dot(hi, ones_blk) + _dot(lo, ones_blk)


def _group_norm64(x, ones_blk, gain):
    return x * lax.rsqrt(_group_sumsq(x, ones_blk) * (1.0 / HEAD_DIM) + EPS) * gain


def _even_proj_kernel(x_ref, pos_ref, gmix_ref, w1_ref, ones_ref, gq_ref, gk_ref, qln_ref, kvln_ref,
                      wuq_ref, wuk_ref, wuv_ref, gmq_ref, gmk_ref, inv_ref,
                      qa_ref, ka_ref, va_ref, qm_ref, km_ref, vm_ref):
    x = x_ref[0]
    h = _rms(x, gmix_ref[...]).astype(BF16)
    z = _dot(h, w1_ref[...])
    ones = ones_ref[...]
    qa = _group_norm64(z[:, 0:512], ones, gq_ref[...]) * (HEAD_DIM ** -0.5 * LOG2E)
    qa_ref[0] = qa.astype(BF16)
    ka = _group_norm64(z[:, 512:640], ones[0:128, 0:128], gk_ref[...])
    ka_ref[0] = ka.astype(BF16)
    va_ref[0] = z[:, 640:768].astype(BF16)

    cqn = _rms(z[:, 768:1024], qln_ref[...]).astype(BF16)
    ckvn = _rms(z[:, 1024:1152], kvln_ref[...]).astype(BF16)
    kr = z[:, 1152:1280]
    qf = _dot(cqn, wuq_ref[...])
    kn = _dot(ckvn, wuk_ref[...])
    vm_ref[0] = _dot_nt(wuv_ref[...], ckvn).astype(BF16)

    ang = pos_ref[0].astype(F32) * inv_ref[...]
    lane = lax.broadcasted_iota(jnp.int32, ang.shape, 1)
    cos = jnp.cos(ang)
    sin = jnp.sin(ang)
    half = MLA_ROPE // 2
    cos_t = jnp.where(lane < MLA_NOPE, 1.0, cos)
    sin_lo = jnp.where((lane >= MLA_NOPE) & (lane < MLA_NOPE + half), -sin, 0.0)
    sin_hi = jnp.where((lane >= MLA_NOPE + half) & (lane < MLA_QK), sin, 0.0)

    def norm_rope(xh, gain):
        r = lax.rsqrt(jnp.sum(xh * xh, axis=-1, keepdims=True) * (1.0 / MLA_QK) + EPS)
        xn = xh * r * gain
        return (xn * cos_t + pltpu.roll(xn, LANES - half, 1) * sin_lo + pltpu.roll(xn, half, 1) * sin_hi)

    gmq = gmq_ref[...]
    gmk = gmk_ref[...]
    ones_lanes = (lane >= MLA_QK) & (lane < MLA_QK + N_SPLIT)
    for hd in range(MLA_HEADS):
        sl = slice(LANES * hd, LANES * (hd + 1))
        qm_ref[0, :, sl] = (norm_rope(qf[:, sl], gmq) * (MLA_QK ** -0.5 * LOG2E)).astype(BF16)
        km_ref[0, :, sl] = jnp.where(ones_lanes, 1.0, norm_rope(kn[:, sl] + kr, gmk)).astype(BF16)


def _even_proj(x, pos3, gmix, w1, ones, gq, gk, qln, kvln, wuq, wuk, wuv, gmq, gmk, inv, ts):
    B, S, D = x.shape
    grid = (B, S // ts)
    tok = lambda c: pl.BlockSpec((1, ts, c), lambda b, i: (b, i, 0))
    consts = [gmix, w1, ones, gq, gk, qln, kvln, wuq, wuk, wuv, gmq, gmk, inv]
    out_cols = [512, 128, 128, 1024, 1024]
    v_rows = MLA_HEADS * MLA_V
    return pl.pallas_call(
        _even_proj_kernel,
        grid=grid,
        in_specs=[tok(D), tok(1)] + [_const_spec(c.shape) for c in consts],
        out_specs=[tok(c) for c in out_cols] + [pl.BlockSpec((1, v_rows, ts), lambda b, i: (b, 0, i))],
        out_shape=([jax.ShapeDtypeStruct((B, S, c), BF16) for c in out_cols]
                   + [jax.ShapeDtypeStruct((B, v_rows, S), BF16)]),
        compiler_params=_params(("parallel", "parallel")),
        name="even_proj",
    )(x, pos3, *consts)


def _swa_kernel(sink_ref, q_ref, k_ref, v_ref, o_ref, *, tq):
    qi = pl.program_id(1)
    win = tq + SWA_WINDOW
    wstart = pl.multiple_of(jnp.maximum(qi * tq - SWA_WINDOW, 0), SWA_WINDOW)
    kw = k_ref[0, pl.ds(wstart, win), :]
    vw = v_ref[0, pl.ds(wstart, win), :]
    row = lax.broadcasted_iota(jnp.int32, (tq, win), 0)
    col = lax.broadcasted_iota(jnp.int32, (tq, win), 1)
    dist = (qi * tq + row) - (wstart + col)
    valid = (dist >= 0) & (dist < SWA_WINDOW)
    distf = dist.astype(F32)
    lane = lax.broadcasted_iota(jnp.int32, (tq, LANES), 1)
    group = SWA_HEADS // SWA_KV_HEADS
    for cb in range(SWA_HEADS // 2):
        qq = q_ref[0, :, LANES * cb:LANES * (cb + 1)].astype(F32)
        halves = []
        for e in range(2):
            hd = 2 * cb + e
            hk = hd // group
            xq = qq if e == hk else pltpu.roll(qq, HEAD_DIM, 1)
            in_kv = (lane >= HEAD_DIM * hk) & (lane < HEAD_DIM * (hk + 1))
            xq = jnp.where(in_kv, xq, 0.0).astype(BF16)
            s = _dot_nt(xq, kw) - (2.0 ** (-8.0 * (hd + 1) / SWA_HEADS) * LOG2E) * distf
            s = jnp.where(valid, s, MASKED)
            sink = sink_ref[hd] * LOG2E
            m = jnp.maximum(jnp.max(s, axis=-1, keepdims=True), sink)
            p = jnp.exp2(s - m)
            l = jnp.sum(p, axis=-1, keepdims=True) + jnp.exp2(sink - m)
            o = _dot(p.astype(BF16), vw) / l
            halves.append(o if e == hk else pltpu.roll(o, HEAD_DIM, 1))
        o_ref[0, :, LANES * cb:LANES * (cb + 1)] = jnp.where(lane < HEAD_DIM, halves[0], halves[1]).astype(BF16)


def _swa_attention(sinks, qa, ka, va, tq):
    B, S, _ = qa.shape
    return pl.pallas_call(
        functools.partial(_swa_kernel, tq=tq),
        grid=(B, S // tq),
        in_specs=[
            pl.BlockSpec(memory_space=pltpu.SMEM),
            pl.BlockSpec((1, tq, 512), lambda b, i: (b, i, 0)),
            pl.BlockSpec((1, S, LANES), lambda b, i: (b, 0, 0)),
            pl.BlockSpec((1, S, LANES), lambda b, i: (b, 0, 0)),
        ],
        out_specs=pl.BlockSpec((1, tq, 512), lambda b, i: (b, i, 0)),
        out_shape=jax.ShapeDtypeStruct((B, S, 512), BF16),
        compiler_params=_params(("parallel", "arbitrary")),
        name="swa_attention",
    )(sinks, qa, ka, va)


def _split3(x):
    hi = x.astype(BF16).astype(F32)
    r = x - hi
    mid = r.astype(BF16).astype(F32)
    lo = (r - mid).astype(BF16).astype(F32)
    return hi, mid, lo


def _lane_tile_sum(p):
    acc = p[:, 0:LANES]
    for t in range(1, p.shape[1] // LANES):
        acc = acc + p[:, LANES * t:LANES * (t + 1)]
    return acc


def _max_sq_norm(k_ref, lanes, gsum, rows):
    n_chunks = k_ref.shape[1] // rows

    def body(c, mx):
        kc = k_ref[0, pl.ds(pl.multiple_of(c * rows, rows), rows), lanes].astype(F32)
        return jnp.maximum(mx, jnp.max(_dot((kc * kc).astype(BF16), gsum), axis=0, keepdims=True))

    return lax.fori_loop(0, n_chunks, body, jnp.zeros((1, LANES), F32))


def _loop_blocks_by_two(n, step, init):
    carry = lax.fori_loop(0, n // 2, lambda t, c: step(2 * t + 1, step(2 * t, c)), init)
    return lax.cond(n % 2 == 1, lambda c: step(n - 1, c), lambda c: c, carry)


def _causal_mask(tq):
    row = lax.broadcasted_iota(jnp.int32, (tq, tq), 0)
    col = lax.broadcasted_iota(jnp.int32, (tq, tq), 1)
    return row >= col


def _mla_running_max(q_ref, k_ref, vt_ref, e, qi, tq):
    causal = _causal_mask(tq)
    q = q_ref[0, :, LANES * e:LANES * (e + 1)]

    def step(j, carry, masked):
        m, l, acc = carry
        ks = pl.multiple_of(j * tq, tq)
        k = k_ref[0, pl.ds(ks, tq), LANES * e:LANES * (e + 1)]
        vt = vt_ref[0, :, pl.ds(ks, tq)]
        s = _dot_nt(q, k)
        if masked:
            s = jnp.where(causal, s, MASKED)
        m_new = jnp.maximum(m, jnp.max(s, axis=-1, keepdims=True))
        alpha = jnp.exp2(m - m_new)
        p = jnp.exp2(s - m_new)
        l = alpha * l + jnp.sum(p, axis=-1, keepdims=True)
        acc = alpha * acc + _dot_nt(p.astype(BF16), vt)
        return m_new, l, acc

    init = (jnp.full((tq, 1), MASKED, F32), jnp.zeros((tq, 1), F32), jnp.zeros((tq, LANES), F32))
    carry = lax.fori_loop(0, qi, lambda j, c: step(j, c, False), init)
    m, l, acc = step(qi, carry, True)
    return acc / l


def _mla_kernel(q_ref, k_ref, vt_ref, o_ref, kmax_ref, *, tq):
    qi = pl.program_id(2)
    lane = lax.broadcasted_iota(jnp.int32, (tq, LANES), 1)
    gsum = (lax.broadcasted_iota(jnp.int32, (LANES, LANES), 0) < MLA_QK).astype(BF16)

    @pl.when(qi == 0)
    def _():
        for e in range(2):
            kmax_ref[e:e + 1, :] = _max_sq_norm(k_ref, slice(LANES * e, LANES * (e + 1)), gsum, tq)

    key_row = lax.broadcasted_iota(jnp.int32, (tq, tq), 0)
    query_col = lax.broadcasted_iota(jnp.int32, (tq, tq), 1)
    causal_t = key_row <= query_col
    q_aug = []
    for e in range(2):
        qf = q_ref[0, :, LANES * e:LANES * (e + 1)].astype(F32)
        bound = jnp.sqrt(_dot((qf * qf).astype(BF16), gsum) * kmax_ref[e:e + 1, :])
        hi, mid, lo = _split3(-bound)
        q_aug.append(jnp.where(lane == MLA_QK, hi, jnp.where(lane == MLA_QK + 1, mid,
                                                             jnp.where(lane == MLA_QK + 2, lo, qf))).astype(BF16))

    def step(j, carry, masked):
        ks = pl.multiple_of(j * tq, tq)
        new = []
        for e in range(2):
            lsum, acc = carry[e]
            st = _dot_nt(k_ref[0, pl.ds(ks, tq), LANES * e:LANES * (e + 1)], q_aug[e])
            if masked:
                st = jnp.where(causal_t, st, MASKED)
            pt = jnp.exp2(st)
            new.append((lsum + jnp.sum(pt.reshape(tq // SUBLANES, SUBLANES, tq), axis=0),
                        acc + _dot(vt_ref[0, MLA_V * e:MLA_V * (e + 1), pl.ds(ks, tq)], pt.astype(BF16))))
        return tuple(new)

    zero = (jnp.zeros((SUBLANES, tq), F32), jnp.zeros((MLA_V, tq), F32))
    carry = _loop_blocks_by_two(qi, lambda j, c: step(j, c, False), (zero, zero))
    carry = step(qi, carry, True)
    sums = [jnp.sum(lsum, axis=0, keepdims=True) for lsum, _ in carry]
    outs = [acc / l for (_, acc), l in zip(carry, sums)]
    ok = jnp.minimum(jnp.min(sums[0]), jnp.min(sums[1])) >= UNDERFLOW_GUARD

    @pl.when(ok)
    def _():
        o_ref[0] = jnp.concatenate(outs, axis=0).T.astype(BF16)

    @pl.when(jnp.logical_not(ok))
    def _():
        slow = [_mla_running_max(q_ref, k_ref, vt_ref, e, qi, tq) for e in range(2)]
        o_ref[0] = jnp.where(lane < MLA_V, slow[0], slow[1]).astype(BF16)


def _mla_attention(qm, km, vm, tq):
    B, S, _ = qm.shape
    return pl.pallas_call(
        functools.partial(_mla_kernel, tq=tq),
        grid=(B, MLA_HEADS // 2, S // tq),
        in_specs=[
            pl.BlockSpec((1, tq, 2 * LANES), lambda b, h, i: (b, i, h)),
            pl.BlockSpec((1, S, 2 * LANES), lambda b, h, i: (b, 0, h)),
            pl.BlockSpec((1, 2 * MLA_V, S), lambda b, h, i: (b, h, 0)),
        ],
        out_specs=pl.BlockSpec((1, tq, LANES), lambda b, h, i: (b, i, h)),
        out_shape=jax.ShapeDtypeStruct((B, S, MLA_HEADS * MLA_V), BF16),
        scratch_shapes=[pltpu.VMEM((8, LANES), F32)],
        compiler_params=_params(("parallel", "parallel", "arbitrary")),
        name="mla_attention",
    )(qm, km, vm)


def _odd_proj_kernel(x_ref, gmix_ref, w_ref, wvt_ref, ones_ref, gq_ref, gk_ref, q_ref, k_ref, vt_ref):
    x = x_ref[0]
    h = _rms(x, gmix_ref[...]).astype(BF16)
    z = _dot(h, w_ref[...])
    ones = ones_ref[...]
    width = DIFF_HEADS * 2 * DIFF_DIM
    for c in range(width // 512):
        sl = slice(512 * c, 512 * (c + 1))
        q = _group_norm64(z[:, sl], ones, gq_ref[...]) * (DIFF_DIM ** -0.5 * LOG2E)
        q_ref[0, :, sl] = q.astype(BF16)
        k = _group_norm64(z[:, width + 512 * c:width + 512 * (c + 1)], ones, gk_ref[...])
        k_ref[0, :, sl] = k.astype(BF16)
    vt_ref[0] = _dot_nt(wvt_ref[...], h).astype(BF16)


def _odd_proj(x, gmix, w_qk, w_vt, ones, gq, gk, ts):
    B, S, D = x.shape
    tok = pl.BlockSpec((1, ts, D), lambda b, i: (b, i, 0))
    consts = [gmix, w_qk, w_vt, ones, gq, gk]
    return pl.pallas_call(
        _odd_proj_kernel,
        grid=(B, S // ts),
        in_specs=[tok] + [_const_spec(c.shape) for c in consts],
        out_specs=[tok, tok, pl.BlockSpec((1, D, ts), lambda b, i: (b, 0, i))],
        out_shape=[jax.ShapeDtypeStruct((B, S, D), BF16)] * 2 + [jax.ShapeDtypeStruct((B, D, S), BF16)],
        compiler_params=_params(("parallel", "parallel")),
        name="odd_proj",
    )(x, *consts)


def _diff_q_parts(q, lane):
    zero = jnp.zeros_like(q)
    return jnp.where(lane < DIFF_DIM, q, zero), jnp.where(lane >= DIFF_DIM, q, zero)


def _diff_running_max(slope, q_ref, k_ref, vt_ref, qi, tq):
    row = lax.broadcasted_iota(jnp.int32, (tq, tq), 0)
    col = lax.broadcasted_iota(jnp.int32, (tq, tq), 1)
    causal = row >= col
    bias = slope * (col - row).astype(F32)
    lane = lax.broadcasted_iota(jnp.int32, (tq, LANES), 1)
    q_parts = _diff_q_parts(q_ref[0], lane)

    def step(j, carry, masked):
        ks = pl.multiple_of(j * tq, tq)
        k = k_ref[0, pl.ds(ks, tq), :]
        vt = vt_ref[0, :, pl.ds(ks, tq)]
        offset = slope * ((j - qi) * tq).astype(F32)
        new = []
        for c in range(2):
            m, l, acc = carry[c]
            s = _dot_nt(q_parts[c], k) + bias
            if masked:
                s = jnp.where(causal, s, MASKED)
            m_new = jnp.maximum(m, jnp.max(s, axis=-1, keepdims=True) + offset)
            alpha = jnp.exp2(m - m_new)
            p = jnp.exp2(s - (m_new - offset))
            l = alpha * l + jnp.sum(p, axis=-1, keepdims=True)
            acc = alpha * acc + _dot_nt(p.astype(BF16), vt)
            new.append((m_new, l, acc))
        return tuple(new)

    one = (jnp.full((tq, 1), MASKED, F32), jnp.zeros((tq, 1), F32), jnp.zeros((tq, LANES), F32))
    carry = lax.fori_loop(0, qi, lambda j, c: step(j, c, False), (one, one))
    (m1, l1, a1), (m2, l2, a2) = step(qi, carry, True)
    return a1 / l1, a2 / l2


def _diff_kernel(slope_ref, qx_ref, q_ref, k_ref, kx_ref, vt_ref, lam_ref, subln_ref, o_ref, kmax_ref, *, tq,
                 lambda_init):
    hd = pl.program_id(1)
    qi = pl.program_id(2)
    lane = lax.broadcasted_iota(jnp.int32, (tq, LANES), 1)
    gi = lax.broadcasted_iota(jnp.int32, (LANES, LANES), 0)
    gsums = ((gi < DIFF_DIM).astype(BF16), (gi >= DIFF_DIM).astype(BF16))

    @pl.when(qi == 0)
    def _():
        for c in range(2):
            kmax_ref[c:c + 1, :] = _max_sq_norm(k_ref, slice(None), gsums[c], tq)

    key_row = lax.broadcasted_iota(jnp.int32, (tq, tq), 0)
    query_col = lax.broadcasted_iota(jnp.int32, (tq, tq), 1)
    causal_t = key_row <= query_col
    q = q_ref[0]
    qf = q.astype(F32)
    q2 = (qf * qf).astype(BF16)
    q_parts = _diff_q_parts(q, lane)
    qx_const = qx_ref[0]
    slope = qx_const[:, 3:4] + qx_const[:, 4:5] + qx_const[:, 5:6]
    rowpos = (qi * tq + lax.broadcasted_iota(jnp.int32, (tq, LANES), 0)).astype(F32)
    q_aug = []
    for c in range(2):
        bound = jnp.sqrt(_dot(q2, gsums[c]) * kmax_ref[c:c + 1, :])
        hi, mid, lo = _split3(-slope * rowpos - bound)
        qx = jnp.where(lane == 2 * N_SPLIT, hi, jnp.where(lane == 2 * N_SPLIT + 1, mid,
                                                           jnp.where(lane == 2 * N_SPLIT + 2, lo, qx_const)))
        q_aug.append(jnp.concatenate([q_parts[c], qx.astype(BF16)], axis=1))

    def step(j, carry, masked):
        ks = pl.multiple_of(j * tq, tq)
        k = jnp.concatenate([k_ref[0, pl.ds(ks, tq), :], kx_ref[pl.ds(ks, tq), :]], axis=1)
        vt = vt_ref[0, :, pl.ds(ks, tq)]
        new = []
        for c in range(2):
            lsum, acc = carry[c]
            st = _dot_nt(k, q_aug[c])
            if masked:
                st = jnp.where(causal_t, st, MASKED)
            pt = jnp.exp2(st)
            new.append((lsum + jnp.sum(pt.reshape(tq // SUBLANES, SUBLANES, tq), axis=0),
                        acc + _dot(vt, pt.astype(BF16))))
        return tuple(new)

    zero = (jnp.zeros((SUBLANES, tq), F32), jnp.zeros((LANES, tq), F32))
    carry = _loop_blocks_by_two(qi, lambda j, c: step(j, c, False), (zero, zero))
    (ls1, a1), (ls2, a2) = step(qi, carry, True)
    l1 = jnp.sum(ls1, axis=0, keepdims=True)
    l2 = jnp.sum(ls2, axis=0, keepdims=True)
    ok = jnp.minimum(jnp.min(l1), jnp.min(l2)) >= UNDERFLOW_GUARD

    lf = lam_ref[...]
    lam = (jnp.exp(jnp.sum(lf[0:1] * lf[1:2], axis=-1, keepdims=True))
           - jnp.exp(jnp.sum(lf[2:3] * lf[3:4], axis=-1, keepdims=True)) + lambda_init)

    def finish(o1, o2):
        o = _rms(o1 - lam * o2, subln_ref[...]) * (1.0 - lambda_init)
        o_ref[0] = o.astype(BF16)

    @pl.when(ok)
    def _():
        finish((a1 / l1).T, (a2 / l2).T)

    @pl.when(jnp.logical_not(ok))
    def _():
        finish(*_diff_running_max(slope_ref[hd] * LOG2E, q_ref, k_ref, vt_ref, qi, tq))


def _alibi_tables(slopes, S):
    s = jnp.asarray(slopes, F32).reshape(-1, 1) * LOG2E
    terms = jnp.concatenate(_split3(s), axis=1)
    qx = jnp.concatenate([terms * float(LANES), terms, jnp.zeros((terms.shape[0], LANES - 2 * N_SPLIT), F32)], axis=1)
    j = np.arange(S)
    kx = np.zeros((S, LANES), np.float32)
    kx[:, 0:N_SPLIT] = (j // LANES)[:, None]
    kx[:, N_SPLIT:2 * N_SPLIT] = (j % LANES)[:, None]
    kx[:, 2 * N_SPLIT:3 * N_SPLIT] = 1.0
    return qx.reshape(-1, 1, LANES), jnp.asarray(kx, dtype=BF16)


def _diff_attention(slopes, q, k, vt, lambdas, subln, tq, lambda_init):
    B, S, D = q.shape
    qx, kx = _alibi_tables(slopes, S)
    return pl.pallas_call(
        functools.partial(_diff_kernel, tq=tq, lambda_init=lambda_init),
        grid=(B, DIFF_HEADS, S // tq),
        in_specs=[
            pl.BlockSpec(memory_space=pltpu.SMEM),
            pl.BlockSpec((1, 1, LANES), lambda b, h, i: (h, 0, 0)),
            pl.BlockSpec((1, tq, LANES), lambda b, h, i: (b, i, h)),
            pl.BlockSpec((1, S, LANES), lambda b, h, i: (b, 0, h)),
            _const_spec(kx.shape),
            pl.BlockSpec((1, LANES, S), lambda b, h, i: (b, h, 0)),
            pl.BlockSpec(lambdas.shape, lambda b, h, i: (0, 0)),
            pl.BlockSpec(subln.shape, lambda b, h, i: (0, 0)),
        ],
        out_specs=pl.BlockSpec((1, tq, LANES), lambda b, h, i: (b, i, h)),
        out_shape=jax.ShapeDtypeStruct((B, S, D), BF16),
        scratch_shapes=[pltpu.VMEM((8, LANES), F32)],
        compiler_params=_params(("parallel", "parallel", "arbitrary")),
        name="diff_attention",
    )(jnp.asarray(slopes, F32), qx, q, k, kx, vt, lambdas, subln)


def _mem_kv_kernel(mem_ref, norm_ref, w_ref, gain_ref, k_ref, v_ref):
    h = _rms(mem_ref[0], norm_ref[0]).astype(BF16)
    z = _dot(h, w_ref[0])
    width = MEM_HEADS * MEM_HEAD_DIM
    gain = gain_ref[0]
    for hd in range(MEM_HEADS):
        sl = slice(MEM_HEAD_DIM * hd, MEM_HEAD_DIM * (hd + 1))
        k_ref[0, 0, :, sl] = _rms(z[:, sl], gain).astype(BF16)
    v_ref[0, 0] = z[:, width:2 * width].astype(BF16)


def _mem_kv(mem, norms, w_kv, gains):
    B, M, D = mem.shape
    L = norms.shape[0]
    width = MEM_HEADS * MEM_HEAD_DIM
    out = pl.BlockSpec((1, 1, M, width), lambda l, b: (l, b, 0, 0))
    return pl.pallas_call(
        _mem_kv_kernel,
        grid=(L, B),
        in_specs=[
            pl.BlockSpec((1, M, D), lambda l, b: (b, 0, 0)),
            pl.BlockSpec((1, 1, D), lambda l, b: (l, 0, 0)),
            pl.BlockSpec((1, D, 2 * width), lambda l, b: (l, 0, 0)),
            pl.BlockSpec((1, 1, MEM_HEAD_DIM), lambda l, b: (l, 0, 0)),
        ],
        out_specs=[out, out],
        out_shape=[jax.ShapeDtypeStruct((L, B, M, width), BF16)] * 2,
        compiler_params=_params(("parallel", "parallel")),
        name="mem_kv",
    )(mem, norms, w_kv, gains)


def _post_kernel(*refs, n_attn):
    x_ref = refs[0]
    attn_refs = refs[1:1 + n_attn]
    wo_refs = refs[1 + n_attn:1 + 2 * n_attn]
    (mqn_ref, wq_ref, qg_ref, mk_ref, mv_ref, wmo_ref, fn_ref, wg_ref, wu_ref, wd_ref, o_ref) = refs[1 + 2 * n_attn:]
    x = x_ref[0]
    for a_ref, w_ref in zip(attn_refs, wo_refs):
        x = x + _dot(a_ref[0], w_ref[...])

    q = _dot(_rms(x, mqn_ref[...]).astype(BF16), wq_ref[...])
    heads = []
    for hd in range(MEM_HEADS):
        sl = slice(MEM_HEAD_DIM * hd, MEM_HEAD_DIM * (hd + 1))
        qh = (_rms(q[:, sl], qg_ref[...]) * (MEM_HEAD_DIM ** -0.5 * LOG2E)).astype(BF16)
        s = _dot_nt(qh, mk_ref[0, :, sl])
        p = jnp.exp2(s - jnp.max(s, axis=-1, keepdims=True))
        l = jnp.sum(p, axis=-1, keepdims=True)
        heads.append((_dot(p.astype(BF16), mv_ref[0, :, sl]) / l).astype(BF16))
    x = x + _dot(jnp.concatenate(heads, axis=-1), wmo_ref[...])

    hf = _rms(x, fn_ref[...]).astype(BF16)
    g = _dot(hf, wg_ref[...])
    u = _dot(hf, wu_ref[...])
    act = (g * jax.nn.sigmoid(g) * u).astype(BF16)
    o_ref[0] = x + _dot(act, wd_ref[...])


def _post_block(x, attns, wos, mqn, wq, qg, mk, mv, wmo, fn, wg, wu, wd, ts):
    B, S, D = x.shape
    n = len(attns)
    tok = lambda c: pl.BlockSpec((1, ts, c), lambda b, i: (b, i, 0))
    memspec = pl.BlockSpec((1,) + mk.shape[1:], lambda b, i: (b, 0, 0))
    consts_a = list(wos) + [mqn, wq, qg]
    consts_b = [wmo, fn, wg, wu, wd]
    return pl.pallas_call(
        functools.partial(_post_kernel, n_attn=n),
        grid=(B, S // ts),
        in_specs=([tok(D)] + [tok(a.shape[-1]) for a in attns] + [_const_spec(c.shape) for c in consts_a]
                  + [memspec, memspec] + [_const_spec(c.shape) for c in consts_b]),
        out_specs=tok(D),
        out_shape=jax.ShapeDtypeStruct((B, S, D), F32),
        compiler_params=_params(("parallel", "parallel")),
        name="post_block",
    )(x, *attns, *consts_a, mk, mv, *consts_b)


def _row(v):
    return v.reshape(1, -1).astype(F32)


def _pad_cols(w, lo, total):
    return jnp.pad(w, ((0, 0), (lo, total - lo - w.shape[1])))


def _even_weights(w_in, w_uq, w_ukv, q_gain, k_gain):
    kr = _pad_cols(w_in[:, 1152:1184], MLA_NOPE, LANES)
    w1 = jnp.concatenate([w_in[:, :1152], kr], axis=1).astype(BF16)
    wuq = jnp.concatenate(
        [_pad_cols(w_uq[:, MLA_QK * h:MLA_QK * (h + 1)], 0, LANES) for h in range(MLA_HEADS)], axis=1).astype(BF16)
    kv_w = MLA_NOPE + MLA_V
    wuk = jnp.concatenate(
        [_pad_cols(w_ukv[:, kv_w * h:kv_w * h + MLA_NOPE], 0, LANES) for h in range(MLA_HEADS)], axis=1).astype(BF16)
    wuv = jnp.concatenate(
        [w_ukv[:, kv_w * h + MLA_NOPE:kv_w * (h + 1)] for h in range(MLA_HEADS)], axis=1).astype(BF16).T
    gmq = _row(jnp.pad(q_gain, (0, LANES - MLA_QK)))
    gmk = _row(jnp.pad(k_gain, (0, LANES - MLA_QK)))
    return w1, wuq, wuk, wuv, gmq, gmk


def _rope_lane_freqs():
    half = MLA_ROPE // 2
    inv = np.zeros((1, LANES), np.float32)
    freqs = (np.float32(ROPE_THETA) ** (-np.arange(half, dtype=np.float32) / np.float32(half))).astype(np.float32)
    inv[0, MLA_NOPE:MLA_NOPE + half] = freqs
    inv[0, MLA_NOPE + half:MLA_QK] = freqs
    return jnp.asarray(inv)


def _ones_blocks(n):
    g = np.arange(n) // HEAD_DIM
    return jnp.asarray((g[:, None] == g[None, :]).astype(np.float32), dtype=BF16)


def kernel(x, mem, positions, mix_norm, ev_w_in, ev_swa_q_gain, ev_swa_k_gain, ev_sinks, ev_q_latent_norm,
           ev_kv_latent_norm, ev_w_uq, ev_w_ukv, ev_mla_q_gain, ev_mla_k_gain, ev_w_out, od_w_qkv, od_q_gain,
           od_k_gain, od_lambda, od_subln, od_w_out, mem_q_norm, mem_kv_norm, mem_w_q, mem_w_kv, mem_q_gain,
           mem_k_gain, mem_w_out, ffn_norm, ffn_w_gate, ffn_w_up, ffn_w_down):
    B, S, D = x.shape
    depth = mix_norm.shape[0]
    ts_proj = min(512, S)
    ts_post = min(256, S)
    tq_swa = min(256, S - SWA_WINDOW)
    tq_flash = min(1024, S)

    ones = _ones_blocks(512)
    pos3 = positions.reshape(B, S, 1)
    inv = _rope_lane_freqs()
    mem_k, mem_v = _mem_kv(mem, mem_kv_norm.reshape(depth, 1, D), mem_w_kv.astype(BF16),
                           mem_k_gain.reshape(depth, 1, MEM_HEAD_DIM))
    diff_slopes = [2.0 ** (-8.0 * (i + 1) / DIFF_HEADS) for i in range(DIFF_HEADS)]

    for l in range(depth):
        if l % 2 == 0:
            e = l // 2
            w1, wuq, wuk, wuv, gmq, gmk = _even_weights(ev_w_in[e], ev_w_uq[e], ev_w_ukv[e], ev_mla_q_gain[e],
                                                        ev_mla_k_gain[e])
            qa, ka, va, qm, km, vm = _even_proj(
                x, pos3, _row(mix_norm[l]), w1, ones, _row(jnp.tile(ev_swa_q_gain[e], SWA_HEADS)),
                _row(jnp.tile(ev_swa_k_gain[e], SWA_KV_HEADS)), _row(ev_q_latent_norm[e]),
                _row(ev_kv_latent_norm[e]), wuq, wuk, wuv, gmq, gmk, inv, ts_proj)
            out_a = _swa_attention(ev_sinks[e].astype(F32), qa, ka, va, tq_swa)
            out_b = _mla_attention(qm, km, vm, tq_flash)
            wo = ev_w_out[e].astype(BF16)
            attns = [out_a, out_b]
            wos = [wo[:512], wo[512:]]
        else:
            o = l // 2
            lambda_init = 0.8 - 0.6 * math.exp(-0.3 * l)
            n_grp = DIFF_HEADS * 2
            w_qkv = od_w_qkv[o].astype(BF16)
            q, k, vt = _odd_proj(x, _row(mix_norm[l]), w_qkv[:, :2 * D], w_qkv[:, 2 * D:].T, ones,
                                 _row(jnp.tile(od_q_gain[o], n_grp // 2)), _row(jnp.tile(od_k_gain[o], n_grp // 2)),
                                 ts_proj)
            out_d = _diff_attention(diff_slopes, q, k, vt, od_lambda[o].astype(F32), _row(od_subln[o]), tq_flash,
                                    lambda_init)
            attns = [out_d]
            wos = [od_w_out[o].astype(BF16)]
        x = _post_block(x, attns, wos, _row(mem_q_norm[l]), mem_w_q[l].astype(BF16), _row(mem_q_gain[l]),
                        mem_k[l], mem_v[l], mem_w_out[l].astype(BF16), _row(ffn_norm[l]),
                        ffn_w_gate[l].astype(BF16), ffn_w_up[l].astype(BF16), ffn_w_down[l].astype(BF16), ts_post)
    return x
```

```python
import functools
import math

import numpy as np
import jax
import jax.numpy as jnp
from jax import lax
from jax.experimental import pallas as pl
from jax.experimental.pallas import tpu as pltpu

F32 = jnp.float32
BF16 = jnp.bfloat16

EPS = 1e-6
MASKED = -1e30
LOG2E = 1.4426950408889634
ROPE_THETA = 10000.0

LANES = 128
SUBLANES = 8
MXU_WIDTH = 256
HEAD_DIM = 64
SWA_HEADS = 8
SWA_KV_HEADS = 2
SWA_WINDOW = 128
MLA_HEADS = 8
MLA_NOPE = 64
MLA_ROPE = 32
MLA_QK = MLA_NOPE + MLA_ROPE
MLA_V = 64
ROPE_HI_LANE = 64
DIFF_HEADS = 8
DIFF_DIM = 64
MEM_HEADS = 4
MEM_HEAD_DIM = 128

VMEM_LIMIT = 56 * 1024 * 1024

UNDERFLOW_GUARD = 2.0 ** -80
N_SPLIT = 3


def _params(sem):
    return pltpu.CompilerParams(dimension_semantics=sem, vmem_limit_bytes=VMEM_LIMIT)


def _const_spec(shape):
    nd = len(shape)
    return pl.BlockSpec(shape, lambda *_: (0,) * nd, pipeline_mode=pl.Buffered(1))


def _rms(x, g):
    ms = jnp.mean(x * x, axis=-1, keepdims=True)
    return x * lax.rsqrt(ms + EPS) * g


def _dot(a, b):
    return jnp.dot(a, b, preferred_element_type=F32)


def _dot_nt(a, b):
    return lax.dot_general(a, b, (((1,), (1,)), ((), ())), preferred_element_type=F32)


def _group_sumsq(x, ones_blk):
    x2 = x * x
    hi = x2.astype(BF16)
    lo = (x2 - hi.astype(F32)).astype(BF16)
    return _dot(hi, ones_blk) + _dot(lo, ones_blk)


def _group_norm64(x, ones_blk, gain):
    chunk = min(ones_blk.shape[0], x.shape[1])
    ssq = jnp.concatenate([_group_sumsq(x[:, c:c + chunk], ones_blk[0:chunk, 0:chunk])
                           for c in range(0, x.shape[1], chunk)], axis=1)
    return x * lax.rsqrt(ssq * (1.0 / HEAD_DIM) + EPS) * gain


def _even_proj_kernel(x_ref, pos_ref, gmix_ref, w1_ref, ones_ref, gq_ref, gk_ref, qln_ref, kvln_ref,
                      wuq_ref, wuk_ref, wuv_ref, gmq_ref, gmk_ref, inv_ref,
                      qa_ref, ka_ref, va_ref, qm_ref, km_ref, vm_ref):
    x = x_ref[0]
    h = _rms(x, gmix_ref[...]).astype(BF16)
    z = _dot(h, w1_ref[...])
    ones = ones_ref[...]
    qa = _group_norm64(z[:, 0:512], ones, gq_ref[...]) * (HEAD_DIM ** -0.5 * LOG2E)
    qa_ref[0] = qa.astype(BF16)
    ka = _group_norm64(z[:, 512:640], ones[0:128, 0:128], gk_ref[...])
    ka_ref[0] = ka.astype(BF16)
    va_ref[0] = z[:, 640:768].astype(BF16)

    cqn = _rms(z[:, 768:1024], qln_ref[...]).astype(BF16)
    ckvn = _rms(z[:, 1024:1152], kvln_ref[...]).astype(BF16)
    kr = z[:, 1152:1280]
    qf = _dot(cqn, wuq_ref[...])
    kn = _dot(ckvn, wuk_ref[...])
    vm_ref[0] = _dot_nt(wuv_ref[...], ckvn).astype(BF16)

    half = MLA_ROPE // 2
    ang = inv_ref[...] * pos_ref[0].astype(F32)
    pad = jnp.zeros((LANES - half, ang.shape[1]), F32)
    cos = jnp.concatenate([jnp.cos(ang), pad], axis=0).T
    sin = jnp.concatenate([jnp.sin(ang), pad], axis=0).T
    lane = lax.broadcasted_iota(jnp.int32, cos.shape, 1)
    in_lo = lane < half
    in_hi = (lane >= ROPE_HI_LANE) & (lane < ROPE_HI_LANE + half)
    cos_t = jnp.where(in_lo, cos, jnp.where(in_hi, pltpu.roll(cos, ROPE_HI_LANE, 1), 1.0))
    sin_t = jnp.where(in_lo, -sin, jnp.where(in_hi, pltpu.roll(sin, ROPE_HI_LANE, 1), 0.0))

    pair = 2 * LANES
    gi = lax.broadcasted_iota(jnp.int32, (pair, pair), 0) // LANES
    gj = lax.broadcasted_iota(jnp.int32, (pair, pair), 1) // LANES
    head_ones = (gi == gj).astype(BF16)

    def norm_rope(xp, gain):
        r = lax.rsqrt(_group_sumsq(xp, head_ones) * (1.0 / MLA_QK) + EPS)
        xn = xp * r * gain
        halves = []
        for t in range(2):
            xh = xn[:, LANES * t:LANES * (t + 1)]
            halves.append(xh * cos_t + pltpu.roll(xh, ROPE_HI_LANE, 1) * sin_t)
        return halves

    gmq = jnp.concatenate([gmq_ref[...]] * 2, axis=1)
    gmk = jnp.concatenate([gmk_ref[...]] * 2, axis=1)
    kr2 = jnp.concatenate([kr, kr], axis=1)
    ones_lanes = (lane >= MLA_QK) & (lane < MLA_QK + N_SPLIT)
    for hp in range(MLA_HEADS // 2):
        sl = slice(pair * hp, pair * (hp + 1))
        q_heads = norm_rope(qf[:, sl], gmq)
        k_heads = norm_rope(kn[:, sl] + kr2, gmk)
        for t in range(2):
            hl = slice(pair * hp + LANES * t, pair * hp + LANES * (t + 1))
            qm_ref[0, :, hl] = (q_heads[t] * (MLA_QK ** -0.5 * LOG2E)).astype(BF16)
            km_ref[0, :, hl] = jnp.where(ones_lanes, 1.0, k_heads[t]).astype(BF16)


def _even_proj(x, pos3, gmix, w1, ones, gq, gk, qln, kvln, wuq, wuk, wuv, gmq, gmk, inv, ts):
    B, S, D = x.shape
    grid = (B, S // ts)
    tok = lambda c: pl.BlockSpec((1, ts, c), lambda b, i: (b, i, 0))
    consts = [gmix, w1, ones, gq, gk, qln, kvln, wuq, wuk, wuv, gmq, gmk, inv]
    out_cols = [512, 128, 128, 1024, 1024]
    v_rows = MLA_HEADS * MLA_V
    return pl.pallas_call(
        _even_proj_kernel,
        grid=grid,
        in_specs=([tok(D), pl.BlockSpec((1, 1, ts), lambda b, i: (b, 0, i))]
                  + [_const_spec(c.shape) for c in consts]),
        out_specs=[tok(c) for c in out_cols] + [pl.BlockSpec((1, v_rows, ts), lambda b, i: (b, 0, i))],
        out_shape=([jax.ShapeDtypeStruct((B, S, c), BF16) for c in out_cols]
                   + [jax.ShapeDtypeStruct((B, v_rows, S), BF16)]),
        compiler_params=_params(("parallel", "parallel")),
        name="even_proj",
    )(x, pos3, *consts)


def _swa_kernel(sink_ref, q_ref, k_ref, v_ref, o_ref, *, tq):
    qi = pl.program_id(1)
    win = tq + SWA_WINDOW
    wstart = pl.multiple_of(jnp.maximum(qi * tq - SWA_WINDOW, 0), SWA_WINDOW)
    kw = k_ref[0, pl.ds(wstart, win), :]
    vw = v_ref[0, pl.ds(wstart, win), :]
    row = lax.broadcasted_iota(jnp.int32, (tq, win), 0)
    col = lax.broadcasted_iota(jnp.int32, (tq, win), 1)
    dist = (qi * tq + row) - (wstart + col)
    valid = (dist >= 0) & (dist < SWA_WINDOW)
    distf = dist.astype(F32)
    lane = lax.broadcasted_iota(jnp.int32, (tq, LANES), 1)
    group = SWA_HEADS // SWA_KV_HEADS
    for cb in range(SWA_HEADS // 2):
        qq = q_ref[0, :, LANES * cb:LANES * (cb + 1)].astype(F32)
        halves = []
        for e in range(2):
            hd = 2 * cb + e
            hk = hd // group
            xq = qq if e == hk else pltpu.roll(qq, HEAD_DIM, 1)
            in_kv = (lane >= HEAD_DIM * hk) & (lane < HEAD_DIM * (hk + 1))
            xq = jnp.where(in_kv, xq, 0.0).astype(BF16)
            s = _dot_nt(xq, kw) - (2.0 ** (-8.0 * (hd + 1) / SWA_HEADS) * LOG2E) * distf
            s = jnp.where(valid, s, MASKED)
            sink = sink_ref[hd] * LOG2E
            m = jnp.maximum(jnp.max(s, axis=-1, keepdims=True), sink)
            p = jnp.exp2(s - m)
            l = jnp.sum(p, axis=-1, keepdims=True) + jnp.exp2(sink - m)
            o = _dot(p.astype(BF16), vw) / l
            halves.append(o if e == hk else pltpu.roll(o, HEAD_DIM, 1))
        o_ref[0, :, LANES * cb:LANES * (cb + 1)] = jnp.where(lane < HEAD_DIM, halves[0], halves[1]).astype(BF16)


def _swa_attention(sinks, qa, ka, va, tq):
    B, S, _ = qa.shape
    return pl.pallas_call(
        functools.partial(_swa_kernel, tq=tq),
        grid=(B, S // tq),
        in_specs=[
            pl.BlockSpec(memory_space=pltpu.SMEM),
            pl.BlockSpec((1, tq, 512), lambda b, i: (b, i, 0)),
            pl.BlockSpec((1, S, LANES), lambda b, i: (b, 0, 0)),
            pl.BlockSpec((1, S, LANES), lambda b, i: (b, 0, 0)),
        ],
        out_specs=pl.BlockSpec((1, tq, 512), lambda b, i: (b, i, 0)),
        out_shape=jax.ShapeDtypeStruct((B, S, 512), BF16),
        compiler_params=_params(("parallel", "arbitrary")),
        name="swa_attention",
    )(sinks, qa, ka, va)


def _split3(x):
    hi = x.astype(BF16).astype(F32)
    r = x - hi
    mid = r.astype(BF16).astype(F32)
    lo = (r - mid).astype(BF16).astype(F32)
    return hi, mid, lo


def _lane_tile_sum(p):
    acc = p[:, 0:LANES]
    for t in range(1, p.shape[1] // LANES):
        acc = acc + p[:, LANES * t:LANES * (t + 1)]
    return acc


def _max_sq_norm(k_ref, lanes, gsum, rows):
    n_chunks = k_ref.shape[1] // rows

    def body(c, mx):
        kc = k_ref[0, pl.ds(pl.multiple_of(c * rows, rows), rows), lanes].astype(F32)
        return jnp.maximum(mx, jnp.max(_dot((kc * kc).astype(BF16), gsum), axis=0, keepdims=True))

    return lax.fori_loop(0, n_chunks, body, jnp.zeros((1, LANES), F32))


def _loop_blocks_by_two(n, step, init):
    carry = lax.fori_loop(0, n // 2, lambda t, c: step(2 * t + 1, step(2 * t, c)), init)
    return lax.cond(n % 2 == 1, lambda c: step(n - 1, c), lambda c: c, carry)


def _causal_mask(tq):
    row = lax.broadcasted_iota(jnp.int32, (tq, tq), 0)
    col = lax.broadcasted_iota(jnp.int32, (tq, tq), 1)
    return row >= col


def _mla_running_max(q_ref, k_ref, vt_ref, e, qi, tq):
    causal = _causal_mask(tq)
    q = q_ref[0, :, LANES * e:LANES * (e + 1)]

    def step(j, carry, masked):
        m, l, acc = carry
        ks = pl.multiple_of(j * tq, tq)
        k = k_ref[0, pl.ds(ks, tq), LANES * e:LANES * (e + 1)]
        vt = vt_ref[0, :, pl.ds(ks, tq)]
        s = _dot_nt(q, k)
        if masked:
            s = jnp.where(causal, s, MASKED)
        m_new = jnp.maximum(m, jnp.max(s, axis=-1, keepdims=True))
        alpha = jnp.exp2(m - m_new)
        p = jnp.exp2(s - m_new)
        l = alpha * l + jnp.sum(p, axis=-1, keepdims=True)
        acc = alpha * acc + _dot_nt(p.astype(BF16), vt)
        return m_new, l, acc

    init = (jnp.full((tq, 1), MASKED, F32), jnp.zeros((tq, 1), F32), jnp.zeros((tq, LANES), F32))
    carry = lax.fori_loop(0, qi, lambda j, c: step(j, c, False), init)
    m, l, acc = step(qi, carry, True)
    return acc / l


def _mla_kernel(q_ref, k_ref, vt_ref, o_ref, kmax_ref, *, tq):
    qi = pl.program_id(2)
    lane = lax.broadcasted_iota(jnp.int32, (tq, LANES), 1)
    gsum = (lax.broadcasted_iota(jnp.int32, (LANES, LANES), 0) < MLA_QK).astype(BF16)

    @pl.when(qi == 0)
    def _():
        for e in range(2):
            kmax_ref[e:e + 1, :] = _max_sq_norm(k_ref, slice(LANES * e, LANES * (e + 1)), gsum, tq)

    key_row = lax.broadcasted_iota(jnp.int32, (tq, tq), 0)
    query_col = lax.broadcasted_iota(jnp.int32, (tq, tq), 1)
    causal_t = key_row <= query_col
    q_aug = []
    for e in range(2):
        qf = q_ref[0, :, LANES * e:LANES * (e + 1)].astype(F32)
        bound = jnp.sqrt(_dot((qf * qf).astype(BF16), gsum) * kmax_ref[e:e + 1, :])
        hi, mid, lo = _split3(-bound)
        q_aug.append(jnp.where(lane == MLA_QK, hi, jnp.where(lane == MLA_QK + 1, mid,
                                                             jnp.where(lane == MLA_QK + 2, lo, qf))).astype(BF16))

    def step(j, carry, masked):
        ks = pl.multiple_of(j * tq, tq)
        new = []
        for e in range(2):
            lsum, acc = carry[e]
            st = _dot_nt(k_ref[0, pl.ds(ks, tq), LANES * e:LANES * (e + 1)], q_aug[e])
            if masked:
                st = jnp.where(causal_t, st, MASKED)
            pt = jnp.exp2(st)
            new.append((lsum + jnp.sum(pt.reshape(tq // SUBLANES, SUBLANES, tq), axis=0),
                        acc + _dot(vt_ref[0, MLA_V * e:MLA_V * (e + 1), pl.ds(ks, tq)], pt.astype(BF16))))
        return tuple(new)

    zero = (jnp.zeros((SUBLANES, tq), F32), jnp.zeros((MLA_V, tq), F32))
    carry = _loop_blocks_by_two(qi, lambda j, c: step(j, c, False), (zero, zero))
    carry = step(qi, carry, True)
    sums = [jnp.sum(lsum, axis=0, keepdims=True) for lsum, _ in carry]
    outs = [acc / l for (_, acc), l in zip(carry, sums)]
    ok = jnp.minimum(jnp.min(sums[0]), jnp.min(sums[1])) >= UNDERFLOW_GUARD

    @pl.when(ok)
    def _():
        o_ref[0] = jnp.concatenate(outs, axis=0).T.astype(BF16)

    @pl.when(jnp.logical_not(ok))
    def _():
        slow = [_mla_running_max(q_ref, k_ref, vt_ref, e, qi, tq) for e in range(2)]
        o_ref[0] = jnp.where(lane < MLA_V, slow[0], slow[1]).astype(BF16)


def _mla_attention(qm, km, vm, tq):
    B, S, _ = qm.shape
    return pl.pallas_call(
        functools.partial(_mla_kernel, tq=tq),
        grid=(B, MLA_HEADS // 2, S // tq),
        in_specs=[
            pl.BlockSpec((1, tq, 2 * LANES), lambda b, h, i: (b, i, h)),
            pl.BlockSpec((1, S, 2 * LANES), lambda b, h, i: (b, 0, h)),
            pl.BlockSpec((1, 2 * MLA_V, S), lambda b, h, i: (b, h, 0)),
        ],
        out_specs=pl.BlockSpec((1, tq, LANES), lambda b, h, i: (b, i, h)),
        out_shape=jax.ShapeDtypeStruct((B, S, MLA_HEADS * MLA_V), BF16),
        scratch_shapes=[pltpu.VMEM((8, LANES), F32)],
        compiler_params=_params(("parallel", "parallel", "arbitrary")),
        name="mla_attention",
    )(qm, km, vm)


def _odd_proj_kernel(x_ref, gmix_ref, w_ref, wvt_ref, ones_ref, gq_ref, gk_ref, q_ref, k_ref, vt_ref):
    x = x_ref[0]
    h = _rms(x, gmix_ref[...]).astype(BF16)
    z = _dot(h, w_ref[...])
    ones = ones_ref[...]
    width = DIFF_HEADS * 2 * DIFF_DIM
    for c in range(width // 512):
        sl = slice(512 * c, 512 * (c + 1))
        q = _group_norm64(z[:, sl], ones, gq_ref[...]) * (DIFF_DIM ** -0.5 * LOG2E)
        q_ref[0, :, sl] = q.astype(BF16)
        k = _group_norm64(z[:, width + 512 * c:width + 512 * (c + 1)], ones, gk_ref[...])
        k_ref[0, :, sl] = k.astype(BF16)
    vt_ref[0] = _dot_nt(wvt_ref[...], h).astype(BF16)


def _odd_proj(x, gmix, w_qk, w_vt, ones, gq, gk, ts):
    B, S, D = x.shape
    tok = pl.BlockSpec((1, ts, D), lambda b, i: (b, i, 0))
    consts = [gmix, w_qk, w_vt, ones, gq, gk]
    return pl.pallas_call(
        _odd_proj_kernel,
        grid=(B, S // ts),
        in_specs=[tok] + [_const_spec(c.shape) for c in consts],
        out_specs=[tok, tok, pl.BlockSpec((1, D, ts), lambda b, i: (b, 0, i))],
        out_shape=[jax.ShapeDtypeStruct((B, S, D), BF16)] * 2 + [jax.ShapeDtypeStruct((B, D, S), BF16)],
        compiler_params=_params(("parallel", "parallel")),
        name="odd_proj",
    )(x, *consts)


def _diff_q_parts(q, lane):
    zero = jnp.zeros_like(q)
    return jnp.where(lane < DIFF_DIM, q, zero), jnp.where(lane >= DIFF_DIM, q, zero)


def _diff_running_max(slope, q_ref, k_ref, vt_ref, qi, tq):
    row = lax.broadcasted_iota(jnp.int32, (tq, tq), 0)
    col = lax.broadcasted_iota(jnp.int32, (tq, tq), 1)
    causal = row >= col
    bias = slope * (col - row).astype(F32)
    lane = lax.broadcasted_iota(jnp.int32, (tq, LANES), 1)
    q_parts = _diff_q_parts(q_ref[0], lane)

    def step(j, carry, masked):
        ks = pl.multiple_of(j * tq, tq)
        k = k_ref[0, pl.ds(ks, tq), :]
        vt = vt_ref[0, :, pl.ds(ks, tq)]
        offset = slope * ((j - qi) * tq).astype(F32)
        new = []
        for c in range(2):
            m, l, acc = carry[c]
            s = _dot_nt(q_parts[c], k) + bias
            if masked:
                s = jnp.where(causal, s, MASKED)
            m_new = jnp.maximum(m, jnp.max(s, axis=-1, keepdims=True) + offset)
            alpha = jnp.exp2(m - m_new)
            p = jnp.exp2(s - (m_new - offset))
            l = alpha * l + jnp.sum(p, axis=-1, keepdims=True)
            acc = alpha * acc + _dot_nt(p.astype(BF16), vt)
            new.append((m_new, l, acc))
        return tuple(new)

    one = (jnp.full((tq, 1), MASKED, F32), jnp.zeros((tq, 1), F32), jnp.zeros((tq, LANES), F32))
    carry = lax.fori_loop(0, qi, lambda j, c: step(j, c, False), (one, one))
    (m1, l1, a1), (m2, l2, a2) = step(qi, carry, True)
    return a1 / l1, a2 / l2


def _diff_kernel(slope_ref, qx_ref, q_ref, k_ref, kx_ref, vt_ref, lam_ref, subln_ref, o_ref, kmax_ref, *, tq,
                 lambda_init):
    hd = pl.program_id(1)
    qi = pl.program_id(2)
    lane = lax.broadcasted_iota(jnp.int32, (tq, LANES), 1)
    gi = lax.broadcasted_iota(jnp.int32, (LANES, LANES), 0)
    gsums = ((gi < DIFF_DIM).astype(BF16), (gi >= DIFF_DIM).astype(BF16))

    @pl.when(qi == 0)
    def _():
        for c in range(2):
            kmax_ref[c:c + 1, :] = _max_sq_norm(k_ref, slice(None), gsums[c], tq)

    key_row = lax.broadcasted_iota(jnp.int32, (tq, tq), 0)
    query_col = lax.broadcasted_iota(jnp.int32, (tq, tq), 1)
    causal_t = key_row <= query_col
    q = q_ref[0]
    qf = q.astype(F32)
    q2 = (qf * qf).astype(BF16)
    q_parts = _diff_q_parts(q, lane)
    qx_const = qx_ref[0]
    slope = qx_const[:, 3:4] + qx_const[:, 4:5] + qx_const[:, 5:6]
    rowpos = (qi * tq + lax.broadcasted_iota(jnp.int32, (tq, LANES), 0)).astype(F32)
    q_aug = []
    for c in range(2):
        bound = jnp.sqrt(_dot(q2, gsums[c]) * kmax_ref[c:c + 1, :])
        hi, mid, lo = _split3(-slope * rowpos - bound)
        qx = jnp.where(lane == 2 * N_SPLIT, hi, jnp.where(lane == 2 * N_SPLIT + 1, mid,
                                                           jnp.where(lane == 2 * N_SPLIT + 2, lo, qx_const)))
        q_aug.append(jnp.concatenate([q_parts[c], qx.astype(BF16)], axis=1))

    def step(j, carry, masked):
        ks = pl.multiple_of(j * tq, tq)
        k = jnp.concatenate([k_ref[0, pl.ds(ks, tq), :], kx_ref[pl.ds(ks, tq), :]], axis=1)
        vt = vt_ref[0, :, pl.ds(ks, tq)]
        new = []
        for c in range(2):
            lsum, acc = carry[c]
            st = _dot_nt(k, q_aug[c])
            if masked:
                st = jnp.where(causal_t, st, MASKED)
            pt = jnp.exp2(st)
            new.append((lsum + jnp.sum(pt.reshape(tq // SUBLANES, SUBLANES, tq), axis=0),
                        acc + _dot(vt, pt.astype(BF16))))
        return tuple(new)

    zero = (jnp.zeros((SUBLANES, tq), F32), jnp.zeros((LANES, tq), F32))
    carry = _loop_blocks_by_two(qi, lambda j, c: step(j, c, False), (zero, zero))
    (ls1, a1), (ls2, a2) = step(qi, carry, True)
    l1 = jnp.sum(ls1, axis=0, keepdims=True)
    l2 = jnp.sum(ls2, axis=0, keepdims=True)
    ok = jnp.minimum(jnp.min(l1), jnp.min(l2)) >= UNDERFLOW_GUARD

    lf = lam_ref[...]
    lam = (jnp.exp(jnp.sum(lf[0:1] * lf[1:2], axis=-1, keepdims=True))
           - jnp.exp(jnp.sum(lf[2:3] * lf[3:4], axis=-1, keepdims=True)) + lambda_init)

    def finish(o1, o2):
        o = _rms(o1 - lam * o2, subln_ref[...]) * (1.0 - lambda_init)
        o_ref[0] = o.astype(BF16)

    @pl.when(ok)
    def _():
        finish((a1 / l1).T, (a2 / l2).T)

    @pl.when(jnp.logical_not(ok))
    def _():
        finish(*_diff_running_max(slope_ref[hd] * LOG2E, q_ref, k_ref, vt_ref, qi, tq))


def _alibi_tables(slopes, S):
    s = jnp.asarray(slopes, F32).reshape(-1, 1) * LOG2E
    terms = jnp.concatenate(_split3(s), axis=1)
    qx = jnp.concatenate([terms * float(LANES), terms, jnp.zeros((terms.shape[0], LANES - 2 * N_SPLIT), F32)], axis=1)
    j = np.arange(S)
    kx = np.zeros((S, LANES), np.float32)
    kx[:, 0:N_SPLIT] = (j // LANES)[:, None]
    kx[:, N_SPLIT:2 * N_SPLIT] = (j % LANES)[:, None]
    kx[:, 2 * N_SPLIT:3 * N_SPLIT] = 1.0
    return qx.reshape(-1, 1, LANES), jnp.asarray(kx, dtype=BF16)


def _diff_attention(slopes, q, k, vt, lambdas, subln, tq, lambda_init):
    B, S, D = q.shape
    qx, kx = _alibi_tables(slopes, S)
    return pl.pallas_call(
        functools.partial(_diff_kernel, tq=tq, lambda_init=lambda_init),
        grid=(B, DIFF_HEADS, S // tq),
        in_specs=[
            pl.BlockSpec(memory_space=pltpu.SMEM),
            pl.BlockSpec((1, 1, LANES), lambda b, h, i: (h, 0, 0)),
            pl.BlockSpec((1, tq, LANES), lambda b, h, i: (b, i, h)),
            pl.BlockSpec((1, S, LANES), lambda b, h, i: (b, 0, h)),
            _const_spec(kx.shape),
            pl.BlockSpec((1, LANES, S), lambda b, h, i: (b, h, 0)),
            pl.BlockSpec(lambdas.shape, lambda b, h, i: (0, 0)),
            pl.BlockSpec(subln.shape, lambda b, h, i: (0, 0)),
        ],
        out_specs=pl.BlockSpec((1, tq, LANES), lambda b, h, i: (b, i, h)),
        out_shape=jax.ShapeDtypeStruct((B, S, D), BF16),
        scratch_shapes=[pltpu.VMEM((8, LANES), F32)],
        compiler_params=_params(("parallel", "parallel", "arbitrary")),
        name="diff_attention",
    )(jnp.asarray(slopes, F32), qx, q, k, kx, vt, lambdas, subln)


def _mem_kv_kernel(mem_ref, norm_ref, w_ref, gain_ref, k_ref, v_ref):
    h = _rms(mem_ref[0], norm_ref[0]).astype(BF16)
    z = _dot(h, w_ref[0])
    width = MEM_HEADS * MEM_HEAD_DIM
    gain = gain_ref[0]
    for hd in range(MEM_HEADS):
        sl = slice(MEM_HEAD_DIM * hd, MEM_HEAD_DIM * (hd + 1))
        k_ref[0, 0, :, sl] = _rms(z[:, sl], gain).astype(BF16)
    v_ref[0, 0] = z[:, width:2 * width].astype(BF16)


def _mem_kv(mem, norms, w_kv, gains):
    B, M, D = mem.shape
    L = norms.shape[0]
    width = MEM_HEADS * MEM_HEAD_DIM
    out = pl.BlockSpec((1, 1, M, width), lambda l, b: (l, b, 0, 0))
    return pl.pallas_call(
        _mem_kv_kernel,
        grid=(L, B),
        in_specs=[
            pl.BlockSpec((1, M, D), lambda l, b: (b, 0, 0)),
            pl.BlockSpec((1, 1, D), lambda l, b: (l, 0, 0)),
            pl.BlockSpec((1, D, 2 * width), lambda l, b: (l, 0, 0)),
            pl.BlockSpec((1, 1, MEM_HEAD_DIM), lambda l, b: (l, 0, 0)),
        ],
        out_specs=[out, out],
        out_shape=[jax.ShapeDtypeStruct((L, B, M, width), BF16)] * 2,
        compiler_params=_params(("parallel", "parallel")),
        name="mem_kv",
    )(mem, norms, w_kv, gains)


def _post_kernel(*refs, n_attn):
    x_ref = refs[0]
    attn_refs = refs[1:1 + n_attn]
    wo_refs = refs[1 + n_attn:1 + 2 * n_attn]
    (mqn_ref, wq_ref, qg_ref, mk_ref, mv_ref, wmo_ref, fn_ref, wg_ref, wu_ref, wd_ref, o_ref) = refs[1 + 2 * n_attn:]
    x = x_ref[0]
    for a_ref, w_ref in zip(attn_refs, wo_refs):
        x = x + _dot(a_ref[0], w_ref[...])

    q = _dot(_rms(x, mqn_ref[...]).astype(BF16), wq_ref[...])
    heads = []
    for hd in range(MEM_HEADS):
        sl = slice(MEM_HEAD_DIM * hd, MEM_HEAD_DIM * (hd + 1))
        qh = (_rms(q[:, sl], qg_ref[...]) * (MEM_HEAD_DIM ** -0.5 * LOG2E)).astype(BF16)
        s = _dot_nt(qh, mk_ref[0, :, sl])
        p = jnp.exp2(s - jnp.max(s, axis=-1, keepdims=True))
        l = jnp.sum(p, axis=-1, keepdims=True)
        heads.append((_dot(p.astype(BF16), mv_ref[0, :, sl]) / l).astype(BF16))
    x = x + _dot(jnp.concatenate(heads, axis=-1), wmo_ref[...])

    hf = _rms(x, fn_ref[...]).astype(BF16)
    g = _dot(hf, wg_ref[...])
    u = _dot(hf, wu_ref[...])
    act = (g * jax.nn.sigmoid(g) * u).astype(BF16)
    o_ref[0] = x + _dot(act, wd_ref[...])


def _post_block(x, attns, wos, mqn, wq, qg, mk, mv, wmo, fn, wg, wu, wd, ts):
    B, S, D = x.shape
    n = len(attns)
    tok = lambda c: pl.BlockSpec((1, ts, c), lambda b, i: (b, i, 0))
    memspec = pl.BlockSpec((1,) + mk.shape[1:], lambda b, i: (b, 0, 0))
    consts_a = list(wos) + [mqn, wq, qg]
    consts_b = [wmo, fn, wg, wu, wd]
    return pl.pallas_call(
        functools.partial(_post_kernel, n_attn=n),
        grid=(B, S // ts),
        in_specs=([tok(D)] + [tok(a.shape[-1]) for a in attns] + [_const_spec(c.shape) for c in consts_a]
                  + [memspec, memspec] + [_const_spec(c.shape) for c in consts_b]),
        out_specs=tok(D),
        out_shape=jax.ShapeDtypeStruct((B, S, D), F32),
        compiler_params=_params(("parallel", "parallel")),
        name="post_block",
    )(x, *attns, *consts_a, mk, mv, *consts_b)


def _row(v):
    return v.reshape(1, -1).astype(F32)


def _pad_cols(w, lo, total):
    return jnp.pad(w, ((0, 0), (lo, total - lo - w.shape[1])))


def _mla_lane_perm():
    half = MLA_ROPE // 2
    lanes = np.arange(MLA_QK)
    return np.where(lanes < half, MLA_NOPE + lanes,
                    np.where(lanes < ROPE_HI_LANE, lanes - half,
                             np.where(lanes < ROPE_HI_LANE + half, lanes + half, lanes - MLA_ROPE)))


def _even_weights(w_in, w_uq, w_ukv, q_gain, k_gain):
    perm = _mla_lane_perm()
    is_nope = perm < MLA_NOPE
    pad = LANES - MLA_QK

    def head_cols(w_head, live):
        return jnp.pad(jnp.where(live[None, :], w_head[:, perm], 0.0), ((0, 0), (0, pad)))

    kr_src = jnp.concatenate([jnp.zeros((w_in.shape[0], MLA_NOPE), w_in.dtype), w_in[:, 1152:1184]], axis=1)
    w1 = jnp.concatenate([w_in[:, :1152], head_cols(kr_src, ~is_nope)], axis=1).astype(BF16)
    wuq = jnp.concatenate(
        [head_cols(w_uq[:, MLA_QK * h:MLA_QK * (h + 1)], np.ones_like(is_nope)) for h in range(MLA_HEADS)],
        axis=1).astype(BF16)
    kv_w = MLA_NOPE + MLA_V
    rope_zeros = jnp.zeros((w_ukv.shape[0], MLA_ROPE), w_ukv.dtype)
    wuk = jnp.concatenate(
        [head_cols(jnp.concatenate([w_ukv[:, kv_w * h:kv_w * h + MLA_NOPE], rope_zeros], axis=1), is_nope)
         for h in range(MLA_HEADS)], axis=1).astype(BF16)
    wuv = jnp.concatenate(
        [w_ukv[:, kv_w * h + MLA_NOPE:kv_w * (h + 1)] for h in range(MLA_HEADS)], axis=1).astype(BF16).T
    gmq = _row(jnp.pad(q_gain[perm], (0, pad)))
    gmk = _row(jnp.pad(k_gain[perm], (0, pad)))
    return w1, wuq, wuk, wuv, gmq, gmk


def _rope_freqs():
    half = MLA_ROPE // 2
    freqs = (np.float32(ROPE_THETA) ** (-np.arange(half, dtype=np.float32) / np.float32(half))).astype(np.float32)
    return jnp.asarray(freqs.reshape(half, 1))


def _ones_blocks(n):
    g = np.arange(n) // HEAD_DIM
    return jnp.asarray((g[:, None] == g[None, :]).astype(np.float32), dtype=BF16)


def kernel(x, mem, positions, mix_norm, ev_w_in, ev_swa_q_gain, ev_swa_k_gain, ev_sinks, ev_q_latent_norm,
           ev_kv_latent_norm, ev_w_uq, ev_w_ukv, ev_mla_q_gain, ev_mla_k_gain, ev_w_out, od_w_qkv, od_q_gain,
           od_k_gain, od_lambda, od_subln, od_w_out, mem_q_norm, mem_kv_norm, mem_w_q, mem_w_kv, mem_q_gain,
           mem_k_gain, mem_w_out, ffn_norm, ffn_w_gate, ffn_w_up, ffn_w_down):
    B, S, D = x.shape
    depth = mix_norm.shape[0]
    ts_proj = min(512, S)
    ts_post = min(256, S)
    tq_swa = min(256, S - SWA_WINDOW)
    tq_flash = min(1024, S)

    ones = _ones_blocks(MXU_WIDTH)
    pos3 = positions.reshape(B, 1, S)
    inv = _rope_freqs()
    mem_k, mem_v = _mem_kv(mem, mem_kv_norm.reshape(depth, 1, D), mem_w_kv.astype(BF16),
                           mem_k_gain.reshape(depth, 1, MEM_HEAD_DIM))
    diff_slopes = [2.0 ** (-8.0 * (i + 1) / DIFF_HEADS) for i in range(DIFF_HEADS)]

    for l in range(depth):
        if l % 2 == 0:
            e = l // 2
            w1, wuq, wuk, wuv, gmq, gmk = _even_weights(ev_w_in[e], ev_w_uq[e], ev_w_ukv[e], ev_mla_q_gain[e],
                                                        ev_mla_k_gain[e])
            qa, ka, va, qm, km, vm = _even_proj(
                x, pos3, _row(mix_norm[l]), w1, ones, _row(jnp.tile(ev_swa_q_gain[e], SWA_HEADS)),
                _row(jnp.tile(ev_swa_k_gain[e], SWA_KV_HEADS)), _row(ev_q_latent_norm[e]),
                _row(ev_kv_latent_norm[e]), wuq, wuk, wuv, gmq, gmk, inv, ts_proj)
            out_a = _swa_attention(ev_sinks[e].astype(F32), qa, ka, va, tq_swa)
            out_b = _mla_attention(qm, km, vm, tq_flash)
            wo = ev_w_out[e].astype(BF16)
            attns = [out_a, out_b]
            wos = [wo[:512], wo[512:]]
        else:
            o = l // 2
            lambda_init = 0.8 - 0.6 * math.exp(-0.3 * l)
            n_grp = DIFF_HEADS * 2
            w_qkv = od_w_qkv[o].astype(BF16)
            q, k, vt = _odd_proj(x, _row(mix_norm[l]), w_qkv[:, :2 * D], w_qkv[:, 2 * D:].T, ones,
                                 _row(jnp.tile(od_q_gain[o], n_grp // 2)), _row(jnp.tile(od_k_gain[o], n_grp // 2)),
                                 ts_proj)
            out_d = _diff_attention(diff_slopes, q, k, vt, od_lambda[o].astype(F32), _row(od_subln[o]), tq_flash,
                                    lambda_init)
            attns = [out_d]
            wos = [od_w_out[o].astype(BF16)]
        x = _post_block(x, attns, wos, _row(mem_q_norm[l]), mem_w_q[l].astype(BF16), _row(mem_q_gain[l]),
                        mem_k[l], mem_v[l], mem_w_out[l].astype(BF16), _row(ffn_norm[l]),
                        ffn_w_gate[l].astype(BF16), ffn_w_up[l].astype(BF16), ffn_w_down[l].astype(BF16), ts_post)
    return x
```

```python
import functools
import math

import numpy as np
import jax
import jax.numpy as jnp
from jax import lax
from jax.experimental import pallas as pl
from jax.experimental.pallas import tpu as pltpu

F32 = jnp.float32
BF16 = jnp.bfloat16

EPS = 1e-6
MASKED = -1e30
LOG2E = 1.4426950408889634
ROPE_THETA = 10000.0

LANES = 128
SUBLANES = 8
MXU_WIDTH = 256
HEAD_DIM = 64
SWA_HEADS = 8
SWA_KV_HEADS = 2
SWA_WINDOW = 128
MLA_HEADS = 8
MLA_NOPE = 64
MLA_ROPE = 32
MLA_QK = MLA_NOPE + MLA_ROPE
MLA_V = 64
ROPE_HI_LANE = 64
DIFF_HEADS = 8
DIFF_DIM = 64
MEM_HEADS = 4
MEM_HEAD_DIM = 128

VMEM_LIMIT = 56 * 1024 * 1024

UNDERFLOW_GUARD = 2.0 ** -80
N_SPLIT = 3


def _params(sem):
    return pltpu.CompilerParams(dimension_semantics=sem, vmem_limit_bytes=VMEM_LIMIT)


def _const_spec(shape):
    nd = len(shape)
    return pl.BlockSpec(shape, lambda *_: (0,) * nd, pipeline_mode=pl.Buffered(1))


def _rms(x, g):
    ms = jnp.mean(x * x, axis=-1, keepdims=True)
    return x * lax.rsqrt(ms + EPS) * g


def _dot(a, b):
    return jnp.dot(a, b, preferred_element_type=F32)


def _dot_nt(a, b):
    return lax.dot_general(a, b, (((1,), (1,)), ((), ())), preferred_element_type=F32)


def _group_sumsq(x, ones_blk):
    x2 = x * x
    hi = x2.astype(BF16)
    lo = (x2 - hi.astype(F32)).astype(BF16)
    return _dot(hi, ones_blk) + _dot(lo, ones_blk)


def _group_norm64(x, ones_blk, gain):
    chunk = min(ones_blk.shape[0], x.shape[1])
    ssq = jnp.concatenate([_group_sumsq(x[:, c:c + chunk], ones_blk[0:chunk, 0:chunk])
                           for c in range(0, x.shape[1], chunk)], axis=1)
    return x * lax.rsqrt(ssq * (1.0 / HEAD_DIM) + EPS) * gain


def _even_proj_kernel(x_ref, pos_ref, gmix_ref, w1_ref, ones_ref, gq_ref, gk_ref, qln_ref, kvln_ref,
                      wuq_ref, wuk_ref, wuv_ref, gmq_ref, gmk_ref, inv_ref,
                      qa_ref, ka_ref, va_ref, qm_ref, km_ref, vm_ref):
    x = x_ref[0]
    h = _rms(x, gmix_ref[...]).astype(BF16)
    z = _dot(h, w1_ref[...])
    ones = ones_ref[...]
    qa = _group_norm64(z[:, 0:512], ones, gq_ref[...]) * (HEAD_DIM ** -0.5 * LOG2E)
    qa_ref[0] = qa.astype(BF16)
    ka = _group_norm64(z[:, 512:640], ones[0:128, 0:128], gk_ref[...])
    ka_ref[0] = ka.astype(BF16)
    va_ref[0] = z[:, 640:768].astype(BF16)

    cqn = _rms(z[:, 768:1024], qln_ref[...]).astype(BF16)
    ckvn = _rms(z[:, 1024:1152], kvln_ref[...]).astype(BF16)
    kr = z[:, 1152:1280]
    qf = _dot(cqn, wuq_ref[...])
    kn = _dot(ckvn, wuk_ref[...])
    vm_ref[0] = _dot_nt(wuv_ref[...], ckvn).astype(BF16)

    half = MLA_ROPE // 2
    ang = inv_ref[...] * pos_ref[0].astype(F32)
    pad = jnp.zeros((LANES - half, ang.shape[1]), F32)
    cos = jnp.concatenate([jnp.cos(ang), pad], axis=0).T
    sin = jnp.concatenate([jnp.sin(ang), pad], axis=0).T
    lane = lax.broadcasted_iota(jnp.int32, cos.shape, 1)
    in_lo = lane < half
    in_hi = (lane >= ROPE_HI_LANE) & (lane < ROPE_HI_LANE + half)
    cos_t = jnp.where(in_lo, cos, jnp.where(in_hi, pltpu.roll(cos, ROPE_HI_LANE, 1), 1.0))
    sin_t = jnp.where(in_lo, -sin, jnp.where(in_hi, pltpu.roll(sin, ROPE_HI_LANE, 1), 0.0))

    pair = 2 * LANES
    gi = lax.broadcasted_iota(jnp.int32, (pair, pair), 0) // LANES
    gj = lax.broadcasted_iota(jnp.int32, (pair, pair), 1) // LANES
    head_ones = (gi == gj).astype(BF16)

    def norm_rope(xp, gain):
        r = lax.rsqrt(_group_sumsq(xp, head_ones) * (1.0 / MLA_QK) + EPS)
        xn = xp * r * gain
        halves = []
        for t in range(2):
            xh = xn[:, LANES * t:LANES * (t + 1)]
            halves.append(xh * cos_t + pltpu.roll(xh, ROPE_HI_LANE, 1) * sin_t)
        return halves

    gmq = jnp.concatenate([gmq_ref[...]] * 2, axis=1)
    gmk = jnp.concatenate([gmk_ref[...]] * 2, axis=1)
    kr2 = jnp.concatenate([kr, kr], axis=1)
    ones_lanes = (lane >= MLA_QK) & (lane < MLA_QK + N_SPLIT)
    for hp in range(MLA_HEADS // 2):
        sl = slice(pair * hp, pair * (hp + 1))
        q_heads = norm_rope(qf[:, sl], gmq)
        k_heads = norm_rope(kn[:, sl] + kr2, gmk)
        for t in range(2):
            hl = slice(pair * hp + LANES * t, pair * hp + LANES * (t + 1))
            qm_ref[0, :, hl] = (q_heads[t] * (MLA_QK ** -0.5 * LOG2E)).astype(BF16)
            km_ref[0, :, hl] = jnp.where(ones_lanes, 1.0, k_heads[t]).astype(BF16)


def _even_proj(x, pos3, gmix, w1, ones, gq, gk, qln, kvln, wuq, wuk, wuv, gmq, gmk, inv, ts):
    B, S, D = x.shape
    grid = (B, S // ts)
    tok = lambda c: pl.BlockSpec((1, ts, c), lambda b, i: (b, i, 0))
    consts = [gmix, w1, ones, gq, gk, qln, kvln, wuq, wuk, wuv, gmq, gmk, inv]
    out_cols = [512, 128, 128, 1024, 1024]
    v_rows = MLA_HEADS * MLA_V
    return pl.pallas_call(
        _even_proj_kernel,
        grid=grid,
        in_specs=([tok(D), pl.BlockSpec((1, 1, ts), lambda b, i: (b, 0, i))]
                  + [_const_spec(c.shape) for c in consts]),
        out_specs=[tok(c) for c in out_cols] + [pl.BlockSpec((1, v_rows, ts), lambda b, i: (b, 0, i))],
        out_shape=([jax.ShapeDtypeStruct((B, S, c), BF16) for c in out_cols]
                   + [jax.ShapeDtypeStruct((B, v_rows, S), BF16)]),
        compiler_params=_params(("parallel", "parallel")),
        name="even_proj",
    )(x, pos3, *consts)


def _swa_kernel(sink_ref, q_ref, k_ref, v_ref, o_ref, *, tq):
    qi = pl.program_id(1)
    win = tq + SWA_WINDOW
    wstart = pl.multiple_of(jnp.maximum(qi * tq - SWA_WINDOW, 0), SWA_WINDOW)
    kw = k_ref[0, pl.ds(wstart, win), :]
    vw = v_ref[0, pl.ds(wstart, win), :]
    row = lax.broadcasted_iota(jnp.int32, (tq, win), 0)
    col = lax.broadcasted_iota(jnp.int32, (tq, win), 1)
    dist = (qi * tq + row) - (wstart + col)
    valid = (dist >= 0) & (dist < SWA_WINDOW)
    distf = dist.astype(F32)
    lane = lax.broadcasted_iota(jnp.int32, (tq, LANES), 1)
    group = SWA_HEADS // SWA_KV_HEADS
    for cb in range(SWA_HEADS // 2):
        qq = q_ref[0, :, LANES * cb:LANES * (cb + 1)].astype(F32)
        halves = []
        for e in range(2):
            hd = 2 * cb + e
            hk = hd // group
            xq = qq if e == hk else pltpu.roll(qq, HEAD_DIM, 1)
            in_kv = (lane >= HEAD_DIM * hk) & (lane < HEAD_DIM * (hk + 1))
            xq = jnp.where(in_kv, xq, 0.0).astype(BF16)
            s = _dot_nt(xq, kw) - (2.0 ** (-8.0 * (hd + 1) / SWA_HEADS) * LOG2E) * distf
            s = jnp.where(valid, s, MASKED)
            sink = sink_ref[hd] * LOG2E
            m = jnp.maximum(jnp.max(s, axis=-1, keepdims=True), sink)
            p = jnp.exp2(s - m)
            l = jnp.sum(p, axis=-1, keepdims=True) + jnp.exp2(sink - m)
            o = _dot(p.astype(BF16), vw) / l
            halves.append(o if e == hk else pltpu.roll(o, HEAD_DIM, 1))
        o_ref[0, :, LANES * cb:LANES * (cb + 1)] = jnp.where(lane < HEAD_DIM, halves[0], halves[1]).astype(BF16)


def _swa_attention(sinks, qa, ka, va, tq):
    B, S, _ = qa.shape
    return pl.pallas_call(
        functools.partial(_swa_kernel, tq=tq),
        grid=(B, S // tq),
        in_specs=[
            pl.BlockSpec(memory_space=pltpu.SMEM),
            pl.BlockSpec((1, tq, 512), lambda b, i: (b, i, 0)),
            pl.BlockSpec((1, S, LANES), lambda b, i: (b, 0, 0)),
            pl.BlockSpec((1, S, LANES), lambda b, i: (b, 0, 0)),
        ],
        out_specs=pl.BlockSpec((1, tq, 512), lambda b, i: (b, i, 0)),
        out_shape=jax.ShapeDtypeStruct((B, S, 512), BF16),
        compiler_params=_params(("parallel", "arbitrary")),
        name="swa_attention",
    )(sinks, qa, ka, va)


def _split3(x):
    hi = x.astype(BF16).astype(F32)
    r = x - hi
    mid = r.astype(BF16).astype(F32)
    lo = (r - mid).astype(BF16).astype(F32)
    return hi, mid, lo


def _lane_tile_sum(p):
    acc = p[:, 0:LANES]
    for t in range(1, p.shape[1] // LANES):
        acc = acc + p[:, LANES * t:LANES * (t + 1)]
    return acc


def _max_sq_norm(k_ref, lanes, gsum, rows):
    n_chunks = k_ref.shape[1] // rows

    def body(c, mx):
        kc = k_ref[0, pl.ds(pl.multiple_of(c * rows, rows), rows), lanes].astype(F32)
        return jnp.maximum(mx, jnp.max(_dot((kc * kc).astype(BF16), gsum), axis=0, keepdims=True))

    return lax.fori_loop(0, n_chunks, body, jnp.zeros((1, LANES), F32))


BLOCKS_PER_TRIP = 4


def _loop_blocks(n, step, init):
    def trip(t, c):
        for u in range(BLOCKS_PER_TRIP):
            c = step(BLOCKS_PER_TRIP * t + u, c)
        return c

    full = n // BLOCKS_PER_TRIP
    carry = lax.fori_loop(0, full, trip, init)
    return lax.fori_loop(full * BLOCKS_PER_TRIP, n, step, carry)


def _add_cols(acc, lo, part):
    if lo == 0:
        return acc + part
    return acc + jnp.concatenate([jnp.zeros((acc.shape[0], lo), acc.dtype), part], axis=1)


def _diagonal_spans(tq):
    h = tq // 2
    return ((0, h, 0), (h, h, h)) if h % MXU_WIDTH == 0 else ((0, tq, 0),)


def _causal_t(key_lo, key_n, q_lo, tq):
    shape = (key_n, tq - q_lo)
    return (key_lo + lax.broadcasted_iota(jnp.int32, shape, 0)) <= (q_lo + lax.broadcasted_iota(jnp.int32, shape, 1))


def _causal_mask(tq):
    row = lax.broadcasted_iota(jnp.int32, (tq, tq), 0)
    col = lax.broadcasted_iota(jnp.int32, (tq, tq), 1)
    return row >= col


def _mla_running_max(q_ref, k_ref, vt_ref, e, qi, tq):
    causal = _causal_mask(tq)
    q = q_ref[0, :, LANES * e:LANES * (e + 1)]

    def step(j, carry, masked):
        m, l, acc = carry
        ks = pl.multiple_of(j * tq, tq)
        k = k_ref[0, pl.ds(ks, tq), LANES * e:LANES * (e + 1)]
        vt = vt_ref[0, :, pl.ds(ks, tq)]
        s = _dot_nt(q, k)
        if masked:
            s = jnp.where(causal, s, MASKED)
        m_new = jnp.maximum(m, jnp.max(s, axis=-1, keepdims=True))
        alpha = jnp.exp2(m - m_new)
        p = jnp.exp2(s - m_new)
        l = alpha * l + jnp.sum(p, axis=-1, keepdims=True)
        acc = alpha * acc + _dot_nt(p.astype(BF16), vt)
        return m_new, l, acc

    init = (jnp.full((tq, 1), MASKED, F32), jnp.zeros((tq, 1), F32), jnp.zeros((tq, LANES), F32))
    carry = lax.fori_loop(0, qi, lambda j, c: step(j, c, False), init)
    m, l, acc = step(qi, carry, True)
    return acc / l


def _mla_kernel(q_ref, k_ref, vt_ref, o_ref, kmax_ref, *, tq):
    qi = pl.program_id(2)
    lane = lax.broadcasted_iota(jnp.int32, (tq, LANES), 1)
    gsum = (lax.broadcasted_iota(jnp.int32, (LANES, LANES), 0) < MLA_QK).astype(BF16)

    @pl.when(qi == 0)
    def _():
        for e in range(2):
            kmax_ref[e:e + 1, :] = _max_sq_norm(k_ref, slice(LANES * e, LANES * (e + 1)), gsum, tq)

    q_aug = []
    for e in range(2):
        qf = q_ref[0, :, LANES * e:LANES * (e + 1)].astype(F32)
        bound = jnp.sqrt(_dot((qf * qf).astype(BF16), gsum) * kmax_ref[e:e + 1, :])
        hi, mid, lo = _split3(-bound)
        q_aug.append(jnp.where(lane == MLA_QK, hi, jnp.where(lane == MLA_QK + 1, mid,
                                                             jnp.where(lane == MLA_QK + 2, lo, qf))).astype(BF16))

    def block(j, carry, key_lo=0, key_n=tq, q_lo=0, masked=False):
        ks = pl.multiple_of(j * tq + key_lo, key_n)
        new = []
        for e in range(2):
            lsum, acc = carry[e]
            st = _dot_nt(k_ref[0, pl.ds(ks, key_n), LANES * e:LANES * (e + 1)], q_aug[e][q_lo:, :])
            if masked:
                st = jnp.where(_causal_t(key_lo, key_n, q_lo, tq), st, MASKED)
            pt = jnp.exp2(st)
            part = jnp.sum(pt.reshape(key_n // SUBLANES, SUBLANES, tq - q_lo), axis=0)
            pv = _dot(vt_ref[0, MLA_V * e:MLA_V * (e + 1), pl.ds(ks, key_n)], pt.astype(BF16))
            new.append((_add_cols(lsum, q_lo, part), _add_cols(acc, q_lo, pv)))
        return tuple(new)

    zero = (jnp.zeros((SUBLANES, tq), F32), jnp.zeros((MLA_V, tq), F32))
    carry = _loop_blocks(qi, block, (zero, zero))
    for key_lo, key_n, q_lo in _diagonal_spans(tq):
        carry = block(qi, carry, key_lo, key_n, q_lo, masked=True)
    sums = [jnp.sum(lsum, axis=0, keepdims=True) for lsum, _ in carry]
    outs = [acc / l for (_, acc), l in zip(carry, sums)]
    ok = jnp.minimum(jnp.min(sums[0]), jnp.min(sums[1])) >= UNDERFLOW_GUARD

    @pl.when(ok)
    def _():
        o_ref[0] = jnp.concatenate(outs, axis=0).T.astype(BF16)

    @pl.when(jnp.logical_not(ok))
    def _():
        slow = [_mla_running_max(q_ref, k_ref, vt_ref, e, qi, tq) for e in range(2)]
        o_ref[0] = jnp.where(lane < MLA_V, slow[0], slow[1]).astype(BF16)


def _mla_attention(qm, km, vm, tq):
    B, S, _ = qm.shape
    return pl.pallas_call(
        functools.partial(_mla_kernel, tq=tq),
        grid=(B, MLA_HEADS // 2, S // tq),
        in_specs=[
            pl.BlockSpec((1, tq, 2 * LANES), lambda b, h, i: (b, i, h)),
            pl.BlockSpec((1, S, 2 * LANES), lambda b, h, i: (b, 0, h)),
            pl.BlockSpec((1, 2 * MLA_V, S), lambda b, h, i: (b, h, 0)),
        ],
        out_specs=pl.BlockSpec((1, tq, LANES), lambda b, h, i: (b, i, h)),
        out_shape=jax.ShapeDtypeStruct((B, S, MLA_HEADS * MLA_V), BF16),
        scratch_shapes=[pltpu.VMEM((8, LANES), F32)],
        compiler_params=_params(("parallel", "parallel", "arbitrary")),
        name="mla_attention",
    )(qm, km, vm)


def _odd_proj_kernel(x_ref, gmix_ref, w_ref, wvt_ref, ones_ref, gq_ref, gk_ref, q_ref, k_ref, vt_ref):
    x = x_ref[0]
    h = _rms(x, gmix_ref[...]).astype(BF16)
    z = _dot(h, w_ref[...])
    ones = ones_ref[...]
    width = DIFF_HEADS * 2 * DIFF_DIM
    for c in range(width // 512):
        sl = slice(512 * c, 512 * (c + 1))
        q = _group_norm64(z[:, sl], ones, gq_ref[...]) * (DIFF_DIM ** -0.5 * LOG2E)
        q_ref[0, :, sl] = q.astype(BF16)
        k = _group_norm64(z[:, width + 512 * c:width + 512 * (c + 1)], ones, gk_ref[...])
        k_ref[0, :, sl] = k.astype(BF16)
    vt_ref[0] = _dot_nt(wvt_ref[...], h).astype(BF16)


def _odd_proj(x, gmix, w_qk, w_vt, ones, gq, gk, ts):
    B, S, D = x.shape
    tok = pl.BlockSpec((1, ts, D), lambda b, i: (b, i, 0))
    consts = [gmix, w_qk, w_vt, ones, gq, gk]
    return pl.pallas_call(
        _odd_proj_kernel,
        grid=(B, S // ts),
        in_specs=[tok] + [_const_spec(c.shape) for c in consts],
        out_specs=[tok, tok, pl.BlockSpec((1, D, ts), lambda b, i: (b, 0, i))],
        out_shape=[jax.ShapeDtypeStruct((B, S, D), BF16)] * 2 + [jax.ShapeDtypeStruct((B, D, S), BF16)],
        compiler_params=_params(("parallel", "parallel")),
        name="odd_proj",
    )(x, *consts)


def _diff_q_parts(q, lane):
    zero = jnp.zeros_like(q)
    return jnp.where(lane < DIFF_DIM, q, zero), jnp.where(lane >= DIFF_DIM, q, zero)


def _diff_running_max(slope, q_ref, k_ref, vt_ref, qi, tq):
    row = lax.broadcasted_iota(jnp.int32, (tq, tq), 0)
    col = lax.broadcasted_iota(jnp.int32, (tq, tq), 1)
    causal = row >= col
    bias = slope * (col - row).astype(F32)
    lane = lax.broadcasted_iota(jnp.int32, (tq, LANES), 1)
    q_parts = _diff_q_parts(q_ref[0], lane)

    def step(j, carry, masked):
        ks = pl.multiple_of(j * tq, tq)
        k = k_ref[0, pl.ds(ks, tq), :]
        vt = vt_ref[0, :, pl.ds(ks, tq)]
        offset = slope * ((j - qi) * tq).astype(F32)
        new = []
        for c in range(2):
            m, l, acc = carry[c]
            s = _dot_nt(q_parts[c], k) + bias
            if masked:
                s = jnp.where(causal, s, MASKED)
            m_new = jnp.maximum(m, jnp.max(s, axis=-1, keepdims=True) + offset)
            alpha = jnp.exp2(m - m_new)
            p = jnp.exp2(s - (m_new - offset))
            l = alpha * l + jnp.sum(p, axis=-1, keepdims=True)
            acc = alpha * acc + _dot_nt(p.astype(BF16), vt)
            new.append((m_new, l, acc))
        return tuple(new)

    one = (jnp.full((tq, 1), MASKED, F32), jnp.zeros((tq, 1), F32), jnp.zeros((tq, LANES), F32))
    carry = lax.fori_loop(0, qi, lambda j, c: step(j, c, False), (one, one))
    (m1, l1, a1), (m2, l2, a2) = step(qi, carry, True)
    return a1 / l1, a2 / l2


def _diff_kernel(slope_ref, qx_ref, q_ref, k_ref, kx_ref, vt_ref, lam_ref, subln_ref, o_ref, kmax_ref, *, tq,
                 lambda_init):
    hd = pl.program_id(1)
    qi = pl.program_id(2)
    lane = lax.broadcasted_iota(jnp.int32, (tq, LANES), 1)
    gi = lax.broadcasted_iota(jnp.int32, (LANES, LANES), 0)
    gsums = ((gi < DIFF_DIM).astype(BF16), (gi >= DIFF_DIM).astype(BF16))

    @pl.when(qi == 0)
    def _():
        for c in range(2):
            kmax_ref[c:c + 1, :] = _max_sq_norm(k_ref, slice(None), gsums[c], tq)

    q = q_ref[0]
    qf = q.astype(F32)
    q2 = (qf * qf).astype(BF16)
    q_parts = _diff_q_parts(q, lane)
    qx_const = qx_ref[0]
    slope = qx_const[:, 3:4] + qx_const[:, 4:5] + qx_const[:, 5:6]
    rowpos = (qi * tq + lax.broadcasted_iota(jnp.int32, (tq, LANES), 0)).astype(F32)
    q_aug = []
    for c in range(2):
        bound = jnp.sqrt(_dot(q2, gsums[c]) * kmax_ref[c:c + 1, :])
        hi, mid, lo = _split3(-slope * rowpos - bound)
        qx = jnp.where(lane == 2 * N_SPLIT, hi, jnp.where(lane == 2 * N_SPLIT + 1, mid,
                                                           jnp.where(lane == 2 * N_SPLIT + 2, lo, qx_const)))
        q_aug.append(jnp.concatenate([q_parts[c], qx.astype(BF16)], axis=1))

    def block(j, carry, key_lo=0, key_n=tq, q_lo=0, masked=False):
        ks = pl.multiple_of(j * tq + key_lo, key_n)
        k = jnp.concatenate([k_ref[0, pl.ds(ks, key_n), :], kx_ref[pl.ds(ks, key_n), :]], axis=1)
        vt = vt_ref[0, :, pl.ds(ks, key_n)]
        new = []
        for c in range(2):
            lsum, acc = carry[c]
            st = _dot_nt(k, q_aug[c][q_lo:, :])
            if masked:
                st = jnp.where(_causal_t(key_lo, key_n, q_lo, tq), st, MASKED)
            pt = jnp.exp2(st)
            part = jnp.sum(pt.reshape(key_n // SUBLANES, SUBLANES, tq - q_lo), axis=0)
            new.append((_add_cols(lsum, q_lo, part), _add_cols(acc, q_lo, _dot(vt, pt.astype(BF16)))))
        return tuple(new)

    zero = (jnp.zeros((SUBLANES, tq), F32), jnp.zeros((LANES, tq), F32))
    carry = _loop_blocks(qi, block, (zero, zero))
    for key_lo, key_n, q_lo in _diagonal_spans(tq):
        carry = block(qi, carry, key_lo, key_n, q_lo, masked=True)
    (ls1, a1), (ls2, a2) = carry
    l1 = jnp.sum(ls1, axis=0, keepdims=True)
    l2 = jnp.sum(ls2, axis=0, keepdims=True)
    ok = jnp.minimum(jnp.min(l1), jnp.min(l2)) >= UNDERFLOW_GUARD

    lf = lam_ref[...]
    lam = (jnp.exp(jnp.sum(lf[0:1] * lf[1:2], axis=-1, keepdims=True))
           - jnp.exp(jnp.sum(lf[2:3] * lf[3:4], axis=-1, keepdims=True)) + lambda_init)

    def finish(o1, o2):
        o = _rms(o1 - lam * o2, subln_ref[...]) * (1.0 - lambda_init)
        o_ref[0] = o.astype(BF16)

    @pl.when(ok)
    def _():
        finish((a1 / l1).T, (a2 / l2).T)

    @pl.when(jnp.logical_not(ok))
    def _():
        finish(*_diff_running_max(slope_ref[hd] * LOG2E, q_ref, k_ref, vt_ref, qi, tq))


def _alibi_tables(slopes, S):
    s = jnp.asarray(slopes, F32).reshape(-1, 1) * LOG2E
    terms = jnp.concatenate(_split3(s), axis=1)
    qx = jnp.concatenate([terms * float(LANES), terms, jnp.zeros((terms.shape[0], LANES - 2 * N_SPLIT), F32)], axis=1)
    j = np.arange(S)
    kx = np.zeros((S, LANES), np.float32)
    kx[:, 0:N_SPLIT] = (j // LANES)[:, None]
    kx[:, N_SPLIT:2 * N_SPLIT] = (j % LANES)[:, None]
    kx[:, 2 * N_SPLIT:3 * N_SPLIT] = 1.0
    return qx.reshape(-1, 1, LANES), jnp.asarray(kx, dtype=BF16)


def _diff_attention(slopes, q, k, vt, lambdas, subln, tq, lambda_init):
    B, S, D = q.shape
    qx, kx = _alibi_tables(slopes, S)
    return pl.pallas_call(
        functools.partial(_diff_kernel, tq=tq, lambda_init=lambda_init),
        grid=(B, DIFF_HEADS, S // tq),
        in_specs=[
            pl.BlockSpec(memory_space=pltpu.SMEM),
            pl.BlockSpec((1, 1, LANES), lambda b, h, i: (h, 0, 0)),
            pl.BlockSpec((1, tq, LANES), lambda b, h, i: (b, i, h)),
            pl.BlockSpec((1, S, LANES), lambda b, h, i: (b, 0, h)),
            _const_spec(kx.shape),
            pl.BlockSpec((1, LANES, S), lambda b, h, i: (b, h, 0)),
            pl.BlockSpec(lambdas.shape, lambda b, h, i: (0, 0)),
            pl.BlockSpec(subln.shape, lambda b, h, i: (0, 0)),
        ],
        out_specs=pl.BlockSpec((1, tq, LANES), lambda b, h, i: (b, i, h)),
        out_shape=jax.ShapeDtypeStruct((B, S, D), BF16),
        scratch_shapes=[pltpu.VMEM((8, LANES), F32)],
        compiler_params=_params(("parallel", "parallel", "arbitrary")),
        name="diff_attention",
    )(jnp.asarray(slopes, F32), qx, q, k, kx, vt, lambdas, subln)


def _mem_kv_kernel(mem_ref, norm_ref, w_ref, gain_ref, k_ref, v_ref):
    h = _rms(mem_ref[0], norm_ref[0]).astype(BF16)
    z = _dot(h, w_ref[0])
    width = MEM_HEADS * MEM_HEAD_DIM
    gain = gain_ref[0]
    for hd in range(MEM_HEADS):
        sl = slice(MEM_HEAD_DIM * hd, MEM_HEAD_DIM * (hd + 1))
        k_ref[0, 0, :, sl] = _rms(z[:, sl], gain).astype(BF16)
    v_ref[0, 0] = z[:, width:2 * width].astype(BF16)


def _mem_kv(mem, norms, w_kv, gains):
    B, M, D = mem.shape
    L = norms.shape[0]
    width = MEM_HEADS * MEM_HEAD_DIM
    out = pl.BlockSpec((1, 1, M, width), lambda l, b: (l, b, 0, 0))
    return pl.pallas_call(
        _mem_kv_kernel,
        grid=(L, B),
        in_specs=[
            pl.BlockSpec((1, M, D), lambda l, b: (b, 0, 0)),
            pl.BlockSpec((1, 1, D), lambda l, b: (l, 0, 0)),
            pl.BlockSpec((1, D, 2 * width), lambda l, b: (l, 0, 0)),
            pl.BlockSpec((1, 1, MEM_HEAD_DIM), lambda l, b: (l, 0, 0)),
        ],
        out_specs=[out, out],
        out_shape=[jax.ShapeDtypeStruct((L, B, M, width), BF16)] * 2,
        compiler_params=_params(("parallel", "parallel")),
        name="mem_kv",
    )(mem, norms, w_kv, gains)


def _post_kernel(*refs, n_attn):
    x_ref = refs[0]
    attn_refs = refs[1:1 + n_attn]
    wo_refs = refs[1 + n_attn:1 + 2 * n_attn]
    (mqn_ref, wq_ref, qg_ref, mk_ref, mv_ref, wmo_ref, fn_ref, wg_ref, wu_ref, wd_ref, o_ref) = refs[1 + 2 * n_attn:]
    x = x_ref[0]
    for a_ref, w_ref in zip(attn_refs, wo_refs):
        x = x + _dot(a_ref[0], w_ref[...])

    q = _dot(_rms(x, mqn_ref[...]).astype(BF16), wq_ref[...])
    heads = []
    for hd in range(MEM_HEADS):
        sl = slice(MEM_HEAD_DIM * hd, MEM_HEAD_DIM * (hd + 1))
        qh = (_rms(q[:, sl], qg_ref[...]) * (MEM_HEAD_DIM ** -0.5 * LOG2E)).astype(BF16)
        s = _dot_nt(qh, mk_ref[0, :, sl])
        p = jnp.exp2(s - jnp.max(s, axis=-1, keepdims=True))
        l = jnp.sum(p, axis=-1, keepdims=True)
        heads.append((_dot(p.astype(BF16), mv_ref[0, :, sl]) / l).astype(BF16))
    x = x + _dot(jnp.concatenate(heads, axis=-1), wmo_ref[...])

    hf = _rms(x, fn_ref[...]).astype(BF16)
    g = _dot(hf, wg_ref[...])
    u = _dot(hf, wu_ref[...])
    act = (g * jax.nn.sigmoid(g) * u).astype(BF16)
    o_ref[0] = x + _dot(act, wd_ref[...])


def _post_block(x, attns, wos, mqn, wq, qg, mk, mv, wmo, fn, wg, wu, wd, ts):
    B, S, D = x.shape
    n = len(attns)
    tok = lambda c: pl.BlockSpec((1, ts, c), lambda b, i: (b, i, 0))
    memspec = pl.BlockSpec((1,) + mk.shape[1:], lambda b, i: (b, 0, 0))
    consts_a = list(wos) + [mqn, wq, qg]
    consts_b = [wmo, fn, wg, wu, wd]
    return pl.pallas_call(
        functools.partial(_post_kernel, n_attn=n),
        grid=(B, S // ts),
        in_specs=([tok(D)] + [tok(a.shape[-1]) for a in attns] + [_const_spec(c.shape) for c in consts_a]
                  + [memspec, memspec] + [_const_spec(c.shape) for c in consts_b]),
        out_specs=tok(D),
        out_shape=jax.ShapeDtypeStruct((B, S, D), F32),
        compiler_params=_params(("parallel", "parallel")),
        name="post_block",
    )(x, *attns, *consts_a, mk, mv, *consts_b)


def _row(v):
    return v.reshape(1, -1).astype(F32)


def _pad_cols(w, lo, total):
    return jnp.pad(w, ((0, 0), (lo, total - lo - w.shape[1])))


def _mla_lane_perm():
    half = MLA_ROPE // 2
    lanes = np.arange(MLA_QK)
    return np.where(lanes < half, MLA_NOPE + lanes,
                    np.where(lanes < ROPE_HI_LANE, lanes - half,
                             np.where(lanes < ROPE_HI_LANE + half, lanes + half, lanes - MLA_ROPE)))


def _even_weights(w_in, w_uq, w_ukv, q_gain, k_gain):
    perm = _mla_lane_perm()
    is_nope = perm < MLA_NOPE
    pad = LANES - MLA_QK

    def head_cols(w_head, live):
        return jnp.pad(jnp.where(live[None, :], w_head[:, perm], 0.0), ((0, 0), (0, pad)))

    kr_src = jnp.concatenate([jnp.zeros((w_in.shape[0], MLA_NOPE), w_in.dtype), w_in[:, 1152:1184]], axis=1)
    w1 = jnp.concatenate([w_in[:, :1152], head_cols(kr_src, ~is_nope)], axis=1).astype(BF16)
    wuq = jnp.concatenate(
        [head_cols(w_uq[:, MLA_QK * h:MLA_QK * (h + 1)], np.ones_like(is_nope)) for h in range(MLA_HEADS)],
        axis=1).astype(BF16)
    kv_w = MLA_NOPE + MLA_V
    rope_zeros = jnp.zeros((w_ukv.shape[0], MLA_ROPE), w_ukv.dtype)
    wuk = jnp.concatenate(
        [head_cols(jnp.concatenate([w_ukv[:, kv_w * h:kv_w * h + MLA_NOPE], rope_zeros], axis=1), is_nope)
         for h in range(MLA_HEADS)], axis=1).astype(BF16)
    wuv = jnp.concatenate(
        [w_ukv[:, kv_w * h + MLA_NOPE:kv_w * (h + 1)] for h in range(MLA_HEADS)], axis=1).astype(BF16).T
    gmq = _row(jnp.pad(q_gain[perm], (0, pad)))
    gmk = _row(jnp.pad(k_gain[perm], (0, pad)))
    return w1, wuq, wuk, wuv, gmq, gmk


def _rope_freqs():
    half = MLA_ROPE // 2
    freqs = (np.float32(ROPE_THETA) ** (-np.arange(half, dtype=np.float32) / np.float32(half))).astype(np.float32)
    return jnp.asarray(freqs.reshape(half, 1))


def _ones_blocks(n):
    g = np.arange(n) // HEAD_DIM
    return jnp.asarray((g[:, None] == g[None, :]).astype(np.float32), dtype=BF16)


def kernel(x, mem, positions, mix_norm, ev_w_in, ev_swa_q_gain, ev_swa_k_gain, ev_sinks, ev_q_latent_norm,
           ev_kv_latent_norm, ev_w_uq, ev_w_ukv, ev_mla_q_gain, ev_mla_k_gain, ev_w_out, od_w_qkv, od_q_gain,
           od_k_gain, od_lambda, od_subln, od_w_out, mem_q_norm, mem_kv_norm, mem_w_q, mem_w_kv, mem_q_gain,
           mem_k_gain, mem_w_out, ffn_norm, ffn_w_gate, ffn_w_up, ffn_w_down):
    B, S, D = x.shape
    depth = mix_norm.shape[0]
    ts_proj = min(512, S)
    ts_post = min(256, S)
    tq_swa = min(256, S - SWA_WINDOW)
    tq_flash = min(1024, S)

    ones = _ones_blocks(MXU_WIDTH)
    pos3 = positions.reshape(B, 1, S)
    inv = _rope_freqs()
    mem_k, mem_v = _mem_kv(mem, mem_kv_norm.reshape(depth, 1, D), mem_w_kv.astype(BF16),
                           mem_k_gain.reshape(depth, 1, MEM_HEAD_DIM))
    diff_slopes = [2.0 ** (-8.0 * (i + 1) / DIFF_HEADS) for i in range(DIFF_HEADS)]

    for l in range(depth):
        if l % 2 == 0:
            e = l // 2
            w1, wuq, wuk, wuv, gmq, gmk = _even_weights(ev_w_in[e], ev_w_uq[e], ev_w_ukv[e], ev_mla_q_gain[e],
                                                        ev_mla_k_gain[e])
            qa, ka, va, qm, km, vm = _even_proj(
                x, pos3, _row(mix_norm[l]), w1, ones, _row(jnp.tile(ev_swa_q_gain[e], SWA_HEADS)),
                _row(jnp.tile(ev_swa_k_gain[e], SWA_KV_HEADS)), _row(ev_q_latent_norm[e]),
                _row(ev_kv_latent_norm[e]), wuq, wuk, wuv, gmq, gmk, inv, ts_proj)
            out_a = _swa_attention(ev_sinks[e].astype(F32), qa, ka, va, tq_swa)
            out_b = _mla_attention(qm, km, vm, tq_flash)
            wo = ev_w_out[e].astype(BF16)
            attns = [out_a, out_b]
            wos = [wo[:512], wo[512:]]
        else:
            o = l // 2
            lambda_init = 0.8 - 0.6 * math.exp(-0.3 * l)
            n_grp = DIFF_HEADS * 2
            w_qkv = od_w_qkv[o].astype(BF16)
            q, k, vt = _odd_proj(x, _row(mix_norm[l]), w_qkv[:, :2 * D], w_qkv[:, 2 * D:].T, ones,
                                 _row(jnp.tile(od_q_gain[o], n_grp // 2)), _row(jnp.tile(od_k_gain[o], n_grp // 2)),
                                 ts_proj)
            out_d = _diff_attention(diff_slopes, q, k, vt, od_lambda[o].astype(F32), _row(od_subln[o]), tq_flash,
                                    lambda_init)
            attns = [out_d]
            wos = [od_w_out[o].astype(BF16)]
        x = _post_block(x, attns, wos, _row(mem_q_norm[l]), mem_w_q[l].astype(BF16), _row(mem_q_gain[l]),
                        mem_k[l], mem_v[l], mem_w_out[l].astype(BF16), _row(ffn_norm[l]),
                        ffn_w_gate[l].astype(BF16), ffn_w_up[l].astype(BF16), ffn_w_down[l].astype(BF16), ts_post)
    return x
```

```python
import functools
import math

import numpy as np
import jax
import jax.numpy as jnp
from jax import lax
from jax.experimental import pallas as pl
from jax.experimental.pallas import tpu as pltpu

F32 = jnp.float32
BF16 = jnp.bfloat16

EPS = 1e-6
MASKED = -1e30
LOG2E = 1.4426950408889634
ROPE_THETA = 10000.0

LANES = 128
SUBLANES = 8
MXU_WIDTH = 256
HEAD_DIM = 64
SWA_HEADS = 8
SWA_KV_HEADS = 2
SWA_WINDOW = 128
MLA_HEADS = 8
MLA_NOPE = 64
MLA_ROPE = 32
MLA_QK = MLA_NOPE + MLA_ROPE
MLA_V = 64
ROPE_HI_LANE = 64
DIFF_HEADS = 8
DIFF_DIM = 64
MEM_HEADS = 4
MEM_HEAD_DIM = 128

VMEM_LIMIT = 56 * 1024 * 1024

UNDERFLOW_GUARD = 2.0 ** -80
N_SPLIT = 3


def _params(sem):
    return pltpu.CompilerParams(dimension_semantics=sem, vmem_limit_bytes=VMEM_LIMIT)


def _const_spec(shape):
    nd = len(shape)
    return pl.BlockSpec(shape, lambda *_: (0,) * nd, pipeline_mode=pl.Buffered(1))


def _rms(x, g):
    ms = jnp.mean(x * x, axis=-1, keepdims=True)
    return x * lax.rsqrt(ms + EPS) * g


def _dot(a, b):
    return jnp.dot(a, b, preferred_element_type=F32)


def _dot_nt(a, b):
    return lax.dot_general(a, b, (((1,), (1,)), ((), ())), preferred_element_type=F32)


def _group_sumsq(x, ones_blk):
    x2 = x * x
    hi = x2.astype(BF16)
    lo = (x2 - hi.astype(F32)).astype(BF16)
    return _dot(hi, ones_blk) + _dot(lo, ones_blk)


def _group_norm64(x, ones_blk, gain):
    chunk = min(ones_blk.shape[0], x.shape[1])
    ssq = jnp.concatenate([_group_sumsq(x[:, c:c + chunk], ones_blk[0:chunk, 0:chunk])
                           for c in range(0, x.shape[1], chunk)], axis=1)
    return x * lax.rsqrt(ssq * (1.0 / HEAD_DIM) + EPS) * gain


def _even_proj_kernel(x_ref, pos_ref, gmix_ref, w1_ref, ones_ref, gq_ref, gk_ref, qln_ref, kvln_ref,
                      wuq_ref, wuk_ref, wuv_ref, gmq_ref, gmk_ref, qshift_ref, inv_ref,
                      qa_ref, ka_ref, va_ref, qm_ref, km_ref, vm_ref):
    x = x_ref[0]
    h = _rms(x, gmix_ref[...]).astype(BF16)
    z = _dot(h, w1_ref[...])
    ones = ones_ref[...]
    qa = _group_norm64(z[:, 0:512], ones, gq_ref[...]) * (HEAD_DIM ** -0.5 * LOG2E)
    qa_ref[0] = qa.astype(BF16)
    ka = _group_norm64(z[:, 512:640], ones[0:128, 0:128], gk_ref[...])
    ka_ref[0] = ka.astype(BF16)
    va_ref[0] = z[:, 640:768].astype(BF16)

    cqn = _rms(z[:, 768:1024], qln_ref[...]).astype(BF16)
    ckvn = _rms(z[:, 1024:1152], kvln_ref[...]).astype(BF16)
    kr = z[:, 1152:1280]
    qf = _dot(cqn, wuq_ref[...])
    kn = _dot(ckvn, wuk_ref[...])
    vm_ref[0] = _dot_nt(wuv_ref[...], ckvn).astype(BF16)

    half = MLA_ROPE // 2
    ang = inv_ref[...] * pos_ref[0].astype(F32)
    pad = jnp.zeros((LANES - half, ang.shape[1]), F32)
    cos = jnp.concatenate([jnp.cos(ang), pad], axis=0).T
    sin = jnp.concatenate([jnp.sin(ang), pad], axis=0).T
    lane = lax.broadcasted_iota(jnp.int32, cos.shape, 1)
    in_lo = lane < half
    in_hi = (lane >= ROPE_HI_LANE) & (lane < ROPE_HI_LANE + half)
    cos_t = jnp.where(in_lo, cos, jnp.where(in_hi, pltpu.roll(cos, ROPE_HI_LANE, 1), 1.0))
    sin_t = jnp.where(in_lo, -sin, jnp.where(in_hi, pltpu.roll(sin, ROPE_HI_LANE, 1), 0.0))

    pair = 2 * LANES
    gi = lax.broadcasted_iota(jnp.int32, (pair, pair), 0) // LANES
    gj = lax.broadcasted_iota(jnp.int32, (pair, pair), 1) // LANES
    head_ones = (gi == gj).astype(BF16)

    def norm_rope(xp, gain):
        r = lax.rsqrt(_group_sumsq(xp, head_ones) * (1.0 / MLA_QK) + EPS)
        xn = xp * r * gain
        halves = []
        for t in range(2):
            xh = xn[:, LANES * t:LANES * (t + 1)]
            halves.append(xh * cos_t + pltpu.roll(xh, ROPE_HI_LANE, 1) * sin_t)
        return halves

    gmq = jnp.concatenate([gmq_ref[...]] * 2, axis=1)
    gmk = jnp.concatenate([gmk_ref[...]] * 2, axis=1)
    kr2 = jnp.concatenate([kr, kr], axis=1)
    shift_lanes = (lane >= MLA_QK) & (lane < MLA_QK + N_SPLIT)
    qshift = qshift_ref[...]
    for hp in range(MLA_HEADS // 2):
        sl = slice(pair * hp, pair * (hp + 1))
        q_heads = norm_rope(qf[:, sl], gmq)
        k_heads = norm_rope(kn[:, sl] + kr2, gmk)
        for t in range(2):
            hl = slice(pair * hp + LANES * t, pair * hp + LANES * (t + 1))
            qm_ref[0, :, hl] = jnp.where(shift_lanes, qshift, q_heads[t] * (MLA_QK ** -0.5 * LOG2E)).astype(BF16)
            km_ref[0, :, hl] = jnp.where(shift_lanes, 1.0, k_heads[t]).astype(BF16)


def _even_proj(x, pos3, gmix, w1, ones, gq, gk, qln, kvln, wuq, wuk, wuv, gmq, gmk, qshift, inv, ts):
    B, S, D = x.shape
    grid = (B, S // ts)
    tok = lambda c: pl.BlockSpec((1, ts, c), lambda b, i: (b, i, 0))
    consts = [gmix, w1, ones, gq, gk, qln, kvln, wuq, wuk, wuv, gmq, gmk, qshift, inv]
    out_cols = [512, 128, 128, 1024, 1024]
    v_rows = MLA_HEADS * MLA_V
    return pl.pallas_call(
        _even_proj_kernel,
        grid=grid,
        in_specs=([tok(D), pl.BlockSpec((1, 1, ts), lambda b, i: (b, 0, i))]
                  + [_const_spec(c.shape) for c in consts]),
        out_specs=[tok(c) for c in out_cols] + [pl.BlockSpec((1, v_rows, ts), lambda b, i: (b, 0, i))],
        out_shape=([jax.ShapeDtypeStruct((B, S, c), BF16) for c in out_cols]
                   + [jax.ShapeDtypeStruct((B, v_rows, S), BF16)]),
        compiler_params=_params(("parallel", "parallel")),
        name="even_proj",
    )(x, pos3, *consts)


def _swa_kernel(sink_ref, q_ref, k_ref, v_ref, o_ref, *, tq):
    qi = pl.program_id(1)
    win = tq + SWA_WINDOW
    wstart = pl.multiple_of(jnp.maximum(qi * tq - SWA_WINDOW, 0), SWA_WINDOW)
    kw = k_ref[0, pl.ds(wstart, win), :]
    vw = v_ref[0, pl.ds(wstart, win), :]
    row = lax.broadcasted_iota(jnp.int32, (tq, win), 0)
    col = lax.broadcasted_iota(jnp.int32, (tq, win), 1)
    dist = (qi * tq + row) - (wstart + col)
    valid = (dist >= 0) & (dist < SWA_WINDOW)
    distf = dist.astype(F32)
    lane = lax.broadcasted_iota(jnp.int32, (tq, LANES), 1)
    group = SWA_HEADS // SWA_KV_HEADS
    for cb in range(SWA_HEADS // 2):
        qq = q_ref[0, :, LANES * cb:LANES * (cb + 1)].astype(F32)
        halves = []
        for e in range(2):
            hd = 2 * cb + e
            hk = hd // group
            xq = qq if e == hk else pltpu.roll(qq, HEAD_DIM, 1)
            in_kv = (lane >= HEAD_DIM * hk) & (lane < HEAD_DIM * (hk + 1))
            xq = jnp.where(in_kv, xq, 0.0).astype(BF16)
            s = _dot_nt(xq, kw) - (2.0 ** (-8.0 * (hd + 1) / SWA_HEADS) * LOG2E) * distf
            s = jnp.where(valid, s, MASKED)
            sink = sink_ref[hd] * LOG2E
            m = jnp.maximum(jnp.max(s, axis=-1, keepdims=True), sink)
            p = jnp.exp2(s - m)
            l = jnp.sum(p, axis=-1, keepdims=True) + jnp.exp2(sink - m)
            o = _dot(p.astype(BF16), vw) / l
            halves.append(o if e == hk else pltpu.roll(o, HEAD_DIM, 1))
        o_ref[0, :, LANES * cb:LANES * (cb + 1)] = jnp.where(lane < HEAD_DIM, halves[0], halves[1]).astype(BF16)


def _swa_attention(sinks, qa, ka, va, tq):
    B, S, _ = qa.shape
    return pl.pallas_call(
        functools.partial(_swa_kernel, tq=tq),
        grid=(B, S // tq),
        in_specs=[
            pl.BlockSpec(memory_space=pltpu.SMEM),
            pl.BlockSpec((1, tq, 512), lambda b, i: (b, i, 0)),
            pl.BlockSpec((1, S, LANES), lambda b, i: (b, 0, 0)),
            pl.BlockSpec((1, S, LANES), lambda b, i: (b, 0, 0)),
        ],
        out_specs=pl.BlockSpec((1, tq, 512), lambda b, i: (b, i, 0)),
        out_shape=jax.ShapeDtypeStruct((B, S, 512), BF16),
        compiler_params=_params(("parallel", "arbitrary")),
        name="swa_attention",
    )(sinks, qa, ka, va)


def _split3(x):
    hi = x.astype(BF16).astype(F32)
    r = x - hi
    mid = r.astype(BF16).astype(F32)
    lo = (r - mid).astype(BF16).astype(F32)
    return hi, mid, lo


def _score_bound(q_gain, k_gain, dim, scale):
    return dim * scale * jnp.max(jnp.abs(q_gain)) * jnp.max(jnp.abs(k_gain))


BLOCKS_PER_TRIP = 4


def _loop_blocks(n, step, init):
    def trip(t, c):
        for u in range(BLOCKS_PER_TRIP):
            c = step(BLOCKS_PER_TRIP * t + u, c)
        return c

    full = n // BLOCKS_PER_TRIP
    carry = lax.fori_loop(0, full, trip, init)
    return lax.fori_loop(full * BLOCKS_PER_TRIP, n, step, carry)


def _add_cols(acc, lo, part):
    if lo == 0:
        return acc + part
    return acc + jnp.concatenate([jnp.zeros((acc.shape[0], lo), acc.dtype), part], axis=1)


def _diagonal_spans(tq):
    h = tq // 2
    return ((0, h, 0), (h, h, h)) if h % MXU_WIDTH == 0 else ((0, tq, 0),)


def _causal_t(key_lo, key_n, q_lo, tq):
    shape = (key_n, tq - q_lo)
    return (key_lo + lax.broadcasted_iota(jnp.int32, shape, 0)) <= (q_lo + lax.broadcasted_iota(jnp.int32, shape, 1))


def _causal_mask(tq):
    row = lax.broadcasted_iota(jnp.int32, (tq, tq), 0)
    col = lax.broadcasted_iota(jnp.int32, (tq, tq), 1)
    return row >= col


def _mla_running_max(q_ref, k_ref, vt_ref, e, qi, tq):
    causal = _causal_mask(tq)
    q = q_ref[0, :, LANES * e:LANES * (e + 1)]

    def step(j, carry, masked):
        m, l, acc = carry
        ks = pl.multiple_of(j * tq, tq)
        k = k_ref[0, pl.ds(ks, tq), LANES * e:LANES * (e + 1)]
        vt = vt_ref[0, :, pl.ds(ks, tq)]
        s = _dot_nt(q, k)
        if masked:
            s = jnp.where(causal, s, MASKED)
        m_new = jnp.maximum(m, jnp.max(s, axis=-1, keepdims=True))
        alpha = jnp.exp2(m - m_new)
        p = jnp.exp2(s - m_new)
        l = alpha * l + jnp.sum(p, axis=-1, keepdims=True)
        acc = alpha * acc + _dot_nt(p.astype(BF16), vt)
        return m_new, l, acc

    init = (jnp.full((tq, 1), MASKED, F32), jnp.zeros((tq, 1), F32), jnp.zeros((tq, LANES), F32))
    carry = lax.fori_loop(0, qi, lambda j, c: step(j, c, False), init)
    m, l, acc = step(qi, carry, True)
    return acc / l


def _mla_kernel(q_ref, k_ref, vt_ref, o_ref, *, tq):
    qi = pl.program_id(2)
    lane = lax.broadcasted_iota(jnp.int32, (tq, LANES), 1)
    q_aug = [q_ref[0, :, LANES * e:LANES * (e + 1)] for e in range(2)]

    def block(j, carry, key_lo=0, key_n=tq, q_lo=0, masked=False):
        ks = pl.multiple_of(j * tq + key_lo, key_n)
        new = []
        for e in range(2):
            lsum, acc = carry[e]
            st = _dot_nt(k_ref[0, pl.ds(ks, key_n), LANES * e:LANES * (e + 1)], q_aug[e][q_lo:, :])
            if masked:
                st = jnp.where(_causal_t(key_lo, key_n, q_lo, tq), st, MASKED)
            pt = jnp.exp2(st)
            part = jnp.sum(pt.reshape(key_n // SUBLANES, SUBLANES, tq - q_lo), axis=0)
            pv = _dot(vt_ref[0, MLA_V * e:MLA_V * (e + 1), pl.ds(ks, key_n)], pt.astype(BF16))
            new.append((_add_cols(lsum, q_lo, part), _add_cols(acc, q_lo, pv)))
        return tuple(new)

    zero = (jnp.zeros((SUBLANES, tq), F32), jnp.zeros((MLA_V, tq), F32))
    carry = _loop_blocks(qi, block, (zero, zero))
    for key_lo, key_n, q_lo in _diagonal_spans(tq):
        carry = block(qi, carry, key_lo, key_n, q_lo, masked=True)
    sums = [jnp.sum(lsum, axis=0, keepdims=True) for lsum, _ in carry]
    outs = [acc / l for (_, acc), l in zip(carry, sums)]
    ok = jnp.minimum(jnp.min(sums[0]), jnp.min(sums[1])) >= UNDERFLOW_GUARD

    @pl.when(ok)
    def _():
        o_ref[0] = jnp.concatenate(outs, axis=0).T.astype(BF16)

    @pl.when(jnp.logical_not(ok))
    def _():
        slow = [_mla_running_max(q_ref, k_ref, vt_ref, e, qi, tq) for e in range(2)]
        o_ref[0] = jnp.where(lane < MLA_V, slow[0], slow[1]).astype(BF16)


def _mla_attention(qm, km, vm, tq):
    B, S, _ = qm.shape
    return pl.pallas_call(
        functools.partial(_mla_kernel, tq=tq),
        grid=(B, MLA_HEADS // 2, S // tq),
        in_specs=[
            pl.BlockSpec((1, tq, 2 * LANES), lambda b, h, i: (b, i, h)),
            pl.BlockSpec((1, S, 2 * LANES), lambda b, h, i: (b, 0, h)),
            pl.BlockSpec((1, 2 * MLA_V, S), lambda b, h, i: (b, h, 0)),
        ],
        out_specs=pl.BlockSpec((1, tq, LANES), lambda b, h, i: (b, i, h)),
        out_shape=jax.ShapeDtypeStruct((B, S, MLA_HEADS * MLA_V), BF16),
        compiler_params=_params(("parallel", "parallel", "arbitrary")),
        name="mla_attention",
    )(qm, km, vm)


def _odd_proj_kernel(x_ref, gmix_ref, w_ref, wvt_ref, ones_ref, gq_ref, gk_ref, q_ref, k_ref, vt_ref):
    x = x_ref[0]
    h = _rms(x, gmix_ref[...]).astype(BF16)
    z = _dot(h, w_ref[...])
    ones = ones_ref[...]
    width = DIFF_HEADS * 2 * DIFF_DIM
    for c in range(width // 512):
        sl = slice(512 * c, 512 * (c + 1))
        q = _group_norm64(z[:, sl], ones, gq_ref[...]) * (DIFF_DIM ** -0.5 * LOG2E)
        q_ref[0, :, sl] = q.astype(BF16)
        k = _group_norm64(z[:, width + 512 * c:width + 512 * (c + 1)], ones, gk_ref[...])
        k_ref[0, :, sl] = k.astype(BF16)
    vt_ref[0] = _dot_nt(wvt_ref[...], h).astype(BF16)


def _odd_proj(x, gmix, w_qk, w_vt, ones, gq, gk, ts):
    B, S, D = x.shape
    tok = pl.BlockSpec((1, ts, D), lambda b, i: (b, i, 0))
    consts = [gmix, w_qk, w_vt, ones, gq, gk]
    return pl.pallas_call(
        _odd_proj_kernel,
        grid=(B, S // ts),
        in_specs=[tok] + [_const_spec(c.shape) for c in consts],
        out_specs=[tok, tok, pl.BlockSpec((1, D, ts), lambda b, i: (b, 0, i))],
        out_shape=[jax.ShapeDtypeStruct((B, S, D), BF16)] * 2 + [jax.ShapeDtypeStruct((B, D, S), BF16)],
        compiler_params=_params(("parallel", "parallel")),
        name="odd_proj",
    )(x, *consts)


def _diff_q_parts(q, lane):
    zero = jnp.zeros_like(q)
    return jnp.where(lane < DIFF_DIM, q, zero), jnp.where(lane >= DIFF_DIM, q, zero)


def _diff_running_max(slope, q_ref, k_ref, vt_ref, qi, tq):
    row = lax.broadcasted_iota(jnp.int32, (tq, tq), 0)
    col = lax.broadcasted_iota(jnp.int32, (tq, tq), 1)
    causal = row >= col
    bias = slope * (col - row).astype(F32)
    lane = lax.broadcasted_iota(jnp.int32, (tq, LANES), 1)
    q_parts = _diff_q_parts(q_ref[0], lane)

    def step(j, carry, masked):
        ks = pl.multiple_of(j * tq, tq)
        k = k_ref[0, pl.ds(ks, tq), :]
        vt = vt_ref[0, :, pl.ds(ks, tq)]
        offset = slope * ((j - qi) * tq).astype(F32)
        new = []
        for c in range(2):
            m, l, acc = carry[c]
            s = _dot_nt(q_parts[c], k) + bias
            if masked:
                s = jnp.where(causal, s, MASKED)
            m_new = jnp.maximum(m, jnp.max(s, axis=-1, keepdims=True) + offset)
            alpha = jnp.exp2(m - m_new)
            p = jnp.exp2(s - (m_new - offset))
            l = alpha * l + jnp.sum(p, axis=-1, keepdims=True)
            acc = alpha * acc + _dot_nt(p.astype(BF16), vt)
            new.append((m_new, l, acc))
        return tuple(new)

    one = (jnp.full((tq, 1), MASKED, F32), jnp.zeros((tq, 1), F32), jnp.zeros((tq, LANES), F32))
    carry = lax.fori_loop(0, qi, lambda j, c: step(j, c, False), (one, one))
    (m1, l1, a1), (m2, l2, a2) = step(qi, carry, True)
    return a1 / l1, a2 / l2


def _diff_kernel(slope_ref, qx_ref, qpos_ref, q_ref, k_ref, kx_ref, vt_ref, lam_ref, subln_ref, o_ref, *, tq,
                 lambda_init):
    hd = pl.program_id(1)
    qi = pl.program_id(2)
    lane = lax.broadcasted_iota(jnp.int32, (tq, LANES), 1)
    q_parts = _diff_q_parts(q_ref[0], lane)
    head_row = qx_ref[0]
    qx = jnp.where(lane < 3 * N_SPLIT, head_row,
                   qpos_ref[...].astype(F32) * head_row[:, SLOPE_LANE:SLOPE_LANE + 1]).astype(BF16)
    q_aug = [jnp.concatenate([part, qx], axis=1) for part in q_parts]

    def block(j, carry, key_lo=0, key_n=tq, q_lo=0, masked=False):
        ks = pl.multiple_of(j * tq + key_lo, key_n)
        k = jnp.concatenate([k_ref[0, pl.ds(ks, key_n), :], kx_ref[pl.ds(ks, key_n), :]], axis=1)
        vt = vt_ref[0, :, pl.ds(ks, key_n)]
        new = []
        for c in range(2):
            lsum, acc = carry[c]
            st = _dot_nt(k, q_aug[c][q_lo:, :])
            if masked:
                st = jnp.where(_causal_t(key_lo, key_n, q_lo, tq), st, MASKED)
            pt = jnp.exp2(st)
            part = jnp.sum(pt.reshape(key_n // SUBLANES, SUBLANES, tq - q_lo), axis=0)
            new.append((_add_cols(lsum, q_lo, part), _add_cols(acc, q_lo, _dot(vt, pt.astype(BF16)))))
        return tuple(new)

    zero = (jnp.zeros((SUBLANES, tq), F32), jnp.zeros((LANES, tq), F32))
    carry = _loop_blocks(qi, block, (zero, zero))
    for key_lo, key_n, q_lo in _diagonal_spans(tq):
        carry = block(qi, carry, key_lo, key_n, q_lo, masked=True)
    (ls1, a1), (ls2, a2) = carry
    l1 = jnp.sum(ls1, axis=0, keepdims=True)
    l2 = jnp.sum(ls2, axis=0, keepdims=True)
    ok = jnp.minimum(jnp.min(l1), jnp.min(l2)) >= UNDERFLOW_GUARD

    lf = lam_ref[...]
    lam = (jnp.exp(jnp.sum(lf[0:1] * lf[1:2], axis=-1, keepdims=True))
           - jnp.exp(jnp.sum(lf[2:3] * lf[3:4], axis=-1, keepdims=True)) + lambda_init)

    def finish(o1, o2):
        o = _rms(o1 - lam * o2, subln_ref[...]) * (1.0 - lambda_init)
        o_ref[0] = o.astype(BF16)

    @pl.when(ok)
    def _():
        finish((a1 / l1).T, (a2 / l2).T)

    @pl.when(jnp.logical_not(ok))
    def _():
        finish(*_diff_running_max(slope_ref[hd] * LOG2E, q_ref, k_ref, vt_ref, qi, tq))


SLOPE_LANE = 5 * N_SPLIT


def _alibi_tables(slopes, bound, S):
    assert all(math.log2(s) == int(math.log2(s)) for s in slopes), "ALiBi slopes must be powers of two"
    n = N_SPLIT
    log2e_terms = jnp.concatenate(_split3(jnp.full((1, 1), LOG2E, F32)), axis=1)
    s = jnp.asarray(slopes, F32).reshape(-1, 1)
    head = jnp.concatenate(
        [s * log2e_terms * float(LANES), s * log2e_terms, jnp.broadcast_to(_shift_row(bound, 0)[:, :n], (len(slopes), n)),
         jnp.zeros((len(slopes), SLOPE_LANE - 3 * n), F32), s, jnp.zeros((len(slopes), LANES - SLOPE_LANE - 1), F32)],
        axis=1)
    idx = np.arange(S)
    hi, lo = (idx // LANES)[:, None].astype(np.float32), (idx % LANES)[:, None].astype(np.float32)
    qpos = np.zeros((S, LANES), np.float32)
    qpos[:, 3 * n:4 * n] = hi * LANES
    qpos[:, 4 * n:5 * n] = lo
    kx = jnp.concatenate(
        [jnp.asarray(np.concatenate([np.repeat(hi, n, 1), np.repeat(lo, n, 1), np.ones((S, n), np.float32)], axis=1)),
         jnp.broadcast_to(-log2e_terms, (S, n)), jnp.broadcast_to(-log2e_terms, (S, n)),
         jnp.zeros((S, LANES - 5 * n), F32)], axis=1)
    return head.reshape(-1, 1, LANES), jnp.asarray(qpos, dtype=BF16), kx.astype(BF16)


def _diff_attention(slopes, bound, q, k, vt, lambdas, subln, tq, lambda_init):
    B, S, D = q.shape
    qx, qpos, kx = _alibi_tables(slopes, bound, S)
    return pl.pallas_call(
        functools.partial(_diff_kernel, tq=tq, lambda_init=lambda_init),
        grid=(B, DIFF_HEADS, S // tq),
        in_specs=[
            pl.BlockSpec(memory_space=pltpu.SMEM),
            pl.BlockSpec((1, 1, LANES), lambda b, h, i: (h, 0, 0)),
            pl.BlockSpec((tq, LANES), lambda b, h, i: (i, 0)),
            pl.BlockSpec((1, tq, LANES), lambda b, h, i: (b, i, h)),
            pl.BlockSpec((1, S, LANES), lambda b, h, i: (b, 0, h)),
            _const_spec(kx.shape),
            pl.BlockSpec((1, LANES, S), lambda b, h, i: (b, h, 0)),
            pl.BlockSpec(lambdas.shape, lambda b, h, i: (0, 0)),
            pl.BlockSpec(subln.shape, lambda b, h, i: (0, 0)),
        ],
        out_specs=pl.BlockSpec((1, tq, LANES), lambda b, h, i: (b, i, h)),
        out_shape=jax.ShapeDtypeStruct((B, S, D), BF16),
        compiler_params=_params(("parallel", "parallel", "arbitrary")),
        name="diff_attention",
    )(jnp.asarray(slopes, F32), qx, qpos, q, k, kx, vt, lambdas, subln)


def _mem_kv_kernel(mem_ref, norm_ref, w_ref, gain_ref, k_ref, v_ref):
    h = _rms(mem_ref[0], norm_ref[0]).astype(BF16)
    z = _dot(h, w_ref[0])
    width = MEM_HEADS * MEM_HEAD_DIM
    gain = gain_ref[0]
    for hd in range(MEM_HEADS):
        sl = slice(MEM_HEAD_DIM * hd, MEM_HEAD_DIM * (hd + 1))
        k_ref[0, 0, :, sl] = _rms(z[:, sl], gain).astype(BF16)
    v_ref[0, 0] = z[:, width:2 * width].astype(BF16)


def _mem_kv(mem, norms, w_kv, gains):
    B, M, D = mem.shape
    L = norms.shape[0]
    width = MEM_HEADS * MEM_HEAD_DIM
    out = pl.BlockSpec((1, 1, M, width), lambda l, b: (l, b, 0, 0))
    return pl.pallas_call(
        _mem_kv_kernel,
        grid=(L, B),
        in_specs=[
            pl.BlockSpec((1, M, D), lambda l, b: (b, 0, 0)),
            pl.BlockSpec((1, 1, D), lambda l, b: (l, 0, 0)),
            pl.BlockSpec((1, D, 2 * width), lambda l, b: (l, 0, 0)),
            pl.BlockSpec((1, 1, MEM_HEAD_DIM), lambda l, b: (l, 0, 0)),
        ],
        out_specs=[out, out],
        out_shape=[jax.ShapeDtypeStruct((L, B, M, width), BF16)] * 2,
        compiler_params=_params(("parallel", "parallel")),
        name="mem_kv",
    )(mem, norms, w_kv, gains)


def _post_kernel(*refs, n_attn):
    x_ref = refs[0]
    attn_refs = refs[1:1 + n_attn]
    wo_refs = refs[1 + n_attn:1 + 2 * n_attn]
    (mqn_ref, wq_ref, qg_ref, mk_ref, mv_ref, wmo_ref, fn_ref, wg_ref, wu_ref, wd_ref, o_ref) = refs[1 + 2 * n_attn:]
    x = x_ref[0]
    for a_ref, w_ref in zip(attn_refs, wo_refs):
        x = x + _dot(a_ref[0], w_ref[...])

    q = _dot(_rms(x, mqn_ref[...]).astype(BF16), wq_ref[...])
    heads = []
    for hd in range(MEM_HEADS):
        sl = slice(MEM_HEAD_DIM * hd, MEM_HEAD_DIM * (hd + 1))
        qh = (_rms(q[:, sl], qg_ref[...]) * (MEM_HEAD_DIM ** -0.5 * LOG2E)).astype(BF16)
        s = _dot_nt(qh, mk_ref[0, :, sl])
        p = jnp.exp2(s - jnp.max(s, axis=-1, keepdims=True))
        l = jnp.sum(p, axis=-1, keepdims=True)
        heads.append((_dot(p.astype(BF16), mv_ref[0, :, sl]) / l).astype(BF16))
    x = x + _dot(jnp.concatenate(heads, axis=-1), wmo_ref[...])

    hf = _rms(x, fn_ref[...]).astype(BF16)
    g = _dot(hf, wg_ref[...])
    u = _dot(hf, wu_ref[...])
    act = (g * jax.nn.sigmoid(g) * u).astype(BF16)
    o_ref[0] = x + _dot(act, wd_ref[...])


def _post_block(x, attns, wos, mqn, wq, qg, mk, mv, wmo, fn, wg, wu, wd, ts):
    B, S, D = x.shape
    n = len(attns)
    tok = lambda c: pl.BlockSpec((1, ts, c), lambda b, i: (b, i, 0))
    memspec = pl.BlockSpec((1,) + mk.shape[1:], lambda b, i: (b, 0, 0))
    consts_a = list(wos) + [mqn, wq, qg]
    consts_b = [wmo, fn, wg, wu, wd]
    return pl.pallas_call(
        functools.partial(_post_kernel, n_attn=n),
        grid=(B, S // ts),
        in_specs=([tok(D)] + [tok(a.shape[-1]) for a in attns] + [_const_spec(c.shape) for c in consts_a]
                  + [memspec, memspec] + [_const_spec(c.shape) for c in consts_b]),
        out_specs=tok(D),
        out_shape=jax.ShapeDtypeStruct((B, S, D), F32),
        compiler_params=_params(("parallel", "parallel")),
        name="post_block",
    )(x, *attns, *consts_a, mk, mv, *consts_b)


def _row(v):
    return v.reshape(1, -1).astype(F32)


def _shift_row(bound, first_lane):
    terms = jnp.stack(_split3(-jnp.asarray(bound, F32))).reshape(1, N_SPLIT)
    return jnp.pad(terms, ((0, 0), (first_lane, LANES - first_lane - N_SPLIT)))


def _pad_cols(w, lo, total):
    return jnp.pad(w, ((0, 0), (lo, total - lo - w.shape[1])))


def _mla_lane_perm():
    half = MLA_ROPE // 2
    lanes = np.arange(MLA_QK)
    return np.where(lanes < half, MLA_NOPE + lanes,
                    np.where(lanes < ROPE_HI_LANE, lanes - half,
                             np.where(lanes < ROPE_HI_LANE + half, lanes + half, lanes - MLA_ROPE)))


def _even_weights(w_in, w_uq, w_ukv, q_gain, k_gain):
    perm = _mla_lane_perm()
    is_nope = perm < MLA_NOPE
    pad = LANES - MLA_QK

    def head_cols(w_head, live):
        return jnp.pad(jnp.where(live[None, :], w_head[:, perm], 0.0), ((0, 0), (0, pad)))

    kr_src = jnp.concatenate([jnp.zeros((w_in.shape[0], MLA_NOPE), w_in.dtype), w_in[:, 1152:1184]], axis=1)
    w1 = jnp.concatenate([w_in[:, :1152], head_cols(kr_src, ~is_nope)], axis=1).astype(BF16)
    wuq = jnp.concatenate(
        [head_cols(w_uq[:, MLA_QK * h:MLA_QK * (h + 1)], np.ones_like(is_nope)) for h in range(MLA_HEADS)],
        axis=1).astype(BF16)
    kv_w = MLA_NOPE + MLA_V
    rope_zeros = jnp.zeros((w_ukv.shape[0], MLA_ROPE), w_ukv.dtype)
    wuk = jnp.concatenate(
        [head_cols(jnp.concatenate([w_ukv[:, kv_w * h:kv_w * h + MLA_NOPE], rope_zeros], axis=1), is_nope)
         for h in range(MLA_HEADS)], axis=1).astype(BF16)
    wuv = jnp.concatenate(
        [w_ukv[:, kv_w * h + MLA_NOPE:kv_w * (h + 1)] for h in range(MLA_HEADS)], axis=1).astype(BF16).T
    gmq = _row(jnp.pad(q_gain[perm], (0, pad)))
    gmk = _row(jnp.pad(k_gain[perm], (0, pad)))
    return w1, wuq, wuk, wuv, gmq, gmk


def _rope_freqs():
    half = MLA_ROPE // 2
    freqs = (np.float32(ROPE_THETA) ** (-np.arange(half, dtype=np.float32) / np.float32(half))).astype(np.float32)
    return jnp.asarray(freqs.reshape(half, 1))


def _ones_blocks(n):
    g = np.arange(n) // HEAD_DIM
    return jnp.asarray((g[:, None] == g[None, :]).astype(np.float32), dtype=BF16)


def kernel(x, mem, positions, mix_norm, ev_w_in, ev_swa_q_gain, ev_swa_k_gain, ev_sinks, ev_q_latent_norm,
           ev_kv_latent_norm, ev_w_uq, ev_w_ukv, ev_mla_q_gain, ev_mla_k_gain, ev_w_out, od_w_qkv, od_q_gain,
           od_k_gain, od_lambda, od_subln, od_w_out, mem_q_norm, mem_kv_norm, mem_w_q, mem_w_kv, mem_q_gain,
           mem_k_gain, mem_w_out, ffn_norm, ffn_w_gate, ffn_w_up, ffn_w_down):
    B, S, D = x.shape
    depth = mix_norm.shape[0]
    ts_proj = min(512, S)
    ts_post = min(256, S)
    tq_swa = min(256, S - SWA_WINDOW)
    tq_flash = min(1024, S)

    ones = _ones_blocks(MXU_WIDTH)
    pos3 = positions.reshape(B, 1, S)
    inv = _rope_freqs()
    mem_k, mem_v = _mem_kv(mem, mem_kv_norm.reshape(depth, 1, D), mem_w_kv.astype(BF16),
                           mem_k_gain.reshape(depth, 1, MEM_HEAD_DIM))
    diff_slopes = [2.0 ** (-8.0 * (i + 1) / DIFF_HEADS) for i in range(DIFF_HEADS)]

    for l in range(depth):
        if l % 2 == 0:
            e = l // 2
            w1, wuq, wuk, wuv, gmq, gmk = _even_weights(ev_w_in[e], ev_w_uq[e], ev_w_ukv[e], ev_mla_q_gain[e],
                                                        ev_mla_k_gain[e])
            qa, ka, va, qm, km, vm = _even_proj(
                x, pos3, _row(mix_norm[l]), w1, ones, _row(jnp.tile(ev_swa_q_gain[e], SWA_HEADS)),
                _row(jnp.tile(ev_swa_k_gain[e], SWA_KV_HEADS)), _row(ev_q_latent_norm[e]),
                _row(ev_kv_latent_norm[e]), wuq, wuk, wuv, gmq, gmk,
                _shift_row(_score_bound(ev_mla_q_gain[e], ev_mla_k_gain[e], MLA_QK, MLA_QK ** -0.5 * LOG2E), MLA_QK),
                inv, ts_proj)
            out_a = _swa_attention(ev_sinks[e].astype(F32), qa, ka, va, tq_swa)
            out_b = _mla_attention(qm, km, vm, tq_flash)
            wo = ev_w_out[e].astype(BF16)
            attns = [out_a, out_b]
            wos = [wo[:512], wo[512:]]
        else:
            o = l // 2
            lambda_init = 0.8 - 0.6 * math.exp(-0.3 * l)
            n_grp = DIFF_HEADS * 2
            w_qkv = od_w_qkv[o].astype(BF16)
            q, k, vt = _odd_proj(x, _row(mix_norm[l]), w_qkv[:, :2 * D], w_qkv[:, 2 * D:].T, ones,
                                 _row(jnp.tile(od_q_gain[o], n_grp // 2)), _row(jnp.tile(od_k_gain[o], n_grp // 2)),
                                 ts_proj)
            bound = _score_bound(od_q_gain[o], od_k_gain[o], DIFF_DIM, DIFF_DIM ** -0.5 * LOG2E)
            out_d = _diff_attention(diff_slopes, bound, q, k, vt, od_lambda[o].astype(F32), _row(od_subln[o]),
                                    tq_flash, lambda_init)
            attns = [out_d]
            wos = [od_w_out[o].astype(BF16)]
        x = _post_block(x, attns, wos, _row(mem_q_norm[l]), mem_w_q[l].astype(BF16), _row(mem_q_gain[l]),
                        mem_k[l], mem_v[l], mem_w_out[l].astype(BF16), _row(ffn_norm[l]),
                        ffn_w_gate[l].astype(BF16), ffn_w_up[l].astype(BF16), ffn_w_down[l].astype(BF16), ts_post)
    return x
```

```python
import functools
import math

import numpy as np
import jax
import jax.numpy as jnp
from jax import lax
from jax.experimental import pallas as pl
from jax.experimental.pallas import tpu as pltpu

F32 = jnp.float32
BF16 = jnp.bfloat16

EPS = 1e-6
MASKED = -1e30
LOG2E = 1.4426950408889634
ROPE_THETA = 10000.0

LANES = 128
SUBLANES = 8
MXU_WIDTH = 256
HEAD_DIM = 64
SWA_HEADS = 8
SWA_KV_HEADS = 2
SWA_WINDOW = 128
MLA_HEADS = 8
MLA_NOPE = 64
MLA_ROPE = 32
MLA_QK = MLA_NOPE + MLA_ROPE
MLA_V = 64
ROPE_HI_LANE = 64
DIFF_HEADS = 8
DIFF_DIM = 64
MEM_HEADS = 4
MEM_HEAD_DIM = 128

VMEM_LIMIT = 56 * 1024 * 1024

UNDERFLOW_GUARD = 2.0 ** -80
N_SPLIT = 3


def _params(sem):
    return pltpu.CompilerParams(dimension_semantics=sem, vmem_limit_bytes=VMEM_LIMIT)


def _const_spec(shape):
    nd = len(shape)
    return pl.BlockSpec(shape, lambda *_: (0,) * nd, pipeline_mode=pl.Buffered(1))


def _rms(x, g):
    ms = jnp.mean(x * x, axis=-1, keepdims=True)
    return x * lax.rsqrt(ms + EPS) * g


def _dot(a, b):
    return jnp.dot(a, b, preferred_element_type=F32)


def _dot_nt(a, b):
    return lax.dot_general(a, b, (((1,), (1,)), ((), ())), preferred_element_type=F32)


def _group_sumsq(x, ones_blk):
    x2 = x * x
    hi = x2.astype(BF16)
    lo = (x2 - hi.astype(F32)).astype(BF16)
    return _dot(hi, ones_blk) + _dot(lo, ones_blk)


def _group_norm64(x, ones_blk, gain):
    chunk = min(ones_blk.shape[0], x.shape[1])
    ssq = jnp.concatenate([_group_sumsq(x[:, c:c + chunk], ones_blk[0:chunk, 0:chunk])
                           for c in range(0, x.shape[1], chunk)], axis=1)
    return x * lax.rsqrt(ssq * (1.0 / HEAD_DIM) + EPS) * gain


def _even_proj_kernel(x_ref, pos_ref, gmix_ref, w1_ref, ones_ref, gq_ref, gk_ref, qln_ref, kvln_ref,
                      wuq_ref, wuk_ref, wuv_ref, gmq_ref, gmk_ref, qshift_ref, inv_ref,
                      qa_ref, ka_ref, va_ref, qm_ref, km_ref, vm_ref):
    x = x_ref[0]
    h = _rms(x, gmix_ref[...]).astype(BF16)
    z = _dot(h, w1_ref[...])
    ones = ones_ref[...]
    qa = _group_norm64(z[:, 0:512], ones, gq_ref[...]) * (HEAD_DIM ** -0.5 * LOG2E)
    qa_ref[0] = qa.astype(BF16)
    ka = _group_norm64(z[:, 512:640], ones[0:128, 0:128], gk_ref[...])
    ka_ref[0] = ka.astype(BF16)
    va_ref[0] = z[:, 640:768].astype(BF16)

    cqn = _rms(z[:, 768:1024], qln_ref[...]).astype(BF16)
    ckvn = _rms(z[:, 1024:1152], kvln_ref[...]).astype(BF16)
    kr = z[:, 1152:1280]
    qf = _dot(cqn, wuq_ref[...])
    kn = _dot(ckvn, wuk_ref[...])
    vm_ref[0] = _dot_nt(wuv_ref[...], ckvn).astype(BF16)

    half = MLA_ROPE // 2
    ang = inv_ref[...] * pos_ref[0].astype(F32)
    pad = jnp.zeros((LANES - half, ang.shape[1]), F32)
    cos = jnp.concatenate([jnp.cos(ang), pad], axis=0).T
    sin = jnp.concatenate([jnp.sin(ang), pad], axis=0).T
    lane = lax.broadcasted_iota(jnp.int32, cos.shape, 1)
    in_lo = lane < half
    in_hi = (lane >= ROPE_HI_LANE) & (lane < ROPE_HI_LANE + half)
    cos_t = jnp.where(in_lo, cos, jnp.where(in_hi, pltpu.roll(cos, ROPE_HI_LANE, 1), 1.0))
    sin_t = jnp.where(in_lo, -sin, jnp.where(in_hi, pltpu.roll(sin, ROPE_HI_LANE, 1), 0.0))

    pair = 2 * LANES
    gi = lax.broadcasted_iota(jnp.int32, (pair, pair), 0) // LANES
    gj = lax.broadcasted_iota(jnp.int32, (pair, pair), 1) // LANES
    head_ones = (gi == gj).astype(BF16)

    def norm_rope(xp, gain):
        r = lax.rsqrt(_group_sumsq(xp, head_ones) * (1.0 / MLA_QK) + EPS)
        xn = xp * r * gain
        halves = []
        for t in range(2):
            xh = xn[:, LANES * t:LANES * (t + 1)]
            halves.append(xh * cos_t + pltpu.roll(xh, ROPE_HI_LANE, 1) * sin_t)
        return halves

    gmq = jnp.concatenate([gmq_ref[...]] * 2, axis=1)
    gmk = jnp.concatenate([gmk_ref[...]] * 2, axis=1)
    kr2 = jnp.concatenate([kr, kr], axis=1)
    shift_lanes = (lane >= MLA_QK) & (lane < MLA_QK + N_SPLIT)
    qshift = qshift_ref[...]
    for hp in range(MLA_HEADS // 2):
        sl = slice(pair * hp, pair * (hp + 1))
        q_heads = norm_rope(qf[:, sl], gmq)
        k_heads = norm_rope(kn[:, sl] + kr2, gmk)
        for t in range(2):
            hl = slice(pair * hp + LANES * t, pair * hp + LANES * (t + 1))
            qm_ref[0, :, hl] = jnp.where(shift_lanes, qshift, q_heads[t] * (MLA_QK ** -0.5 * LOG2E)).astype(BF16)
            km_ref[0, :, hl] = jnp.where(shift_lanes, 1.0, k_heads[t]).astype(BF16)


def _even_proj(x, pos3, gmix, w1, ones, gq, gk, qln, kvln, wuq, wuk, wuv, gmq, gmk, qshift, inv, ts):
    B, S, D = x.shape
    grid = (B, S // ts)
    tok = lambda c: pl.BlockSpec((1, ts, c), lambda b, i: (b, i, 0))
    consts = [gmix, w1, ones, gq, gk, qln, kvln, wuq, wuk, wuv, gmq, gmk, qshift, inv]
    out_cols = [512, 128, 128, 1024, 1024]
    v_rows = MLA_HEADS * MLA_V
    return pl.pallas_call(
        _even_proj_kernel,
        grid=grid,
        in_specs=([tok(D), pl.BlockSpec((1, 1, ts), lambda b, i: (b, 0, i))]
                  + [_const_spec(c.shape) for c in consts]),
        out_specs=[tok(c) for c in out_cols] + [pl.BlockSpec((1, v_rows, ts), lambda b, i: (b, 0, i))],
        out_shape=([jax.ShapeDtypeStruct((B, S, c), BF16) for c in out_cols]
                   + [jax.ShapeDtypeStruct((B, v_rows, S), BF16)]),
        compiler_params=_params(("parallel", "parallel")),
        name="even_proj",
    )(x, pos3, *consts)


def _swa_place_q(q_ref, hd, lane):
    hk = hd // (SWA_HEADS // SWA_KV_HEADS)
    qq = q_ref[0, :, LANES * (hd // 2):LANES * (hd // 2 + 1)].astype(F32)
    xq = qq if hd % 2 == hk else pltpu.roll(qq, HEAD_DIM, 1)
    return jnp.where((lane >= HEAD_DIM * hk) & (lane < HEAD_DIM * (hk + 1)), xq, 0.0).astype(BF16)


def _swa_place_out(o_even, o_odd, cb, lane):
    placed = []
    for e, o in enumerate((o_even, o_odd)):
        hk = (2 * cb + e) // (SWA_HEADS // SWA_KV_HEADS)
        placed.append(o if e == hk else pltpu.roll(o, HEAD_DIM, 1))
    return jnp.where(lane < HEAD_DIM, placed[0], placed[1]).astype(BF16)


def _swa_running_max(sink_ref, q_ref, k_ref, v_ref, o_ref, qi, tq):
    win = tq + SWA_WINDOW
    wstart = pl.multiple_of(jnp.maximum(qi * tq - SWA_WINDOW, 0), SWA_WINDOW)
    kw = k_ref[0, pl.ds(wstart, win), :]
    vw = v_ref[0, pl.ds(wstart, win), :]
    row = lax.broadcasted_iota(jnp.int32, (tq, win), 0)
    col = lax.broadcasted_iota(jnp.int32, (tq, win), 1)
    dist = (qi * tq + row) - (wstart + col)
    valid = (dist >= 0) & (dist < SWA_WINDOW)
    distf = dist.astype(F32)
    lane = lax.broadcasted_iota(jnp.int32, (tq, LANES), 1)
    for cb in range(SWA_HEADS // 2):
        outs = []
        for e in range(2):
            hd = 2 * cb + e
            s = _dot_nt(_swa_place_q(q_ref, hd, lane), kw) - (2.0 ** (-8.0 * (hd + 1) / SWA_HEADS) * LOG2E) * distf
            s = jnp.where(valid, s, MASKED)
            sink = sink_ref[hd] * LOG2E
            m = jnp.maximum(jnp.max(s, axis=-1, keepdims=True), sink)
            p = jnp.exp2(s - m)
            l = jnp.sum(p, axis=-1, keepdims=True) + jnp.exp2(sink - m)
            outs.append(_dot(p.astype(BF16), vw) / l)
        o_ref[0, :, LANES * cb:LANES * (cb + 1)] = _swa_place_out(outs[0], outs[1], cb, lane)


def _swa_kernel(sink_ref, q_ref, k_ref, v_ref, o_ref, *, tq):
    qi = pl.program_id(1)
    sub = SWA_WINDOW
    win = 2 * sub
    n_sub = tq // sub
    bound = sink_ref[SWA_HEADS]
    lane_q = lax.broadcasted_iota(jnp.int32, (tq, LANES), 1)
    lane = lax.broadcasted_iota(jnp.int32, (sub, LANES), 1)
    row = lax.broadcasted_iota(jnp.int32, (sub, win), 0)
    col = lax.broadcasted_iota(jnp.int32, (sub, win), 1)

    def bias_mask(first_key_offset, slope):
        dist = row + first_key_offset - col
        return jnp.where((dist >= 0) & (dist < SWA_WINDOW), -slope * dist.astype(F32) - bound, MASKED)

    ones_blk = jnp.ones((win, LANES), BF16)
    windows = []
    for r in range(n_sub):
        wstart = qi * tq + sub * (r - 1)
        wstart = pl.multiple_of(jnp.maximum(wstart, 0) if r == 0 else wstart, sub)
        windows.append((k_ref[0, pl.ds(wstart, win), :],
                        jnp.concatenate([v_ref[0, pl.ds(wstart, win), :], ones_blk], axis=1)))

    outs = [[None] * SWA_HEADS for _ in range(n_sub)]
    min_sum = None
    for hd in range(SWA_HEADS):
        slope = 2.0 ** (-8.0 * (hd + 1) / SWA_HEADS) * LOG2E
        bm = bias_mask(sub, slope)
        bm_first = jnp.where(qi == 0, bias_mask(0, slope), bm)
        xq = _swa_place_q(q_ref, hd, lane_q)
        sink_term = jnp.exp2(sink_ref[hd] * LOG2E - bound)
        for r in range(n_sub):
            kw, v_ones = windows[r]
            s = _dot_nt(xq[sub * r:sub * (r + 1)], kw) + (bm_first if r == 0 else bm)
            ov = _dot(jnp.exp2(s).astype(BF16), v_ones)
            l = ov[:, LANES:] + sink_term
            outs[r][hd] = ov[:, :LANES] / l
            piece_min = jnp.min(l)
            min_sum = piece_min if min_sum is None else jnp.minimum(min_sum, piece_min)
    ok = min_sum >= UNDERFLOW_GUARD

    @pl.when(ok)
    def _():
        for r in range(n_sub):
            for cb in range(SWA_HEADS // 2):
                o_ref[0, sub * r:sub * (r + 1), LANES * cb:LANES * (cb + 1)] = _swa_place_out(
                    outs[r][2 * cb], outs[r][2 * cb + 1], cb, lane)

    @pl.when(jnp.logical_not(ok))
    def _():
        _swa_running_max(sink_ref, q_ref, k_ref, v_ref, o_ref, qi, tq)


def _swa_attention(sinks, qa, ka, va, tq):
    B, S, _ = qa.shape
    return pl.pallas_call(
        functools.partial(_swa_kernel, tq=tq),
        grid=(B, S // tq),
        in_specs=[
            pl.BlockSpec(memory_space=pltpu.SMEM),
            pl.BlockSpec((1, tq, 512), lambda b, i: (b, i, 0)),
            pl.BlockSpec((1, S, LANES), lambda b, i: (b, 0, 0)),
            pl.BlockSpec((1, S, LANES), lambda b, i: (b, 0, 0)),
        ],
        out_specs=pl.BlockSpec((1, tq, 512), lambda b, i: (b, i, 0)),
        out_shape=jax.ShapeDtypeStruct((B, S, 512), BF16),
        compiler_params=_params(("parallel", "arbitrary")),
        name="swa_attention",
    )(sinks, qa, ka, va)


def _split3(x):
    hi = x.astype(BF16).astype(F32)
    r = x - hi
    mid = r.astype(BF16).astype(F32)
    lo = (r - mid).astype(BF16).astype(F32)
    return hi, mid, lo


def _score_bound(q_gain, k_gain, dim, scale):
    return dim * scale * jnp.max(jnp.abs(q_gain)) * jnp.max(jnp.abs(k_gain))


FFN_CHUNKS = 2

BLOCKS_PER_TRIP = 4


def _loop_blocks(n, step, init):
    def trip(t, c):
        for u in range(BLOCKS_PER_TRIP):
            c = step(BLOCKS_PER_TRIP * t + u, c)
        return c

    full = n // BLOCKS_PER_TRIP
    carry = lax.fori_loop(0, full, trip, init)
    return lax.fori_loop(full * BLOCKS_PER_TRIP, n, step, carry)


def _add_cols(acc, lo, part):
    if lo == 0:
        return acc + part
    return acc + jnp.concatenate([jnp.zeros((acc.shape[0], lo), acc.dtype), part], axis=1)


def _diagonal_spans(tq):
    h = tq // 2
    return ((0, h, 0), (h, h, h)) if h % MXU_WIDTH == 0 else ((0, tq, 0),)


def _causal_t(key_lo, key_n, q_lo, tq):
    shape = (key_n, tq - q_lo)
    return (key_lo + lax.broadcasted_iota(jnp.int32, shape, 0)) <= (q_lo + lax.broadcasted_iota(jnp.int32, shape, 1))


def _causal_mask(tq):
    row = lax.broadcasted_iota(jnp.int32, (tq, tq), 0)
    col = lax.broadcasted_iota(jnp.int32, (tq, tq), 1)
    return row >= col


def _mla_running_max(q_ref, k_ref, vt_ref, e, qi, tq):
    causal = _causal_mask(tq)
    q = q_ref[0, :, LANES * e:LANES * (e + 1)]

    def step(j, carry, masked):
        m, l, acc = carry
        ks = pl.multiple_of(j * tq, tq)
        k = k_ref[0, pl.ds(ks, tq), LANES * e:LANES * (e + 1)]
        vt = vt_ref[0, :, pl.ds(ks, tq)]
        s = _dot_nt(q, k)
        if masked:
            s = jnp.where(causal, s, MASKED)
        m_new = jnp.maximum(m, jnp.max(s, axis=-1, keepdims=True))
        alpha = jnp.exp2(m - m_new)
        p = jnp.exp2(s - m_new)
        l = alpha * l + jnp.sum(p, axis=-1, keepdims=True)
        acc = alpha * acc + _dot_nt(p.astype(BF16), vt)
        return m_new, l, acc

    init = (jnp.full((tq, 1), MASKED, F32), jnp.zeros((tq, 1), F32), jnp.zeros((tq, LANES), F32))
    carry = lax.fori_loop(0, qi, lambda j, c: step(j, c, False), init)
    m, l, acc = step(qi, carry, True)
    return acc / l


def _mla_kernel(q_ref, k_ref, vt_ref, o_ref, *, tq):
    qi = pl.program_id(2)
    lane = lax.broadcasted_iota(jnp.int32, (tq, LANES), 1)
    q_aug = [q_ref[0, :, LANES * e:LANES * (e + 1)] for e in range(2)]

    def block(j, carry, key_lo=0, key_n=tq, q_lo=0, masked=False):
        ks = pl.multiple_of(j * tq + key_lo, key_n)
        new = []
        for e in range(2):
            lsum, acc = carry[e]
            st = _dot_nt(k_ref[0, pl.ds(ks, key_n), LANES * e:LANES * (e + 1)], q_aug[e][q_lo:, :])
            if masked:
                st = jnp.where(_causal_t(key_lo, key_n, q_lo, tq), st, MASKED)
            pt = jnp.exp2(st)
            part = jnp.sum(pt.reshape(key_n // SUBLANES, SUBLANES, tq - q_lo), axis=0)
            pv = _dot(vt_ref[0, MLA_V * e:MLA_V * (e + 1), pl.ds(ks, key_n)], pt.astype(BF16))
            new.append((_add_cols(lsum, q_lo, part), _add_cols(acc, q_lo, pv)))
        return tuple(new)

    zero = (jnp.zeros((SUBLANES, tq), F32), jnp.zeros((MLA_V, tq), F32))
    carry = _loop_blocks(qi, block, (zero, zero))
    for key_lo, key_n, q_lo in _diagonal_spans(tq):
        carry = block(qi, carry, key_lo, key_n, q_lo, masked=True)
    sums = [jnp.sum(lsum, axis=0, keepdims=True) for lsum, _ in carry]
    outs = [acc / l for (_, acc), l in zip(carry, sums)]
    ok = jnp.minimum(jnp.min(sums[0]), jnp.min(sums[1])) >= UNDERFLOW_GUARD

    @pl.when(ok)
    def _():
        o_ref[0] = jnp.concatenate(outs, axis=0).T.astype(BF16)

    @pl.when(jnp.logical_not(ok))
    def _():
        slow = [_mla_running_max(q_ref, k_ref, vt_ref, e, qi, tq) for e in range(2)]
        o_ref[0] = jnp.where(lane < MLA_V, slow[0], slow[1]).astype(BF16)


def _mla_attention(qm, km, vm, tq):
    B, S, _ = qm.shape
    return pl.pallas_call(
        functools.partial(_mla_kernel, tq=tq),
        grid=(B, MLA_HEADS // 2, S // tq),
        in_specs=[
            pl.BlockSpec((1, tq, 2 * LANES), lambda b, h, i: (b, i, h)),
            pl.BlockSpec((1, S, 2 * LANES), lambda b, h, i: (b, 0, h)),
            pl.BlockSpec((1, 2 * MLA_V, S), lambda b, h, i: (b, h, 0)),
        ],
        out_specs=pl.BlockSpec((1, tq, LANES), lambda b, h, i: (b, i, h)),
        out_shape=jax.ShapeDtypeStruct((B, S, MLA_HEADS * MLA_V), BF16),
        compiler_params=_params(("parallel", "parallel", "arbitrary")),
        name="mla_attention",
    )(qm, km, vm)


def _odd_proj_kernel(x_ref, gmix_ref, w_ref, wvt_ref, ones_ref, gq_ref, gk_ref, q_ref, k_ref, vt_ref):
    x = x_ref[0]
    h = _rms(x, gmix_ref[...]).astype(BF16)
    z = _dot(h, w_ref[...])
    ones = ones_ref[...]
    width = DIFF_HEADS * 2 * DIFF_DIM
    for c in range(width // 512):
        sl = slice(512 * c, 512 * (c + 1))
        q = _group_norm64(z[:, sl], ones, gq_ref[...]) * (DIFF_DIM ** -0.5 * LOG2E)
        q_ref[0, :, sl] = q.astype(BF16)
        k = _group_norm64(z[:, width + 512 * c:width + 512 * (c + 1)], ones, gk_ref[...])
        k_ref[0, :, sl] = k.astype(BF16)
    vt_ref[0] = _dot_nt(wvt_ref[...], h).astype(BF16)


def _odd_proj(x, gmix, w_qk, w_vt, ones, gq, gk, ts):
    B, S, D = x.shape
    tok = pl.BlockSpec((1, ts, D), lambda b, i: (b, i, 0))
    consts = [gmix, w_qk, w_vt, ones, gq, gk]
    return pl.pallas_call(
        _odd_proj_kernel,
        grid=(B, S // ts),
        in_specs=[tok] + [_const_spec(c.shape) for c in consts],
        out_specs=[tok, tok, pl.BlockSpec((1, D, ts), lambda b, i: (b, 0, i))],
        out_shape=[jax.ShapeDtypeStruct((B, S, D), BF16)] * 2 + [jax.ShapeDtypeStruct((B, D, S), BF16)],
        compiler_params=_params(("parallel", "parallel")),
        name="odd_proj",
    )(x, *consts)


def _diff_q_parts(q, lane):
    zero = jnp.zeros_like(q)
    return jnp.where(lane < DIFF_DIM, q, zero), jnp.where(lane >= DIFF_DIM, q, zero)


def _diff_running_max(slope, q_ref, k_ref, vt_ref, qi, tq):
    row = lax.broadcasted_iota(jnp.int32, (tq, tq), 0)
    col = lax.broadcasted_iota(jnp.int32, (tq, tq), 1)
    causal = row >= col
    bias = slope * (col - row).astype(F32)
    lane = lax.broadcasted_iota(jnp.int32, (tq, LANES), 1)
    q_parts = _diff_q_parts(q_ref[0], lane)

    def step(j, carry, masked):
        ks = pl.multiple_of(j * tq, tq)
        k = k_ref[0, pl.ds(ks, tq), :]
        vt = vt_ref[0, :, pl.ds(ks, tq)]
        offset = slope * ((j - qi) * tq).astype(F32)
        new = []
        for c in range(2):
            m, l, acc = carry[c]
            s = _dot_nt(q_parts[c], k) + bias
            if masked:
                s = jnp.where(causal, s, MASKED)
            m_new = jnp.maximum(m, jnp.max(s, axis=-1, keepdims=True) + offset)
            alpha = jnp.exp2(m - m_new)
            p = jnp.exp2(s - (m_new - offset))
            l = alpha * l + jnp.sum(p, axis=-1, keepdims=True)
            acc = alpha * acc + _dot_nt(p.astype(BF16), vt)
            new.append((m_new, l, acc))
        return tuple(new)

    one = (jnp.full((tq, 1), MASKED, F32), jnp.zeros((tq, 1), F32), jnp.zeros((tq, LANES), F32))
    carry = lax.fori_loop(0, qi, lambda j, c: step(j, c, False), (one, one))
    (m1, l1, a1), (m2, l2, a2) = step(qi, carry, True)
    return a1 / l1, a2 / l2


def _diff_kernel(slope_ref, qx_ref, qpos_ref, q_ref, k_ref, kx_ref, vt_ref, lam_ref, subln_ref, o_ref, *, tq,
                 lambda_init):
    hd = pl.program_id(1)
    qi = pl.program_id(2)
    lane = lax.broadcasted_iota(jnp.int32, (tq, LANES), 1)
    q_parts = _diff_q_parts(q_ref[0], lane)
    head_row = qx_ref[0]
    qx = jnp.where(lane < 3 * N_SPLIT, head_row,
                   qpos_ref[...].astype(F32) * head_row[:, SLOPE_LANE:SLOPE_LANE + 1]).astype(BF16)
    q_aug = [jnp.concatenate([part, qx], axis=1) for part in q_parts]

    def block(j, carry, key_lo=0, key_n=tq, q_lo=0, masked=False):
        ks = pl.multiple_of(j * tq + key_lo, key_n)
        k = jnp.concatenate([k_ref[0, pl.ds(ks, key_n), :], kx_ref[pl.ds(ks, key_n), :]], axis=1)
        vt = vt_ref[0, :, pl.ds(ks, key_n)]
        new = []
        for c in range(2):
            lsum, acc = carry[c]
            st = _dot_nt(k, q_aug[c][q_lo:, :])
            if masked:
                st = jnp.where(_causal_t(key_lo, key_n, q_lo, tq), st, MASKED)
            pt = jnp.exp2(st)
            part = jnp.sum(pt.reshape(key_n // SUBLANES, SUBLANES, tq - q_lo), axis=0)
            new.append((_add_cols(lsum, q_lo, part), _add_cols(acc, q_lo, _dot(vt, pt.astype(BF16)))))
        return tuple(new)

    zero = (jnp.zeros((SUBLANES, tq), F32), jnp.zeros((LANES, tq), F32))
    carry = _loop_blocks(qi, block, (zero, zero))
    for key_lo, key_n, q_lo in _diagonal_spans(tq):
        carry = block(qi, carry, key_lo, key_n, q_lo, masked=True)
    (ls1, a1), (ls2, a2) = carry
    l1 = jnp.sum(ls1, axis=0, keepdims=True)
    l2 = jnp.sum(ls2, axis=0, keepdims=True)
    ok = jnp.minimum(jnp.min(l1), jnp.min(l2)) >= UNDERFLOW_GUARD

    lf = lam_ref[...]
    lam = (jnp.exp(jnp.sum(lf[0:1] * lf[1:2], axis=-1, keepdims=True))
           - jnp.exp(jnp.sum(lf[2:3] * lf[3:4], axis=-1, keepdims=True)) + lambda_init)

    def finish(o1, o2):
        o = _rms(o1 - lam * o2, subln_ref[...]) * (1.0 - lambda_init)
        o_ref[0] = o.astype(BF16)

    @pl.when(ok)
    def _():
        finish((a1 / l1).T, (a2 / l2).T)

    @pl.when(jnp.logical_not(ok))
    def _():
        finish(*_diff_running_max(slope_ref[hd] * LOG2E, q_ref, k_ref, vt_ref, qi, tq))


SLOPE_LANE = 5 * N_SPLIT


def _alibi_tables(slopes, bound, S):
    assert all(math.log2(s) == int(math.log2(s)) for s in slopes), "ALiBi slopes must be powers of two"
    n = N_SPLIT
    log2e_terms = jnp.concatenate(_split3(jnp.full((1, 1), LOG2E, F32)), axis=1)
    s = jnp.asarray(slopes, F32).reshape(-1, 1)
    head = jnp.concatenate(
        [s * log2e_terms * float(LANES), s * log2e_terms, jnp.broadcast_to(_shift_row(bound, 0)[:, :n], (len(slopes), n)),
         jnp.zeros((len(slopes), SLOPE_LANE - 3 * n), F32), s, jnp.zeros((len(slopes), LANES - SLOPE_LANE - 1), F32)],
        axis=1)
    idx = np.arange(S)
    hi, lo = (idx // LANES)[:, None].astype(np.float32), (idx % LANES)[:, None].astype(np.float32)
    qpos = np.zeros((S, LANES), np.float32)
    qpos[:, 3 * n:4 * n] = hi * LANES
    qpos[:, 4 * n:5 * n] = lo
    kx = jnp.concatenate(
        [jnp.asarray(np.concatenate([np.repeat(hi, n, 1), np.repeat(lo, n, 1), np.ones((S, n), np.float32)], axis=1)),
         jnp.broadcast_to(-log2e_terms, (S, n)), jnp.broadcast_to(-log2e_terms, (S, n)),
         jnp.zeros((S, LANES - 5 * n), F32)], axis=1)
    return head.reshape(-1, 1, LANES), jnp.asarray(qpos, dtype=BF16), kx.astype(BF16)


def _diff_attention(slopes, bound, q, k, vt, lambdas, subln, tq, lambda_init):
    B, S, D = q.shape
    qx, qpos, kx = _alibi_tables(slopes, bound, S)
    return pl.pallas_call(
        functools.partial(_diff_kernel, tq=tq, lambda_init=lambda_init),
        grid=(B, DIFF_HEADS, S // tq),
        in_specs=[
            pl.BlockSpec(memory_space=pltpu.SMEM),
            pl.BlockSpec((1, 1, LANES), lambda b, h, i: (h, 0, 0)),
            pl.BlockSpec((tq, LANES), lambda b, h, i: (i, 0)),
            pl.BlockSpec((1, tq, LANES), lambda b, h, i: (b, i, h)),
            pl.BlockSpec((1, S, LANES), lambda b, h, i: (b, 0, h)),
            _const_spec(kx.shape),
            pl.BlockSpec((1, LANES, S), lambda b, h, i: (b, h, 0)),
            pl.BlockSpec(lambdas.shape, lambda b, h, i: (0, 0)),
            pl.BlockSpec(subln.shape, lambda b, h, i: (0, 0)),
        ],
        out_specs=pl.BlockSpec((1, tq, LANES), lambda b, h, i: (b, i, h)),
        out_shape=jax.ShapeDtypeStruct((B, S, D), BF16),
        compiler_params=_params(("parallel", "parallel", "arbitrary")),
        name="diff_attention",
    )(jnp.asarray(slopes, F32), qx, qpos, q, k, kx, vt, lambdas, subln)


def _mem_kv_kernel(mem_ref, norm_ref, w_ref, gain_ref, k_ref, v_ref):
    h = _rms(mem_ref[0], norm_ref[0]).astype(BF16)
    z = _dot(h, w_ref[0])
    width = MEM_HEADS * MEM_HEAD_DIM
    gain = gain_ref[0]
    for hd in range(MEM_HEADS):
        sl = slice(MEM_HEAD_DIM * hd, MEM_HEAD_DIM * (hd + 1))
        k_ref[0, 0, :, sl] = _rms(z[:, sl], gain).astype(BF16)
    v_ref[0, 0] = z[:, width:2 * width].astype(BF16)


def _mem_kv(mem, norms, w_kv, gains):
    B, M, D = mem.shape
    L = norms.shape[0]
    width = MEM_HEADS * MEM_HEAD_DIM
    out = pl.BlockSpec((1, 1, M, width), lambda l, b: (l, b, 0, 0))
    return pl.pallas_call(
        _mem_kv_kernel,
        grid=(L, B),
        in_specs=[
            pl.BlockSpec((1, M, D), lambda l, b: (b, 0, 0)),
            pl.BlockSpec((1, 1, D), lambda l, b: (l, 0, 0)),
            pl.BlockSpec((1, D, 2 * width), lambda l, b: (l, 0, 0)),
            pl.BlockSpec((1, 1, MEM_HEAD_DIM), lambda l, b: (l, 0, 0)),
        ],
        out_specs=[out, out],
        out_shape=[jax.ShapeDtypeStruct((L, B, M, width), BF16)] * 2,
        compiler_params=_params(("parallel", "parallel")),
        name="mem_kv",
    )(mem, norms, w_kv, gains)


def _post_kernel(*refs, n_attn):
    x_ref = refs[0]
    attn_refs = refs[1:1 + n_attn]
    wo_refs = refs[1 + n_attn:1 + 2 * n_attn]
    (mqn_ref, wq_ref, qg_ref, mk_ref, mv_ref, wmo_ref, fn_ref, wg_ref, wu_ref, wd_ref, o_ref) = refs[1 + 2 * n_attn:]
    x = x_ref[0]
    for a_ref, w_ref in zip(attn_refs, wo_refs):
        x = x + _dot(a_ref[0], w_ref[...])

    q = _dot(_rms(x, mqn_ref[...]).astype(BF16), wq_ref[...])
    heads = []
    for hd in range(MEM_HEADS):
        sl = slice(MEM_HEAD_DIM * hd, MEM_HEAD_DIM * (hd + 1))
        qh = (_rms(q[:, sl], qg_ref[...]) * (MEM_HEAD_DIM ** -0.5 * LOG2E)).astype(BF16)
        s = _dot_nt(qh, mk_ref[0, :, sl])
        p = jnp.exp2(s - jnp.max(s, axis=-1, keepdims=True))
        l = jnp.sum(p, axis=-1, keepdims=True)
        heads.append((_dot(p.astype(BF16), mv_ref[0, :, sl]) / l).astype(BF16))
    x = x + _dot(jnp.concatenate(heads, axis=-1), wmo_ref[...])

    hf = _rms(x, fn_ref[...]).astype(BF16)
    d_ff = wg_ref.shape[1]
    chunk = d_ff // FFN_CHUNKS
    y = x
    for c in range(FFN_CHUNKS):
        cols = slice(chunk * c, chunk * (c + 1))
        g = _dot(hf, wg_ref[:, cols])
        u = _dot(hf, wu_ref[:, cols])
        act = (g * jax.nn.sigmoid(g) * u).astype(BF16)
        y = y + _dot(act, wd_ref[cols, :])
    o_ref[0] = y


def _post_block(x, attns, wos, mqn, wq, qg, mk, mv, wmo, fn, wg, wu, wd, ts):
    B, S, D = x.shape
    n = len(attns)
    tok = lambda c: pl.BlockSpec((1, ts, c), lambda b, i: (b, i, 0))
    memspec = pl.BlockSpec((1,) + mk.shape[1:], lambda b, i: (b, 0, 0))
    consts_a = list(wos) + [mqn, wq, qg]
    consts_b = [wmo, fn, wg, wu, wd]
    return pl.pallas_call(
        functools.partial(_post_kernel, n_attn=n),
        grid=(B, S // ts),
        in_specs=([tok(D)] + [tok(a.shape[-1]) for a in attns] + [_const_spec(c.shape) for c in consts_a]
                  + [memspec, memspec] + [_const_spec(c.shape) for c in consts_b]),
        out_specs=tok(D),
        out_shape=jax.ShapeDtypeStruct((B, S, D), F32),
        compiler_params=_params(("parallel", "parallel")),
        name="post_block",
    )(x, *attns, *consts_a, mk, mv, *consts_b)


def _row(v):
    return v.reshape(1, -1).astype(F32)


def _shift_row(bound, first_lane):
    terms = jnp.stack(_split3(-jnp.asarray(bound, F32))).reshape(1, N_SPLIT)
    return jnp.pad(terms, ((0, 0), (first_lane, LANES - first_lane - N_SPLIT)))


def _pad_cols(w, lo, total):
    return jnp.pad(w, ((0, 0), (lo, total - lo - w.shape[1])))


def _mla_lane_perm():
    half = MLA_ROPE // 2
    lanes = np.arange(MLA_QK)
    return np.where(lanes < half, MLA_NOPE + lanes,
                    np.where(lanes < ROPE_HI_LANE, lanes - half,
                             np.where(lanes < ROPE_HI_LANE + half, lanes + half, lanes - MLA_ROPE)))


def _even_weights(w_in, w_uq, w_ukv, q_gain, k_gain):
    perm = _mla_lane_perm()
    is_nope = perm < MLA_NOPE
    pad = LANES - MLA_QK

    def head_cols(w_head, live):
        return jnp.pad(jnp.where(live[None, :], w_head[:, perm], 0.0), ((0, 0), (0, pad)))

    kr_src = jnp.concatenate([jnp.zeros((w_in.shape[0], MLA_NOPE), w_in.dtype), w_in[:, 1152:1184]], axis=1)
    w1 = jnp.concatenate([w_in[:, :1152], head_cols(kr_src, ~is_nope)], axis=1).astype(BF16)
    wuq = jnp.concatenate(
        [head_cols(w_uq[:, MLA_QK * h:MLA_QK * (h + 1)], np.ones_like(is_nope)) for h in range(MLA_HEADS)],
        axis=1).astype(BF16)
    kv_w = MLA_NOPE + MLA_V
    rope_zeros = jnp.zeros((w_ukv.shape[0], MLA_ROPE), w_ukv.dtype)
    wuk = jnp.concatenate(
        [head_cols(jnp.concatenate([w_ukv[:, kv_w * h:kv_w * h + MLA_NOPE], rope_zeros], axis=1), is_nope)
         for h in range(MLA_HEADS)], axis=1).astype(BF16)
    wuv = jnp.concatenate(
        [w_ukv[:, kv_w * h + MLA_NOPE:kv_w * (h + 1)] for h in range(MLA_HEADS)], axis=1).astype(BF16).T
    gmq = _row(jnp.pad(q_gain[perm], (0, pad)))
    gmk = _row(jnp.pad(k_gain[perm], (0, pad)))
    return w1, wuq, wuk, wuv, gmq, gmk


def _rope_freqs():
    half = MLA_ROPE // 2
    freqs = (np.float32(ROPE_THETA) ** (-np.arange(half, dtype=np.float32) / np.float32(half))).astype(np.float32)
    return jnp.asarray(freqs.reshape(half, 1))


def _ones_blocks(n):
    g = np.arange(n) // HEAD_DIM
    return jnp.asarray((g[:, None] == g[None, :]).astype(np.float32), dtype=BF16)


def kernel(x, mem, positions, mix_norm, ev_w_in, ev_swa_q_gain, ev_swa_k_gain, ev_sinks, ev_q_latent_norm,
           ev_kv_latent_norm, ev_w_uq, ev_w_ukv, ev_mla_q_gain, ev_mla_k_gain, ev_w_out, od_w_qkv, od_q_gain,
           od_k_gain, od_lambda, od_subln, od_w_out, mem_q_norm, mem_kv_norm, mem_w_q, mem_w_kv, mem_q_gain,
           mem_k_gain, mem_w_out, ffn_norm, ffn_w_gate, ffn_w_up, ffn_w_down):
    B, S, D = x.shape
    depth = mix_norm.shape[0]
    ts_proj = min(512, S)
    ts_post = min(512, S)
    tq_swa = min(512, S - SWA_WINDOW)
    tq_flash = min(1024, S)

    ones = _ones_blocks(MXU_WIDTH)
    pos3 = positions.reshape(B, 1, S)
    inv = _rope_freqs()
    mem_k, mem_v = _mem_kv(mem, mem_kv_norm.reshape(depth, 1, D), mem_w_kv.astype(BF16),
                           mem_k_gain.reshape(depth, 1, MEM_HEAD_DIM))
    diff_slopes = [2.0 ** (-8.0 * (i + 1) / DIFF_HEADS) for i in range(DIFF_HEADS)]

    for l in range(depth):
        if l % 2 == 0:
            e = l // 2
            w1, wuq, wuk, wuv, gmq, gmk = _even_weights(ev_w_in[e], ev_w_uq[e], ev_w_ukv[e], ev_mla_q_gain[e],
                                                        ev_mla_k_gain[e])
            qa, ka, va, qm, km, vm = _even_proj(
                x, pos3, _row(mix_norm[l]), w1, ones, _row(jnp.tile(ev_swa_q_gain[e], SWA_HEADS)),
                _row(jnp.tile(ev_swa_k_gain[e], SWA_KV_HEADS)), _row(ev_q_latent_norm[e]),
                _row(ev_kv_latent_norm[e]), wuq, wuk, wuv, gmq, gmk,
                _shift_row(_score_bound(ev_mla_q_gain[e], ev_mla_k_gain[e], MLA_QK, MLA_QK ** -0.5 * LOG2E), MLA_QK),
                inv, ts_proj)
            swa_bound = _score_bound(ev_swa_q_gain[e], ev_swa_k_gain[e], HEAD_DIM, HEAD_DIM ** -0.5 * LOG2E)
            swa_scalars = jnp.concatenate([ev_sinks[e].astype(F32), swa_bound.reshape(1).astype(F32)])
            out_a = _swa_attention(swa_scalars, qa, ka, va, tq_swa)
            out_b = _mla_attention(qm, km, vm, tq_flash)
            wo = ev_w_out[e].astype(BF16)
            attns = [out_a, out_b]
            wos = [wo[:512], wo[512:]]
        else:
            o = l // 2
            lambda_init = 0.8 - 0.6 * math.exp(-0.3 * l)
            n_grp = DIFF_HEADS * 2
            w_qkv = od_w_qkv[o].astype(BF16)
            q, k, vt = _odd_proj(x, _row(mix_norm[l]), w_qkv[:, :2 * D], w_qkv[:, 2 * D:].T, ones,
                                 _row(jnp.tile(od_q_gain[o], n_grp // 2)), _row(jnp.tile(od_k_gain[o], n_grp // 2)),
                                 ts_proj)
            bound = _score_bound(od_q_gain[o], od_k_gain[o], DIFF_DIM, DIFF_DIM ** -0.5 * LOG2E)
            out_d = _diff_attention(diff_slopes, bound, q, k, vt, od_lambda[o].astype(F32), _row(od_subln[o]),
                                    tq_flash, lambda_init)
            attns = [out_d]
            wos = [od_w_out[o].astype(BF16)]
        x = _post_block(x, attns, wos, _row(mem_q_norm[l]), mem_w_q[l].astype(BF16), _row(mem_q_gain[l]),
                        mem_k[l], mem_v[l], mem_w_out[l].astype(BF16), _row(ffn_norm[l]),
                        ffn_w_gate[l].astype(BF16), ffn_w_up[l].astype(BF16), ffn_w_down[l].astype(BF16), ts_post)
    return x
```

```python
import functools
import math

import numpy as np
import jax
import jax.numpy as jnp
from jax import lax
from jax.experimental import pallas as pl
from jax.experimental.pallas import tpu as pltpu

F32 = jnp.float32
BF16 = jnp.bfloat16

EPS = 1e-6
MASKED = -1e30
LOG2E = 1.4426950408889634
ROPE_THETA = 10000.0

LANES = 128
SUBLANES = 8
MXU_WIDTH = 256
HEAD_DIM = 64
SWA_HEADS = 8
SWA_KV_HEADS = 2
SWA_WINDOW = 128
MLA_HEADS = 8
MLA_NOPE = 64
MLA_ROPE = 32
MLA_QK = MLA_NOPE + MLA_ROPE
MLA_V = 64
ROPE_HI_LANE = 64
DIFF_HEADS = 8
DIFF_DIM = 64
MEM_HEADS = 4
MEM_HEAD_DIM = 128
MLA_Q_RANK = 256
MLA_KV_RANK = 128
SWA_Q_W = SWA_HEADS * HEAD_DIM
SWA_KV_W = SWA_KV_HEADS * HEAD_DIM
EVEN_COLS = tuple(np.cumsum([SWA_Q_W, SWA_KV_W, SWA_KV_W, MLA_Q_RANK, MLA_KV_RANK]).tolist())

VMEM_LIMIT = 56 * 1024 * 1024

UNDERFLOW_GUARD = 2.0 ** -80
N_SPLIT = 3


def _params(sem):
    return pltpu.CompilerParams(dimension_semantics=sem, vmem_limit_bytes=VMEM_LIMIT)


def _const_spec(shape):
    nd = len(shape)
    return pl.BlockSpec(shape, lambda *_: (0,) * nd, pipeline_mode=pl.Buffered(1))


def _rms(x, g):
    ms = jnp.mean(x * x, axis=-1, keepdims=True)
    return x * lax.rsqrt(ms + EPS) * g


def _dot(a, b):
    return jnp.dot(a, b, preferred_element_type=F32)


def _dot_nt(a, b):
    return lax.dot_general(a, b, (((1,), (1,)), ((), ())), preferred_element_type=F32)


def _group_sumsq(x, ones_blk):
    x2 = x * x
    hi = x2.astype(BF16)
    lo = (x2 - hi.astype(F32)).astype(BF16)
    return _dot(hi, ones_blk) + _dot(lo, ones_blk)


def _group_norm64(x, ones_blk, gain):
    chunk = min(ones_blk.shape[0], x.shape[1])
    ssq = jnp.concatenate([_group_sumsq(x[:, c:c + chunk], ones_blk[0:chunk, 0:chunk])
                           for c in range(0, x.shape[1], chunk)], axis=1)
    return x * lax.rsqrt(ssq * (1.0 / HEAD_DIM) + EPS) * gain


def _even_proj_kernel(x_ref, pos_ref, gmix_ref, w1_ref, ones_ref, gq_ref, gk_ref, qln_ref, kvln_ref,
                      wuq_ref, wuk_ref, wuv_ref, gmq_ref, gmk_ref, qshift_ref, inv_ref,
                      qa_ref, ka_ref, va_ref, qm_ref, km_ref, vm_ref):
    x = x_ref[0]
    h = _rms(x, gmix_ref[...]).astype(BF16)
    z = _dot(h, w1_ref[...])
    c_ka, c_va, c_cq, c_ckv, c_kr = EVEN_COLS
    ones = ones_ref[...]
    qa = _group_norm64(z[:, 0:c_ka], ones, gq_ref[...]) * (HEAD_DIM ** -0.5 * LOG2E)
    qa_ref[0] = qa.astype(BF16)
    ka = _group_norm64(z[:, c_ka:c_va], ones[0:SWA_KV_W, 0:SWA_KV_W], gk_ref[...])
    ka_ref[0] = ka.astype(BF16)
    va_ref[0] = z[:, c_va:c_cq].astype(BF16)

    cqn = _rms(z[:, c_cq:c_ckv], qln_ref[...]).astype(BF16)
    ckvn = _rms(z[:, c_ckv:c_kr], kvln_ref[...]).astype(BF16)
    kr = z[:, c_kr:c_kr + LANES]
    qf = _dot(cqn, wuq_ref[...])
    kn = _dot(ckvn, wuk_ref[...])
    vm_ref[0] = _dot_nt(wuv_ref[...], ckvn).astype(BF16)

    half = MLA_ROPE // 2
    ang = inv_ref[...] * pos_ref[0].astype(F32)
    pad = jnp.zeros((LANES - half, ang.shape[1]), F32)
    cos = jnp.concatenate([jnp.cos(ang), pad], axis=0).T
    sin = jnp.concatenate([jnp.sin(ang), pad], axis=0).T
    lane = lax.broadcasted_iota(jnp.int32, cos.shape, 1)
    in_lo = lane < half
    in_hi = (lane >= ROPE_HI_LANE) & (lane < ROPE_HI_LANE + half)
    cos_t = jnp.where(in_lo, cos, jnp.where(in_hi, pltpu.roll(cos, ROPE_HI_LANE, 1), 1.0))
    sin_t = jnp.where(in_lo, -sin, jnp.where(in_hi, pltpu.roll(sin, ROPE_HI_LANE, 1), 0.0))

    pair = 2 * LANES
    gi = lax.broadcasted_iota(jnp.int32, (pair, pair), 0) // LANES
    gj = lax.broadcasted_iota(jnp.int32, (pair, pair), 1) // LANES
    head_ones = (gi == gj).astype(BF16)

    def norm_rope(xp, gain):
        r = lax.rsqrt(_group_sumsq(xp, head_ones) * (1.0 / MLA_QK) + EPS)
        xn = xp * r * gain
        halves = []
        for t in range(2):
            xh = xn[:, LANES * t:LANES * (t + 1)]
            halves.append(xh * cos_t + pltpu.roll(xh, ROPE_HI_LANE, 1) * sin_t)
        return halves

    gmq = jnp.concatenate([gmq_ref[...]] * 2, axis=1)
    gmk = jnp.concatenate([gmk_ref[...]] * 2, axis=1)
    kr2 = jnp.concatenate([kr, kr], axis=1)
    shift_lanes = (lane >= MLA_QK) & (lane < MLA_QK + N_SPLIT)
    qshift = qshift_ref[...]
    for hp in range(MLA_HEADS // 2):
        sl = slice(pair * hp, pair * (hp + 1))
        q_heads = norm_rope(qf[:, sl], gmq)
        k_heads = norm_rope(kn[:, sl] + kr2, gmk)
        for t in range(2):
            hl = slice(pair * hp + LANES * t, pair * hp + LANES * (t + 1))
            qm_ref[0, :, hl] = jnp.where(shift_lanes, qshift, q_heads[t] * (MLA_QK ** -0.5 * LOG2E)).astype(BF16)
            km_ref[0, :, hl] = jnp.where(shift_lanes, 1.0, k_heads[t]).astype(BF16)


def _even_proj(x, pos3, gmix, w1, ones, gq, gk, qln, kvln, wuq, wuk, wuv, gmq, gmk, qshift, inv, ts):
    B, S, D = x.shape
    grid = (B, S // ts)
    tok = lambda c: pl.BlockSpec((1, ts, c), lambda b, i: (b, i, 0))
    consts = [gmix, w1, ones, gq, gk, qln, kvln, wuq, wuk, wuv, gmq, gmk, qshift, inv]
    out_cols = [SWA_Q_W, SWA_KV_W, SWA_KV_W, MLA_HEADS * LANES, MLA_HEADS * LANES]
    v_rows = MLA_HEADS * MLA_V
    return pl.pallas_call(
        _even_proj_kernel,
        grid=grid,
        in_specs=([tok(D), pl.BlockSpec((1, 1, ts), lambda b, i: (b, 0, i))]
                  + [_const_spec(c.shape) for c in consts]),
        out_specs=[tok(c) for c in out_cols] + [pl.BlockSpec((1, v_rows, ts), lambda b, i: (b, 0, i))],
        out_shape=([jax.ShapeDtypeStruct((B, S, c), BF16) for c in out_cols]
                   + [jax.ShapeDtypeStruct((B, v_rows, S), BF16)]),
        compiler_params=_params(("parallel", "parallel")),
        name="even_proj",
    )(x, pos3, *consts)


def _swa_place_q(q_ref, hd, lane):
    hk = hd // (SWA_HEADS // SWA_KV_HEADS)
    qq = q_ref[0, :, LANES * (hd // 2):LANES * (hd // 2 + 1)].astype(F32)
    xq = qq if hd % 2 == hk else pltpu.roll(qq, HEAD_DIM, 1)
    return jnp.where((lane >= HEAD_DIM * hk) & (lane < HEAD_DIM * (hk + 1)), xq, 0.0).astype(BF16)


def _swa_place_out(o_even, o_odd, cb, lane):
    placed = []
    for e, o in enumerate((o_even, o_odd)):
        hk = (2 * cb + e) // (SWA_HEADS // SWA_KV_HEADS)
        placed.append(o if e == hk else pltpu.roll(o, HEAD_DIM, 1))
    return jnp.where(lane < HEAD_DIM, placed[0], placed[1]).astype(BF16)


def _swa_running_max(sink_ref, q_ref, k_ref, v_ref, o_ref, qi, tq):
    win = tq + SWA_WINDOW
    wstart = pl.multiple_of(jnp.maximum(qi * tq - SWA_WINDOW, 0), SWA_WINDOW)
    kw = k_ref[0, pl.ds(wstart, win), :]
    vw = v_ref[0, pl.ds(wstart, win), :]
    row = lax.broadcasted_iota(jnp.int32, (tq, win), 0)
    col = lax.broadcasted_iota(jnp.int32, (tq, win), 1)
    dist = (qi * tq + row) - (wstart + col)
    valid = (dist >= 0) & (dist < SWA_WINDOW)
    distf = dist.astype(F32)
    lane = lax.broadcasted_iota(jnp.int32, (tq, LANES), 1)
    for cb in range(SWA_HEADS // 2):
        outs = []
        for e in range(2):
            hd = 2 * cb + e
            s = _dot_nt(_swa_place_q(q_ref, hd, lane), kw) - (2.0 ** (-8.0 * (hd + 1) / SWA_HEADS) * LOG2E) * distf
            s = jnp.where(valid, s, MASKED)
            sink = sink_ref[hd] * LOG2E
            m = jnp.maximum(jnp.max(s, axis=-1, keepdims=True), sink)
            p = jnp.exp2(s - m)
            l = jnp.sum(p, axis=-1, keepdims=True) + jnp.exp2(sink - m)
            outs.append(_dot(p.astype(BF16), vw) / l)
        o_ref[0, :, LANES * cb:LANES * (cb + 1)] = _swa_place_out(outs[0], outs[1], cb, lane)


def _swa_kernel(sink_ref, q_ref, k_ref, v_ref, o_ref, *, tq):
    qi = pl.program_id(1)
    sub = SWA_WINDOW
    win = 2 * sub
    n_sub = tq // sub
    bound = sink_ref[SWA_HEADS]
    lane_q = lax.broadcasted_iota(jnp.int32, (tq, LANES), 1)
    lane = lax.broadcasted_iota(jnp.int32, (sub, LANES), 1)
    row = lax.broadcasted_iota(jnp.int32, (sub, win), 0)
    col = lax.broadcasted_iota(jnp.int32, (sub, win), 1)

    def bias_mask(first_key_offset, slope):
        dist = row + first_key_offset - col
        return jnp.where((dist >= 0) & (dist < SWA_WINDOW), -slope * dist.astype(F32) - bound, MASKED)

    ones_blk = jnp.ones((win, LANES), BF16)
    windows = []
    for r in range(n_sub):
        wstart = qi * tq + sub * (r - 1)
        wstart = pl.multiple_of(jnp.maximum(wstart, 0) if r == 0 else wstart, sub)
        windows.append((k_ref[0, pl.ds(wstart, win), :],
                        jnp.concatenate([v_ref[0, pl.ds(wstart, win), :], ones_blk], axis=1)))

    outs = [[None] * SWA_HEADS for _ in range(n_sub)]
    min_sum = None
    for hd in range(SWA_HEADS):
        slope = 2.0 ** (-8.0 * (hd + 1) / SWA_HEADS) * LOG2E
        bm = bias_mask(sub, slope)
        bm_first = jnp.where(qi == 0, bias_mask(0, slope), bm)
        xq = _swa_place_q(q_ref, hd, lane_q)
        sink_term = jnp.exp2(sink_ref[hd] * LOG2E - bound)
        for r in range(n_sub):
            kw, v_ones = windows[r]
            s = _dot_nt(xq[sub * r:sub * (r + 1)], kw) + (bm_first if r == 0 else bm)
            ov = _dot(jnp.exp2(s).astype(BF16), v_ones)
            l = ov[:, LANES:] + sink_term
            outs[r][hd] = ov[:, :LANES] / l
            piece_min = jnp.min(l)
            min_sum = piece_min if min_sum is None else jnp.minimum(min_sum, piece_min)
    ok = min_sum >= UNDERFLOW_GUARD

    @pl.when(ok)
    def _():
        for r in range(n_sub):
            for cb in range(SWA_HEADS // 2):
                o_ref[0, sub * r:sub * (r + 1), LANES * cb:LANES * (cb + 1)] = _swa_place_out(
                    outs[r][2 * cb], outs[r][2 * cb + 1], cb, lane)

    @pl.when(jnp.logical_not(ok))
    def _():
        _swa_running_max(sink_ref, q_ref, k_ref, v_ref, o_ref, qi, tq)


def _swa_attention(sinks, qa, ka, va, tq):
    B, S, _ = qa.shape
    return pl.pallas_call(
        functools.partial(_swa_kernel, tq=tq),
        grid=(B, S // tq),
        in_specs=[
            pl.BlockSpec(memory_space=pltpu.SMEM),
            pl.BlockSpec((1, tq, SWA_Q_W), lambda b, i: (b, i, 0)),
            pl.BlockSpec((1, S, SWA_KV_W), lambda b, i: (b, 0, 0)),
            pl.BlockSpec((1, S, SWA_KV_W), lambda b, i: (b, 0, 0)),
        ],
        out_specs=pl.BlockSpec((1, tq, SWA_Q_W), lambda b, i: (b, i, 0)),
        out_shape=jax.ShapeDtypeStruct((B, S, SWA_Q_W), BF16),
        compiler_params=_params(("parallel", "arbitrary")),
        name="swa_attention",
    )(sinks, qa, ka, va)


def _split3(x):
    hi = x.astype(BF16).astype(F32)
    r = x - hi
    mid = r.astype(BF16).astype(F32)
    lo = (r - mid).astype(BF16).astype(F32)
    return hi, mid, lo


def _score_bound(q_gain, k_gain, dim, scale):
    return dim * scale * jnp.max(jnp.abs(q_gain)) * jnp.max(jnp.abs(k_gain))


FFN_CHUNKS = 2

BLOCKS_PER_TRIP = 4


def _loop_blocks(n, step, init):
    def trip(t, c):
        for u in range(BLOCKS_PER_TRIP):
            c = step(BLOCKS_PER_TRIP * t + u, c)
        return c

    full = n // BLOCKS_PER_TRIP
    carry = lax.fori_loop(0, full, trip, init)
    return lax.fori_loop(full * BLOCKS_PER_TRIP, n, step, carry)


def _add_cols(acc, lo, part):
    if lo == 0:
        return acc + part
    return acc + jnp.concatenate([jnp.zeros((acc.shape[0], lo), acc.dtype), part], axis=1)


def _diagonal_spans(tq):
    h = tq // 2
    return ((0, h, 0), (h, h, h)) if h % MXU_WIDTH == 0 else ((0, tq, 0),)


def _causal_t(key_lo, key_n, q_lo, tq):
    shape = (key_n, tq - q_lo)
    return (key_lo + lax.broadcasted_iota(jnp.int32, shape, 0)) <= (q_lo + lax.broadcasted_iota(jnp.int32, shape, 1))


def _causal_mask(tq):
    row = lax.broadcasted_iota(jnp.int32, (tq, tq), 0)
    col = lax.broadcasted_iota(jnp.int32, (tq, tq), 1)
    return row >= col


def _mla_running_max(q_ref, k_ref, vt_ref, e, qi, tq):
    causal = _causal_mask(tq)
    q = q_ref[0, :, LANES * e:LANES * (e + 1)]

    def step(j, carry, masked):
        m, l, acc = carry
        ks = pl.multiple_of(j * tq, tq)
        k = k_ref[0, pl.ds(ks, tq), LANES * e:LANES * (e + 1)]
        vt = vt_ref[0, :, pl.ds(ks, tq)]
        s = _dot_nt(q, k)
        if masked:
            s = jnp.where(causal, s, MASKED)
        m_new = jnp.maximum(m, jnp.max(s, axis=-1, keepdims=True))
        alpha = jnp.exp2(m - m_new)
        p = jnp.exp2(s - m_new)
        l = alpha * l + jnp.sum(p, axis=-1, keepdims=True)
        acc = alpha * acc + _dot_nt(p.astype(BF16), vt)
        return m_new, l, acc

    init = (jnp.full((tq, 1), MASKED, F32), jnp.zeros((tq, 1), F32), jnp.zeros((tq, LANES), F32))
    carry = lax.fori_loop(0, qi, lambda j, c: step(j, c, False), init)
    m, l, acc = step(qi, carry, True)
    return acc / l


def _mla_kernel(q_ref, k_ref, vt_ref, o_ref, *, tq):
    qi = pl.program_id(2)
    lane = lax.broadcasted_iota(jnp.int32, (tq, LANES), 1)
    q_aug = [q_ref[0, :, LANES * e:LANES * (e + 1)] for e in range(2)]

    def block(j, carry, key_lo=0, key_n=tq, q_lo=0, masked=False):
        ks = pl.multiple_of(j * tq + key_lo, key_n)
        new = []
        for e in range(2):
            lsum, acc = carry[e]
            st = _dot_nt(k_ref[0, pl.ds(ks, key_n), LANES * e:LANES * (e + 1)], q_aug[e][q_lo:, :])
            if masked:
                st = jnp.where(_causal_t(key_lo, key_n, q_lo, tq), st, MASKED)
            pt = jnp.exp2(st)
            part = jnp.sum(pt.reshape(key_n // SUBLANES, SUBLANES, tq - q_lo), axis=0)
            pv = _dot(vt_ref[0, MLA_V * e:MLA_V * (e + 1), pl.ds(ks, key_n)], pt.astype(BF16))
            new.append((_add_cols(lsum, q_lo, part), _add_cols(acc, q_lo, pv)))
        return tuple(new)

    zero = (jnp.zeros((SUBLANES, tq), F32), jnp.zeros((MLA_V, tq), F32))
    carry = _loop_blocks(qi, block, (zero, zero))
    for key_lo, key_n, q_lo in _diagonal_spans(tq):
        carry = block(qi, carry, key_lo, key_n, q_lo, masked=True)
    sums = [jnp.sum(lsum, axis=0, keepdims=True) for lsum, _ in carry]
    outs = [acc / l for (_, acc), l in zip(carry, sums)]
    ok = jnp.minimum(jnp.min(sums[0]), jnp.min(sums[1])) >= UNDERFLOW_GUARD

    @pl.when(ok)
    def _():
        o_ref[0] = jnp.concatenate(outs, axis=0).T.astype(BF16)

    @pl.when(jnp.logical_not(ok))
    def _():
        slow = [_mla_running_max(q_ref, k_ref, vt_ref, e, qi, tq) for e in range(2)]
        o_ref[0] = jnp.where(lane < MLA_V, slow[0], slow[1]).astype(BF16)


def _mla_attention(qm, km, vm, tq):
    B, S, _ = qm.shape
    return pl.pallas_call(
        functools.partial(_mla_kernel, tq=tq),
        grid=(B, MLA_HEADS // 2, S // tq),
        in_specs=[
            pl.BlockSpec((1, tq, 2 * LANES), lambda b, h, i: (b, i, h)),
            pl.BlockSpec((1, S, 2 * LANES), lambda b, h, i: (b, 0, h)),
            pl.BlockSpec((1, 2 * MLA_V, S), lambda b, h, i: (b, h, 0)),
        ],
        out_specs=pl.BlockSpec((1, tq, LANES), lambda b, h, i: (b, i, h)),
        out_shape=jax.ShapeDtypeStruct((B, S, MLA_HEADS * MLA_V), BF16),
        compiler_params=_params(("parallel", "parallel", "arbitrary")),
        name="mla_attention",
    )(qm, km, vm)


def _odd_proj_kernel(x_ref, gmix_ref, w_ref, wvt_ref, ones_ref, gq_ref, gk_ref, q_ref, k_ref, vt_ref):
    x = x_ref[0]
    h = _rms(x, gmix_ref[...]).astype(BF16)
    z = _dot(h, w_ref[...])
    ones = ones_ref[...]
    width = DIFF_HEADS * 2 * DIFF_DIM
    piece = gq_ref.shape[1]
    for c in range(0, width, piece):
        q = _group_norm64(z[:, c:c + piece], ones, gq_ref[...]) * (DIFF_DIM ** -0.5 * LOG2E)
        q_ref[0, :, c:c + piece] = q.astype(BF16)
        k = _group_norm64(z[:, width + c:width + c + piece], ones, gk_ref[...])
        k_ref[0, :, c:c + piece] = k.astype(BF16)
    vt_ref[0] = _dot_nt(wvt_ref[...], h).astype(BF16)


def _odd_proj(x, gmix, w_qk, w_vt, ones, gq, gk, ts):
    B, S, D = x.shape
    tok = pl.BlockSpec((1, ts, D), lambda b, i: (b, i, 0))
    consts = [gmix, w_qk, w_vt, ones, gq, gk]
    return pl.pallas_call(
        _odd_proj_kernel,
        grid=(B, S // ts),
        in_specs=[tok] + [_const_spec(c.shape) for c in consts],
        out_specs=[tok, tok, pl.BlockSpec((1, D, ts), lambda b, i: (b, 0, i))],
        out_shape=[jax.ShapeDtypeStruct((B, S, D), BF16)] * 2 + [jax.ShapeDtypeStruct((B, D, S), BF16)],
        compiler_params=_params(("parallel", "parallel")),
        name="odd_proj",
    )(x, *consts)


def _diff_q_parts(q, lane):
    zero = jnp.zeros_like(q)
    return jnp.where(lane < DIFF_DIM, q, zero), jnp.where(lane >= DIFF_DIM, q, zero)


def _diff_running_max(slope, q_ref, k_ref, vt_ref, qi, tq):
    row = lax.broadcasted_iota(jnp.int32, (tq, tq), 0)
    col = lax.broadcasted_iota(jnp.int32, (tq, tq), 1)
    causal = row >= col
    bias = slope * (col - row).astype(F32)
    lane = lax.broadcasted_iota(jnp.int32, (tq, LANES), 1)
    q_parts = _diff_q_parts(q_ref[0], lane)

    def step(j, carry, masked):
        ks = pl.multiple_of(j * tq, tq)
        k = k_ref[0, pl.ds(ks, tq), :]
        vt = vt_ref[0, :, pl.ds(ks, tq)]
        offset = slope * ((j - qi) * tq).astype(F32)
        new = []
        for c in range(2):
            m, l, acc = carry[c]
            s = _dot_nt(q_parts[c], k) + bias
            if masked:
                s = jnp.where(causal, s, MASKED)
            m_new = jnp.maximum(m, jnp.max(s, axis=-1, keepdims=True) + offset)
            alpha = jnp.exp2(m - m_new)
            p = jnp.exp2(s - (m_new - offset))
            l = alpha * l + jnp.sum(p, axis=-1, keepdims=True)
            acc = alpha * acc + _dot_nt(p.astype(BF16), vt)
            new.append((m_new, l, acc))
        return tuple(new)

    one = (jnp.full((tq, 1), MASKED, F32), jnp.zeros((tq, 1), F32), jnp.zeros((tq, LANES), F32))
    carry = lax.fori_loop(0, qi, lambda j, c: step(j, c, False), (one, one))
    (m1, l1, a1), (m2, l2, a2) = step(qi, carry, True)
    return a1 / l1, a2 / l2


def _diff_kernel(slope_ref, qx_ref, qpos_ref, q_ref, k_ref, kx_ref, vt_ref, lam_ref, subln_ref, o_ref, *, tq,
                 lambda_init):
    hd = pl.program_id(1)
    qi = pl.program_id(2)
    lane = lax.broadcasted_iota(jnp.int32, (tq, LANES), 1)
    q_parts = _diff_q_parts(q_ref[0], lane)
    head_row = qx_ref[0]
    qx = jnp.where(lane < 3 * N_SPLIT, head_row,
                   qpos_ref[...].astype(F32) * head_row[:, SLOPE_LANE:SLOPE_LANE + 1]).astype(BF16)
    q_aug = [jnp.concatenate([part, qx], axis=1) for part in q_parts]

    def block(j, carry, key_lo=0, key_n=tq, q_lo=0, masked=False):
        ks = pl.multiple_of(j * tq + key_lo, key_n)
        k = jnp.concatenate([k_ref[0, pl.ds(ks, key_n), :], kx_ref[pl.ds(ks, key_n), :]], axis=1)
        vt = vt_ref[0, :, pl.ds(ks, key_n)]
        new = []
        for c in range(2):
            lsum, acc = carry[c]
            st = _dot_nt(k, q_aug[c][q_lo:, :])
            if masked:
                st = jnp.where(_causal_t(key_lo, key_n, q_lo, tq), st, MASKED)
            pt = jnp.exp2(st)
            part = jnp.sum(pt.reshape(key_n // SUBLANES, SUBLANES, tq - q_lo), axis=0)
            new.append((_add_cols(lsum, q_lo, part), _add_cols(acc, q_lo, _dot(vt, pt.astype(BF16)))))
        return tuple(new)

    zero = (jnp.zeros((SUBLANES, tq), F32), jnp.zeros((LANES, tq), F32))
    carry = _loop_blocks(qi, block, (zero, zero))
    for key_lo, key_n, q_lo in _diagonal_spans(tq):
        carry = block(qi, carry, key_lo, key_n, q_lo, masked=True)
    (ls1, a1), (ls2, a2) = carry
    l1 = jnp.sum(ls1, axis=0, keepdims=True)
    l2 = jnp.sum(ls2, axis=0, keepdims=True)
    ok = jnp.minimum(jnp.min(l1), jnp.min(l2)) >= UNDERFLOW_GUARD

    lf = lam_ref[...]
    lam = (jnp.exp(jnp.sum(lf[0:1] * lf[1:2], axis=-1, keepdims=True))
           - jnp.exp(jnp.sum(lf[2:3] * lf[3:4], axis=-1, keepdims=True)) + lambda_init)

    gain = subln_ref[...] * (1.0 - lambda_init)

    @pl.when(ok)
    def _():
        ot = a1 / l1 - lam * (a2 / l2)
        ot = ot * lax.rsqrt(jnp.mean(ot * ot, axis=0, keepdims=True) + EPS)
        o_ref[0] = (ot.T * gain).astype(BF16)

    @pl.when(jnp.logical_not(ok))
    def _():
        o1, o2 = _diff_running_max(slope_ref[hd] * LOG2E, q_ref, k_ref, vt_ref, qi, tq)
        o_ref[0] = _rms(o1 - lam * o2, gain).astype(BF16)


SLOPE_LANE = 5 * N_SPLIT


def _alibi_tables(slopes, bound, S):
    assert all(math.log2(s) == int(math.log2(s)) for s in slopes), "ALiBi slopes must be powers of two"
    n = N_SPLIT
    log2e_terms = jnp.concatenate(_split3(jnp.full((1, 1), LOG2E, F32)), axis=1)
    s = jnp.asarray(slopes, F32).reshape(-1, 1)
    head = jnp.concatenate(
        [s * log2e_terms * float(LANES), s * log2e_terms, jnp.broadcast_to(_shift_row(bound, 0)[:, :n], (len(slopes), n)),
         jnp.zeros((len(slopes), SLOPE_LANE - 3 * n), F32), s, jnp.zeros((len(slopes), LANES - SLOPE_LANE - 1), F32)],
        axis=1)
    idx = np.arange(S)
    hi, lo = (idx // LANES)[:, None].astype(np.float32), (idx % LANES)[:, None].astype(np.float32)
    qpos = np.zeros((S, LANES), np.float32)
    qpos[:, 3 * n:4 * n] = hi * LANES
    qpos[:, 4 * n:5 * n] = lo
    kx = jnp.concatenate(
        [jnp.asarray(np.concatenate([np.repeat(hi, n, 1), np.repeat(lo, n, 1), np.ones((S, n), np.float32)], axis=1)),
         jnp.broadcast_to(-log2e_terms, (S, n)), jnp.broadcast_to(-log2e_terms, (S, n)),
         jnp.zeros((S, LANES - 5 * n), F32)], axis=1)
    return head.reshape(-1, 1, LANES), jnp.asarray(qpos, dtype=BF16), kx.astype(BF16)


def _diff_attention(slopes, bound, q, k, vt, lambdas, subln, tq, lambda_init):
    B, S, D = q.shape
    qx, qpos, kx = _alibi_tables(slopes, bound, S)
    return pl.pallas_call(
        functools.partial(_diff_kernel, tq=tq, lambda_init=lambda_init),
        grid=(B, DIFF_HEADS, S // tq),
        in_specs=[
            pl.BlockSpec(memory_space=pltpu.SMEM),
            pl.BlockSpec((1, 1, LANES), lambda b, h, i: (h, 0, 0)),
            pl.BlockSpec((tq, LANES), lambda b, h, i: (i, 0)),
            pl.BlockSpec((1, tq, LANES), lambda b, h, i: (b, i, h)),
            pl.BlockSpec((1, S, LANES), lambda b, h, i: (b, 0, h)),
            _const_spec(kx.shape),
            pl.BlockSpec((1, LANES, S), lambda b, h, i: (b, h, 0)),
            pl.BlockSpec(lambdas.shape, lambda b, h, i: (0, 0)),
            pl.BlockSpec(subln.shape, lambda b, h, i: (0, 0)),
        ],
        out_specs=pl.BlockSpec((1, tq, LANES), lambda b, h, i: (b, i, h)),
        out_shape=jax.ShapeDtypeStruct((B, S, D), BF16),
        compiler_params=_params(("parallel", "parallel", "arbitrary")),
        name="diff_attention",
    )(jnp.asarray(slopes, F32), qx, qpos, q, k, kx, vt, lambdas, subln)


def _mem_kv_kernel(mem_ref, norm_ref, w_ref, gain_ref, k_ref, v_ref):
    h = _rms(mem_ref[0], norm_ref[0]).astype(BF16)
    z = _dot(h, w_ref[0])
    width = MEM_HEADS * MEM_HEAD_DIM
    gain = gain_ref[0]
    for hd in range(MEM_HEADS):
        sl = slice(MEM_HEAD_DIM * hd, MEM_HEAD_DIM * (hd + 1))
        k_ref[0, 0, :, sl] = _rms(z[:, sl], gain).astype(BF16)
    v_ref[0, 0] = z[:, width:2 * width].astype(BF16)


def _mem_kv(mem, norms, w_kv, gains):
    B, M, D = mem.shape
    L = norms.shape[0]
    width = MEM_HEADS * MEM_HEAD_DIM
    out = pl.BlockSpec((1, 1, M, width), lambda l, b: (l, b, 0, 0))
    return pl.pallas_call(
        _mem_kv_kernel,
        grid=(L, B),
        in_specs=[
            pl.BlockSpec((1, M, D), lambda l, b: (b, 0, 0)),
            pl.BlockSpec((1, 1, D), lambda l, b: (l, 0, 0)),
            pl.BlockSpec((1, D, 2 * width), lambda l, b: (l, 0, 0)),
            pl.BlockSpec((1, 1, MEM_HEAD_DIM), lambda l, b: (l, 0, 0)),
        ],
        out_specs=[out, out],
        out_shape=[jax.ShapeDtypeStruct((L, B, M, width), BF16)] * 2,
        compiler_params=_params(("parallel", "parallel")),
        name="mem_kv",
    )(mem, norms, w_kv, gains)


def _post_kernel(*refs, n_attn):
    x_ref = refs[0]
    attn_refs = refs[1:1 + n_attn]
    wo_refs = refs[1 + n_attn:1 + 2 * n_attn]
    (mqn_ref, wq_ref, qg_ref, mk_ref, mv_ref, wmo_ref, fn_ref, wg_ref, wu_ref, wd_ref, o_ref) = refs[1 + 2 * n_attn:]
    x = x_ref[0]
    for a_ref, w_ref in zip(attn_refs, wo_refs):
        x = x + _dot(a_ref[0], w_ref[...])

    q = _dot(_rms(x, mqn_ref[...]).astype(BF16), wq_ref[...])
    heads = []
    for hd in range(MEM_HEADS):
        sl = slice(MEM_HEAD_DIM * hd, MEM_HEAD_DIM * (hd + 1))
        qh = (_rms(q[:, sl], qg_ref[...]) * (MEM_HEAD_DIM ** -0.5 * LOG2E)).astype(BF16)
        s = _dot_nt(qh, mk_ref[0, :, sl])
        p = jnp.exp2(s - jnp.max(s, axis=-1, keepdims=True))
        l = jnp.sum(p, axis=-1, keepdims=True)
        heads.append((_dot(p.astype(BF16), mv_ref[0, :, sl]) / l).astype(BF16))
    x = x + _dot(jnp.concatenate(heads, axis=-1), wmo_ref[...])

    hf = _rms(x, fn_ref[...]).astype(BF16)
    d_ff = wg_ref.shape[1]
    chunk = d_ff // FFN_CHUNKS
    y = x
    for c in range(FFN_CHUNKS):
        cols = slice(chunk * c, chunk * (c + 1))
        g = _dot(hf, wg_ref[:, cols])
        u = _dot(hf, wu_ref[:, cols])
        act = (g * jax.nn.sigmoid(g) * u).astype(BF16)
        y = y + _dot(act, wd_ref[cols, :])
    o_ref[0] = y


def _post_block(x, attns, wos, mqn, wq, qg, mk, mv, wmo, fn, wg, wu, wd, ts):
    B, S, D = x.shape
    n = len(attns)
    tok = lambda c: pl.BlockSpec((1, ts, c), lambda b, i: (b, i, 0))
    memspec = pl.BlockSpec((1,) + mk.shape[1:], lambda b, i: (b, 0, 0))
    consts_a = list(wos) + [mqn, wq, qg]
    consts_b = [wmo, fn, wg, wu, wd]
    return pl.pallas_call(
        functools.partial(_post_kernel, n_attn=n),
        grid=(B, S // ts),
        in_specs=([tok(D)] + [tok(a.shape[-1]) for a in attns] + [_const_spec(c.shape) for c in consts_a]
                  + [memspec, memspec] + [_const_spec(c.shape) for c in consts_b]),
        out_specs=tok(D),
        out_shape=jax.ShapeDtypeStruct((B, S, D), F32),
        compiler_params=_params(("parallel", "parallel")),
        name="post_block",
    )(x, *attns, *consts_a, mk, mv, *consts_b)


def _row(v):
    return v.reshape(1, -1).astype(F32)


def _shift_row(bound, first_lane):
    terms = jnp.stack(_split3(-jnp.asarray(bound, F32))).reshape(1, N_SPLIT)
    return jnp.pad(terms, ((0, 0), (first_lane, LANES - first_lane - N_SPLIT)))


def _pad_cols(w, lo, total):
    return jnp.pad(w, ((0, 0), (lo, total - lo - w.shape[1])))


def _mla_lane_perm():
    half = MLA_ROPE // 2
    lanes = np.arange(MLA_QK)
    return np.where(lanes < half, MLA_NOPE + lanes,
                    np.where(lanes < ROPE_HI_LANE, lanes - half,
                             np.where(lanes < ROPE_HI_LANE + half, lanes + half, lanes - MLA_ROPE)))


def _even_weights(w_in, w_uq, w_ukv, q_gain, k_gain):
    perm = _mla_lane_perm()
    is_nope = perm < MLA_NOPE
    pad = LANES - MLA_QK

    def head_cols(w_head, live):
        return jnp.pad(jnp.where(live[None, :], w_head[:, perm], 0.0), ((0, 0), (0, pad)))

    c_kr = EVEN_COLS[-1]
    kr_src = jnp.concatenate([jnp.zeros((w_in.shape[0], MLA_NOPE), w_in.dtype), w_in[:, c_kr:c_kr + MLA_ROPE]], axis=1)
    w1 = jnp.concatenate([w_in[:, :c_kr], head_cols(kr_src, ~is_nope)], axis=1).astype(BF16)
    wuq = jnp.concatenate(
        [head_cols(w_uq[:, MLA_QK * h:MLA_QK * (h + 1)], np.ones_like(is_nope)) for h in range(MLA_HEADS)],
        axis=1).astype(BF16)
    kv_w = MLA_NOPE + MLA_V
    rope_zeros = jnp.zeros((w_ukv.shape[0], MLA_ROPE), w_ukv.dtype)
    wuk = jnp.concatenate(
        [head_cols(jnp.concatenate([w_ukv[:, kv_w * h:kv_w * h + MLA_NOPE], rope_zeros], axis=1), is_nope)
         for h in range(MLA_HEADS)], axis=1).astype(BF16)
    wuv = jnp.concatenate(
        [w_ukv[:, kv_w * h + MLA_NOPE:kv_w * (h + 1)] for h in range(MLA_HEADS)], axis=1).astype(BF16).T
    gmq = _row(jnp.pad(q_gain[perm], (0, pad)))
    gmk = _row(jnp.pad(k_gain[perm], (0, pad)))
    return w1, wuq, wuk, wuv, gmq, gmk


def _rope_freqs():
    half = MLA_ROPE // 2
    freqs = (np.float32(ROPE_THETA) ** (-np.arange(half, dtype=np.float32) / np.float32(half))).astype(np.float32)
    return jnp.asarray(freqs.reshape(half, 1))


def _ones_blocks(n):
    g = np.arange(n) // HEAD_DIM
    return jnp.asarray((g[:, None] == g[None, :]).astype(np.float32), dtype=BF16)


def _tile_sizes(S):
    token_tile = min(2 * MXU_WIDTH, S)
    return token_tile, token_tile, min(2 * MXU_WIDTH, S - SWA_WINDOW), min(4 * MXU_WIDTH, S)


def kernel(x, mem, positions, mix_norm, ev_w_in, ev_swa_q_gain, ev_swa_k_gain, ev_sinks, ev_q_latent_norm,
           ev_kv_latent_norm, ev_w_uq, ev_w_ukv, ev_mla_q_gain, ev_mla_k_gain, ev_w_out, od_w_qkv, od_q_gain,
           od_k_gain, od_lambda, od_subln, od_w_out, mem_q_norm, mem_kv_norm, mem_w_q, mem_w_kv, mem_q_gain,
           mem_k_gain, mem_w_out, ffn_norm, ffn_w_gate, ffn_w_up, ffn_w_down):
    B, S, D = x.shape
    depth = mix_norm.shape[0]
    ts_proj, ts_post, tq_swa, tq_flash = _tile_sizes(S)

    ones = _ones_blocks(MXU_WIDTH)
    pos3 = positions.reshape(B, 1, S)
    inv = _rope_freqs()
    mem_k, mem_v = _mem_kv(mem, mem_kv_norm.reshape(depth, 1, D), mem_w_kv.astype(BF16),
                           mem_k_gain.reshape(depth, 1, MEM_HEAD_DIM))
    diff_slopes = [2.0 ** (-8.0 * (i + 1) / DIFF_HEADS) for i in range(DIFF_HEADS)]

    for l in range(depth):
        if l % 2 == 0:
            e = l // 2
            w1, wuq, wuk, wuv, gmq, gmk = _even_weights(ev_w_in[e], ev_w_uq[e], ev_w_ukv[e], ev_mla_q_gain[e],
                                                        ev_mla_k_gain[e])
            qa, ka, va, qm, km, vm = _even_proj(
                x, pos3, _row(mix_norm[l]), w1, ones, _row(jnp.tile(ev_swa_q_gain[e], SWA_HEADS)),
                _row(jnp.tile(ev_swa_k_gain[e], SWA_KV_HEADS)), _row(ev_q_latent_norm[e]),
                _row(ev_kv_latent_norm[e]), wuq, wuk, wuv, gmq, gmk,
                _shift_row(_score_bound(ev_mla_q_gain[e], ev_mla_k_gain[e], MLA_QK, MLA_QK ** -0.5 * LOG2E), MLA_QK),
                inv, ts_proj)
            swa_bound = _score_bound(ev_swa_q_gain[e], ev_swa_k_gain[e], HEAD_DIM, HEAD_DIM ** -0.5 * LOG2E)
            swa_scalars = jnp.concatenate([ev_sinks[e].astype(F32), swa_bound.reshape(1).astype(F32)])
            out_a = _swa_attention(swa_scalars, qa, ka, va, tq_swa)
            out_b = _mla_attention(qm, km, vm, tq_flash)
            wo = ev_w_out[e].astype(BF16)
            attns = [out_a, out_b]
            wos = [wo[:SWA_Q_W], wo[SWA_Q_W:]]
        else:
            o = l // 2
            lambda_init = 0.8 - 0.6 * math.exp(-0.3 * l)
            n_grp = DIFF_HEADS * 2
            w_qkv = od_w_qkv[o].astype(BF16)
            q, k, vt = _odd_proj(x, _row(mix_norm[l]), w_qkv[:, :2 * D], w_qkv[:, 2 * D:].T, ones,
                                 _row(jnp.tile(od_q_gain[o], n_grp // 2)), _row(jnp.tile(od_k_gain[o], n_grp // 2)),
                                 ts_proj)
            bound = _score_bound(od_q_gain[o], od_k_gain[o], DIFF_DIM, DIFF_DIM ** -0.5 * LOG2E)
            out_d = _diff_attention(diff_slopes, bound, q, k, vt, od_lambda[o].astype(F32), _row(od_subln[o]),
                                    tq_flash, lambda_init)
            attns = [out_d]
            wos = [od_w_out[o].astype(BF16)]
        x = _post_block(x, attns, wos, _row(mem_q_norm[l]), mem_w_q[l].astype(BF16), _row(mem_q_gain[l]),
                        mem_k[l], mem_v[l], mem_w_out[l].astype(BF16), _row(ffn_norm[l]),
                        ffn_w_gate[l].astype(BF16), ffn_w_up[l].astype(BF16), ffn_w_down[l].astype(BF16), ts_post)
    return x
```

```python
import functools
import math

import numpy as np
import jax
import jax.numpy as jnp
from jax import lax
from jax.experimental import pallas as pl
from jax.experimental.pallas import tpu as pltpu

F32 = jnp.float32
BF16 = jnp.bfloat16

EPS = 1e-6
MASKED = -1e30
LOG2E = 1.4426950408889634
ROPE_THETA = 10000.0

LANES = 128
SUBLANES = 8
MXU_WIDTH = 256
HEAD_DIM = 64
SWA_HEADS = 8
SWA_KV_HEADS = 2
SWA_WINDOW = 128
MLA_HEADS = 8
MLA_NOPE = 64
MLA_ROPE = 32
MLA_QK = MLA_NOPE + MLA_ROPE
MLA_V = 64
ROPE_HI_LANE = 64
DIFF_HEADS = 8
DIFF_DIM = 64
MEM_HEADS = 4
MEM_HEAD_DIM = 128
MLA_Q_RANK = 256
MLA_KV_RANK = 128
SWA_Q_W = SWA_HEADS * HEAD_DIM
SWA_KV_W = SWA_KV_HEADS * HEAD_DIM
EVEN_COLS = tuple(np.cumsum([SWA_Q_W, SWA_KV_W, SWA_KV_W, MLA_Q_RANK, MLA_KV_RANK]).tolist())

VMEM_LIMIT = 56 * 1024 * 1024

UNDERFLOW_GUARD = 2.0 ** -80
N_SPLIT = 3


def _params(sem):
    return pltpu.CompilerParams(dimension_semantics=sem, vmem_limit_bytes=VMEM_LIMIT)


def _const_spec(shape):
    nd = len(shape)
    return pl.BlockSpec(shape, lambda *_: (0,) * nd, pipeline_mode=pl.Buffered(1))


def _rms(x, g):
    ms = jnp.mean(x * x, axis=-1, keepdims=True)
    return x * lax.rsqrt(ms + EPS) * g


def _dot(a, b):
    return jnp.dot(a, b, preferred_element_type=F32)


def _dot_nt(a, b):
    return lax.dot_general(a, b, (((1,), (1,)), ((), ())), preferred_element_type=F32)


def _group_sumsq(x, ones_blk):
    x2 = x * x
    hi = x2.astype(BF16)
    lo = (x2 - hi.astype(F32)).astype(BF16)
    return _dot(hi, ones_blk) + _dot(lo, ones_blk)


def _group_norm64(x, ones_blk, gain):
    chunk = min(ones_blk.shape[0], x.shape[1])
    ssq = jnp.concatenate([_group_sumsq(x[:, c:c + chunk], ones_blk[0:chunk, 0:chunk])
                           for c in range(0, x.shape[1], chunk)], axis=1)
    return x * lax.rsqrt(ssq * (1.0 / HEAD_DIM) + EPS) * gain


def _even_proj_kernel(x_ref, pos_ref, gmix_ref, w1_ref, ones_ref, gq_ref, gk_ref, qln_ref, kvln_ref,
                      wuq_ref, wuk_ref, wuv_ref, gmq_ref, gmk_ref, qshift_ref, inv_ref,
                      qa_ref, ka_ref, va_ref, qm_ref, km_ref, vm_ref):
    x = x_ref[0]
    h = _rms(x, gmix_ref[...]).astype(BF16)
    z = _dot(h, w1_ref[...])
    c_ka, c_va, c_cq, c_ckv, c_kr = EVEN_COLS
    ones = ones_ref[...]
    qa = _group_norm64(z[:, 0:c_ka], ones, gq_ref[...]) * (HEAD_DIM ** -0.5 * LOG2E)
    qa_ref[0] = qa.astype(BF16)
    ka = _group_norm64(z[:, c_ka:c_va], ones[0:SWA_KV_W, 0:SWA_KV_W], gk_ref[...])
    ka_ref[0] = ka.astype(BF16)
    va_ref[0] = z[:, c_va:c_cq].astype(BF16)

    cqn = _rms(z[:, c_cq:c_ckv], qln_ref[...]).astype(BF16)
    ckvn = _rms(z[:, c_ckv:c_kr], kvln_ref[...]).astype(BF16)
    kr = z[:, c_kr:c_kr + LANES]
    qf = _dot(cqn, wuq_ref[...])
    kn = _dot(ckvn, wuk_ref[...])
    vm_ref[0] = _dot_nt(wuv_ref[...], ckvn).astype(BF16)

    half = MLA_ROPE // 2
    ang = inv_ref[...] * pos_ref[0].astype(F32)
    pad = jnp.zeros((LANES - half, ang.shape[1]), F32)
    cos = jnp.concatenate([jnp.cos(ang), pad], axis=0).T
    sin = jnp.concatenate([jnp.sin(ang), pad], axis=0).T
    lane = lax.broadcasted_iota(jnp.int32, cos.shape, 1)
    in_lo = lane < half
    in_hi = (lane >= ROPE_HI_LANE) & (lane < ROPE_HI_LANE + half)
    cos_t = jnp.where(in_lo, cos, jnp.where(in_hi, pltpu.roll(cos, ROPE_HI_LANE, 1), 1.0))
    sin_t = jnp.where(in_lo, -sin, jnp.where(in_hi, pltpu.roll(sin, ROPE_HI_LANE, 1), 0.0))

    pair = 2 * LANES
    gi = lax.broadcasted_iota(jnp.int32, (pair, pair), 0) // LANES
    gj = lax.broadcasted_iota(jnp.int32, (pair, pair), 1) // LANES
    head_ones = (gi == gj).astype(BF16)

    def norm_rope(xp, gain):
        r = lax.rsqrt(_group_sumsq(xp, head_ones) * (1.0 / MLA_QK) + EPS)
        xn = xp * r * gain
        halves = []
        for t in range(2):
            xh = xn[:, LANES * t:LANES * (t + 1)]
            halves.append(xh * cos_t + pltpu.roll(xh, ROPE_HI_LANE, 1) * sin_t)
        return halves

    gmq = jnp.concatenate([gmq_ref[...]] * 2, axis=1)
    gmk = jnp.concatenate([gmk_ref[...]] * 2, axis=1)
    kr2 = jnp.concatenate([kr, kr], axis=1)
    shift_lanes = (lane >= MLA_QK) & (lane < MLA_QK + N_SPLIT)
    qshift = qshift_ref[...]
    for hp in range(MLA_HEADS // 2):
        sl = slice(pair * hp, pair * (hp + 1))
        q_heads = norm_rope(qf[:, sl], gmq)
        k_heads = norm_rope(kn[:, sl] + kr2, gmk)
        for t in range(2):
            hl = slice(pair * hp + LANES * t, pair * hp + LANES * (t + 1))
            qm_ref[0, :, hl] = jnp.where(shift_lanes, qshift, q_heads[t] * (MLA_QK ** -0.5 * LOG2E)).astype(BF16)
            km_ref[0, :, hl] = jnp.where(shift_lanes, 1.0, k_heads[t]).astype(BF16)


def _even_proj(x, pos3, gmix, w1, ones, gq, gk, qln, kvln, wuq, wuk, wuv, gmq, gmk, qshift, inv, ts):
    B, S, D = x.shape
    grid = (B, S // ts)
    tok = lambda c: pl.BlockSpec((1, ts, c), lambda b, i: (b, i, 0))
    consts = [gmix, w1, ones, gq, gk, qln, kvln, wuq, wuk, wuv, gmq, gmk, qshift, inv]
    out_cols = [SWA_Q_W, SWA_KV_W, SWA_KV_W, MLA_HEADS * LANES, MLA_HEADS * LANES]
    v_rows = MLA_HEADS * MLA_V
    return pl.pallas_call(
        _even_proj_kernel,
        grid=grid,
        in_specs=([tok(D), pl.BlockSpec((1, 1, ts), lambda b, i: (b, 0, i))]
                  + [_const_spec(c.shape) for c in consts]),
        out_specs=[tok(c) for c in out_cols] + [pl.BlockSpec((1, v_rows, ts), lambda b, i: (b, 0, i))],
        out_shape=([jax.ShapeDtypeStruct((B, S, c), BF16) for c in out_cols]
                   + [jax.ShapeDtypeStruct((B, v_rows, S), BF16)]),
        compiler_params=_params(("parallel", "parallel")),
        name="even_proj",
    )(x, pos3, *consts)


def _swa_place_q(q_ref, hd, lane):
    hk = hd // (SWA_HEADS // SWA_KV_HEADS)
    qq = q_ref[0, :, LANES * (hd // 2):LANES * (hd // 2 + 1)].astype(F32)
    xq = qq if hd % 2 == hk else pltpu.roll(qq, HEAD_DIM, 1)
    return jnp.where((lane >= HEAD_DIM * hk) & (lane < HEAD_DIM * (hk + 1)), xq, 0.0).astype(BF16)


def _swa_place_out(o_even, o_odd, cb, lane):
    placed = []
    for e, o in enumerate((o_even, o_odd)):
        hk = (2 * cb + e) // (SWA_HEADS // SWA_KV_HEADS)
        placed.append(o if e == hk else pltpu.roll(o, HEAD_DIM, 1))
    return jnp.where(lane < HEAD_DIM, placed[0], placed[1]).astype(BF16)


def _swa_running_max(sink_ref, q_ref, k_ref, v_ref, o_ref, qi, tq):
    win = tq + SWA_WINDOW
    wstart = pl.multiple_of(jnp.maximum(qi * tq - SWA_WINDOW, 0), SWA_WINDOW)
    kw = k_ref[0, pl.ds(wstart, win), :]
    vw = v_ref[0, pl.ds(wstart, win), :]
    row = lax.broadcasted_iota(jnp.int32, (tq, win), 0)
    col = lax.broadcasted_iota(jnp.int32, (tq, win), 1)
    dist = (qi * tq + row) - (wstart + col)
    valid = (dist >= 0) & (dist < SWA_WINDOW)
    distf = dist.astype(F32)
    lane = lax.broadcasted_iota(jnp.int32, (tq, LANES), 1)
    for cb in range(SWA_HEADS // 2):
        outs = []
        for e in range(2):
            hd = 2 * cb + e
            s = _dot_nt(_swa_place_q(q_ref, hd, lane), kw) - (2.0 ** (-8.0 * (hd + 1) / SWA_HEADS) * LOG2E) * distf
            s = jnp.where(valid, s, MASKED)
            sink = sink_ref[hd] * LOG2E
            m = jnp.maximum(jnp.max(s, axis=-1, keepdims=True), sink)
            p = jnp.exp2(s - m)
            l = jnp.sum(p, axis=-1, keepdims=True) + jnp.exp2(sink - m)
            outs.append(_dot(p.astype(BF16), vw) / l)
        o_ref[0, :, LANES * cb:LANES * (cb + 1)] = _swa_place_out(outs[0], outs[1], cb, lane)


def _swa_kernel(sink_ref, q_ref, k_ref, v_ref, o_ref, *, tq):
    qi = pl.program_id(1)
    sub = SWA_WINDOW
    win = 2 * sub
    n_sub = tq // sub
    bound = sink_ref[SWA_HEADS]
    lane_q = lax.broadcasted_iota(jnp.int32, (tq, LANES), 1)
    lane = lax.broadcasted_iota(jnp.int32, (sub, LANES), 1)
    row = lax.broadcasted_iota(jnp.int32, (sub, win), 0)
    col = lax.broadcasted_iota(jnp.int32, (sub, win), 1)

    def bias_mask(first_key_offset, slope):
        dist = row + first_key_offset - col
        return jnp.where((dist >= 0) & (dist < SWA_WINDOW), -slope * dist.astype(F32) - bound, MASKED)

    ones_blk = jnp.ones((win, LANES), BF16)
    windows = []
    for r in range(n_sub):
        wstart = qi * tq + sub * (r - 1)
        wstart = pl.multiple_of(jnp.maximum(wstart, 0) if r == 0 else wstart, sub)
        windows.append((k_ref[0, pl.ds(wstart, win), :],
                        jnp.concatenate([v_ref[0, pl.ds(wstart, win), :], ones_blk], axis=1)))

    outs = [[None] * SWA_HEADS for _ in range(n_sub)]
    min_sum = None
    for hd in range(SWA_HEADS):
        slope = 2.0 ** (-8.0 * (hd + 1) / SWA_HEADS) * LOG2E
        bm = bias_mask(sub, slope)
        bm_first = jnp.where(qi == 0, bias_mask(0, slope), bm)
        xq = _swa_place_q(q_ref, hd, lane_q)
        sink_term = jnp.exp2(sink_ref[hd] * LOG2E - bound)
        for r in range(n_sub):
            kw, v_ones = windows[r]
            s = _dot_nt(xq[sub * r:sub * (r + 1)], kw) + (bm_first if r == 0 else bm)
            ov = _dot(jnp.exp2(s).astype(BF16), v_ones)
            l = ov[:, LANES:] + sink_term
            outs[r][hd] = ov[:, :LANES] / l
            piece_min = jnp.min(l)
            min_sum = piece_min if min_sum is None else jnp.minimum(min_sum, piece_min)
    ok = min_sum >= UNDERFLOW_GUARD

    @pl.when(ok)
    def _():
        for r in range(n_sub):
            for cb in range(SWA_HEADS // 2):
                o_ref[0, sub * r:sub * (r + 1), LANES * cb:LANES * (cb + 1)] = _swa_place_out(
                    outs[r][2 * cb], outs[r][2 * cb + 1], cb, lane)

    @pl.when(jnp.logical_not(ok))
    def _():
        _swa_running_max(sink_ref, q_ref, k_ref, v_ref, o_ref, qi, tq)


def _swa_attention(sinks, qa, ka, va, tq):
    B, S, _ = qa.shape
    return pl.pallas_call(
        functools.partial(_swa_kernel, tq=tq),
        grid=(B, S // tq),
        in_specs=[
            pl.BlockSpec(memory_space=pltpu.SMEM),
            pl.BlockSpec((1, tq, SWA_Q_W), lambda b, i: (b, i, 0)),
            pl.BlockSpec((1, S, SWA_KV_W), lambda b, i: (b, 0, 0)),
            pl.BlockSpec((1, S, SWA_KV_W), lambda b, i: (b, 0, 0)),
        ],
        out_specs=pl.BlockSpec((1, tq, SWA_Q_W), lambda b, i: (b, i, 0)),
        out_shape=jax.ShapeDtypeStruct((B, S, SWA_Q_W), BF16),
        compiler_params=_params(("parallel", "arbitrary")),
        name="swa_attention",
    )(sinks, qa, ka, va)


def _split3(x):
    hi = x.astype(BF16).astype(F32)
    r = x - hi
    mid = r.astype(BF16).astype(F32)
    lo = (r - mid).astype(BF16).astype(F32)
    return hi, mid, lo


def _score_bound(q_gain, k_gain, dim, scale):
    return dim * scale * jnp.max(jnp.abs(q_gain)) * jnp.max(jnp.abs(k_gain))


FFN_CHUNKS = 2

BLOCKS_PER_TRIP = 4


def _causal_blocks(n, block, tq, init):
    assert BLOCKS_PER_TRIP == 4

    def trip(t, c):
        for u in range(BLOCKS_PER_TRIP):
            c = block(BLOCKS_PER_TRIP * t + u, c)
        return c

    def diagonal(c):
        for key_lo, key_n, q_lo in _diagonal_spans(tq):
            c = block(n, c, key_lo, key_n, q_lo, masked=True)
        return c

    full = n // BLOCKS_PER_TRIP
    done = full * BLOCKS_PER_TRIP
    left = n - done
    carry = lax.fori_loop(0, full, trip, init)
    carry = lax.cond(left >= 2, lambda c: block(done + 1, block(done, c)), lambda c: c, carry)
    return lax.cond(left % 2 == 1, lambda c: diagonal(block(n - 1, c)), diagonal, carry)


def _add_cols(acc, lo, part):
    if lo == 0:
        return acc + part
    return acc + jnp.concatenate([jnp.zeros((acc.shape[0], lo), acc.dtype), part], axis=1)


def _diagonal_spans(tq):
    h = tq // 2
    return ((0, h, 0), (h, h, h)) if h % MXU_WIDTH == 0 else ((0, tq, 0),)


def _causal_t(key_lo, key_n, q_lo, tq):
    shape = (key_n, tq - q_lo)
    return (key_lo + lax.broadcasted_iota(jnp.int32, shape, 0)) <= (q_lo + lax.broadcasted_iota(jnp.int32, shape, 1))


def _causal_mask(tq):
    row = lax.broadcasted_iota(jnp.int32, (tq, tq), 0)
    col = lax.broadcasted_iota(jnp.int32, (tq, tq), 1)
    return row >= col


def _mla_running_max(q_ref, k_ref, vt_ref, e, qi, tq):
    causal = _causal_mask(tq)
    q = q_ref[0, :, LANES * e:LANES * (e + 1)]

    def step(j, carry, masked):
        m, l, acc = carry
        ks = pl.multiple_of(j * tq, tq)
        k = k_ref[0, pl.ds(ks, tq), LANES * e:LANES * (e + 1)]
        vt = vt_ref[0, :, pl.ds(ks, tq)]
        s = _dot_nt(q, k)
        if masked:
            s = jnp.where(causal, s, MASKED)
        m_new = jnp.maximum(m, jnp.max(s, axis=-1, keepdims=True))
        alpha = jnp.exp2(m - m_new)
        p = jnp.exp2(s - m_new)
        l = alpha * l + jnp.sum(p, axis=-1, keepdims=True)
        acc = alpha * acc + _dot_nt(p.astype(BF16), vt)
        return m_new, l, acc

    init = (jnp.full((tq, 1), MASKED, F32), jnp.zeros((tq, 1), F32), jnp.zeros((tq, LANES), F32))
    carry = lax.fori_loop(0, qi, lambda j, c: step(j, c, False), init)
    m, l, acc = step(qi, carry, True)
    return acc / l


def _mla_kernel(q_ref, k_ref, vt_ref, o_ref, *, tq):
    qi = pl.program_id(2)
    lane = lax.broadcasted_iota(jnp.int32, (tq, LANES), 1)
    q_aug = [q_ref[0, :, LANES * e:LANES * (e + 1)] for e in range(2)]

    def block(j, carry, key_lo=0, key_n=tq, q_lo=0, masked=False):
        ks = pl.multiple_of(j * tq + key_lo, key_n)
        new = []
        for e in range(2):
            lsum, acc = carry[e]
            st = _dot_nt(k_ref[0, pl.ds(ks, key_n), LANES * e:LANES * (e + 1)], q_aug[e][q_lo:, :])
            if masked:
                st = jnp.where(_causal_t(key_lo, key_n, q_lo, tq), st, MASKED)
            pt = jnp.exp2(st)
            part = jnp.sum(pt.reshape(key_n // SUBLANES, SUBLANES, tq - q_lo), axis=0)
            pv = _dot(vt_ref[0, MLA_V * e:MLA_V * (e + 1), pl.ds(ks, key_n)], pt.astype(BF16))
            new.append((_add_cols(lsum, q_lo, part), _add_cols(acc, q_lo, pv)))
        return tuple(new)

    zero = (jnp.zeros((SUBLANES, tq), F32), jnp.zeros((MLA_V, tq), F32))
    carry = _causal_blocks(qi, block, tq, (zero, zero))
    sums = [jnp.sum(lsum, axis=0, keepdims=True) for lsum, _ in carry]
    outs = [acc / l for (_, acc), l in zip(carry, sums)]
    ok = jnp.minimum(jnp.min(sums[0]), jnp.min(sums[1])) >= UNDERFLOW_GUARD

    @pl.when(ok)
    def _():
        o_ref[0] = jnp.concatenate(outs, axis=0).T.astype(BF16)

    @pl.when(jnp.logical_not(ok))
    def _():
        slow = [_mla_running_max(q_ref, k_ref, vt_ref, e, qi, tq) for e in range(2)]
        o_ref[0] = jnp.where(lane < MLA_V, slow[0], slow[1]).astype(BF16)


def _mla_attention(qm, km, vm, tq):
    B, S, _ = qm.shape
    return pl.pallas_call(
        functools.partial(_mla_kernel, tq=tq),
        grid=(B, MLA_HEADS // 2, S // tq),
        in_specs=[
            pl.BlockSpec((1, tq, 2 * LANES), lambda b, h, i: (b, i, h)),
            pl.BlockSpec((1, S, 2 * LANES), lambda b, h, i: (b, 0, h)),
            pl.BlockSpec((1, 2 * MLA_V, S), lambda b, h, i: (b, h, 0)),
        ],
        out_specs=pl.BlockSpec((1, tq, LANES), lambda b, h, i: (b, i, h)),
        out_shape=jax.ShapeDtypeStruct((B, S, MLA_HEADS * MLA_V), BF16),
        compiler_params=_params(("parallel", "parallel", "arbitrary")),
        name="mla_attention",
    )(qm, km, vm)


def _odd_proj_kernel(x_ref, gmix_ref, w_ref, wvt_ref, ones_ref, gq_ref, gk_ref, q_ref, k_ref, vt_ref):
    x = x_ref[0]
    h = _rms(x, gmix_ref[...]).astype(BF16)
    z = _dot(h, w_ref[...])
    ones = ones_ref[...]
    width = DIFF_HEADS * 2 * DIFF_DIM
    piece = gq_ref.shape[1]
    for c in range(0, width, piece):
        q = _group_norm64(z[:, c:c + piece], ones, gq_ref[...]) * (DIFF_DIM ** -0.5 * LOG2E)
        q_ref[0, :, c:c + piece] = q.astype(BF16)
        k = _group_norm64(z[:, width + c:width + c + piece], ones, gk_ref[...])
        k_ref[0, :, c:c + piece] = k.astype(BF16)
    vt_ref[0] = _dot_nt(wvt_ref[...], h).astype(BF16)


def _odd_proj(x, gmix, w_qk, w_vt, ones, gq, gk, ts):
    B, S, D = x.shape
    tok = pl.BlockSpec((1, ts, D), lambda b, i: (b, i, 0))
    consts = [gmix, w_qk, w_vt, ones, gq, gk]
    return pl.pallas_call(
        _odd_proj_kernel,
        grid=(B, S // ts),
        in_specs=[tok] + [_const_spec(c.shape) for c in consts],
        out_specs=[tok, tok, pl.BlockSpec((1, D, ts), lambda b, i: (b, 0, i))],
        out_shape=[jax.ShapeDtypeStruct((B, S, D), BF16)] * 2 + [jax.ShapeDtypeStruct((B, D, S), BF16)],
        compiler_params=_params(("parallel", "parallel")),
        name="odd_proj",
    )(x, *consts)


def _diff_q_parts(q, lane):
    zero = jnp.zeros_like(q)
    return jnp.where(lane < DIFF_DIM, q, zero), jnp.where(lane >= DIFF_DIM, q, zero)


def _diff_running_max(slope, q_ref, k_ref, vt_ref, qi, tq):
    row = lax.broadcasted_iota(jnp.int32, (tq, tq), 0)
    col = lax.broadcasted_iota(jnp.int32, (tq, tq), 1)
    causal = row >= col
    bias = slope * (col - row).astype(F32)
    lane = lax.broadcasted_iota(jnp.int32, (tq, LANES), 1)
    q_parts = _diff_q_parts(q_ref[0], lane)

    def step(j, carry, masked):
        ks = pl.multiple_of(j * tq, tq)
        k = k_ref[0, pl.ds(ks, tq), :]
        vt = vt_ref[0, :, pl.ds(ks, tq)]
        offset = slope * ((j - qi) * tq).astype(F32)
        new = []
        for c in range(2):
            m, l, acc = carry[c]
            s = _dot_nt(q_parts[c], k) + bias
            if masked:
                s = jnp.where(causal, s, MASKED)
            m_new = jnp.maximum(m, jnp.max(s, axis=-1, keepdims=True) + offset)
            alpha = jnp.exp2(m - m_new)
            p = jnp.exp2(s - (m_new - offset))
            l = alpha * l + jnp.sum(p, axis=-1, keepdims=True)
            acc = alpha * acc + _dot_nt(p.astype(BF16), vt)
            new.append((m_new, l, acc))
        return tuple(new)

    one = (jnp.full((tq, 1), MASKED, F32), jnp.zeros((tq, 1), F32), jnp.zeros((tq, LANES), F32))
    carry = lax.fori_loop(0, qi, lambda j, c: step(j, c, False), (one, one))
    (m1, l1, a1), (m2, l2, a2) = step(qi, carry, True)
    return a1 / l1, a2 / l2


def _diff_kernel(slope_ref, qx_ref, qpos_ref, q_ref, k_ref, kx_ref, vt_ref, lam_ref, subln_ref, o_ref, *, tq,
                 lambda_init):
    hd = pl.program_id(1)
    qi = pl.program_id(2)
    lane = lax.broadcasted_iota(jnp.int32, (tq, LANES), 1)
    q_parts = _diff_q_parts(q_ref[0], lane)
    head_row = qx_ref[0]
    qx = jnp.where(lane < 3 * N_SPLIT, head_row,
                   qpos_ref[...].astype(F32) * head_row[:, SLOPE_LANE:SLOPE_LANE + 1]).astype(BF16)
    q_aug = [jnp.concatenate([part, qx], axis=1) for part in q_parts]

    def block(j, carry, key_lo=0, key_n=tq, q_lo=0, masked=False):
        ks = pl.multiple_of(j * tq + key_lo, key_n)
        k = jnp.concatenate([k_ref[0, pl.ds(ks, key_n), :], kx_ref[pl.ds(ks, key_n), :]], axis=1)
        vt = vt_ref[0, :, pl.ds(ks, key_n)]
        new = []
        for c in range(2):
            lsum, acc = carry[c]
            st = _dot_nt(k, q_aug[c][q_lo:, :])
            if masked:
                st = jnp.where(_causal_t(key_lo, key_n, q_lo, tq), st, MASKED)
            pt = jnp.exp2(st)
            part = jnp.sum(pt.reshape(key_n // SUBLANES, SUBLANES, tq - q_lo), axis=0)
            new.append((_add_cols(lsum, q_lo, part), _add_cols(acc, q_lo, _dot(vt, pt.astype(BF16)))))
        return tuple(new)

    zero = (jnp.zeros((SUBLANES, tq), F32), jnp.zeros((LANES, tq), F32))
    (ls1, a1), (ls2, a2) = _causal_blocks(qi, block, tq, (zero, zero))
    l1 = jnp.sum(ls1, axis=0, keepdims=True)
    l2 = jnp.sum(ls2, axis=0, keepdims=True)
    ok = jnp.minimum(jnp.min(l1), jnp.min(l2)) >= UNDERFLOW_GUARD

    lf = lam_ref[...]
    lam = (jnp.exp(jnp.sum(lf[0:1] * lf[1:2], axis=-1, keepdims=True))
           - jnp.exp(jnp.sum(lf[2:3] * lf[3:4], axis=-1, keepdims=True)) + lambda_init)

    gain = subln_ref[...] * (1.0 - lambda_init)

    @pl.when(ok)
    def _():
        ot = a1 / l1 - lam * (a2 / l2)
        ot = ot * lax.rsqrt(jnp.mean(ot * ot, axis=0, keepdims=True) + EPS)
        o_ref[0] = (ot.T * gain).astype(BF16)

    @pl.when(jnp.logical_not(ok))
    def _():
        o1, o2 = _diff_running_max(slope_ref[hd] * LOG2E, q_ref, k_ref, vt_ref, qi, tq)
        o_ref[0] = _rms(o1 - lam * o2, gain).astype(BF16)


SLOPE_LANE = 5 * N_SPLIT


def _alibi_tables(slopes, bound, S):
    assert all(math.log2(s) == int(math.log2(s)) for s in slopes), "ALiBi slopes must be powers of two"
    n = N_SPLIT
    log2e_terms = jnp.concatenate(_split3(jnp.full((1, 1), LOG2E, F32)), axis=1)
    s = jnp.asarray(slopes, F32).reshape(-1, 1)
    head = jnp.concatenate(
        [s * log2e_terms * float(LANES), s * log2e_terms, jnp.broadcast_to(_shift_row(bound, 0)[:, :n], (len(slopes), n)),
         jnp.zeros((len(slopes), SLOPE_LANE - 3 * n), F32), s, jnp.zeros((len(slopes), LANES - SLOPE_LANE - 1), F32)],
        axis=1)
    idx = np.arange(S)
    hi, lo = (idx // LANES)[:, None].astype(np.float32), (idx % LANES)[:, None].astype(np.float32)
    qpos = np.zeros((S, LANES), np.float32)
    qpos[:, 3 * n:4 * n] = hi * LANES
    qpos[:, 4 * n:5 * n] = lo
    kx = jnp.concatenate(
        [jnp.asarray(np.concatenate([np.repeat(hi, n, 1), np.repeat(lo, n, 1), np.ones((S, n), np.float32)], axis=1)),
         jnp.broadcast_to(-log2e_terms, (S, n)), jnp.broadcast_to(-log2e_terms, (S, n)),
         jnp.zeros((S, LANES - 5 * n), F32)], axis=1)
    return head.reshape(-1, 1, LANES), jnp.asarray(qpos, dtype=BF16), kx.astype(BF16)


def _diff_attention(slopes, bound, q, k, vt, lambdas, subln, tq, lambda_init):
    B, S, D = q.shape
    qx, qpos, kx = _alibi_tables(slopes, bound, S)
    return pl.pallas_call(
        functools.partial(_diff_kernel, tq=tq, lambda_init=lambda_init),
        grid=(B, DIFF_HEADS, S // tq),
        in_specs=[
            pl.BlockSpec(memory_space=pltpu.SMEM),
            pl.BlockSpec((1, 1, LANES), lambda b, h, i: (h, 0, 0)),
            pl.BlockSpec((tq, LANES), lambda b, h, i: (i, 0)),
            pl.BlockSpec((1, tq, LANES), lambda b, h, i: (b, i, h)),
            pl.BlockSpec((1, S, LANES), lambda b, h, i: (b, 0, h)),
            _const_spec(kx.shape),
            pl.BlockSpec((1, LANES, S), lambda b, h, i: (b, h, 0)),
            pl.BlockSpec(lambdas.shape, lambda b, h, i: (0, 0)),
            pl.BlockSpec(subln.shape, lambda b, h, i: (0, 0)),
        ],
        out_specs=pl.BlockSpec((1, tq, LANES), lambda b, h, i: (b, i, h)),
        out_shape=jax.ShapeDtypeStruct((B, S, D), BF16),
        compiler_params=_params(("parallel", "parallel", "arbitrary")),
        name="diff_attention",
    )(jnp.asarray(slopes, F32), qx, qpos, q, k, kx, vt, lambdas, subln)


def _mem_kv_kernel(mem_ref, norm_ref, w_ref, gain_ref, k_ref, v_ref):
    h = _rms(mem_ref[0], norm_ref[0]).astype(BF16)
    z = _dot(h, w_ref[0])
    width = MEM_HEADS * MEM_HEAD_DIM
    gain = gain_ref[0]
    for hd in range(MEM_HEADS):
        sl = slice(MEM_HEAD_DIM * hd, MEM_HEAD_DIM * (hd + 1))
        k_ref[0, 0, :, sl] = _rms(z[:, sl], gain).astype(BF16)
    v_ref[0, 0] = z[:, width:2 * width].astype(BF16)


def _mem_kv(mem, norms, w_kv, gains):
    B, M, D = mem.shape
    L = norms.shape[0]
    width = MEM_HEADS * MEM_HEAD_DIM
    out = pl.BlockSpec((1, 1, M, width), lambda l, b: (l, b, 0, 0))
    return pl.pallas_call(
        _mem_kv_kernel,
        grid=(L, B),
        in_specs=[
            pl.BlockSpec((1, M, D), lambda l, b: (b, 0, 0)),
            pl.BlockSpec((1, 1, D), lambda l, b: (l, 0, 0)),
            pl.BlockSpec((1, D, 2 * width), lambda l, b: (l, 0, 0)),
            pl.BlockSpec((1, 1, MEM_HEAD_DIM), lambda l, b: (l, 0, 0)),
        ],
        out_specs=[out, out],
        out_shape=[jax.ShapeDtypeStruct((L, B, M, width), BF16)] * 2,
        compiler_params=_params(("parallel", "parallel")),
        name="mem_kv",
    )(mem, norms, w_kv, gains)


def _post_kernel(*refs, n_attn):
    x_ref = refs[0]
    attn_refs = refs[1:1 + n_attn]
    wo_refs = refs[1 + n_attn:1 + 2 * n_attn]
    (mqn_ref, wq_ref, qg_ref, mk_ref, mv_ref, wmo_ref, fn_ref, wg_ref, wu_ref, wd_ref, o_ref) = refs[1 + 2 * n_attn:]
    x = x_ref[0]
    for a_ref, w_ref in zip(attn_refs, wo_refs):
        x = x + _dot(a_ref[0], w_ref[...])

    q = _dot(_rms(x, mqn_ref[...]).astype(BF16), wq_ref[...])
    heads = []
    for hd in range(MEM_HEADS):
        sl = slice(MEM_HEAD_DIM * hd, MEM_HEAD_DIM * (hd + 1))
        qh = (_rms(q[:, sl], qg_ref[...]) * (MEM_HEAD_DIM ** -0.5 * LOG2E)).astype(BF16)
        s = _dot_nt(qh, mk_ref[0, :, sl])
        p = jnp.exp2(s - jnp.max(s, axis=-1, keepdims=True))
        l = jnp.sum(p, axis=-1, keepdims=True)
        heads.append((_dot(p.astype(BF16), mv_ref[0, :, sl]) / l).astype(BF16))
    x = x + _dot(jnp.concatenate(heads, axis=-1), wmo_ref[...])

    hf = _rms(x, fn_ref[...]).astype(BF16)
    d_ff = wg_ref.shape[1]
    chunk = d_ff // FFN_CHUNKS
    y = x
    for c in range(FFN_CHUNKS):
        cols = slice(chunk * c, chunk * (c + 1))
        g = _dot(hf, wg_ref[:, cols])
        u = _dot(hf, wu_ref[:, cols])
        act = (g * jax.nn.sigmoid(g) * u).astype(BF16)
        y = y + _dot(act, wd_ref[cols, :])
    o_ref[0] = y


def _post_block(x, attns, wos, mqn, wq, qg, mk, mv, wmo, fn, wg, wu, wd, ts):
    B, S, D = x.shape
    n = len(attns)
    tok = lambda c: pl.BlockSpec((1, ts, c), lambda b, i: (b, i, 0))
    memspec = pl.BlockSpec((1,) + mk.shape[1:], lambda b, i: (b, 0, 0))
    consts_a = list(wos) + [mqn, wq, qg]
    consts_b = [wmo, fn, wg, wu, wd]
    return pl.pallas_call(
        functools.partial(_post_kernel, n_attn=n),
        grid=(B, S // ts),
        in_specs=([tok(D)] + [tok(a.shape[-1]) for a in attns] + [_const_spec(c.shape) for c in consts_a]
                  + [memspec, memspec] + [_const_spec(c.shape) for c in consts_b]),
        out_specs=tok(D),
        out_shape=jax.ShapeDtypeStruct((B, S, D), F32),
        compiler_params=_params(("parallel", "parallel")),
        name="post_block",
    )(x, *attns, *consts_a, mk, mv, *consts_b)


def _row(v):
    return v.reshape(1, -1).astype(F32)


def _shift_row(bound, first_lane):
    terms = jnp.stack(_split3(-jnp.asarray(bound, F32))).reshape(1, N_SPLIT)
    return jnp.pad(terms, ((0, 0), (first_lane, LANES - first_lane - N_SPLIT)))


def _pad_cols(w, lo, total):
    return jnp.pad(w, ((0, 0), (lo, total - lo - w.shape[1])))


def _mla_lane_perm():
    half = MLA_ROPE // 2
    lanes = np.arange(MLA_QK)
    return np.where(lanes < half, MLA_NOPE + lanes,
                    np.where(lanes < ROPE_HI_LANE, lanes - half,
                             np.where(lanes < ROPE_HI_LANE + half, lanes + half, lanes - MLA_ROPE)))


def _even_weights(w_in, w_uq, w_ukv, q_gain, k_gain):
    perm = _mla_lane_perm()
    is_nope = perm < MLA_NOPE
    pad = LANES - MLA_QK

    def head_cols(w_head, live):
        return jnp.pad(jnp.where(live[None, :], w_head[:, perm], 0.0), ((0, 0), (0, pad)))

    c_kr = EVEN_COLS[-1]
    kr_src = jnp.concatenate([jnp.zeros((w_in.shape[0], MLA_NOPE), w_in.dtype), w_in[:, c_kr:c_kr + MLA_ROPE]], axis=1)
    w1 = jnp.concatenate([w_in[:, :c_kr], head_cols(kr_src, ~is_nope)], axis=1).astype(BF16)
    wuq = jnp.concatenate(
        [head_cols(w_uq[:, MLA_QK * h:MLA_QK * (h + 1)], np.ones_like(is_nope)) for h in range(MLA_HEADS)],
        axis=1).astype(BF16)
    kv_w = MLA_NOPE + MLA_V
    rope_zeros = jnp.zeros((w_ukv.shape[0], MLA_ROPE), w_ukv.dtype)
    wuk = jnp.concatenate(
        [head_cols(jnp.concatenate([w_ukv[:, kv_w * h:kv_w * h + MLA_NOPE], rope_zeros], axis=1), is_nope)
         for h in range(MLA_HEADS)], axis=1).astype(BF16)
    wuv = jnp.concatenate(
        [w_ukv[:, kv_w * h + MLA_NOPE:kv_w * (h + 1)] for h in range(MLA_HEADS)], axis=1).astype(BF16).T
    gmq = _row(jnp.pad(q_gain[perm], (0, pad)))
    gmk = _row(jnp.pad(k_gain[perm], (0, pad)))
    return w1, wuq, wuk, wuv, gmq, gmk


def _rope_freqs():
    half = MLA_ROPE // 2
    freqs = (np.float32(ROPE_THETA) ** (-np.arange(half, dtype=np.float32) / np.float32(half))).astype(np.float32)
    return jnp.asarray(freqs.reshape(half, 1))


def _ones_blocks(n):
    g = np.arange(n) // HEAD_DIM
    return jnp.asarray((g[:, None] == g[None, :]).astype(np.float32), dtype=BF16)


def _tile_sizes(S):
    token_tile = min(2 * MXU_WIDTH, S)
    return token_tile, token_tile, min(2 * MXU_WIDTH, S - SWA_WINDOW), min(4 * MXU_WIDTH, S)


def kernel(x, mem, positions, mix_norm, ev_w_in, ev_swa_q_gain, ev_swa_k_gain, ev_sinks, ev_q_latent_norm,
           ev_kv_latent_norm, ev_w_uq, ev_w_ukv, ev_mla_q_gain, ev_mla_k_gain, ev_w_out, od_w_qkv, od_q_gain,
           od_k_gain, od_lambda, od_subln, od_w_out, mem_q_norm, mem_kv_norm, mem_w_q, mem_w_kv, mem_q_gain,
           mem_k_gain, mem_w_out, ffn_norm, ffn_w_gate, ffn_w_up, ffn_w_down):
    B, S, D = x.shape
    depth = mix_norm.shape[0]
    ts_proj, ts_post, tq_swa, tq_flash = _tile_sizes(S)

    ones = _ones_blocks(MXU_WIDTH)
    pos3 = positions.reshape(B, 1, S)
    inv = _rope_freqs()
    mem_k, mem_v = _mem_kv(mem, mem_kv_norm.reshape(depth, 1, D), mem_w_kv.astype(BF16),
                           mem_k_gain.reshape(depth, 1, MEM_HEAD_DIM))
    diff_slopes = [2.0 ** (-8.0 * (i + 1) / DIFF_HEADS) for i in range(DIFF_HEADS)]

    for l in range(depth):
        if l % 2 == 0:
            e = l // 2
            w1, wuq, wuk, wuv, gmq, gmk = _even_weights(ev_w_in[e], ev_w_uq[e], ev_w_ukv[e], ev_mla_q_gain[e],
                                                        ev_mla_k_gain[e])
            qa, ka, va, qm, km, vm = _even_proj(
                x, pos3, _row(mix_norm[l]), w1, ones, _row(jnp.tile(ev_swa_q_gain[e], SWA_HEADS)),
                _row(jnp.tile(ev_swa_k_gain[e], SWA_KV_HEADS)), _row(ev_q_latent_norm[e]),
                _row(ev_kv_latent_norm[e]), wuq, wuk, wuv, gmq, gmk,
                _shift_row(_score_bound(ev_mla_q_gain[e], ev_mla_k_gain[e], MLA_QK, MLA_QK ** -0.5 * LOG2E), MLA_QK),
                inv, ts_proj)
            swa_bound = _score_bound(ev_swa_q_gain[e], ev_swa_k_gain[e], HEAD_DIM, HEAD_DIM ** -0.5 * LOG2E)
            swa_scalars = jnp.concatenate([ev_sinks[e].astype(F32), swa_bound.reshape(1).astype(F32)])
            out_a = _swa_attention(swa_scalars, qa, ka, va, tq_swa)
            out_b = _mla_attention(qm, km, vm, tq_flash)
            wo = ev_w_out[e].astype(BF16)
            attns = [out_a, out_b]
            wos = [wo[:SWA_Q_W], wo[SWA_Q_W:]]
        else:
            o = l // 2
            lambda_init = 0.8 - 0.6 * math.exp(-0.3 * l)
            n_grp = DIFF_HEADS * 2
            w_qkv = od_w_qkv[o].astype(BF16)
            q, k, vt = _odd_proj(x, _row(mix_norm[l]), w_qkv[:, :2 * D], w_qkv[:, 2 * D:].T, ones,
                                 _row(jnp.tile(od_q_gain[o], n_grp // 2)), _row(jnp.tile(od_k_gain[o], n_grp // 2)),
                                 ts_proj)
            bound = _score_bound(od_q_gain[o], od_k_gain[o], DIFF_DIM, DIFF_DIM ** -0.5 * LOG2E)
            out_d = _diff_attention(diff_slopes, bound, q, k, vt, od_lambda[o].astype(F32), _row(od_subln[o]),
                                    tq_flash, lambda_init)
            attns = [out_d]
            wos = [od_w_out[o].astype(BF16)]
        x = _post_block(x, attns, wos, _row(mem_q_norm[l]), mem_w_q[l].astype(BF16), _row(mem_q_gain[l]),
                        mem_k[l], mem_v[l], mem_w_out[l].astype(BF16), _row(ffn_norm[l]),
                        ffn_w_gate[l].astype(BF16), ffn_w_up[l].astype(BF16), ffn_w_down[l].astype(BF16), ts_post)
    return x
```

```python
import functools
import math

import numpy as np
import jax
import jax.numpy as jnp
from jax import lax
from jax.experimental import pallas as pl
from jax.experimental.pallas import tpu as pltpu

F32 = jnp.float32
BF16 = jnp.bfloat16

EPS = 1e-6
MASKED = -1e30
LOG2E = 1.4426950408889634
ROPE_THETA = 10000.0

LANES = 128
SUBLANES = 8
MXU_WIDTH = 256
HEAD_DIM = 64
SWA_HEADS = 8
SWA_KV_HEADS = 2
SWA_WINDOW = 128
MLA_HEADS = 8
MLA_NOPE = 64
MLA_ROPE = 32
MLA_QK = MLA_NOPE + MLA_ROPE
MLA_V = 64
ROPE_HI_LANE = 64
DIFF_HEADS = 8
DIFF_DIM = 64
MEM_HEADS = 4
MEM_HEAD_DIM = 128
MLA_Q_RANK = 256
MLA_KV_RANK = 128
SWA_Q_W = SWA_HEADS * HEAD_DIM
SWA_KV_W = SWA_KV_HEADS * HEAD_DIM
EVEN_COLS = tuple(np.cumsum([SWA_Q_W, SWA_KV_W, SWA_KV_W, MLA_Q_RANK, MLA_KV_RANK]).tolist())

VMEM_LIMIT = 56 * 1024 * 1024

UNDERFLOW_GUARD = 2.0 ** -80
N_SPLIT = 3


def _params(sem):
    return pltpu.CompilerParams(dimension_semantics=sem, vmem_limit_bytes=VMEM_LIMIT)


def _const_spec(shape):
    nd = len(shape)
    return pl.BlockSpec(shape, lambda *_: (0,) * nd, pipeline_mode=pl.Buffered(1))


def _rms(x, g):
    ms = jnp.mean(x * x, axis=-1, keepdims=True)
    return x * lax.rsqrt(ms + EPS) * g


def _dot(a, b):
    return jnp.dot(a, b, preferred_element_type=F32)


def _dot_nt(a, b):
    return lax.dot_general(a, b, (((1,), (1,)), ((), ())), preferred_element_type=F32)


def _group_sumsq(x, ones_blk):
    x2 = x * x
    hi = x2.astype(BF16)
    lo = (x2 - hi.astype(F32)).astype(BF16)
    return _dot(hi, ones_blk) + _dot(lo, ones_blk)


def _group_norm64(x, ones_blk, gain):
    chunk = min(ones_blk.shape[0], x.shape[1])
    ssq = jnp.concatenate([_group_sumsq(x[:, c:c + chunk], ones_blk[0:chunk, 0:chunk])
                           for c in range(0, x.shape[1], chunk)], axis=1)
    return x * lax.rsqrt(ssq * (1.0 / HEAD_DIM) + EPS) * gain


def _even_proj_kernel(x_ref, pos_ref, gmix_ref, w1_ref, ones_ref, gq_ref, gk_ref, qln_ref, kvln_ref,
                      wuq_ref, wuk_ref, wuv_ref, gmq_ref, gmk_ref, qshift_ref, inv_ref,
                      qa_ref, ka_ref, va_ref, qm_ref, km_ref, vm_ref):
    x = x_ref[0]
    h = _rms(x, gmix_ref[...]).astype(BF16)
    z = _dot(h, w1_ref[...])
    c_ka, c_va, c_cq, c_ckv, c_kr = EVEN_COLS
    ones = ones_ref[...]
    qa = _group_norm64(z[:, 0:c_ka], ones, gq_ref[...]) * (HEAD_DIM ** -0.5 * LOG2E)
    qa_ref[0] = qa.astype(BF16)
    ka = _group_norm64(z[:, c_ka:c_va], ones[0:SWA_KV_W, 0:SWA_KV_W], gk_ref[...])
    ka_ref[0] = ka.astype(BF16)
    va_ref[0] = z[:, c_va:c_cq].astype(BF16)

    cqn = _rms(z[:, c_cq:c_ckv], qln_ref[...]).astype(BF16)
    ckvn = _rms(z[:, c_ckv:c_kr], kvln_ref[...]).astype(BF16)
    kr = z[:, c_kr:c_kr + LANES]
    qf = _dot(cqn, wuq_ref[...])
    kn = _dot(ckvn, wuk_ref[...])
    vm_ref[0] = _dot_nt(wuv_ref[...], ckvn).astype(BF16)

    half = MLA_ROPE // 2
    ang = inv_ref[...] * pos_ref[0].astype(F32)
    pad = jnp.zeros((LANES - half, ang.shape[1]), F32)
    cos = jnp.concatenate([jnp.cos(ang), pad], axis=0).T
    sin = jnp.concatenate([jnp.sin(ang), pad], axis=0).T
    lane = lax.broadcasted_iota(jnp.int32, cos.shape, 1)
    in_lo = lane < half
    in_hi = (lane >= ROPE_HI_LANE) & (lane < ROPE_HI_LANE + half)
    cos_t = jnp.where(in_lo, cos, jnp.where(in_hi, pltpu.roll(cos, ROPE_HI_LANE, 1), 1.0))
    sin_t = jnp.where(in_lo, -sin, jnp.where(in_hi, pltpu.roll(sin, ROPE_HI_LANE, 1), 0.0))

    pair = 2 * LANES
    gi = lax.broadcasted_iota(jnp.int32, (pair, pair), 0) // LANES
    gj = lax.broadcasted_iota(jnp.int32, (pair, pair), 1) // LANES
    head_ones = (gi == gj).astype(BF16)

    def norm_rope(xp, gain):
        r = lax.rsqrt(_group_sumsq(xp, head_ones) * (1.0 / MLA_QK) + EPS)
        xn = xp * r * gain
        halves = []
        for t in range(2):
            xh = xn[:, LANES * t:LANES * (t + 1)]
            halves.append(xh * cos_t + pltpu.roll(xh, ROPE_HI_LANE, 1) * sin_t)
        return halves

    gmq = jnp.concatenate([gmq_ref[...]] * 2, axis=1)
    gmk = jnp.concatenate([gmk_ref[...]] * 2, axis=1)
    kr2 = jnp.concatenate([kr, kr], axis=1)
    shift_lanes = (lane >= MLA_QK) & (lane < MLA_QK + N_SPLIT)
    qshift = qshift_ref[...]
    for hp in range(MLA_HEADS // 2):
        sl = slice(pair * hp, pair * (hp + 1))
        q_heads = norm_rope(qf[:, sl], gmq)
        k_heads = norm_rope(kn[:, sl] + kr2, gmk)
        for t in range(2):
            hl = slice(pair * hp + LANES * t, pair * hp + LANES * (t + 1))
            qm_ref[0, :, hl] = jnp.where(shift_lanes, qshift, q_heads[t] * (MLA_QK ** -0.5 * LOG2E)).astype(BF16)
            km_ref[0, :, hl] = jnp.where(shift_lanes, 1.0, k_heads[t]).astype(BF16)


def _even_proj(x, pos3, gmix, w1, ones, gq, gk, qln, kvln, wuq, wuk, wuv, gmq, gmk, qshift, inv, ts):
    B, S, D = x.shape
    grid = (B, S // ts)
    tok = lambda c: pl.BlockSpec((1, ts, c), lambda b, i: (b, i, 0))
    consts = [gmix, w1, ones, gq, gk, qln, kvln, wuq, wuk, wuv, gmq, gmk, qshift, inv]
    out_cols = [SWA_Q_W, SWA_KV_W, SWA_KV_W, MLA_HEADS * LANES, MLA_HEADS * LANES]
    v_rows = MLA_HEADS * MLA_V
    return pl.pallas_call(
        _even_proj_kernel,
        grid=grid,
        in_specs=([tok(D), pl.BlockSpec((1, 1, ts), lambda b, i: (b, 0, i))]
                  + [_const_spec(c.shape) for c in consts]),
        out_specs=[tok(c) for c in out_cols] + [pl.BlockSpec((1, v_rows, ts), lambda b, i: (b, 0, i))],
        out_shape=([jax.ShapeDtypeStruct((B, S, c), BF16) for c in out_cols]
                   + [jax.ShapeDtypeStruct((B, v_rows, S), BF16)]),
        compiler_params=_params(("parallel", "parallel")),
        name="even_proj",
    )(x, pos3, *consts)


def _swa_place_q(q_ref, hd, lane):
    hk = hd // (SWA_HEADS // SWA_KV_HEADS)
    qq = q_ref[0, :, LANES * (hd // 2):LANES * (hd // 2 + 1)].astype(F32)
    xq = qq if hd % 2 == hk else pltpu.roll(qq, HEAD_DIM, 1)
    return jnp.where((lane >= HEAD_DIM * hk) & (lane < HEAD_DIM * (hk + 1)), xq, 0.0).astype(BF16)


def _swa_place_out(o_even, o_odd, cb, lane):
    placed = []
    for e, o in enumerate((o_even, o_odd)):
        hk = (2 * cb + e) // (SWA_HEADS // SWA_KV_HEADS)
        placed.append(o if e == hk else pltpu.roll(o, HEAD_DIM, 1))
    return jnp.where(lane < HEAD_DIM, placed[0], placed[1]).astype(BF16)


def _swa_running_max(sink_ref, q_ref, k_ref, v_ref, o_ref, qi, tq):
    win = tq + SWA_WINDOW
    wstart = pl.multiple_of(jnp.maximum(qi * tq - SWA_WINDOW, 0), SWA_WINDOW)
    kw = k_ref[0, pl.ds(wstart, win), :]
    vw = v_ref[0, pl.ds(wstart, win), :]
    row = lax.broadcasted_iota(jnp.int32, (tq, win), 0)
    col = lax.broadcasted_iota(jnp.int32, (tq, win), 1)
    dist = (qi * tq + row) - (wstart + col)
    valid = (dist >= 0) & (dist < SWA_WINDOW)
    distf = dist.astype(F32)
    lane = lax.broadcasted_iota(jnp.int32, (tq, LANES), 1)
    for cb in range(SWA_HEADS // 2):
        outs = []
        for e in range(2):
            hd = 2 * cb + e
            s = _dot_nt(_swa_place_q(q_ref, hd, lane), kw) - (2.0 ** (-8.0 * (hd + 1) / SWA_HEADS) * LOG2E) * distf
            s = jnp.where(valid, s, MASKED)
            sink = sink_ref[hd] * LOG2E
            m = jnp.maximum(jnp.max(s, axis=-1, keepdims=True), sink)
            p = jnp.exp2(s - m)
            l = jnp.sum(p, axis=-1, keepdims=True) + jnp.exp2(sink - m)
            outs.append(_dot(p.astype(BF16), vw) / l)
        o_ref[0, :, LANES * cb:LANES * (cb + 1)] = _swa_place_out(outs[0], outs[1], cb, lane)


def _swa_kernel(sink_ref, q_ref, k_ref, v_ref, o_ref, *, tq):
    qi = pl.program_id(1)
    sub = SWA_WINDOW
    win = 2 * sub
    n_sub = tq // sub
    bound = sink_ref[SWA_HEADS]
    lane_q = lax.broadcasted_iota(jnp.int32, (tq, LANES), 1)
    lane = lax.broadcasted_iota(jnp.int32, (sub, LANES), 1)
    row = lax.broadcasted_iota(jnp.int32, (sub, win), 0)
    col = lax.broadcasted_iota(jnp.int32, (sub, win), 1)

    def bias_mask(first_key_offset, slope):
        dist = row + first_key_offset - col
        return jnp.where((dist >= 0) & (dist < SWA_WINDOW), -slope * dist.astype(F32) - bound, MASKED)

    ones_blk = jnp.ones((win, LANES), BF16)
    windows = []
    for r in range(n_sub):
        wstart = qi * tq + sub * (r - 1)
        wstart = pl.multiple_of(jnp.maximum(wstart, 0) if r == 0 else wstart, sub)
        windows.append((k_ref[0, pl.ds(wstart, win), :],
                        jnp.concatenate([v_ref[0, pl.ds(wstart, win), :], ones_blk], axis=1)))

    outs = [[None] * SWA_HEADS for _ in range(n_sub)]
    min_sum = None
    for hd in range(SWA_HEADS):
        slope = 2.0 ** (-8.0 * (hd + 1) / SWA_HEADS) * LOG2E
        bm = bias_mask(sub, slope)
        bm_first = jnp.where(qi == 0, bias_mask(0, slope), bm)
        xq = _swa_place_q(q_ref, hd, lane_q)
        sink_term = jnp.exp2(sink_ref[hd] * LOG2E - bound)
        for r in range(n_sub):
            kw, v_ones = windows[r]
            s = _dot_nt(xq[sub * r:sub * (r + 1)], kw) + (bm_first if r == 0 else bm)
            ov = _dot(jnp.exp2(s).astype(BF16), v_ones)
            l = ov[:, LANES:] + sink_term
            outs[r][hd] = ov[:, :LANES] / l
            piece_min = jnp.min(l)
            min_sum = piece_min if min_sum is None else jnp.minimum(min_sum, piece_min)
    ok = min_sum >= UNDERFLOW_GUARD

    @pl.when(ok)
    def _():
        for r in range(n_sub):
            for cb in range(SWA_HEADS // 2):
                o_ref[0, sub * r:sub * (r + 1), LANES * cb:LANES * (cb + 1)] = _swa_place_out(
                    outs[r][2 * cb], outs[r][2 * cb + 1], cb, lane)

    @pl.when(jnp.logical_not(ok))
    def _():
        _swa_running_max(sink_ref, q_ref, k_ref, v_ref, o_ref, qi, tq)


def _swa_attention(sinks, qa, ka, va, tq):
    B, S, _ = qa.shape
    return pl.pallas_call(
        functools.partial(_swa_kernel, tq=tq),
        grid=(B, S // tq),
        in_specs=[
            pl.BlockSpec(memory_space=pltpu.SMEM),
            pl.BlockSpec((1, tq, SWA_Q_W), lambda b, i: (b, i, 0)),
            pl.BlockSpec((1, S, SWA_KV_W), lambda b, i: (b, 0, 0)),
            pl.BlockSpec((1, S, SWA_KV_W), lambda b, i: (b, 0, 0)),
        ],
        out_specs=pl.BlockSpec((1, tq, SWA_Q_W), lambda b, i: (b, i, 0)),
        out_shape=jax.ShapeDtypeStruct((B, S, SWA_Q_W), BF16),
        compiler_params=_params(("parallel", "arbitrary")),
        name="swa_attention",
    )(sinks, qa, ka, va)


def _split3(x):
    hi = x.astype(BF16).astype(F32)
    r = x - hi
    mid = r.astype(BF16).astype(F32)
    lo = (r - mid).astype(BF16).astype(F32)
    return hi, mid, lo


def _score_bound(q_gain, k_gain, dim, scale):
    return dim * scale * jnp.max(jnp.abs(q_gain)) * jnp.max(jnp.abs(k_gain))


FFN_CHUNKS = 2

BLOCKS_PER_TRIP = 4


def _loop_blocks(n, step, init):
    def trip(t, c):
        for u in range(BLOCKS_PER_TRIP):
            c = step(BLOCKS_PER_TRIP * t + u, c)
        return c

    full = n // BLOCKS_PER_TRIP
    carry = lax.fori_loop(0, full, trip, init)
    return lax.fori_loop(full * BLOCKS_PER_TRIP, n, step, carry)


def _add_cols(acc, lo, part):
    if lo == 0:
        return acc + part
    return acc + jnp.concatenate([jnp.zeros((acc.shape[0], lo), acc.dtype), part], axis=1)


def _diagonal_spans(tq):
    h = tq // 2
    return ((0, h, 0), (h, h, h)) if h % MXU_WIDTH == 0 else ((0, tq, 0),)


def _causal_t(key_lo, key_n, q_lo, tq):
    shape = (key_n, tq - q_lo)
    return (key_lo + lax.broadcasted_iota(jnp.int32, shape, 0)) <= (q_lo + lax.broadcasted_iota(jnp.int32, shape, 1))


def _causal_mask(tq):
    row = lax.broadcasted_iota(jnp.int32, (tq, tq), 0)
    col = lax.broadcasted_iota(jnp.int32, (tq, tq), 1)
    return row >= col


def _mla_running_max(q_ref, k_ref, vt_ref, e, qi, tq):
    causal = _causal_mask(tq)
    q = q_ref[0, :, LANES * e:LANES * (e + 1)]

    def step(j, carry, masked):
        m, l, acc = carry
        ks = pl.multiple_of(j * tq, tq)
        k = k_ref[0, pl.ds(ks, tq), LANES * e:LANES * (e + 1)]
        vt = vt_ref[0, :, pl.ds(ks, tq)]
        s = _dot_nt(q, k)
        if masked:
            s = jnp.where(causal, s, MASKED)
        m_new = jnp.maximum(m, jnp.max(s, axis=-1, keepdims=True))
        alpha = jnp.exp2(m - m_new)
        p = jnp.exp2(s - m_new)
        l = alpha * l + jnp.sum(p, axis=-1, keepdims=True)
        acc = alpha * acc + _dot_nt(p.astype(BF16), vt)
        return m_new, l, acc

    init = (jnp.full((tq, 1), MASKED, F32), jnp.zeros((tq, 1), F32), jnp.zeros((tq, LANES), F32))
    carry = lax.fori_loop(0, qi, lambda j, c: step(j, c, False), init)
    m, l, acc = step(qi, carry, True)
    return acc / l


def _mla_kernel(q_ref, k_ref, vt_ref, o_ref, *, tq):
    qi = pl.program_id(2)
    lane = lax.broadcasted_iota(jnp.int32, (tq, LANES), 1)
    q_aug = [q_ref[0, :, LANES * e:LANES * (e + 1)] for e in range(2)]

    def block(j, carry, key_lo=0, key_n=tq, q_lo=0, masked=False):
        ks = pl.multiple_of(j * tq + key_lo, key_n)
        new = []
        for e in range(2):
            lsum, acc = carry[e]
            st = _dot_nt(k_ref[0, pl.ds(ks, key_n), LANES * e:LANES * (e + 1)], q_aug[e][q_lo:, :])
            if masked:
                st = jnp.where(_causal_t(key_lo, key_n, q_lo, tq), st, MASKED)
            pt = jnp.exp2(st)
            part = jnp.sum(pt.reshape(key_n // SUBLANES, SUBLANES, tq - q_lo), axis=0)
            pv = _dot(vt_ref[0, MLA_V * e:MLA_V * (e + 1), pl.ds(ks, key_n)], pt.astype(BF16))
            new.append((_add_cols(lsum, q_lo, part), _add_cols(acc, q_lo, pv)))
        return tuple(new)

    zero = (jnp.zeros((SUBLANES, tq), F32), jnp.zeros((MLA_V, tq), F32))
    carry = _loop_blocks(qi, block, (zero, zero))
    for key_lo, key_n, q_lo in _diagonal_spans(tq):
        carry = block(qi, carry, key_lo, key_n, q_lo, masked=True)
    sums = [jnp.sum(lsum, axis=0, keepdims=True) for lsum, _ in carry]
    outs = [acc / l for (_, acc), l in zip(carry, sums)]
    ok = jnp.minimum(jnp.min(sums[0]), jnp.min(sums[1])) >= UNDERFLOW_GUARD

    @pl.when(ok)
    def _():
        o_ref[0] = jnp.concatenate(outs, axis=0).T.astype(BF16)

    @pl.when(jnp.logical_not(ok))
    def _():
        slow = [_mla_running_max(q_ref, k_ref, vt_ref, e, qi, tq) for e in range(2)]
        o_ref[0] = jnp.where(lane < MLA_V, slow[0], slow[1]).astype(BF16)


def _mla_attention(qm, km, vm, tq):
    B, S, _ = qm.shape
    return pl.pallas_call(
        functools.partial(_mla_kernel, tq=tq),
        grid=(B, MLA_HEADS // 2, S // tq),
        in_specs=[
            pl.BlockSpec((1, tq, 2 * LANES), lambda b, h, i: (b, i, h)),
            pl.BlockSpec((1, S, 2 * LANES), lambda b, h, i: (b, 0, h)),
            pl.BlockSpec((1, 2 * MLA_V, S), lambda b, h, i: (b, h, 0)),
        ],
        out_specs=pl.BlockSpec((1, tq, LANES), lambda b, h, i: (b, i, h)),
        out_shape=jax.ShapeDtypeStruct((B, S, MLA_HEADS * MLA_V), BF16),
        compiler_params=_params(("parallel", "parallel", "arbitrary")),
        name="mla_attention",
    )(qm, km, vm)


def _odd_proj_kernel(x_ref, gmix_ref, w_ref, wvt_ref, ones_ref, gq_ref, gk_ref, q_ref, k_ref, vt_ref):
    x = x_ref[0]
    h = _rms(x, gmix_ref[...]).astype(BF16)
    z = _dot(h, w_ref[...])
    ones = ones_ref[...]
    width = DIFF_HEADS * 2 * DIFF_DIM
    piece = gq_ref.shape[1]
    for c in range(0, width, piece):
        q = _group_norm64(z[:, c:c + piece], ones, gq_ref[...]) * (DIFF_DIM ** -0.5 * LOG2E)
        q_ref[0, :, c:c + piece] = q.astype(BF16)
        k = _group_norm64(z[:, width + c:width + c + piece], ones, gk_ref[...])
        k_ref[0, :, c:c + piece] = k.astype(BF16)
    vt_ref[0] = _dot_nt(wvt_ref[...], h).astype(BF16)


def _odd_proj(x, gmix, w_qk, w_vt, ones, gq, gk, ts):
    B, S, D = x.shape
    tok = pl.BlockSpec((1, ts, D), lambda b, i: (b, i, 0))
    consts = [gmix, w_qk, w_vt, ones, gq, gk]
    return pl.pallas_call(
        _odd_proj_kernel,
        grid=(B, S // ts),
        in_specs=[tok] + [_const_spec(c.shape) for c in consts],
        out_specs=[tok, tok, pl.BlockSpec((1, D, ts), lambda b, i: (b, 0, i))],
        out_shape=[jax.ShapeDtypeStruct((B, S, D), BF16)] * 2 + [jax.ShapeDtypeStruct((B, D, S), BF16)],
        compiler_params=_params(("parallel", "parallel")),
        name="odd_proj",
    )(x, *consts)


def _diff_q_parts(q, lane):
    zero = jnp.zeros_like(q)
    return jnp.where(lane < DIFF_DIM, q, zero), jnp.where(lane >= DIFF_DIM, q, zero)


def _diff_running_max(slope, q_ref, k_ref, vt_ref, qi, tq):
    row = lax.broadcasted_iota(jnp.int32, (tq, tq), 0)
    col = lax.broadcasted_iota(jnp.int32, (tq, tq), 1)
    causal = row >= col
    bias = slope * (col - row).astype(F32)
    lane = lax.broadcasted_iota(jnp.int32, (tq, LANES), 1)
    q_parts = _diff_q_parts(q_ref[0], lane)

    def step(j, carry, masked):
        ks = pl.multiple_of(j * tq, tq)
        k = k_ref[0, pl.ds(ks, tq), :]
        vt = vt_ref[0, :, pl.ds(ks, tq)]
        offset = slope * ((j - qi) * tq).astype(F32)
        new = []
        for c in range(2):
            m, l, acc = carry[c]
            s = _dot_nt(q_parts[c], k) + bias
            if masked:
                s = jnp.where(causal, s, MASKED)
            m_new = jnp.maximum(m, jnp.max(s, axis=-1, keepdims=True) + offset)
            alpha = jnp.exp2(m - m_new)
            p = jnp.exp2(s - (m_new - offset))
            l = alpha * l + jnp.sum(p, axis=-1, keepdims=True)
            acc = alpha * acc + _dot_nt(p.astype(BF16), vt)
            new.append((m_new, l, acc))
        return tuple(new)

    one = (jnp.full((tq, 1), MASKED, F32), jnp.zeros((tq, 1), F32), jnp.zeros((tq, LANES), F32))
    carry = lax.fori_loop(0, qi, lambda j, c: step(j, c, False), (one, one))
    (m1, l1, a1), (m2, l2, a2) = step(qi, carry, True)
    return a1 / l1, a2 / l2


def _diff_kernel(slope_ref, qx_ref, qpos_ref, q_ref, k_ref, kx_ref, vt_ref, lam_ref, subln_ref, o_ref, *, tq,
                 lambda_init):
    hd = pl.program_id(1)
    qi = pl.program_id(2)
    lane = lax.broadcasted_iota(jnp.int32, (tq, LANES), 1)
    q_parts = _diff_q_parts(q_ref[0], lane)
    head_row = qx_ref[0]
    qx = jnp.where(lane < 3 * N_SPLIT, head_row,
                   qpos_ref[...].astype(F32) * head_row[:, SLOPE_LANE:SLOPE_LANE + 1]).astype(BF16)
    q_aug = [jnp.concatenate([part, qx], axis=1) for part in q_parts]

    def block(j, carry, key_lo=0, key_n=tq, q_lo=0, masked=False):
        ks = pl.multiple_of(j * tq + key_lo, key_n)
        k = jnp.concatenate([k_ref[0, pl.ds(ks, key_n), :], kx_ref[pl.ds(ks, key_n), :]], axis=1)
        vt = vt_ref[0, :, pl.ds(ks, key_n)]
        new = []
        for c in range(2):
            lsum, acc = carry[c]
            st = _dot_nt(k, q_aug[c][q_lo:, :])
            if masked:
                st = jnp.where(_causal_t(key_lo, key_n, q_lo, tq), st, MASKED)
            pt = jnp.exp2(st)
            part = jnp.sum(pt.reshape(key_n // SUBLANES, SUBLANES, tq - q_lo), axis=0)
            new.append((_add_cols(lsum, q_lo, part), _add_cols(acc, q_lo, _dot(vt, pt.astype(BF16)))))
        return tuple(new)

    zero = (jnp.zeros((SUBLANES, tq), F32), jnp.zeros((LANES, tq), F32))
    carry = _loop_blocks(qi, block, (zero, zero))
    for key_lo, key_n, q_lo in _diagonal_spans(tq):
        carry = block(qi, carry, key_lo, key_n, q_lo, masked=True)
    (ls1, a1), (ls2, a2) = carry
    l1 = jnp.sum(ls1, axis=0, keepdims=True)
    l2 = jnp.sum(ls2, axis=0, keepdims=True)
    ok = jnp.minimum(jnp.min(l1), jnp.min(l2)) >= UNDERFLOW_GUARD

    lf = lam_ref[...]
    lam = (jnp.exp(jnp.sum(lf[0:1] * lf[1:2], axis=-1, keepdims=True))
           - jnp.exp(jnp.sum(lf[2:3] * lf[3:4], axis=-1, keepdims=True)) + lambda_init)

    gain = subln_ref[...] * (1.0 - lambda_init)

    @pl.when(ok)
    def _():
        ot = a1 / l1 - lam * (a2 / l2)
        ot = ot * lax.rsqrt(jnp.mean(ot * ot, axis=0, keepdims=True) + EPS)
        o_ref[0] = (ot.T * gain).astype(BF16)

    @pl.when(jnp.logical_not(ok))
    def _():
        o1, o2 = _diff_running_max(slope_ref[hd] * LOG2E, q_ref, k_ref, vt_ref, qi, tq)
        o_ref[0] = _rms(o1 - lam * o2, gain).astype(BF16)


SLOPE_LANE = 5 * N_SPLIT


def _alibi_tables(slopes, bound, S):
    assert all(math.log2(s) == int(math.log2(s)) for s in slopes), "ALiBi slopes must be powers of two"
    n = N_SPLIT
    log2e_terms = jnp.concatenate(_split3(jnp.full((1, 1), LOG2E, F32)), axis=1)
    s = jnp.asarray(slopes, F32).reshape(-1, 1)
    head = jnp.concatenate(
        [s * log2e_terms * float(LANES), s * log2e_terms, jnp.broadcast_to(_shift_row(bound, 0)[:, :n], (len(slopes), n)),
         jnp.zeros((len(slopes), SLOPE_LANE - 3 * n), F32), s, jnp.zeros((len(slopes), LANES - SLOPE_LANE - 1), F32)],
        axis=1)
    idx = np.arange(S)
    hi, lo = (idx // LANES)[:, None].astype(np.float32), (idx % LANES)[:, None].astype(np.float32)
    qpos = np.zeros((S, LANES), np.float32)
    qpos[:, 3 * n:4 * n] = hi * LANES
    qpos[:, 4 * n:5 * n] = lo
    kx = jnp.concatenate(
        [jnp.asarray(np.concatenate([np.repeat(hi, n, 1), np.repeat(lo, n, 1), np.ones((S, n), np.float32)], axis=1)),
         jnp.broadcast_to(-log2e_terms, (S, n)), jnp.broadcast_to(-log2e_terms, (S, n)),
         jnp.zeros((S, LANES - 5 * n), F32)], axis=1)
    return head.reshape(-1, 1, LANES), jnp.asarray(qpos, dtype=BF16), kx.astype(BF16)


def _diff_attention(slopes, bound, q, k, vt, lambdas, subln, tq, lambda_init):
    B, S, D = q.shape
    qx, qpos, kx = _alibi_tables(slopes, bound, S)
    return pl.pallas_call(
        functools.partial(_diff_kernel, tq=tq, lambda_init=lambda_init),
        grid=(B, DIFF_HEADS, S // tq),
        in_specs=[
            pl.BlockSpec(memory_space=pltpu.SMEM),
            pl.BlockSpec((1, 1, LANES), lambda b, h, i: (h, 0, 0)),
            pl.BlockSpec((tq, LANES), lambda b, h, i: (i, 0)),
            pl.BlockSpec((1, tq, LANES), lambda b, h, i: (b, i, h)),
            pl.BlockSpec((1, S, LANES), lambda b, h, i: (b, 0, h)),
            _const_spec(kx.shape),
            pl.BlockSpec((1, LANES, S), lambda b, h, i: (b, h, 0)),
            pl.BlockSpec(lambdas.shape, lambda b, h, i: (0, 0)),
            pl.BlockSpec(subln.shape, lambda b, h, i: (0, 0)),
        ],
        out_specs=pl.BlockSpec((1, tq, LANES), lambda b, h, i: (b, i, h)),
        out_shape=jax.ShapeDtypeStruct((B, S, D), BF16),
        compiler_params=_params(("parallel", "parallel", "arbitrary")),
        name="diff_attention",
    )(jnp.asarray(slopes, F32), qx, qpos, q, k, kx, vt, lambdas, subln)


def _mem_kv_kernel(mem_ref, norm_ref, w_ref, gain_ref, k_ref, v_ref):
    h = _rms(mem_ref[0], norm_ref[0]).astype(BF16)
    z = _dot(h, w_ref[0])
    width = MEM_HEADS * MEM_HEAD_DIM
    gain = gain_ref[0]
    for hd in range(MEM_HEADS):
        sl = slice(MEM_HEAD_DIM * hd, MEM_HEAD_DIM * (hd + 1))
        k_ref[0, 0, :, sl] = _rms(z[:, sl], gain).astype(BF16)
    v_ref[0, 0] = z[:, width:2 * width].astype(BF16)


def _mem_kv(mem, norms, w_kv, gains):
    B, M, D = mem.shape
    L = norms.shape[0]
    width = MEM_HEADS * MEM_HEAD_DIM
    out = pl.BlockSpec((1, 1, M, width), lambda l, b: (l, b, 0, 0))
    return pl.pallas_call(
        _mem_kv_kernel,
        grid=(L, B),
        in_specs=[
            pl.BlockSpec((1, M, D), lambda l, b: (b, 0, 0)),
            pl.BlockSpec((1, 1, D), lambda l, b: (l, 0, 0)),
            pl.BlockSpec((1, D, 2 * width), lambda l, b: (l, 0, 0)),
            pl.BlockSpec((1, 1, MEM_HEAD_DIM), lambda l, b: (l, 0, 0)),
        ],
        out_specs=[out, out],
        out_shape=[jax.ShapeDtypeStruct((L, B, M, width), BF16)] * 2,
        compiler_params=_params(("parallel", "parallel")),
        name="mem_kv",
    )(mem, norms, w_kv, gains)


def _post_kernel(*refs, n_attn):
    x_ref = refs[0]
    attn_refs = refs[1:1 + n_attn]
    wo_refs = refs[1 + n_attn:1 + 2 * n_attn]
    (mqn_ref, wq_ref, qg_ref, mk_ref, mv_ref, wmo_ref, fn_ref, wg_ref, wu_ref, wd_ref, o_ref) = refs[1 + 2 * n_attn:]
    x = x_ref[0]
    for a_ref, w_ref in zip(attn_refs, wo_refs):
        x = x + _dot(a_ref[0], w_ref[...])

    q = _dot(_rms(x, mqn_ref[...]).astype(BF16), wq_ref[...])
    heads = []
    for hd in range(MEM_HEADS):
        sl = slice(MEM_HEAD_DIM * hd, MEM_HEAD_DIM * (hd + 1))
        qh = (_rms(q[:, sl], qg_ref[...]) * (MEM_HEAD_DIM ** -0.5 * LOG2E)).astype(BF16)
        s = _dot_nt(qh, mk_ref[0, :, sl])
        p = jnp.exp2(s - jnp.max(s, axis=-1, keepdims=True))
        l = jnp.sum(p, axis=-1, keepdims=True)
        heads.append((_dot(p.astype(BF16), mv_ref[0, :, sl]) / l).astype(BF16))
    x = x + _dot(jnp.concatenate(heads, axis=-1), wmo_ref[...])

    hf = _rms(x, fn_ref[...]).astype(BF16)
    d_ff = wg_ref.shape[1]
    chunk = d_ff // FFN_CHUNKS
    y = x
    for c in range(FFN_CHUNKS):
        cols = slice(chunk * c, chunk * (c + 1))
        g = _dot(hf, wg_ref[:, cols])
        u = _dot(hf, wu_ref[:, cols])
        act = (g * jax.nn.sigmoid(g) * u).astype(BF16)
        y = y + _dot(act, wd_ref[cols, :])
    o_ref[0] = y


def _post_block(x, attns, wos, mqn, wq, qg, mk, mv, wmo, fn, wg, wu, wd, ts):
    B, S, D = x.shape
    n = len(attns)
    tok = lambda c: pl.BlockSpec((1, ts, c), lambda b, i: (b, i, 0))
    memspec = pl.BlockSpec((1,) + mk.shape[1:], lambda b, i: (b, 0, 0))
    consts_a = list(wos) + [mqn, wq, qg]
    consts_b = [wmo, fn, wg, wu, wd]
    return pl.pallas_call(
        functools.partial(_post_kernel, n_attn=n),
        grid=(B, S // ts),
        in_specs=([tok(D)] + [tok(a.shape[-1]) for a in attns] + [_const_spec(c.shape) for c in consts_a]
                  + [memspec, memspec] + [_const_spec(c.shape) for c in consts_b]),
        out_specs=tok(D),
        out_shape=jax.ShapeDtypeStruct((B, S, D), F32),
        compiler_params=_params(("parallel", "parallel")),
        name="post_block",
    )(x, *attns, *consts_a, mk, mv, *consts_b)


def _row(v):
    return v.reshape(1, -1).astype(F32)


def _shift_row(bound, first_lane):
    terms = jnp.stack(_split3(-jnp.asarray(bound, F32))).reshape(1, N_SPLIT)
    return jnp.pad(terms, ((0, 0), (first_lane, LANES - first_lane - N_SPLIT)))


def _mla_lane_perm():
    half = MLA_ROPE // 2
    lanes = np.arange(MLA_QK)
    return np.where(lanes < half, MLA_NOPE + lanes,
                    np.where(lanes < ROPE_HI_LANE, lanes - half,
                             np.where(lanes < ROPE_HI_LANE + half, lanes + half, lanes - MLA_ROPE)))


def _even_weights(w_in, w_uq, w_ukv, q_gain, k_gain):
    perm = _mla_lane_perm()
    is_nope = perm < MLA_NOPE
    pad = LANES - MLA_QK

    def head_cols(w_head, live):
        return jnp.pad(jnp.where(live[None, :], w_head[:, perm], 0.0), ((0, 0), (0, pad)))

    c_kr = EVEN_COLS[-1]
    kr_src = jnp.concatenate([jnp.zeros((w_in.shape[0], MLA_NOPE), w_in.dtype), w_in[:, c_kr:c_kr + MLA_ROPE]], axis=1)
    w1 = jnp.concatenate([w_in[:, :c_kr], head_cols(kr_src, ~is_nope)], axis=1).astype(BF16)
    wuq = jnp.concatenate(
        [head_cols(w_uq[:, MLA_QK * h:MLA_QK * (h + 1)], np.ones_like(is_nope)) for h in range(MLA_HEADS)],
        axis=1).astype(BF16)
    kv_w = MLA_NOPE + MLA_V
    rope_zeros = jnp.zeros((w_ukv.shape[0], MLA_ROPE), w_ukv.dtype)
    wuk = jnp.concatenate(
        [head_cols(jnp.concatenate([w_ukv[:, kv_w * h:kv_w * h + MLA_NOPE], rope_zeros], axis=1), is_nope)
         for h in range(MLA_HEADS)], axis=1).astype(BF16)
    wuv = jnp.concatenate(
        [w_ukv[:, kv_w * h + MLA_NOPE:kv_w * (h + 1)] for h in range(MLA_HEADS)], axis=1).astype(BF16).T
    gmq = _row(jnp.pad(q_gain[perm], (0, pad)))
    gmk = _row(jnp.pad(k_gain[perm], (0, pad)))
    return w1, wuq, wuk, wuv, gmq, gmk


def _rope_freqs():
    half = MLA_ROPE // 2
    freqs = (np.float32(ROPE_THETA) ** (-np.arange(half, dtype=np.float32) / np.float32(half))).astype(np.float32)
    return jnp.asarray(freqs.reshape(half, 1))


def _ones_blocks(n):
    g = np.arange(n) // HEAD_DIM
    return jnp.asarray((g[:, None] == g[None, :]).astype(np.float32), dtype=BF16)


def _tile_sizes(S):
    token_tile = min(2 * MXU_WIDTH, S)
    return token_tile, token_tile, min(2 * MXU_WIDTH, S - SWA_WINDOW), min(4 * MXU_WIDTH, S)


def kernel(x, mem, positions, mix_norm, ev_w_in, ev_swa_q_gain, ev_swa_k_gain, ev_sinks, ev_q_latent_norm,
           ev_kv_latent_norm, ev_w_uq, ev_w_ukv, ev_mla_q_gain, ev_mla_k_gain, ev_w_out, od_w_qkv, od_q_gain,
           od_k_gain, od_lambda, od_subln, od_w_out, mem_q_norm, mem_kv_norm, mem_w_q, mem_w_kv, mem_q_gain,
           mem_k_gain, mem_w_out, ffn_norm, ffn_w_gate, ffn_w_up, ffn_w_down):
    B, S, D = x.shape
    depth = mix_norm.shape[0]
    ts_proj, ts_post, tq_swa, tq_flash = _tile_sizes(S)

    ones = _ones_blocks(MXU_WIDTH)
    pos3 = positions.reshape(B, 1, S)
    inv = _rope_freqs()
    mem_k, mem_v = _mem_kv(mem, mem_kv_norm.reshape(depth, 1, D), mem_w_kv.astype(BF16),
                           mem_k_gain.reshape(depth, 1, MEM_HEAD_DIM))
    diff_slopes = [2.0 ** (-8.0 * (i + 1) / DIFF_HEADS) for i in range(DIFF_HEADS)]

    for l in range(depth):
        if l % 2 == 0:
            e = l // 2
            w1, wuq, wuk, wuv, gmq, gmk = _even_weights(ev_w_in[e], ev_w_uq[e], ev_w_ukv[e], ev_mla_q_gain[e],
                                                        ev_mla_k_gain[e])
            qa, ka, va, qm, km, vm = _even_proj(
                x, pos3, _row(mix_norm[l]), w1, ones, _row(jnp.tile(ev_swa_q_gain[e], SWA_HEADS)),
                _row(jnp.tile(ev_swa_k_gain[e], SWA_KV_HEADS)), _row(ev_q_latent_norm[e]),
                _row(ev_kv_latent_norm[e]), wuq, wuk, wuv, gmq, gmk,
                _shift_row(_score_bound(ev_mla_q_gain[e], ev_mla_k_gain[e], MLA_QK, MLA_QK ** -0.5 * LOG2E), MLA_QK),
                inv, ts_proj)
            swa_bound = _score_bound(ev_swa_q_gain[e], ev_swa_k_gain[e], HEAD_DIM, HEAD_DIM ** -0.5 * LOG2E)
            swa_scalars = jnp.concatenate([ev_sinks[e].astype(F32), swa_bound.reshape(1).astype(F32)])
            out_a = _swa_attention(swa_scalars, qa, ka, va, tq_swa)
            out_b = _mla_attention(qm, km, vm, tq_flash)
            wo = ev_w_out[e].astype(BF16)
            attns = [out_a, out_b]
            wos = [wo[:SWA_Q_W], wo[SWA_Q_W:]]
        else:
            o = l // 2
            lambda_init = 0.8 - 0.6 * math.exp(-0.3 * l)
            n_grp = DIFF_HEADS * 2
            w_qkv = od_w_qkv[o].astype(BF16)
            q, k, vt = _odd_proj(x, _row(mix_norm[l]), w_qkv[:, :2 * D], w_qkv[:, 2 * D:].T, ones,
                                 _row(jnp.tile(od_q_gain[o], n_grp // 2)), _row(jnp.tile(od_k_gain[o], n_grp // 2)),
                                 ts_proj)
            bound = _score_bound(od_q_gain[o], od_k_gain[o], DIFF_DIM, DIFF_DIM ** -0.5 * LOG2E)
            out_d = _diff_attention(diff_slopes, bound, q, k, vt, od_lambda[o].astype(F32), _row(od_subln[o]),
                                    tq_flash, lambda_init)
            attns = [out_d]
            wos = [od_w_out[o].astype(BF16)]
        x = _post_block(x, attns, wos, _row(mem_q_norm[l]), mem_w_q[l].astype(BF16), _row(mem_q_gain[l]),
                        mem_k[l], mem_v[l], mem_w_out[l].astype(BF16), _row(ffn_norm[l]),
                        ffn_w_gate[l].astype(BF16), ffn_w_up[l].astype(BF16), ffn_w_down[l].astype(BF16), ts_post)
    return x
```

```python
import functools
import math

import numpy as np
import jax
import jax.numpy as jnp
from jax import lax
from jax.experimental import pallas as pl
from jax.experimental.pallas import tpu as pltpu

F32 = jnp.float32
BF16 = jnp.bfloat16

EPS = 1e-6
MASKED = -1e30
LOG2E = 1.4426950408889634
ROPE_THETA = 10000.0

LANES = 128
SUBLANES = 8
MXU_WIDTH = 256
HEAD_DIM = 64
SWA_HEADS = 8
SWA_KV_HEADS = 2
SWA_WINDOW = 128
MLA_HEADS = 8
MLA_NOPE = 64
MLA_ROPE = 32
MLA_QK = MLA_NOPE + MLA_ROPE
MLA_V = 64
ROPE_HI_LANE = 64
DIFF_HEADS = 8
DIFF_DIM = 64
MEM_HEADS = 4
MEM_HEAD_DIM = 128
MLA_Q_RANK = 256
MLA_KV_RANK = 128
SWA_Q_W = SWA_HEADS * HEAD_DIM
SWA_KV_W = SWA_KV_HEADS * HEAD_DIM
EVEN_COLS = tuple(np.cumsum([SWA_Q_W, SWA_KV_W, SWA_KV_W, MLA_Q_RANK, MLA_KV_RANK]).tolist())

VMEM_LIMIT = 56 * 1024 * 1024

UNDERFLOW_GUARD = 2.0 ** -80
N_SPLIT = 3


def _params(sem):
    return pltpu.CompilerParams(dimension_semantics=sem, vmem_limit_bytes=VMEM_LIMIT)


def _const_spec(shape):
    nd = len(shape)
    return pl.BlockSpec(shape, lambda *_: (0,) * nd, pipeline_mode=pl.Buffered(1))


def _rms(x, g):
    ms = jnp.mean(x * x, axis=-1, keepdims=True)
    return x * lax.rsqrt(ms + EPS) * g


def _dot(a, b):
    return jnp.dot(a, b, preferred_element_type=F32)


def _dot_nt(a, b):
    return lax.dot_general(a, b, (((1,), (1,)), ((), ())), preferred_element_type=F32)


def _group_sumsq(x, ones_blk):
    x2 = x * x
    hi = x2.astype(BF16)
    lo = (x2 - hi.astype(F32)).astype(BF16)
    return _dot(hi, ones_blk) + _dot(lo, ones_blk)


def _group_norm64(x, ones_blk, gain):
    chunk = min(ones_blk.shape[0], x.shape[1])
    ssq = jnp.concatenate([_group_sumsq(x[:, c:c + chunk], ones_blk[0:chunk, 0:chunk])
                           for c in range(0, x.shape[1], chunk)], axis=1)
    return x * lax.rsqrt(ssq * (1.0 / HEAD_DIM) + EPS) * gain


def _even_proj_kernel(x_ref, pos_ref, gmix_ref, w1_ref, ones_ref, gq_ref, gk_ref, qln_ref, kvln_ref,
                      wuq_ref, wuk_ref, wuv_ref, gmq_ref, gmk_ref, qshift_ref, inv_ref,
                      qa_ref, ka_ref, va_ref, qm_ref, km_ref, vm_ref):
    x = x_ref[0]
    h = _rms(x, gmix_ref[...]).astype(BF16)
    z = _dot(h, w1_ref[...])
    c_ka, c_va, c_cq, c_ckv, c_kr = EVEN_COLS
    ones = ones_ref[...]
    qa = _group_norm64(z[:, 0:c_ka], ones, gq_ref[...]) * (HEAD_DIM ** -0.5 * LOG2E)
    qa_ref[0] = qa.astype(BF16)
    ka = _group_norm64(z[:, c_ka:c_va], ones[0:SWA_KV_W, 0:SWA_KV_W], gk_ref[...])
    ka_ref[0] = ka.astype(BF16)
    va_ref[0] = z[:, c_va:c_cq].astype(BF16)

    cqn = _rms(z[:, c_cq:c_ckv], qln_ref[...]).astype(BF16)
    ckvn = _rms(z[:, c_ckv:c_kr], kvln_ref[...]).astype(BF16)
    kr = z[:, c_kr:c_kr + LANES]
    qf = _dot(cqn, wuq_ref[...])
    kn = _dot(ckvn, wuk_ref[...])
    vm_ref[0] = _dot_nt(wuv_ref[...], ckvn).astype(BF16)

    half = MLA_ROPE // 2
    ang = inv_ref[...] * pos_ref[0].astype(F32)
    pad = jnp.zeros((LANES - half, ang.shape[1]), F32)
    cos = jnp.concatenate([jnp.cos(ang), pad], axis=0).T
    sin = jnp.concatenate([jnp.sin(ang), pad], axis=0).T
    lane = lax.broadcasted_iota(jnp.int32, cos.shape, 1)
    in_lo = lane < half
    in_hi = (lane >= ROPE_HI_LANE) & (lane < ROPE_HI_LANE + half)
    cos_t = jnp.where(in_lo, cos, jnp.where(in_hi, pltpu.roll(cos, ROPE_HI_LANE, 1), 1.0))
    sin_t = jnp.where(in_lo, -sin, jnp.where(in_hi, pltpu.roll(sin, ROPE_HI_LANE, 1), 0.0))

    pair = 2 * LANES
    gi = lax.broadcasted_iota(jnp.int32, (pair, pair), 0) // LANES
    gj = lax.broadcasted_iota(jnp.int32, (pair, pair), 1) // LANES
    head_ones = (gi == gj).astype(BF16)

    def norm_rope(xp, gain):
        r = lax.rsqrt(_group_sumsq(xp, head_ones) * (1.0 / MLA_QK) + EPS)
        xn = xp * r * gain
        halves = []
        for t in range(2):
            xh = xn[:, LANES * t:LANES * (t + 1)]
            halves.append(xh * cos_t + pltpu.roll(xh, ROPE_HI_LANE, 1) * sin_t)
        return halves

    gmq = jnp.concatenate([gmq_ref[...]] * 2, axis=1)
    gmk = jnp.concatenate([gmk_ref[...]] * 2, axis=1)
    kr2 = jnp.concatenate([kr, kr], axis=1)
    shift_lanes = (lane >= MLA_QK) & (lane < MLA_QK + N_SPLIT)
    qshift = qshift_ref[...]
    for hp in range(MLA_HEADS // 2):
        sl = slice(pair * hp, pair * (hp + 1))
        q_heads = norm_rope(qf[:, sl], gmq)
        k_heads = norm_rope(kn[:, sl] + kr2, gmk)
        for t in range(2):
            hl = slice(pair * hp + LANES * t, pair * hp + LANES * (t + 1))
            qm_ref[0, :, hl] = jnp.where(shift_lanes, qshift, q_heads[t] * (MLA_QK ** -0.5 * LOG2E)).astype(BF16)
            km_ref[0, :, hl] = jnp.where(shift_lanes, 1.0, k_heads[t]).astype(BF16)


def _even_proj(x, pos3, gmix, w1, ones, gq, gk, qln, kvln, wuq, wuk, wuv, gmq, gmk, qshift, inv, ts):
    B, S, D = x.shape
    grid = (B, S // ts)
    tok = lambda c: pl.BlockSpec((1, ts, c), lambda b, i: (b, i, 0))
    consts = [gmix, w1, ones, gq, gk, qln, kvln, wuq, wuk, wuv, gmq, gmk, qshift, inv]
    out_cols = [SWA_Q_W, SWA_KV_W, SWA_KV_W, MLA_HEADS * LANES, MLA_HEADS * LANES]
    v_rows = MLA_HEADS * MLA_V
    return pl.pallas_call(
        _even_proj_kernel,
        grid=grid,
        in_specs=([tok(D), pl.BlockSpec((1, 1, ts), lambda b, i: (b, 0, i))]
                  + [_const_spec(c.shape) for c in consts]),
        out_specs=[tok(c) for c in out_cols] + [pl.BlockSpec((1, v_rows, ts), lambda b, i: (b, 0, i))],
        out_shape=([jax.ShapeDtypeStruct((B, S, c), BF16) for c in out_cols]
                   + [jax.ShapeDtypeStruct((B, v_rows, S), BF16)]),
        compiler_params=_params(("parallel", "parallel")),
        name="even_proj",
    )(x, pos3, *consts)


def _swa_place_q(q_ref, hd, lane):
    hk = hd // (SWA_HEADS // SWA_KV_HEADS)
    qq = q_ref[0, :, LANES * (hd // 2):LANES * (hd // 2 + 1)].astype(F32)
    xq = qq if hd % 2 == hk else pltpu.roll(qq, HEAD_DIM, 1)
    return jnp.where((lane >= HEAD_DIM * hk) & (lane < HEAD_DIM * (hk + 1)), xq, 0.0).astype(BF16)


def _swa_place_out(o_even, o_odd, cb, lane):
    placed = []
    for e, o in enumerate((o_even, o_odd)):
        hk = (2 * cb + e) // (SWA_HEADS // SWA_KV_HEADS)
        placed.append(o if e == hk else pltpu.roll(o, HEAD_DIM, 1))
    return jnp.where(lane < HEAD_DIM, placed[0], placed[1]).astype(BF16)


def _swa_running_max(sink_ref, q_ref, k_ref, v_ref, o_ref, qi, tq):
    win = tq + SWA_WINDOW
    wstart = pl.multiple_of(jnp.maximum(qi * tq - SWA_WINDOW, 0), SWA_WINDOW)
    kw = k_ref[0, pl.ds(wstart, win), :]
    vw = v_ref[0, pl.ds(wstart, win), :]
    row = lax.broadcasted_iota(jnp.int32, (tq, win), 0)
    col = lax.broadcasted_iota(jnp.int32, (tq, win), 1)
    dist = (qi * tq + row) - (wstart + col)
    valid = (dist >= 0) & (dist < SWA_WINDOW)
    distf = dist.astype(F32)
    lane = lax.broadcasted_iota(jnp.int32, (tq, LANES), 1)
    for cb in range(SWA_HEADS // 2):
        outs = []
        for e in range(2):
            hd = 2 * cb + e
            s = _dot_nt(_swa_place_q(q_ref, hd, lane), kw) - (2.0 ** (-8.0 * (hd + 1) / SWA_HEADS) * LOG2E) * distf
            s = jnp.where(valid, s, MASKED)
            sink = sink_ref[hd] * LOG2E
            m = jnp.maximum(jnp.max(s, axis=-1, keepdims=True), sink)
            p = jnp.exp2(s - m)
            l = jnp.sum(p, axis=-1, keepdims=True) + jnp.exp2(sink - m)
            outs.append(_dot(p.astype(BF16), vw) / l)
        o_ref[0, :, LANES * cb:LANES * (cb + 1)] = _swa_place_out(outs[0], outs[1], cb, lane)


def _swa_kernel(sink_ref, q_ref, k_ref, v_ref, o_ref, *, tq):
    qi = pl.program_id(1)
    sub = SWA_WINDOW
    win = 2 * sub
    n_sub = tq // sub
    bound = sink_ref[SWA_HEADS]
    lane_q = lax.broadcasted_iota(jnp.int32, (tq, LANES), 1)
    lane = lax.broadcasted_iota(jnp.int32, (sub, LANES), 1)
    row = lax.broadcasted_iota(jnp.int32, (sub, win), 0)
    col = lax.broadcasted_iota(jnp.int32, (sub, win), 1)

    def bias_mask(first_key_offset, slope):
        dist = row + first_key_offset - col
        return jnp.where((dist >= 0) & (dist < SWA_WINDOW), -slope * dist.astype(F32) - bound, MASKED)

    ones_blk = jnp.ones((win, LANES), BF16)
    windows = []
    for r in range(n_sub):
        wstart = qi * tq + sub * (r - 1)
        wstart = pl.multiple_of(jnp.maximum(wstart, 0) if r == 0 else wstart, sub)
        windows.append((k_ref[0, pl.ds(wstart, win), :],
                        jnp.concatenate([v_ref[0, pl.ds(wstart, win), :], ones_blk], axis=1)))

    outs = [[None] * SWA_HEADS for _ in range(n_sub)]
    min_sum = None
    for hd in range(SWA_HEADS):
        slope = 2.0 ** (-8.0 * (hd + 1) / SWA_HEADS) * LOG2E
        bm = bias_mask(sub, slope)
        bm_first = jnp.where(qi == 0, bias_mask(0, slope), bm)
        xq = _swa_place_q(q_ref, hd, lane_q)
        sink_term = jnp.exp2(sink_ref[hd] * LOG2E - bound)
        for r in range(n_sub):
            kw, v_ones = windows[r]
            s = _dot_nt(xq[sub * r:sub * (r + 1)], kw) + (bm_first if r == 0 else bm)
            ov = _dot(jnp.exp2(s).astype(BF16), v_ones)
            l = ov[:, LANES:] + sink_term
            outs[r][hd] = ov[:, :LANES] / l
            min_sum = l if min_sum is None else jnp.minimum(min_sum, l)
    for r in range(n_sub):
        for cb in range(SWA_HEADS // 2):
            o_ref[0, sub * r:sub * (r + 1), LANES * cb:LANES * (cb + 1)] = _swa_place_out(
                outs[r][2 * cb], outs[r][2 * cb + 1], cb, lane)

    @pl.when(jnp.logical_not(jnp.min(min_sum) >= UNDERFLOW_GUARD))
    def _():
        _swa_running_max(sink_ref, q_ref, k_ref, v_ref, o_ref, qi, tq)


def _swa_attention(sinks, qa, ka, va, tq):
    B, S, _ = qa.shape
    return pl.pallas_call(
        functools.partial(_swa_kernel, tq=tq),
        grid=(B, S // tq),
        in_specs=[
            pl.BlockSpec(memory_space=pltpu.SMEM),
            pl.BlockSpec((1, tq, SWA_Q_W), lambda b, i: (b, i, 0)),
            pl.BlockSpec((1, S, SWA_KV_W), lambda b, i: (b, 0, 0)),
            pl.BlockSpec((1, S, SWA_KV_W), lambda b, i: (b, 0, 0)),
        ],
        out_specs=pl.BlockSpec((1, tq, SWA_Q_W), lambda b, i: (b, i, 0)),
        out_shape=jax.ShapeDtypeStruct((B, S, SWA_Q_W), BF16),
        compiler_params=_params(("parallel", "arbitrary")),
        name="swa_attention",
    )(sinks, qa, ka, va)


def _split3(x):
    hi = x.astype(BF16).astype(F32)
    r = x - hi
    mid = r.astype(BF16).astype(F32)
    lo = (r - mid).astype(BF16).astype(F32)
    return hi, mid, lo


def _score_bound(q_gain, k_gain, dim, scale):
    return dim * scale * jnp.max(jnp.abs(q_gain)) * jnp.max(jnp.abs(k_gain))


FFN_CHUNKS = 2

BLOCKS_PER_TRIP = 4


def _loop_blocks(n, step, init):
    def trip(t, c):
        for u in range(BLOCKS_PER_TRIP):
            c = step(BLOCKS_PER_TRIP * t + u, c)
        return c

    full = n // BLOCKS_PER_TRIP
    carry = lax.fori_loop(0, full, trip, init)
    return lax.fori_loop(full * BLOCKS_PER_TRIP, n, step, carry)


def _add_cols(acc, lo, part):
    if lo == 0:
        return acc + part
    return acc + jnp.concatenate([jnp.zeros((acc.shape[0], lo), acc.dtype), part], axis=1)


def _diagonal_spans(tq):
    h = tq // 2
    return ((0, h, 0), (h, h, h)) if h % MXU_WIDTH == 0 else ((0, tq, 0),)


def _causal_t(key_lo, key_n, q_lo, tq):
    shape = (key_n, tq - q_lo)
    return (key_lo + lax.broadcasted_iota(jnp.int32, shape, 0)) <= (q_lo + lax.broadcasted_iota(jnp.int32, shape, 1))


def _causal_mask(tq):
    row = lax.broadcasted_iota(jnp.int32, (tq, tq), 0)
    col = lax.broadcasted_iota(jnp.int32, (tq, tq), 1)
    return row >= col


def _mla_running_max(q_ref, k_ref, vt_ref, e, qi, tq):
    causal = _causal_mask(tq)
    q = q_ref[0, :, LANES * e:LANES * (e + 1)]

    def step(j, carry, masked):
        m, l, acc = carry
        ks = pl.multiple_of(j * tq, tq)
        k = k_ref[0, pl.ds(ks, tq), LANES * e:LANES * (e + 1)]
        vt = vt_ref[0, :, pl.ds(ks, tq)]
        s = _dot_nt(q, k)
        if masked:
            s = jnp.where(causal, s, MASKED)
        m_new = jnp.maximum(m, jnp.max(s, axis=-1, keepdims=True))
        alpha = jnp.exp2(m - m_new)
        p = jnp.exp2(s - m_new)
        l = alpha * l + jnp.sum(p, axis=-1, keepdims=True)
        acc = alpha * acc + _dot_nt(p.astype(BF16), vt)
        return m_new, l, acc

    init = (jnp.full((tq, 1), MASKED, F32), jnp.zeros((tq, 1), F32), jnp.zeros((tq, LANES), F32))
    carry = lax.fori_loop(0, qi, lambda j, c: step(j, c, False), init)
    m, l, acc = step(qi, carry, True)
    return acc / l


def _mla_kernel(q_ref, k_ref, vt_ref, o_ref, *, tq):
    qi = pl.program_id(2)
    lane = lax.broadcasted_iota(jnp.int32, (tq, LANES), 1)
    q_aug = [q_ref[0, :, LANES * e:LANES * (e + 1)] for e in range(2)]

    def block(j, carry, key_lo=0, key_n=tq, q_lo=0, masked=False):
        ks = pl.multiple_of(j * tq + key_lo, key_n)
        new = []
        for e in range(2):
            lsum, acc = carry[e]
            st = _dot_nt(k_ref[0, pl.ds(ks, key_n), LANES * e:LANES * (e + 1)], q_aug[e][q_lo:, :])
            if masked:
                st = jnp.where(_causal_t(key_lo, key_n, q_lo, tq), st, MASKED)
            pt = jnp.exp2(st)
            part = jnp.sum(pt.reshape(key_n // SUBLANES, SUBLANES, tq - q_lo), axis=0)
            pv = _dot(vt_ref[0, MLA_V * e:MLA_V * (e + 1), pl.ds(ks, key_n)], pt.astype(BF16))
            new.append((_add_cols(lsum, q_lo, part), _add_cols(acc, q_lo, pv)))
        return tuple(new)

    zero = (jnp.zeros((SUBLANES, tq), F32), jnp.zeros((MLA_V, tq), F32))
    carry = _loop_blocks(qi, block, (zero, zero))
    for key_lo, key_n, q_lo in _diagonal_spans(tq):
        carry = block(qi, carry, key_lo, key_n, q_lo, masked=True)
    sums = [jnp.sum(lsum, axis=0, keepdims=True) for lsum, _ in carry]
    outs = [acc / l for (_, acc), l in zip(carry, sums)]
    o_ref[0] = jnp.concatenate(outs, axis=0).T.astype(BF16)

    @pl.when(jnp.logical_not(jnp.min(jnp.minimum(sums[0], sums[1])) >= UNDERFLOW_GUARD))
    def _():
        slow = [_mla_running_max(q_ref, k_ref, vt_ref, e, qi, tq) for e in range(2)]
        o_ref[0] = jnp.where(lane < MLA_V, slow[0], slow[1]).astype(BF16)


def _mla_attention(qm, km, vm, tq):
    B, S, _ = qm.shape
    return pl.pallas_call(
        functools.partial(_mla_kernel, tq=tq),
        grid=(B, MLA_HEADS // 2, S // tq),
        in_specs=[
            pl.BlockSpec((1, tq, 2 * LANES), lambda b, h, i: (b, i, h)),
            pl.BlockSpec((1, S, 2 * LANES), lambda b, h, i: (b, 0, h)),
            pl.BlockSpec((1, 2 * MLA_V, S), lambda b, h, i: (b, h, 0)),
        ],
        out_specs=pl.BlockSpec((1, tq, LANES), lambda b, h, i: (b, i, h)),
        out_shape=jax.ShapeDtypeStruct((B, S, MLA_HEADS * MLA_V), BF16),
        compiler_params=_params(("parallel", "parallel", "arbitrary")),
        name="mla_attention",
    )(qm, km, vm)


def _odd_proj_kernel(x_ref, gmix_ref, w_ref, wvt_ref, ones_ref, gq_ref, gk_ref, q_ref, k_ref, vt_ref):
    x = x_ref[0]
    h = _rms(x, gmix_ref[...]).astype(BF16)
    z = _dot(h, w_ref[...])
    ones = ones_ref[...]
    width = DIFF_HEADS * 2 * DIFF_DIM
    piece = gq_ref.shape[1]
    for c in range(0, width, piece):
        q = _group_norm64(z[:, c:c + piece], ones, gq_ref[...]) * (DIFF_DIM ** -0.5 * LOG2E)
        q_ref[0, :, c:c + piece] = q.astype(BF16)
        k = _group_norm64(z[:, width + c:width + c + piece], ones, gk_ref[...])
        k_ref[0, :, c:c + piece] = k.astype(BF16)
    vt_ref[0] = _dot_nt(wvt_ref[...], h).astype(BF16)


def _odd_proj(x, gmix, w_qk, w_vt, ones, gq, gk, ts):
    B, S, D = x.shape
    tok = pl.BlockSpec((1, ts, D), lambda b, i: (b, i, 0))
    consts = [gmix, w_qk, w_vt, ones, gq, gk]
    return pl.pallas_call(
        _odd_proj_kernel,
        grid=(B, S // ts),
        in_specs=[tok] + [_const_spec(c.shape) for c in consts],
        out_specs=[tok, tok, pl.BlockSpec((1, D, ts), lambda b, i: (b, 0, i))],
        out_shape=[jax.ShapeDtypeStruct((B, S, D), BF16)] * 2 + [jax.ShapeDtypeStruct((B, D, S), BF16)],
        compiler_params=_params(("parallel", "parallel")),
        name="odd_proj",
    )(x, *consts)


def _diff_q_parts(q, lane):
    zero = jnp.zeros_like(q)
    return jnp.where(lane < DIFF_DIM, q, zero), jnp.where(lane >= DIFF_DIM, q, zero)


def _diff_running_max(slope, q_ref, k_ref, vt_ref, qi, tq):
    row = lax.broadcasted_iota(jnp.int32, (tq, tq), 0)
    col = lax.broadcasted_iota(jnp.int32, (tq, tq), 1)
    causal = row >= col
    bias = slope * (col - row).astype(F32)
    lane = lax.broadcasted_iota(jnp.int32, (tq, LANES), 1)
    q_parts = _diff_q_parts(q_ref[0], lane)

    def step(j, carry, masked):
        ks = pl.multiple_of(j * tq, tq)
        k = k_ref[0, pl.ds(ks, tq), :]
        vt = vt_ref[0, :, pl.ds(ks, tq)]
        offset = slope * ((j - qi) * tq).astype(F32)
        new = []
        for c in range(2):
            m, l, acc = carry[c]
            s = _dot_nt(q_parts[c], k) + bias
            if masked:
                s = jnp.where(causal, s, MASKED)
            m_new = jnp.maximum(m, jnp.max(s, axis=-1, keepdims=True) + offset)
            alpha = jnp.exp2(m - m_new)
            p = jnp.exp2(s - (m_new - offset))
            l = alpha * l + jnp.sum(p, axis=-1, keepdims=True)
            acc = alpha * acc + _dot_nt(p.astype(BF16), vt)
            new.append((m_new, l, acc))
        return tuple(new)

    one = (jnp.full((tq, 1), MASKED, F32), jnp.zeros((tq, 1), F32), jnp.zeros((tq, LANES), F32))
    carry = lax.fori_loop(0, qi, lambda j, c: step(j, c, False), (one, one))
    (m1, l1, a1), (m2, l2, a2) = step(qi, carry, True)
    return a1 / l1, a2 / l2


def _diff_kernel(slope_ref, qx_ref, qpos_ref, q_ref, k_ref, kx_ref, vt_ref, lam_ref, subln_ref, o_ref, *, tq,
                 lambda_init):
    hd = pl.program_id(1)
    qi = pl.program_id(2)
    lane = lax.broadcasted_iota(jnp.int32, (tq, LANES), 1)
    q_parts = _diff_q_parts(q_ref[0], lane)
    head_row = qx_ref[0]
    qx = jnp.where(lane < 3 * N_SPLIT, head_row,
                   qpos_ref[...].astype(F32) * head_row[:, SLOPE_LANE:SLOPE_LANE + 1]).astype(BF16)
    q_aug = [jnp.concatenate([part, qx], axis=1) for part in q_parts]
    lf = lam_ref[...]
    lam = (jnp.exp(jnp.sum(lf[0:1] * lf[1:2], axis=-1, keepdims=True))
           - jnp.exp(jnp.sum(lf[2:3] * lf[3:4], axis=-1, keepdims=True)) + lambda_init)
    gain = subln_ref[...] * (1.0 - lambda_init)

    def block(j, carry, key_lo=0, key_n=tq, q_lo=0, masked=False):
        ks = pl.multiple_of(j * tq + key_lo, key_n)
        k = jnp.concatenate([k_ref[0, pl.ds(ks, key_n), :], kx_ref[pl.ds(ks, key_n), :]], axis=1)
        vt = vt_ref[0, :, pl.ds(ks, key_n)]
        new = []
        for c in range(2):
            lsum, acc = carry[c]
            st = _dot_nt(k, q_aug[c][q_lo:, :])
            if masked:
                st = jnp.where(_causal_t(key_lo, key_n, q_lo, tq), st, MASKED)
            pt = jnp.exp2(st)
            part = jnp.sum(pt.reshape(key_n // SUBLANES, SUBLANES, tq - q_lo), axis=0)
            new.append((_add_cols(lsum, q_lo, part), _add_cols(acc, q_lo, _dot(vt, pt.astype(BF16)))))
        return tuple(new)

    zero = (jnp.zeros((SUBLANES, tq), F32), jnp.zeros((LANES, tq), F32))
    carry = _loop_blocks(qi, block, (zero, zero))
    for key_lo, key_n, q_lo in _diagonal_spans(tq):
        carry = block(qi, carry, key_lo, key_n, q_lo, masked=True)
    (ls1, a1), (ls2, a2) = carry
    l1 = jnp.sum(ls1, axis=0, keepdims=True)
    l2 = jnp.sum(ls2, axis=0, keepdims=True)

    ot = a1 / l1 - lam * (a2 / l2)
    ot = ot * lax.rsqrt(jnp.mean(ot * ot, axis=0, keepdims=True) + EPS)
    o_ref[0] = (ot.T * gain).astype(BF16)

    @pl.when(jnp.logical_not(jnp.min(jnp.minimum(l1, l2)) >= UNDERFLOW_GUARD))
    def _():
        o1, o2 = _diff_running_max(slope_ref[hd] * LOG2E, q_ref, k_ref, vt_ref, qi, tq)
        o_ref[0] = _rms(o1 - lam * o2, gain).astype(BF16)


SLOPE_LANE = 5 * N_SPLIT


def _alibi_tables(slopes, bound, S):
    assert all(math.log2(s) == int(math.log2(s)) for s in slopes), "ALiBi slopes must be powers of two"
    n = N_SPLIT
    log2e_terms = jnp.concatenate(_split3(jnp.full((1, 1), LOG2E, F32)), axis=1)
    s = jnp.asarray(slopes, F32).reshape(-1, 1)
    head = jnp.concatenate(
        [s * log2e_terms * float(LANES), s * log2e_terms, jnp.broadcast_to(_shift_row(bound, 0)[:, :n], (len(slopes), n)),
         jnp.zeros((len(slopes), SLOPE_LANE - 3 * n), F32), s, jnp.zeros((len(slopes), LANES - SLOPE_LANE - 1), F32)],
        axis=1)
    idx = np.arange(S)
    hi, lo = (idx // LANES)[:, None].astype(np.float32), (idx % LANES)[:, None].astype(np.float32)
    qpos = np.zeros((S, LANES), np.float32)
    qpos[:, 3 * n:4 * n] = hi * LANES
    qpos[:, 4 * n:5 * n] = lo
    kx = jnp.concatenate(
        [jnp.asarray(np.concatenate([np.repeat(hi, n, 1), np.repeat(lo, n, 1), np.ones((S, n), np.float32)], axis=1)),
         jnp.broadcast_to(-log2e_terms, (S, n)), jnp.broadcast_to(-log2e_terms, (S, n)),
         jnp.zeros((S, LANES - 5 * n), F32)], axis=1)
    return head.reshape(-1, 1, LANES), jnp.asarray(qpos, dtype=BF16), kx.astype(BF16)


def _diff_attention(slopes, bound, q, k, vt, lambdas, subln, tq, lambda_init):
    B, S, D = q.shape
    qx, qpos, kx = _alibi_tables(slopes, bound, S)
    return pl.pallas_call(
        functools.partial(_diff_kernel, tq=tq, lambda_init=lambda_init),
        grid=(B, DIFF_HEADS, S // tq),
        in_specs=[
            pl.BlockSpec(memory_space=pltpu.SMEM),
            pl.BlockSpec((1, 1, LANES), lambda b, h, i: (h, 0, 0)),
            pl.BlockSpec((tq, LANES), lambda b, h, i: (i, 0)),
            pl.BlockSpec((1, tq, LANES), lambda b, h, i: (b, i, h)),
            pl.BlockSpec((1, S, LANES), lambda b, h, i: (b, 0, h)),
            _const_spec(kx.shape),
            pl.BlockSpec((1, LANES, S), lambda b, h, i: (b, h, 0)),
            pl.BlockSpec(lambdas.shape, lambda b, h, i: (0, 0)),
            pl.BlockSpec(subln.shape, lambda b, h, i: (0, 0)),
        ],
        out_specs=pl.BlockSpec((1, tq, LANES), lambda b, h, i: (b, i, h)),
        out_shape=jax.ShapeDtypeStruct((B, S, D), BF16),
        compiler_params=_params(("parallel", "parallel", "arbitrary")),
        name="diff_attention",
    )(jnp.asarray(slopes, F32), qx, qpos, q, k, kx, vt, lambdas, subln)


def _mem_kv_kernel(mem_ref, norm_ref, w_ref, gain_ref, k_ref, v_ref):
    h = _rms(mem_ref[0], norm_ref[0]).astype(BF16)
    z = _dot(h, w_ref[0])
    width = MEM_HEADS * MEM_HEAD_DIM
    gain = gain_ref[0]
    for hd in range(MEM_HEADS):
        sl = slice(MEM_HEAD_DIM * hd, MEM_HEAD_DIM * (hd + 1))
        k_ref[0, 0, :, sl] = _rms(z[:, sl], gain).astype(BF16)
    v_ref[0, 0] = z[:, width:2 * width].astype(BF16)


def _mem_kv(mem, norms, w_kv, gains):
    B, M, D = mem.shape
    L = norms.shape[0]
    width = MEM_HEADS * MEM_HEAD_DIM
    out = pl.BlockSpec((1, 1, M, width), lambda l, b: (l, b, 0, 0))
    return pl.pallas_call(
        _mem_kv_kernel,
        grid=(L, B),
        in_specs=[
            pl.BlockSpec((1, M, D), lambda l, b: (b, 0, 0)),
            pl.BlockSpec((1, 1, D), lambda l, b: (l, 0, 0)),
            pl.BlockSpec((1, D, 2 * width), lambda l, b: (l, 0, 0)),
            pl.BlockSpec((1, 1, MEM_HEAD_DIM), lambda l, b: (l, 0, 0)),
        ],
        out_specs=[out, out],
        out_shape=[jax.ShapeDtypeStruct((L, B, M, width), BF16)] * 2,
        compiler_params=_params(("parallel", "parallel")),
        name="mem_kv",
    )(mem, norms, w_kv, gains)


def _post_kernel(*refs, n_attn):
    x_ref = refs[0]
    attn_refs = refs[1:1 + n_attn]
    wo_refs = refs[1 + n_attn:1 + 2 * n_attn]
    (mqn_ref, wq_ref, qg_ref, mk_ref, mv_ref, wmo_ref, fn_ref, wg_ref, wu_ref, wd_ref, o_ref) = refs[1 + 2 * n_attn:]
    x = x_ref[0]
    for a_ref, w_ref in zip(attn_refs, wo_refs):
        x = x + _dot(a_ref[0], w_ref[...])

    q = _dot(_rms(x, mqn_ref[...]).astype(BF16), wq_ref[...])
    heads = []
    for hd in range(MEM_HEADS):
        sl = slice(MEM_HEAD_DIM * hd, MEM_HEAD_DIM * (hd + 1))
        qh = (_rms(q[:, sl], qg_ref[...]) * (MEM_HEAD_DIM ** -0.5 * LOG2E)).astype(BF16)
        s = _dot_nt(qh, mk_ref[0, :, sl])
        p = jnp.exp2(s - jnp.max(s, axis=-1, keepdims=True))
        l = jnp.sum(p, axis=-1, keepdims=True)
        heads.append((_dot(p.astype(BF16), mv_ref[0, :, sl]) / l).astype(BF16))
    x = x + _dot(jnp.concatenate(heads, axis=-1), wmo_ref[...])

    hf = _rms(x, fn_ref[...]).astype(BF16)
    d_ff = wg_ref.shape[1]
    chunk = d_ff // FFN_CHUNKS
    y = x
    for c in range(FFN_CHUNKS):
        cols = slice(chunk * c, chunk * (c + 1))
        g = _dot(hf, wg_ref[:, cols])
        u = _dot(hf, wu_ref[:, cols])
        act = (g * jax.nn.sigmoid(g) * u).astype(BF16)
        y = y + _dot(act, wd_ref[cols, :])
    o_ref[0] = y


def _post_block(x, attns, wos, mqn, wq, qg, mk, mv, wmo, fn, wg, wu, wd, ts):
    B, S, D = x.shape
    n = len(attns)
    tok = lambda c: pl.BlockSpec((1, ts, c), lambda b, i: (b, i, 0))
    memspec = pl.BlockSpec((1,) + mk.shape[1:], lambda b, i: (b, 0, 0))
    consts_a = list(wos) + [mqn, wq, qg]
    consts_b = [wmo, fn, wg, wu, wd]
    return pl.pallas_call(
        functools.partial(_post_kernel, n_attn=n),
        grid=(B, S // ts),
        in_specs=([tok(D)] + [tok(a.shape[-1]) for a in attns] + [_const_spec(c.shape) for c in consts_a]
                  + [memspec, memspec] + [_const_spec(c.shape) for c in consts_b]),
        out_specs=tok(D),
        out_shape=jax.ShapeDtypeStruct((B, S, D), F32),
        compiler_params=_params(("parallel", "parallel")),
        name="post_block",
    )(x, *attns, *consts_a, mk, mv, *consts_b)


def _row(v):
    return v.reshape(1, -1).astype(F32)


def _shift_row(bound, first_lane):
    terms = jnp.stack(_split3(-jnp.asarray(bound, F32))).reshape(1, N_SPLIT)
    return jnp.pad(terms, ((0, 0), (first_lane, LANES - first_lane - N_SPLIT)))


def _mla_lane_perm():
    half = MLA_ROPE // 2
    lanes = np.arange(MLA_QK)
    return np.where(lanes < half, MLA_NOPE + lanes,
                    np.where(lanes < ROPE_HI_LANE, lanes - half,
                             np.where(lanes < ROPE_HI_LANE + half, lanes + half, lanes - MLA_ROPE)))


def _even_weights(w_in, w_uq, w_ukv, q_gain, k_gain):
    perm = _mla_lane_perm()
    is_nope = perm < MLA_NOPE
    pad = LANES - MLA_QK

    def head_cols(w_head, live):
        return jnp.pad(jnp.where(live[None, :], w_head[:, perm], 0.0), ((0, 0), (0, pad)))

    c_kr = EVEN_COLS[-1]
    kr_src = jnp.concatenate([jnp.zeros((w_in.shape[0], MLA_NOPE), w_in.dtype), w_in[:, c_kr:c_kr + MLA_ROPE]], axis=1)
    w1 = jnp.concatenate([w_in[:, :c_kr], head_cols(kr_src, ~is_nope)], axis=1).astype(BF16)
    wuq = jnp.concatenate(
        [head_cols(w_uq[:, MLA_QK * h:MLA_QK * (h + 1)], np.ones_like(is_nope)) for h in range(MLA_HEADS)],
        axis=1).astype(BF16)
    kv_w = MLA_NOPE + MLA_V
    rope_zeros = jnp.zeros((w_ukv.shape[0], MLA_ROPE), w_ukv.dtype)
    wuk = jnp.concatenate(
        [head_cols(jnp.concatenate([w_ukv[:, kv_w * h:kv_w * h + MLA_NOPE], rope_zeros], axis=1), is_nope)
         for h in range(MLA_HEADS)], axis=1).astype(BF16)
    wuv = jnp.concatenate(
        [w_ukv[:, kv_w * h + MLA_NOPE:kv_w * (h + 1)] for h in range(MLA_HEADS)], axis=1).astype(BF16).T
    gmq = _row(jnp.pad(q_gain[perm], (0, pad)))
    gmk = _row(jnp.pad(k_gain[perm], (0, pad)))
    return w1, wuq, wuk, wuv, gmq, gmk


def _rope_freqs():
    half = MLA_ROPE // 2
    freqs = (np.float32(ROPE_THETA) ** (-np.arange(half, dtype=np.float32) / np.float32(half))).astype(np.float32)
    return jnp.asarray(freqs.reshape(half, 1))


def _ones_blocks(n):
    g = np.arange(n) // HEAD_DIM
    return jnp.asarray((g[:, None] == g[None, :]).astype(np.float32), dtype=BF16)


def _tile_sizes(S):
    token_tile = min(2 * MXU_WIDTH, S)
    return token_tile, token_tile, min(2 * MXU_WIDTH, S - SWA_WINDOW), min(4 * MXU_WIDTH, S)


def kernel(x, mem, positions, mix_norm, ev_w_in, ev_swa_q_gain, ev_swa_k_gain, ev_sinks, ev_q_latent_norm,
           ev_kv_latent_norm, ev_w_uq, ev_w_ukv, ev_mla_q_gain, ev_mla_k_gain, ev_w_out, od_w_qkv, od_q_gain,
           od_k_gain, od_lambda, od_subln, od_w_out, mem_q_norm, mem_kv_norm, mem_w_q, mem_w_kv, mem_q_gain,
           mem_k_gain, mem_w_out, ffn_norm, ffn_w_gate, ffn_w_up, ffn_w_down):
    B, S, D = x.shape
    depth = mix_norm.shape[0]
    ts_proj, ts_post, tq_swa, tq_flash = _tile_sizes(S)

    ones = _ones_blocks(MXU_WIDTH)
    pos3 = positions.reshape(B, 1, S)
    inv = _rope_freqs()
    mem_k, mem_v = _mem_kv(mem, mem_kv_norm.reshape(depth, 1, D), mem_w_kv.astype(BF16),
                           mem_k_gain.reshape(depth, 1, MEM_HEAD_DIM))
    diff_slopes = [2.0 ** (-8.0 * (i + 1) / DIFF_HEADS) for i in range(DIFF_HEADS)]

    for l in range(depth):
        if l % 2 == 0:
            e = l // 2
            w1, wuq, wuk, wuv, gmq, gmk = _even_weights(ev_w_in[e], ev_w_uq[e], ev_w_ukv[e], ev_mla_q_gain[e],
                                                        ev_mla_k_gain[e])
            qa, ka, va, qm, km, vm = _even_proj(
                x, pos3, _row(mix_norm[l]), w1, ones, _row(jnp.tile(ev_swa_q_gain[e], SWA_HEADS)),
                _row(jnp.tile(ev_swa_k_gain[e], SWA_KV_HEADS)), _row(ev_q_latent_norm[e]),
                _row(ev_kv_latent_norm[e]), wuq, wuk, wuv, gmq, gmk,
                _shift_row(_score_bound(ev_mla_q_gain[e], ev_mla_k_gain[e], MLA_QK, MLA_QK ** -0.5 * LOG2E), MLA_QK),
                inv, ts_proj)
            swa_bound = _score_bound(ev_swa_q_gain[e], ev_swa_k_gain[e], HEAD_DIM, HEAD_DIM ** -0.5 * LOG2E)
            swa_scalars = jnp.concatenate([ev_sinks[e].astype(F32), swa_bound.reshape(1).astype(F32)])
            out_a = _swa_attention(swa_scalars, qa, ka, va, tq_swa)
            out_b = _mla_attention(qm, km, vm, tq_flash)
            wo = ev_w_out[e].astype(BF16)
            attns = [out_a, out_b]
            wos = [wo[:SWA_Q_W], wo[SWA_Q_W:]]
        else:
            o = l // 2
            lambda_init = 0.8 - 0.6 * math.exp(-0.3 * l)
            n_grp = DIFF_HEADS * 2
            w_qkv = od_w_qkv[o].astype(BF16)
            q, k, vt = _odd_proj(x, _row(mix_norm[l]), w_qkv[:, :2 * D], w_qkv[:, 2 * D:].T, ones,
                                 _row(jnp.tile(od_q_gain[o], n_grp // 2)), _row(jnp.tile(od_k_gain[o], n_grp // 2)),
                                 ts_proj)
            bound = _score_bound(od_q_gain[o], od_k_gain[o], DIFF_DIM, DIFF_DIM ** -0.5 * LOG2E)
            out_d = _diff_attention(diff_slopes, bound, q, k, vt, od_lambda[o].astype(F32), _row(od_subln[o]),
                                    tq_flash, lambda_init)
            attns = [out_d]
            wos = [od_w_out[o].astype(BF16)]
        x = _post_block(x, attns, wos, _row(mem_q_norm[l]), mem_w_q[l].astype(BF16), _row(mem_q_gain[l]),
                        mem_k[l], mem_v[l], mem_w_out[l].astype(BF16), _row(ffn_norm[l]),
                        ffn_w_gate[l].astype(BF16), ffn_w_up[l].astype(BF16), ffn_w_down[l].astype(BF16), ts_post)
    return x
```

```python
import functools
import math

import numpy as np
import jax
import jax.numpy as jnp
from jax import lax
from jax.experimental import pallas as pl
from jax.experimental.pallas import tpu as pltpu

F32 = jnp.float32
BF16 = jnp.bfloat16

EPS = 1e-6
MASKED = -1e30
LOG2E = 1.4426950408889634
ROPE_THETA = 10000.0

LANES = 128
SUBLANES = 8
MXU_WIDTH = 256
HEAD_DIM = 64
SWA_HEADS = 8
SWA_KV_HEADS = 2
SWA_WINDOW = 128
MLA_HEADS = 8
MLA_NOPE = 64
MLA_ROPE = 32
MLA_QK = MLA_NOPE + MLA_ROPE
MLA_V = 64
ROPE_HI_LANE = 64
DIFF_HEADS = 8
DIFF_DIM = 64
MEM_HEADS = 4
MEM_HEAD_DIM = 128
MLA_Q_RANK = 256
MLA_KV_RANK = 128
SWA_Q_W = SWA_HEADS * HEAD_DIM
SWA_KV_W = SWA_KV_HEADS * HEAD_DIM
EVEN_COLS = tuple(np.cumsum([SWA_Q_W, SWA_KV_W, SWA_KV_W, MLA_Q_RANK, MLA_KV_RANK]).tolist())

VMEM_LIMIT = 56 * 1024 * 1024

UNDERFLOW_GUARD = 2.0 ** -80
N_SPLIT = 3


def _params(sem):
    return pltpu.CompilerParams(dimension_semantics=sem, vmem_limit_bytes=VMEM_LIMIT)


def _const_spec(shape):
    nd = len(shape)
    return pl.BlockSpec(shape, lambda *_: (0,) * nd, pipeline_mode=pl.Buffered(1))


def _rms(x, g):
    ms = jnp.mean(x * x, axis=-1, keepdims=True)
    return x * lax.rsqrt(ms + EPS) * g


def _dot(a, b):
    return jnp.dot(a, b, preferred_element_type=F32)


def _dot_nt(a, b):
    return lax.dot_general(a, b, (((1,), (1,)), ((), ())), preferred_element_type=F32)


def _group_sumsq(x, ones_blk):
    x2 = x * x
    hi = x2.astype(BF16)
    lo = (x2 - hi.astype(F32)).astype(BF16)
    return _dot(hi, ones_blk) + _dot(lo, ones_blk)


def _group_norm64(x, ones_blk, gain):
    chunk = min(ones_blk.shape[0], x.shape[1])
    ssq = jnp.concatenate([_group_sumsq(x[:, c:c + chunk], ones_blk[0:chunk, 0:chunk])
                           for c in range(0, x.shape[1], chunk)], axis=1)
    return x * lax.rsqrt(ssq * (1.0 / HEAD_DIM) + EPS) * gain


def _even_proj_kernel(x_ref, pos_ref, gmix_ref, w1_ref, ones_ref, gq_ref, gk_ref, qln_ref, kvln_ref,
                      wuq_ref, wuk_ref, wuv_ref, gmq_ref, gmk_ref, qshift_ref, inv_ref,
                      qa_ref, ka_ref, va_ref, qm_ref, km_ref, vm_ref):
    x = x_ref[0]
    h = _rms(x, gmix_ref[...]).astype(BF16)
    z = _dot(h, w1_ref[...])
    c_ka, c_va, c_cq, c_ckv, c_kr = EVEN_COLS
    ones = ones_ref[...]
    qa = _group_norm64(z[:, 0:c_ka], ones, gq_ref[...]) * (HEAD_DIM ** -0.5 * LOG2E)
    qa_ref[0] = qa.astype(BF16)
    ka = _group_norm64(z[:, c_ka:c_va], ones[0:SWA_KV_W, 0:SWA_KV_W], gk_ref[...])
    ka_ref[0] = ka.astype(BF16)
    va_ref[0] = z[:, c_va:c_cq].astype(BF16)

    cqn = _rms(z[:, c_cq:c_ckv], qln_ref[...]).astype(BF16)
    ckvn = _rms(z[:, c_ckv:c_kr], kvln_ref[...]).astype(BF16)
    kr = z[:, c_kr:c_kr + LANES]
    qf = _dot(cqn, wuq_ref[...])
    kn = _dot(ckvn, wuk_ref[...])
    vm_ref[0] = _dot_nt(wuv_ref[...], ckvn).astype(BF16)

    half = MLA_ROPE // 2
    ang = inv_ref[...] * pos_ref[0].astype(F32)
    pad = jnp.zeros((LANES - half, ang.shape[1]), F32)
    cos = jnp.concatenate([jnp.cos(ang), pad], axis=0).T
    sin = jnp.concatenate([jnp.sin(ang), pad], axis=0).T
    lane = lax.broadcasted_iota(jnp.int32, cos.shape, 1)
    in_lo = lane < half
    in_hi = (lane >= ROPE_HI_LANE) & (lane < ROPE_HI_LANE + half)
    cos_t = jnp.where(in_lo, cos, jnp.where(in_hi, pltpu.roll(cos, ROPE_HI_LANE, 1), 1.0))
    sin_t = jnp.where(in_lo, -sin, jnp.where(in_hi, pltpu.roll(sin, ROPE_HI_LANE, 1), 0.0))

    pair = 2 * LANES
    gi = lax.broadcasted_iota(jnp.int32, (pair, pair), 0) // LANES
    gj = lax.broadcasted_iota(jnp.int32, (pair, pair), 1) // LANES
    head_ones = (gi == gj).astype(BF16)

    def norm_rope(xp, gain):
        r = lax.rsqrt(_group_sumsq(xp, head_ones) * (1.0 / MLA_QK) + EPS)
        xn = xp * r * gain
        halves = []
        for t in range(2):
            xh = xn[:, LANES * t:LANES * (t + 1)]
            halves.append(xh * cos_t + pltpu.roll(xh, ROPE_HI_LANE, 1) * sin_t)
        return halves

    gmq = jnp.concatenate([gmq_ref[...]] * 2, axis=1)
    gmk = jnp.concatenate([gmk_ref[...]] * 2, axis=1)
    kr2 = jnp.concatenate([kr, kr], axis=1)
    shift_lanes = (lane >= MLA_QK) & (lane < MLA_QK + N_SPLIT)
    qshift = qshift_ref[...]
    for hp in range(MLA_HEADS // 2):
        sl = slice(pair * hp, pair * (hp + 1))
        q_heads = norm_rope(qf[:, sl], gmq)
        k_heads = norm_rope(kn[:, sl] + kr2, gmk)
        for t in range(2):
            hl = slice(pair * hp + LANES * t, pair * hp + LANES * (t + 1))
            qm_ref[0, :, hl] = jnp.where(shift_lanes, qshift, q_heads[t] * (MLA_QK ** -0.5 * LOG2E)).astype(BF16)
            km_ref[0, :, hl] = jnp.where(shift_lanes, 1.0, k_heads[t]).astype(BF16)


def _even_proj(x, pos3, gmix, w1, ones, gq, gk, qln, kvln, wuq, wuk, wuv, gmq, gmk, qshift, inv, ts):
    B, S, D = x.shape
    grid = (B, S // ts)
    tok = lambda c: pl.BlockSpec((1, ts, c), lambda b, i: (b, i, 0))
    consts = [gmix, w1, ones, gq, gk, qln, kvln, wuq, wuk, wuv, gmq, gmk, qshift, inv]
    out_cols = [SWA_Q_W, SWA_KV_W, SWA_KV_W, MLA_HEADS * LANES, MLA_HEADS * LANES]
    v_rows = MLA_HEADS * MLA_V
    return pl.pallas_call(
        _even_proj_kernel,
        grid=grid,
        in_specs=([tok(D), pl.BlockSpec((1, 1, ts), lambda b, i: (b, 0, i))]
                  + [_const_spec(c.shape) for c in consts]),
        out_specs=[tok(c) for c in out_cols] + [pl.BlockSpec((1, v_rows, ts), lambda b, i: (b, 0, i))],
        out_shape=([jax.ShapeDtypeStruct((B, S, c), BF16) for c in out_cols]
                   + [jax.ShapeDtypeStruct((B, v_rows, S), BF16)]),
        compiler_params=_params(("parallel", "parallel")),
        name="even_proj",
    )(x, pos3, *consts)


def _swa_place_q(q_ref, hd, lane):
    hk = hd // (SWA_HEADS // SWA_KV_HEADS)
    qq = q_ref[0, :, LANES * (hd // 2):LANES * (hd // 2 + 1)].astype(F32)
    xq = qq if hd % 2 == hk else pltpu.roll(qq, HEAD_DIM, 1)
    return jnp.where((lane >= HEAD_DIM * hk) & (lane < HEAD_DIM * (hk + 1)), xq, 0.0).astype(BF16)


def _swa_place_out(o_even, o_odd, cb, lane):
    placed = []
    for e, o in enumerate((o_even, o_odd)):
        hk = (2 * cb + e) // (SWA_HEADS // SWA_KV_HEADS)
        placed.append(o if e == hk else pltpu.roll(o, HEAD_DIM, 1))
    return jnp.where(lane < HEAD_DIM, placed[0], placed[1]).astype(BF16)


def _swa_running_max(sink_ref, q_ref, k_ref, v_ref, o_ref, qi, tq):
    win = tq + SWA_WINDOW
    wstart = pl.multiple_of(jnp.maximum(qi * tq - SWA_WINDOW, 0), SWA_WINDOW)
    kw = k_ref[0, pl.ds(wstart, win), :]
    vw = v_ref[0, pl.ds(wstart, win), :]
    row = lax.broadcasted_iota(jnp.int32, (tq, win), 0)
    col = lax.broadcasted_iota(jnp.int32, (tq, win), 1)
    dist = (qi * tq + row) - (wstart + col)
    valid = (dist >= 0) & (dist < SWA_WINDOW)
    distf = dist.astype(F32)
    lane = lax.broadcasted_iota(jnp.int32, (tq, LANES), 1)
    for cb in range(SWA_HEADS // 2):
        outs = []
        for e in range(2):
            hd = 2 * cb + e
            s = _dot_nt(_swa_place_q(q_ref, hd, lane), kw) - (2.0 ** (-8.0 * (hd + 1) / SWA_HEADS) * LOG2E) * distf
            s = jnp.where(valid, s, MASKED)
            sink = sink_ref[hd] * LOG2E
            m = jnp.maximum(jnp.max(s, axis=-1, keepdims=True), sink)
            p = jnp.exp2(s - m)
            l = jnp.sum(p, axis=-1, keepdims=True) + jnp.exp2(sink - m)
            outs.append(_dot(p.astype(BF16), vw) / l)
        o_ref[0, :, LANES * cb:LANES * (cb + 1)] = _swa_place_out(outs[0], outs[1], cb, lane)


def _swa_kernel(sink_ref, q_ref, k_ref, v_ref, o_ref, *, tq):
    qi = pl.program_id(1)
    sub = SWA_WINDOW
    win = 2 * sub
    n_sub = tq // sub
    bound = sink_ref[SWA_HEADS]
    lane_q = lax.broadcasted_iota(jnp.int32, (tq, LANES), 1)
    lane = lax.broadcasted_iota(jnp.int32, (sub, LANES), 1)
    row = lax.broadcasted_iota(jnp.int32, (sub, win), 0)
    col = lax.broadcasted_iota(jnp.int32, (sub, win), 1)

    def bias_mask(first_key_offset, slope):
        dist = row + first_key_offset - col
        return jnp.where((dist >= 0) & (dist < SWA_WINDOW), -slope * dist.astype(F32) - bound, MASKED)

    ones_blk = jnp.ones((win, LANES), BF16)
    windows = []
    for r in range(n_sub):
        wstart = qi * tq + sub * (r - 1)
        wstart = pl.multiple_of(jnp.maximum(wstart, 0) if r == 0 else wstart, sub)
        windows.append((k_ref[0, pl.ds(wstart, win), :],
                        jnp.concatenate([v_ref[0, pl.ds(wstart, win), :], ones_blk], axis=1)))

    outs = [[None] * SWA_HEADS for _ in range(n_sub)]
    min_sum = None
    for hd in range(SWA_HEADS):
        slope = 2.0 ** (-8.0 * (hd + 1) / SWA_HEADS) * LOG2E
        bm = bias_mask(sub, slope)
        bm_first = jnp.where(qi == 0, bias_mask(0, slope), bm)
        xq = _swa_place_q(q_ref, hd, lane_q)
        sink_term = jnp.exp2(sink_ref[hd] * LOG2E - bound)
        for r in range(n_sub):
            kw, v_ones = windows[r]
            s = _dot_nt(xq[sub * r:sub * (r + 1)], kw) + (bm_first if r == 0 else bm)
            ov = _dot(jnp.exp2(s).astype(BF16), v_ones)
            l = ov[:, LANES:] + sink_term
            outs[r][hd] = ov[:, :LANES] / l
            min_sum = l if min_sum is None else jnp.minimum(min_sum, l)
    for r in range(n_sub):
        for cb in range(SWA_HEADS // 2):
            o_ref[0, sub * r:sub * (r + 1), LANES * cb:LANES * (cb + 1)] = _swa_place_out(
                outs[r][2 * cb], outs[r][2 * cb + 1], cb, lane)

    @pl.when(jnp.logical_not(jnp.min(min_sum) >= UNDERFLOW_GUARD))
    def _():
        _swa_running_max(sink_ref, q_ref, k_ref, v_ref, o_ref, qi, tq)


def _swa_attention(sinks, qa, ka, va, tq):
    B, S, _ = qa.shape
    return pl.pallas_call(
        functools.partial(_swa_kernel, tq=tq),
        grid=(B, S // tq),
        in_specs=[
            pl.BlockSpec(memory_space=pltpu.SMEM),
            pl.BlockSpec((1, tq, SWA_Q_W), lambda b, i: (b, i, 0)),
            pl.BlockSpec((1, S, SWA_KV_W), lambda b, i: (b, 0, 0)),
            pl.BlockSpec((1, S, SWA_KV_W), lambda b, i: (b, 0, 0)),
        ],
        out_specs=pl.BlockSpec((1, tq, SWA_Q_W), lambda b, i: (b, i, 0)),
        out_shape=jax.ShapeDtypeStruct((B, S, SWA_Q_W), BF16),
        compiler_params=_params(("parallel", "arbitrary")),
        name="swa_attention",
    )(sinks, qa, ka, va)


def _split3(x):
    hi = x.astype(BF16).astype(F32)
    r = x - hi
    mid = r.astype(BF16).astype(F32)
    lo = (r - mid).astype(BF16).astype(F32)
    return hi, mid, lo


def _score_bound(q_gain, k_gain, dim, scale):
    return dim * scale * jnp.max(jnp.abs(q_gain)) * jnp.max(jnp.abs(k_gain))


BLOCKS_PER_TRIP = 4


def _loop_blocks(n, step, init):
    def trip(t, c):
        for u in range(BLOCKS_PER_TRIP):
            c = step(BLOCKS_PER_TRIP * t + u, c)
        return c

    full = n // BLOCKS_PER_TRIP
    carry = lax.fori_loop(0, full, trip, init)
    return lax.fori_loop(full * BLOCKS_PER_TRIP, n, step, carry)


def _add_cols(acc, lo, part):
    if lo == 0:
        return acc + part
    return acc + jnp.concatenate([jnp.zeros((acc.shape[0], lo), acc.dtype), part], axis=1)


def _diagonal_spans(tq):
    h = tq // 2
    return ((0, h, 0), (h, h, h)) if h % MXU_WIDTH == 0 else ((0, tq, 0),)


def _causal_t(key_lo, key_n, q_lo, tq):
    shape = (key_n, tq - q_lo)
    return (key_lo + lax.broadcasted_iota(jnp.int32, shape, 0)) <= (q_lo + lax.broadcasted_iota(jnp.int32, shape, 1))


def _causal_mask(tq):
    row = lax.broadcasted_iota(jnp.int32, (tq, tq), 0)
    col = lax.broadcasted_iota(jnp.int32, (tq, tq), 1)
    return row >= col


def _mla_running_max(q_ref, k_ref, vt_ref, e, qi, tq):
    causal = _causal_mask(tq)
    q = q_ref[0, :, LANES * e:LANES * (e + 1)]

    def step(j, carry, masked):
        m, l, acc = carry
        ks = pl.multiple_of(j * tq, tq)
        k = k_ref[0, pl.ds(ks, tq), LANES * e:LANES * (e + 1)]
        vt = vt_ref[0, :, pl.ds(ks, tq)]
        s = _dot_nt(q, k)
        if masked:
            s = jnp.where(causal, s, MASKED)
        m_new = jnp.maximum(m, jnp.max(s, axis=-1, keepdims=True))
        alpha = jnp.exp2(m - m_new)
        p = jnp.exp2(s - m_new)
        l = alpha * l + jnp.sum(p, axis=-1, keepdims=True)
        acc = alpha * acc + _dot_nt(p.astype(BF16), vt)
        return m_new, l, acc

    init = (jnp.full((tq, 1), MASKED, F32), jnp.zeros((tq, 1), F32), jnp.zeros((tq, LANES), F32))
    carry = lax.fori_loop(0, qi, lambda j, c: step(j, c, False), init)
    m, l, acc = step(qi, carry, True)
    return acc / l


def _mla_kernel(q_ref, k_ref, vt_ref, o_ref, *, tq):
    qi = pl.program_id(2)
    lane = lax.broadcasted_iota(jnp.int32, (tq, LANES), 1)
    q_aug = [q_ref[0, :, LANES * e:LANES * (e + 1)] for e in range(2)]

    def block(j, carry, key_lo=0, key_n=tq, q_lo=0, masked=False):
        ks = pl.multiple_of(j * tq + key_lo, key_n)
        new = []
        for e in range(2):
            lsum, acc = carry[e]
            st = _dot_nt(k_ref[0, pl.ds(ks, key_n), LANES * e:LANES * (e + 1)], q_aug[e][q_lo:, :])
            if masked:
                st = jnp.where(_causal_t(key_lo, key_n, q_lo, tq), st, MASKED)
            pt = jnp.exp2(st)
            part = jnp.sum(pt.reshape(key_n // SUBLANES, SUBLANES, tq - q_lo), axis=0)
            pv = _dot(vt_ref[0, MLA_V * e:MLA_V * (e + 1), pl.ds(ks, key_n)], pt.astype(BF16))
            new.append((_add_cols(lsum, q_lo, part), _add_cols(acc, q_lo, pv)))
        return tuple(new)

    zero = (jnp.zeros((SUBLANES, tq), F32), jnp.zeros((MLA_V, tq), F32))
    carry = _loop_blocks(qi, block, (zero, zero))
    for key_lo, key_n, q_lo in _diagonal_spans(tq):
        carry = block(qi, carry, key_lo, key_n, q_lo, masked=True)
    sums = [jnp.sum(lsum, axis=0, keepdims=True) for lsum, _ in carry]
    outs = [acc / l for (_, acc), l in zip(carry, sums)]
    o_ref[0] = jnp.concatenate(outs, axis=0).T.astype(BF16)

    @pl.when(jnp.logical_not(jnp.min(jnp.minimum(sums[0], sums[1])) >= UNDERFLOW_GUARD))
    def _():
        slow = [_mla_running_max(q_ref, k_ref, vt_ref, e, qi, tq) for e in range(2)]
        o_ref[0] = jnp.where(lane < MLA_V, slow[0], slow[1]).astype(BF16)


def _mla_attention(qm, km, vm, tq):
    B, S, _ = qm.shape
    return pl.pallas_call(
        functools.partial(_mla_kernel, tq=tq),
        grid=(B, MLA_HEADS // 2, S // tq),
        in_specs=[
            pl.BlockSpec((1, tq, 2 * LANES), lambda b, h, i: (b, i, h)),
            pl.BlockSpec((1, S, 2 * LANES), lambda b, h, i: (b, 0, h)),
            pl.BlockSpec((1, 2 * MLA_V, S), lambda b, h, i: (b, h, 0)),
        ],
        out_specs=pl.BlockSpec((1, tq, LANES), lambda b, h, i: (b, i, h)),
        out_shape=jax.ShapeDtypeStruct((B, S, MLA_HEADS * MLA_V), BF16),
        compiler_params=_params(("parallel", "parallel", "arbitrary")),
        name="mla_attention",
    )(qm, km, vm)


def _odd_proj_kernel(x_ref, gmix_ref, w_ref, wvt_ref, ones_ref, gq_ref, gk_ref, q_ref, k_ref, vt_ref):
    x = x_ref[0]
    h = _rms(x, gmix_ref[...]).astype(BF16)
    z = _dot(h, w_ref[...])
    ones = ones_ref[...]
    width = DIFF_HEADS * 2 * DIFF_DIM
    piece = gq_ref.shape[1]
    for c in range(0, width, piece):
        q = _group_norm64(z[:, c:c + piece], ones, gq_ref[...]) * (DIFF_DIM ** -0.5 * LOG2E)
        q_ref[0, :, c:c + piece] = q.astype(BF16)
        k = _group_norm64(z[:, width + c:width + c + piece], ones, gk_ref[...])
        k_ref[0, :, c:c + piece] = k.astype(BF16)
    vt_ref[0] = _dot_nt(wvt_ref[...], h).astype(BF16)


def _odd_proj(x, gmix, w_qk, w_vt, ones, gq, gk, ts):
    B, S, D = x.shape
    tok = pl.BlockSpec((1, ts, D), lambda b, i: (b, i, 0))
    consts = [gmix, w_qk, w_vt, ones, gq, gk]
    return pl.pallas_call(
        _odd_proj_kernel,
        grid=(B, S // ts),
        in_specs=[tok] + [_const_spec(c.shape) for c in consts],
        out_specs=[tok, tok, pl.BlockSpec((1, D, ts), lambda b, i: (b, 0, i))],
        out_shape=[jax.ShapeDtypeStruct((B, S, D), BF16)] * 2 + [jax.ShapeDtypeStruct((B, D, S), BF16)],
        compiler_params=_params(("parallel", "parallel")),
        name="odd_proj",
    )(x, *consts)


def _diff_q_parts(q, lane):
    zero = jnp.zeros_like(q)
    return jnp.where(lane < DIFF_DIM, q, zero), jnp.where(lane >= DIFF_DIM, q, zero)


def _diff_running_max(slope, q_ref, k_ref, vt_ref, qi, tq):
    row = lax.broadcasted_iota(jnp.int32, (tq, tq), 0)
    col = lax.broadcasted_iota(jnp.int32, (tq, tq), 1)
    causal = row >= col
    bias = slope * (col - row).astype(F32)
    lane = lax.broadcasted_iota(jnp.int32, (tq, LANES), 1)
    q_parts = _diff_q_parts(q_ref[0], lane)

    def step(j, carry, masked):
        ks = pl.multiple_of(j * tq, tq)
        k = k_ref[0, pl.ds(ks, tq), :]
        vt = vt_ref[0, :, pl.ds(ks, tq)]
        offset = slope * ((j - qi) * tq).astype(F32)
        new = []
        for c in range(2):
            m, l, acc = carry[c]
            s = _dot_nt(q_parts[c], k) + bias
            if masked:
                s = jnp.where(causal, s, MASKED)
            m_new = jnp.maximum(m, jnp.max(s, axis=-1, keepdims=True) + offset)
            alpha = jnp.exp2(m - m_new)
            p = jnp.exp2(s - (m_new - offset))
            l = alpha * l + jnp.sum(p, axis=-1, keepdims=True)
            acc = alpha * acc + _dot_nt(p.astype(BF16), vt)
            new.append((m_new, l, acc))
        return tuple(new)

    one = (jnp.full((tq, 1), MASKED, F32), jnp.zeros((tq, 1), F32), jnp.zeros((tq, LANES), F32))
    carry = lax.fori_loop(0, qi, lambda j, c: step(j, c, False), (one, one))
    (m1, l1, a1), (m2, l2, a2) = step(qi, carry, True)
    return a1 / l1, a2 / l2


def _diff_kernel(slope_ref, qx_ref, qpos_ref, q_ref, k_ref, kx_ref, vt_ref, lam_ref, subln_ref, o_ref, *, tq,
                 lambda_init):
    hd = pl.program_id(1)
    qi = pl.program_id(2)
    lane = lax.broadcasted_iota(jnp.int32, (tq, LANES), 1)
    q_parts = _diff_q_parts(q_ref[0], lane)
    head_row = qx_ref[0]
    qx = jnp.where(lane < 3 * N_SPLIT, head_row,
                   qpos_ref[...].astype(F32) * head_row[:, SLOPE_LANE:SLOPE_LANE + 1]).astype(BF16)
    q_aug = [jnp.concatenate([part, qx], axis=1) for part in q_parts]
    lf = lam_ref[...]
    lam = (jnp.exp(jnp.sum(lf[0:1] * lf[1:2], axis=-1, keepdims=True))
           - jnp.exp(jnp.sum(lf[2:3] * lf[3:4], axis=-1, keepdims=True)) + lambda_init)
    gain = subln_ref[...] * (1.0 - lambda_init)

    def block(j, carry, key_lo=0, key_n=tq, q_lo=0, masked=False):
        ks = pl.multiple_of(j * tq + key_lo, key_n)
        k = jnp.concatenate([k_ref[0, pl.ds(ks, key_n), :], kx_ref[pl.ds(ks, key_n), :]], axis=1)
        vt = vt_ref[0, :, pl.ds(ks, key_n)]
        new = []
        for c in range(2):
            lsum, acc = carry[c]
            st = _dot_nt(k, q_aug[c][q_lo:, :])
            if masked:
                st = jnp.where(_causal_t(key_lo, key_n, q_lo, tq), st, MASKED)
            pt = jnp.exp2(st)
            part = jnp.sum(pt.reshape(key_n // SUBLANES, SUBLANES, tq - q_lo), axis=0)
            new.append((_add_cols(lsum, q_lo, part), _add_cols(acc, q_lo, _dot(vt, pt.astype(BF16)))))
        return tuple(new)

    zero = (jnp.zeros((SUBLANES, tq), F32), jnp.zeros((LANES, tq), F32))
    carry = _loop_blocks(qi, block, (zero, zero))
    for key_lo, key_n, q_lo in _diagonal_spans(tq):
        carry = block(qi, carry, key_lo, key_n, q_lo, masked=True)
    (ls1, a1), (ls2, a2) = carry
    l1 = jnp.sum(ls1, axis=0, keepdims=True)
    l2 = jnp.sum(ls2, axis=0, keepdims=True)

    ot = a1 / l1 - lam * (a2 / l2)
    ot = ot * lax.rsqrt(jnp.mean(ot * ot, axis=0, keepdims=True) + EPS)
    o_ref[0] = (ot.T * gain).astype(BF16)

    @pl.when(jnp.logical_not(jnp.min(jnp.minimum(l1, l2)) >= UNDERFLOW_GUARD))
    def _():
        o1, o2 = _diff_running_max(slope_ref[hd] * LOG2E, q_ref, k_ref, vt_ref, qi, tq)
        o_ref[0] = _rms(o1 - lam * o2, gain).astype(BF16)


SLOPE_LANE = 5 * N_SPLIT


def _alibi_tables(slopes, bound, S):
    assert all(math.log2(s) == int(math.log2(s)) for s in slopes), "ALiBi slopes must be powers of two"
    n = N_SPLIT
    log2e_terms = jnp.concatenate(_split3(jnp.full((1, 1), LOG2E, F32)), axis=1)
    s = jnp.asarray(slopes, F32).reshape(-1, 1)
    head = jnp.concatenate(
        [s * log2e_terms * float(LANES), s * log2e_terms, jnp.broadcast_to(_shift_row(bound, 0)[:, :n], (len(slopes), n)),
         jnp.zeros((len(slopes), SLOPE_LANE - 3 * n), F32), s, jnp.zeros((len(slopes), LANES - SLOPE_LANE - 1), F32)],
        axis=1)
    idx = np.arange(S)
    hi, lo = (idx // LANES)[:, None].astype(np.float32), (idx % LANES)[:, None].astype(np.float32)
    qpos = np.zeros((S, LANES), np.float32)
    qpos[:, 3 * n:4 * n] = hi * LANES
    qpos[:, 4 * n:5 * n] = lo
    kx = jnp.concatenate(
        [jnp.asarray(np.concatenate([np.repeat(hi, n, 1), np.repeat(lo, n, 1), np.ones((S, n), np.float32)], axis=1)),
         jnp.broadcast_to(-log2e_terms, (S, n)), jnp.broadcast_to(-log2e_terms, (S, n)),
         jnp.zeros((S, LANES - 5 * n), F32)], axis=1)
    return head.reshape(-1, 1, LANES), jnp.asarray(qpos, dtype=BF16), kx.astype(BF16)


def _diff_attention(slopes, bound, q, k, vt, lambdas, subln, tq, lambda_init):
    B, S, D = q.shape
    qx, qpos, kx = _alibi_tables(slopes, bound, S)
    return pl.pallas_call(
        functools.partial(_diff_kernel, tq=tq, lambda_init=lambda_init),
        grid=(B, DIFF_HEADS, S // tq),
        in_specs=[
            pl.BlockSpec(memory_space=pltpu.SMEM),
            pl.BlockSpec((1, 1, LANES), lambda b, h, i: (h, 0, 0)),
            pl.BlockSpec((tq, LANES), lambda b, h, i: (i, 0)),
            pl.BlockSpec((1, tq, LANES), lambda b, h, i: (b, i, h)),
            pl.BlockSpec((1, S, LANES), lambda b, h, i: (b, 0, h)),
            _const_spec(kx.shape),
            pl.BlockSpec((1, LANES, S), lambda b, h, i: (b, h, 0)),
            pl.BlockSpec(lambdas.shape, lambda b, h, i: (0, 0)),
            pl.BlockSpec(subln.shape, lambda b, h, i: (0, 0)),
        ],
        out_specs=pl.BlockSpec((1, tq, LANES), lambda b, h, i: (b, i, h)),
        out_shape=jax.ShapeDtypeStruct((B, S, D), BF16),
        compiler_params=_params(("parallel", "parallel", "arbitrary")),
        name="diff_attention",
    )(jnp.asarray(slopes, F32), qx, qpos, q, k, kx, vt, lambdas, subln)


def _mem_kv_kernel(mem_ref, norm_ref, w_ref, gain_ref, k_ref, v_ref):
    h = _rms(mem_ref[0], norm_ref[0]).astype(BF16)
    z = _dot(h, w_ref[0])
    width = MEM_HEADS * MEM_HEAD_DIM
    gain = gain_ref[0]
    for hd in range(MEM_HEADS):
        sl = slice(MEM_HEAD_DIM * hd, MEM_HEAD_DIM * (hd + 1))
        k_ref[0, 0, :, sl] = _rms(z[:, sl], gain).astype(BF16)
    v_ref[0, 0] = z[:, width:2 * width].astype(BF16)


def _mem_kv(mem, norms, w_kv, gains):
    B, M, D = mem.shape
    L = norms.shape[0]
    width = MEM_HEADS * MEM_HEAD_DIM
    out = pl.BlockSpec((1, 1, M, width), lambda l, b: (l, b, 0, 0))
    return pl.pallas_call(
        _mem_kv_kernel,
        grid=(L, B),
        in_specs=[
            pl.BlockSpec((1, M, D), lambda l, b: (b, 0, 0)),
            pl.BlockSpec((1, 1, D), lambda l, b: (l, 0, 0)),
            pl.BlockSpec((1, D, 2 * width), lambda l, b: (l, 0, 0)),
            pl.BlockSpec((1, 1, MEM_HEAD_DIM), lambda l, b: (l, 0, 0)),
        ],
        out_specs=[out, out],
        out_shape=[jax.ShapeDtypeStruct((L, B, M, width), BF16)] * 2,
        compiler_params=_params(("parallel", "parallel")),
        name="mem_kv",
    )(mem, norms, w_kv, gains)


def _post_kernel(*refs, n_attn):
    x_ref = refs[0]
    attn_refs = refs[1:1 + n_attn]
    wo_refs = refs[1 + n_attn:1 + 2 * n_attn]
    (mqn_ref, wq_ref, qg_ref, mk_ref, mv_ref, wmo_ref, fn_ref, wg_ref, wu_ref, wd_ref, o_ref) = refs[1 + 2 * n_attn:]
    x = x_ref[0]
    for a_ref, w_ref in zip(attn_refs, wo_refs):
        x = x + _dot(a_ref[0], w_ref[...])

    q = _dot(_rms(x, mqn_ref[...]).astype(BF16), wq_ref[...])
    heads = []
    for hd in range(MEM_HEADS):
        sl = slice(MEM_HEAD_DIM * hd, MEM_HEAD_DIM * (hd + 1))
        qh = (_rms(q[:, sl], qg_ref[...]) * (MEM_HEAD_DIM ** -0.5 * LOG2E)).astype(BF16)
        s = _dot_nt(qh, mk_ref[0, :, sl])
        p = jnp.exp2(s - jnp.max(s, axis=-1, keepdims=True))
        l = jnp.sum(p, axis=-1, keepdims=True)
        heads.append((_dot(p.astype(BF16), mv_ref[0, :, sl]) / l).astype(BF16))
    x = x + _dot(jnp.concatenate(heads, axis=-1), wmo_ref[...])

    hf = _rms(x, fn_ref[...]).astype(BF16)
    g = _dot(hf, wg_ref[...])
    u = _dot(hf, wu_ref[...])
    act = (g * jax.nn.sigmoid(g) * u).astype(BF16)
    o_ref[0] = x + _dot(act, wd_ref[...])


def _post_block(x, attns, wos, mqn, wq, qg, mk, mv, wmo, fn, wg, wu, wd, ts):
    B, S, D = x.shape
    n = len(attns)
    tok = lambda c: pl.BlockSpec((1, ts, c), lambda b, i: (b, i, 0))
    memspec = pl.BlockSpec((1,) + mk.shape[1:], lambda b, i: (b, 0, 0))
    consts_a = list(wos) + [mqn, wq, qg]
    consts_b = [wmo, fn, wg, wu, wd]
    return pl.pallas_call(
        functools.partial(_post_kernel, n_attn=n),
        grid=(B, S // ts),
        in_specs=([tok(D)] + [tok(a.shape[-1]) for a in attns] + [_const_spec(c.shape) for c in consts_a]
                  + [memspec, memspec] + [_const_spec(c.shape) for c in consts_b]),
        out_specs=tok(D),
        out_shape=jax.ShapeDtypeStruct((B, S, D), F32),
        compiler_params=_params(("parallel", "parallel")),
        name="post_block",
    )(x, *attns, *consts_a, mk, mv, *consts_b)


def _row(v):
    return v.reshape(1, -1).astype(F32)


def _shift_row(bound, first_lane):
    terms = jnp.stack(_split3(-jnp.asarray(bound, F32))).reshape(1, N_SPLIT)
    return jnp.pad(terms, ((0, 0), (first_lane, LANES - first_lane - N_SPLIT)))


def _mla_lane_perm():
    half = MLA_ROPE // 2
    lanes = np.arange(MLA_QK)
    return np.where(lanes < half, MLA_NOPE + lanes,
                    np.where(lanes < ROPE_HI_LANE, lanes - half,
                             np.where(lanes < ROPE_HI_LANE + half, lanes + half, lanes - MLA_ROPE)))


def _even_weights(w_in, w_uq, w_ukv, q_gain, k_gain):
    perm = _mla_lane_perm()
    is_nope = perm < MLA_NOPE
    pad = LANES - MLA_QK

    def head_cols(w_head, live):
        return jnp.pad(jnp.where(live[None, :], w_head[:, perm], 0.0), ((0, 0), (0, pad)))

    c_kr = EVEN_COLS[-1]
    kr_src = jnp.concatenate([jnp.zeros((w_in.shape[0], MLA_NOPE), w_in.dtype), w_in[:, c_kr:c_kr + MLA_ROPE]], axis=1)
    w1 = jnp.concatenate([w_in[:, :c_kr], head_cols(kr_src, ~is_nope)], axis=1).astype(BF16)
    wuq = jnp.concatenate(
        [head_cols(w_uq[:, MLA_QK * h:MLA_QK * (h + 1)], np.ones_like(is_nope)) for h in range(MLA_HEADS)],
        axis=1).astype(BF16)
    kv_w = MLA_NOPE + MLA_V
    rope_zeros = jnp.zeros((w_ukv.shape[0], MLA_ROPE), w_ukv.dtype)
    wuk = jnp.concatenate(
        [head_cols(jnp.concatenate([w_ukv[:, kv_w * h:kv_w * h + MLA_NOPE], rope_zeros], axis=1), is_nope)
         for h in range(MLA_HEADS)], axis=1).astype(BF16)
    wuv = jnp.concatenate(
        [w_ukv[:, kv_w * h + MLA_NOPE:kv_w * (h + 1)] for h in range(MLA_HEADS)], axis=1).astype(BF16).T
    gmq = _row(jnp.pad(q_gain[perm], (0, pad)))
    gmk = _row(jnp.pad(k_gain[perm], (0, pad)))
    return w1, wuq, wuk, wuv, gmq, gmk


def _rope_freqs():
    half = MLA_ROPE // 2
    freqs = (np.float32(ROPE_THETA) ** (-np.arange(half, dtype=np.float32) / np.float32(half))).astype(np.float32)
    return jnp.asarray(freqs.reshape(half, 1))


def _ones_blocks(n):
    g = np.arange(n) // HEAD_DIM
    return jnp.asarray((g[:, None] == g[None, :]).astype(np.float32), dtype=BF16)


def _tile_sizes(S):
    token_tile = min(2 * MXU_WIDTH, S)
    return token_tile, token_tile, min(2 * MXU_WIDTH, S - SWA_WINDOW), min(4 * MXU_WIDTH, S)


def kernel(x, mem, positions, mix_norm, ev_w_in, ev_swa_q_gain, ev_swa_k_gain, ev_sinks, ev_q_latent_norm,
           ev_kv_latent_norm, ev_w_uq, ev_w_ukv, ev_mla_q_gain, ev_mla_k_gain, ev_w_out, od_w_qkv, od_q_gain,
           od_k_gain, od_lambda, od_subln, od_w_out, mem_q_norm, mem_kv_norm, mem_w_q, mem_w_kv, mem_q_gain,
           mem_k_gain, mem_w_out, ffn_norm, ffn_w_gate, ffn_w_up, ffn_w_down):
    B, S, D = x.shape
    depth = mix_norm.shape[0]
    ts_proj, ts_post, tq_swa, tq_flash = _tile_sizes(S)

    ones = _ones_blocks(MXU_WIDTH)
    pos3 = positions.reshape(B, 1, S)
    inv = _rope_freqs()
    mem_k, mem_v = _mem_kv(mem, mem_kv_norm.reshape(depth, 1, D), mem_w_kv.astype(BF16),
                           mem_k_gain.reshape(depth, 1, MEM_HEAD_DIM))
    diff_slopes = [2.0 ** (-8.0 * (i + 1) / DIFF_HEADS) for i in range(DIFF_HEADS)]

    for l in range(depth):
        if l % 2 == 0:
            e = l // 2
            w1, wuq, wuk, wuv, gmq, gmk = _even_weights(ev_w_in[e], ev_w_uq[e], ev_w_ukv[e], ev_mla_q_gain[e],
                                                        ev_mla_k_gain[e])
            qa, ka, va, qm, km, vm = _even_proj(
                x, pos3, _row(mix_norm[l]), w1, ones, _row(jnp.tile(ev_swa_q_gain[e], SWA_HEADS)),
                _row(jnp.tile(ev_swa_k_gain[e], SWA_KV_HEADS)), _row(ev_q_latent_norm[e]),
                _row(ev_kv_latent_norm[e]), wuq, wuk, wuv, gmq, gmk,
                _shift_row(_score_bound(ev_mla_q_gain[e], ev_mla_k_gain[e], MLA_QK, MLA_QK ** -0.5 * LOG2E), MLA_QK),
                inv, ts_proj)
            swa_bound = _score_bound(ev_swa_q_gain[e], ev_swa_k_gain[e], HEAD_DIM, HEAD_DIM ** -0.5 * LOG2E)
            swa_scalars = jnp.concatenate([ev_sinks[e].astype(F32), swa_bound.reshape(1).astype(F32)])
            out_a = _swa_attention(swa_scalars, qa, ka, va, tq_swa)
            out_b = _mla_attention(qm, km, vm, tq_flash)
            wo = ev_w_out[e].astype(BF16)
            attns = [out_a, out_b]
            wos = [wo[:SWA_Q_W], wo[SWA_Q_W:]]
        else:
            o = l // 2
            lambda_init = 0.8 - 0.6 * math.exp(-0.3 * l)
            n_grp = DIFF_HEADS * 2
            w_qkv = od_w_qkv[o].astype(BF16)
            q, k, vt = _odd_proj(x, _row(mix_norm[l]), w_qkv[:, :2 * D], w_qkv[:, 2 * D:].T, ones,
                                 _row(jnp.tile(od_q_gain[o], n_grp // 2)), _row(jnp.tile(od_k_gain[o], n_grp // 2)),
                                 ts_proj)
            bound = _score_bound(od_q_gain[o], od_k_gain[o], DIFF_DIM, DIFF_DIM ** -0.5 * LOG2E)
            out_d = _diff_attention(diff_slopes, bound, q, k, vt, od_lambda[o].astype(F32), _row(od_subln[o]),
                                    tq_flash, lambda_init)
            attns = [out_d]
            wos = [od_w_out[o].astype(BF16)]
        x = _post_block(x, attns, wos, _row(mem_q_norm[l]), mem_w_q[l].astype(BF16), _row(mem_q_gain[l]),
                        mem_k[l], mem_v[l], mem_w_out[l].astype(BF16), _row(ffn_norm[l]),
                        ffn_w_gate[l].astype(BF16), ffn_w_up[l].astype(BF16), ffn_w_down[l].astype(BF16), ts_post)
    return x
```

```python
import functools
import math

import numpy as np
import jax
import jax.numpy as jnp
from jax import lax
from jax.experimental import pallas as pl
from jax.experimental.pallas import tpu as pltpu

F32 = jnp.float32
BF16 = jnp.bfloat16

EPS = 1e-6
MASKED = -1e30
LOG2E = 1.4426950408889634
ROPE_THETA = 10000.0

LANES = 128
SUBLANES = 8
MXU_WIDTH = 256
HEAD_DIM = 64
SWA_HEADS = 8
SWA_KV_HEADS = 2
SWA_WINDOW = 128
MLA_HEADS = 8
MLA_NOPE = 64
MLA_ROPE = 32
MLA_QK = MLA_NOPE + MLA_ROPE
MLA_V = 64
ROPE_HI_LANE = 64
DIFF_HEADS = 8
DIFF_DIM = 64
MEM_HEADS = 4
MEM_HEAD_DIM = 128
MLA_Q_RANK = 256
MLA_KV_RANK = 128
SWA_Q_W = SWA_HEADS * HEAD_DIM
SWA_KV_W = SWA_KV_HEADS * HEAD_DIM
EVEN_COLS = tuple(np.cumsum([SWA_Q_W, SWA_KV_W, SWA_KV_W, MLA_Q_RANK, MLA_KV_RANK]).tolist())

VMEM_LIMIT = 56 * 1024 * 1024

UNDERFLOW_GUARD = 2.0 ** -80
N_SPLIT = 3


def _params(sem):
    return pltpu.CompilerParams(dimension_semantics=sem, vmem_limit_bytes=VMEM_LIMIT)


def _const_spec(shape):
    nd = len(shape)
    return pl.BlockSpec(shape, lambda *_: (0,) * nd, pipeline_mode=pl.Buffered(1))


def _rms(x, g):
    ms = jnp.mean(x * x, axis=-1, keepdims=True)
    return x * lax.rsqrt(ms + EPS) * g


def _dot(a, b):
    return jnp.dot(a, b, preferred_element_type=F32)


def _dot_nt(a, b):
    return lax.dot_general(a, b, (((1,), (1,)), ((), ())), preferred_element_type=F32)


def _group_sumsq(x, ones_blk):
    x2 = x * x
    hi = x2.astype(BF16)
    lo = (x2 - hi.astype(F32)).astype(BF16)
    return _dot(hi, ones_blk) + _dot(lo, ones_blk)


def _group_norm64(x, ones_blk, gain):
    chunk = min(ones_blk.shape[0], x.shape[1])
    ssq = jnp.concatenate([_group_sumsq(x[:, c:c + chunk], ones_blk[0:chunk, 0:chunk])
                           for c in range(0, x.shape[1], chunk)], axis=1)
    return x * lax.rsqrt(ssq * (1.0 / HEAD_DIM) + EPS) * gain


def _even_proj_kernel(x_ref, pos_ref, gmix_ref, w1_ref, ones_ref, gq_ref, gk_ref, qln_ref, kvln_ref,
                      wuq_ref, wuk_ref, wuv_ref, gmq_ref, gmk_ref, qshift_ref, inv_ref,
                      qa_ref, ka_ref, va_ref, qm_ref, km_ref, vm_ref):
    x = x_ref[0]
    h = _rms(x, gmix_ref[...]).astype(BF16)
    z = _dot(h, w1_ref[...])
    c_ka, c_va, c_cq, c_ckv, c_kr = EVEN_COLS
    ones = ones_ref[...]
    qa = _group_norm64(z[:, 0:c_ka], ones, gq_ref[...]) * (HEAD_DIM ** -0.5 * LOG2E)
    qa_ref[0] = qa.astype(BF16)
    ka = _group_norm64(z[:, c_ka:c_va], ones[0:SWA_KV_W, 0:SWA_KV_W], gk_ref[...])
    ka_ref[0] = ka.astype(BF16)
    va_ref[0] = z[:, c_va:c_cq].astype(BF16)

    cqn = _rms(z[:, c_cq:c_ckv], qln_ref[...]).astype(BF16)
    ckvn = _rms(z[:, c_ckv:c_kr], kvln_ref[...]).astype(BF16)
    kr = z[:, c_kr:c_kr + LANES]
    qf = _dot(cqn, wuq_ref[...])
    kn = _dot(ckvn, wuk_ref[...])
    vm_ref[0] = _dot_nt(wuv_ref[...], ckvn).astype(BF16)

    half = MLA_ROPE // 2
    ang = inv_ref[...] * pos_ref[0].astype(F32)
    pad = jnp.zeros((LANES - half, ang.shape[1]), F32)
    cos = jnp.concatenate([jnp.cos(ang), pad], axis=0).T
    sin = jnp.concatenate([jnp.sin(ang), pad], axis=0).T
    lane = lax.broadcasted_iota(jnp.int32, cos.shape, 1)
    in_lo = lane < half
    in_hi = (lane >= ROPE_HI_LANE) & (lane < ROPE_HI_LANE + half)
    cos_t = jnp.where(in_lo, cos, jnp.where(in_hi, pltpu.roll(cos, ROPE_HI_LANE, 1), 1.0))
    sin_t = jnp.where(in_lo, -sin, jnp.where(in_hi, pltpu.roll(sin, ROPE_HI_LANE, 1), 0.0))

    pair = 2 * LANES
    gi = lax.broadcasted_iota(jnp.int32, (pair, pair), 0) // LANES
    gj = lax.broadcasted_iota(jnp.int32, (pair, pair), 1) // LANES
    head_ones = (gi == gj).astype(BF16)

    def norm_rope(xp, gain):
        r = lax.rsqrt(_group_sumsq(xp, head_ones) * (1.0 / MLA_QK) + EPS)
        xn = xp * r * gain
        halves = []
        for t in range(2):
            xh = xn[:, LANES * t:LANES * (t + 1)]
            halves.append(xh * cos_t + pltpu.roll(xh, ROPE_HI_LANE, 1) * sin_t)
        return halves

    gmq = jnp.concatenate([gmq_ref[...]] * 2, axis=1)
    gmk = jnp.concatenate([gmk_ref[...]] * 2, axis=1)
    kr2 = jnp.concatenate([kr, kr], axis=1)
    shift_lanes = (lane >= MLA_QK) & (lane < MLA_QK + N_SPLIT)
    qshift = qshift_ref[...]
    for hp in range(MLA_HEADS // 2):
        sl = slice(pair * hp, pair * (hp + 1))
        q_heads = norm_rope(qf[:, sl], gmq)
        k_heads = norm_rope(kn[:, sl] + kr2, gmk)
        for t in range(2):
            hl = slice(pair * hp + LANES * t, pair * hp + LANES * (t + 1))
            qm_ref[0, :, hl] = jnp.where(shift_lanes, qshift, q_heads[t] * (MLA_QK ** -0.5 * LOG2E)).astype(BF16)
            km_ref[0, :, hl] = jnp.where(shift_lanes, 1.0, k_heads[t]).astype(BF16)


def _even_proj(x, pos3, gmix, w1, ones, gq, gk, qln, kvln, wuq, wuk, wuv, gmq, gmk, qshift, inv, ts):
    B, S, D = x.shape
    grid = (B, S // ts)
    tok = lambda c: pl.BlockSpec((1, ts, c), lambda b, i: (b, i, 0))
    consts = [gmix, w1, ones, gq, gk, qln, kvln, wuq, wuk, wuv, gmq, gmk, qshift, inv]
    out_cols = [SWA_Q_W, SWA_KV_W, SWA_KV_W, MLA_HEADS * LANES, MLA_HEADS * LANES]
    v_rows = MLA_HEADS * MLA_V
    return pl.pallas_call(
        _even_proj_kernel,
        grid=grid,
        in_specs=([tok(D), pl.BlockSpec((1, 1, ts), lambda b, i: (b, 0, i))]
                  + [_const_spec(c.shape) for c in consts]),
        out_specs=[tok(c) for c in out_cols] + [pl.BlockSpec((1, v_rows, ts), lambda b, i: (b, 0, i))],
        out_shape=([jax.ShapeDtypeStruct((B, S, c), BF16) for c in out_cols]
                   + [jax.ShapeDtypeStruct((B, v_rows, S), BF16)]),
        compiler_params=_params(("parallel", "parallel")),
        name="even_proj",
    )(x, pos3, *consts)


def _swa_place_q(q_ref, hd, lane):
    hk = hd // (SWA_HEADS // SWA_KV_HEADS)
    qq = q_ref[0, :, LANES * (hd // 2):LANES * (hd // 2 + 1)].astype(F32)
    xq = qq if hd % 2 == hk else pltpu.roll(qq, HEAD_DIM, 1)
    return jnp.where((lane >= HEAD_DIM * hk) & (lane < HEAD_DIM * (hk + 1)), xq, 0.0).astype(BF16)


def _swa_place_out(o_even, o_odd, cb, lane):
    placed = []
    for e, o in enumerate((o_even, o_odd)):
        hk = (2 * cb + e) // (SWA_HEADS // SWA_KV_HEADS)
        placed.append(o if e == hk else pltpu.roll(o, HEAD_DIM, 1))
    return jnp.where(lane < HEAD_DIM, placed[0], placed[1]).astype(BF16)


def _swa_running_max(sink_ref, q_ref, k_ref, v_ref, o_ref, qi, tq):
    win = tq + SWA_WINDOW
    wstart = pl.multiple_of(jnp.maximum(qi * tq - SWA_WINDOW, 0), SWA_WINDOW)
    kw = k_ref[0, pl.ds(wstart, win), :]
    vw = v_ref[0, pl.ds(wstart, win), :]
    row = lax.broadcasted_iota(jnp.int32, (tq, win), 0)
    col = lax.broadcasted_iota(jnp.int32, (tq, win), 1)
    dist = (qi * tq + row) - (wstart + col)
    valid = (dist >= 0) & (dist < SWA_WINDOW)
    distf = dist.astype(F32)
    lane = lax.broadcasted_iota(jnp.int32, (tq, LANES), 1)
    for cb in range(SWA_HEADS // 2):
        outs = []
        for e in range(2):
            hd = 2 * cb + e
            s = _dot_nt(_swa_place_q(q_ref, hd, lane), kw) - (2.0 ** (-8.0 * (hd + 1) / SWA_HEADS) * LOG2E) * distf
            s = jnp.where(valid, s, MASKED)
            sink = sink_ref[hd] * LOG2E
            m = jnp.maximum(jnp.max(s, axis=-1, keepdims=True), sink)
            p = jnp.exp2(s - m)
            l = jnp.sum(p, axis=-1, keepdims=True) + jnp.exp2(sink - m)
            outs.append(_dot(p.astype(BF16), vw) / l)
        o_ref[0, :, LANES * cb:LANES * (cb + 1)] = _swa_place_out(outs[0], outs[1], cb, lane)


def _swa_kernel(sink_ref, q_ref, k_ref, v_ref, o_ref, *, tq):
    qi = pl.program_id(1)
    sub = SWA_WINDOW
    win = 2 * sub
    n_sub = tq // sub
    bound = sink_ref[SWA_HEADS]
    lane_q = lax.broadcasted_iota(jnp.int32, (tq, LANES), 1)
    lane = lax.broadcasted_iota(jnp.int32, (sub, LANES), 1)
    row = lax.broadcasted_iota(jnp.int32, (sub, win), 0)
    col = lax.broadcasted_iota(jnp.int32, (sub, win), 1)

    def bias_mask(first_key_offset, slope):
        dist = row + first_key_offset - col
        return jnp.where((dist >= 0) & (dist < SWA_WINDOW), -slope * dist.astype(F32) - bound, MASKED)

    ones_blk = jnp.ones((win, LANES), BF16)
    windows = []
    for r in range(n_sub):
        wstart = qi * tq + sub * (r - 1)
        wstart = pl.multiple_of(jnp.maximum(wstart, 0) if r == 0 else wstart, sub)
        windows.append((k_ref[0, pl.ds(wstart, win), :],
                        jnp.concatenate([v_ref[0, pl.ds(wstart, win), :], ones_blk], axis=1)))

    outs = [[None] * SWA_HEADS for _ in range(n_sub)]
    min_sum = None
    for hd in range(SWA_HEADS):
        slope = 2.0 ** (-8.0 * (hd + 1) / SWA_HEADS) * LOG2E
        bm = bias_mask(sub, slope)
        bm_first = jnp.where(qi == 0, bias_mask(0, slope), bm)
        xq = _swa_place_q(q_ref, hd, lane_q)
        sink_term = jnp.exp2(sink_ref[hd] * LOG2E - bound)
        for r in range(n_sub):
            kw, v_ones = windows[r]
            s = _dot_nt(xq[sub * r:sub * (r + 1)], kw) + (bm_first if r == 0 else bm)
            ov = _dot(jnp.exp2(s).astype(BF16), v_ones)
            l = ov[:, LANES:] + sink_term
            outs[r][hd] = ov[:, :LANES] / l
            min_sum = l if min_sum is None else jnp.minimum(min_sum, l)
    for r in range(n_sub):
        for cb in range(SWA_HEADS // 2):
            o_ref[0, sub * r:sub * (r + 1), LANES * cb:LANES * (cb + 1)] = _swa_place_out(
                outs[r][2 * cb], outs[r][2 * cb + 1], cb, lane)

    @pl.when(jnp.logical_not(jnp.min(min_sum) >= UNDERFLOW_GUARD))
    def _():
        _swa_running_max(sink_ref, q_ref, k_ref, v_ref, o_ref, qi, tq)


def _swa_attention(sinks, qa, ka, va, tq):
    B, S, _ = qa.shape
    return pl.pallas_call(
        functools.partial(_swa_kernel, tq=tq),
        grid=(B, S // tq),
        in_specs=[
            pl.BlockSpec(memory_space=pltpu.SMEM),
            pl.BlockSpec((1, tq, SWA_Q_W), lambda b, i: (b, i, 0)),
            pl.BlockSpec((1, S, SWA_KV_W), lambda b, i: (b, 0, 0)),
            pl.BlockSpec((1, S, SWA_KV_W), lambda b, i: (b, 0, 0)),
        ],
        out_specs=pl.BlockSpec((1, tq, SWA_Q_W), lambda b, i: (b, i, 0)),
        out_shape=jax.ShapeDtypeStruct((B, S, SWA_Q_W), BF16),
        compiler_params=_params(("parallel", "arbitrary")),
        name="swa_attention",
    )(sinks, qa, ka, va)


def _split3(x):
    hi = x.astype(BF16).astype(F32)
    r = x - hi
    mid = r.astype(BF16).astype(F32)
    lo = (r - mid).astype(BF16).astype(F32)
    return hi, mid, lo


def _score_bound(q_gain, k_gain, dim, scale):
    return dim * scale * jnp.max(jnp.abs(q_gain)) * jnp.max(jnp.abs(k_gain))


BLOCKS_PER_TRIP = 4


def _loop_blocks(n, step, init):
    def trip(t, c):
        for u in range(BLOCKS_PER_TRIP):
            c = step(BLOCKS_PER_TRIP * t + u, c)
        return c

    full = n // BLOCKS_PER_TRIP
    carry = lax.fori_loop(0, full, trip, init)
    return lax.fori_loop(full * BLOCKS_PER_TRIP, n, step, carry)


def _add_cols(acc, lo, part):
    if lo == 0:
        return acc + part
    return acc + jnp.concatenate([jnp.zeros((acc.shape[0], lo), acc.dtype), part], axis=1)


def _diagonal_spans(tq):
    h = tq // 2
    return ((0, h, 0), (h, h, h)) if h % MXU_WIDTH == 0 else ((0, tq, 0),)


def _causal_t(key_lo, key_n, q_lo, tq):
    shape = (key_n, tq - q_lo)
    return (key_lo + lax.broadcasted_iota(jnp.int32, shape, 0)) <= (q_lo + lax.broadcasted_iota(jnp.int32, shape, 1))


def _causal_mask(tq):
    row = lax.broadcasted_iota(jnp.int32, (tq, tq), 0)
    col = lax.broadcasted_iota(jnp.int32, (tq, tq), 1)
    return row >= col


def _mla_running_max(q_ref, k_ref, vt_ref, e, qi, tq):
    causal = _causal_mask(tq)
    q = q_ref[0, :, LANES * e:LANES * (e + 1)]

    def step(j, carry, masked):
        m, l, acc = carry
        ks = pl.multiple_of(j * tq, tq)
        k = k_ref[0, pl.ds(ks, tq), LANES * e:LANES * (e + 1)]
        vt = vt_ref[0, :, pl.ds(ks, tq)]
        s = _dot_nt(q, k)
        if masked:
            s = jnp.where(causal, s, MASKED)
        m_new = jnp.maximum(m, jnp.max(s, axis=-1, keepdims=True))
        alpha = jnp.exp2(m - m_new)
        p = jnp.exp2(s - m_new)
        l = alpha * l + jnp.sum(p, axis=-1, keepdims=True)
        acc = alpha * acc + _dot_nt(p.astype(BF16), vt)
        return m_new, l, acc

    init = (jnp.full((tq, 1), MASKED, F32), jnp.zeros((tq, 1), F32), jnp.zeros((tq, LANES), F32))
    carry = lax.fori_loop(0, qi, lambda j, c: step(j, c, False), init)
    m, l, acc = step(qi, carry, True)
    return acc / l


def _mla_kernel(q_ref, k_ref, vt_ref, o_ref, *, tq):
    qi = pl.program_id(2)
    lane = lax.broadcasted_iota(jnp.int32, (tq, LANES), 1)
    q_aug = [q_ref[0, :, LANES * e:LANES * (e + 1)] for e in range(2)]

    def block(j, carry, key_lo=0, key_n=tq, q_lo=0, masked=False):
        ks = pl.multiple_of(j * tq + key_lo, key_n)
        new = []
        for e in range(2):
            lsum, acc = carry[e]
            st = _dot_nt(k_ref[0, pl.ds(ks, key_n), LANES * e:LANES * (e + 1)], q_aug[e][q_lo:, :])
            if masked:
                st = jnp.where(_causal_t(key_lo, key_n, q_lo, tq), st, MASKED)
            pt = jnp.exp2(st)
            part = jnp.sum(pt.reshape(key_n // SUBLANES, SUBLANES, tq - q_lo), axis=0)
            pv = _dot(vt_ref[0, MLA_V * e:MLA_V * (e + 1), pl.ds(ks, key_n)], pt.astype(BF16))
            new.append((_add_cols(lsum, q_lo, part), _add_cols(acc, q_lo, pv)))
        return tuple(new)

    zero = (jnp.zeros((SUBLANES, tq), F32), jnp.zeros((MLA_V, tq), F32))
    carry = _loop_blocks(qi, block, (zero, zero))
    for key_lo, key_n, q_lo in _diagonal_spans(tq):
        carry = block(qi, carry, key_lo, key_n, q_lo, masked=True)
    sums = [jnp.sum(lsum, axis=0, keepdims=True) for lsum, _ in carry]
    outs = [acc / l for (_, acc), l in zip(carry, sums)]
    o_ref[0] = jnp.concatenate(outs, axis=0).T.astype(BF16)

    @pl.when(jnp.logical_not(jnp.min(jnp.minimum(sums[0], sums[1])) >= UNDERFLOW_GUARD))
    def _():
        slow = [_mla_running_max(q_ref, k_ref, vt_ref, e, qi, tq) for e in range(2)]
        o_ref[0] = jnp.where(lane < MLA_V, slow[0], slow[1]).astype(BF16)


def _mla_attention(qm, km, vm, tq):
    B, S, _ = qm.shape
    return pl.pallas_call(
        functools.partial(_mla_kernel, tq=tq),
        grid=(B, MLA_HEADS // 2, S // tq),
        in_specs=[
            pl.BlockSpec((1, tq, 2 * LANES), lambda b, h, i: (b, i, h)),
            pl.BlockSpec((1, S, 2 * LANES), lambda b, h, i: (b, 0, h)),
            pl.BlockSpec((1, 2 * MLA_V, S), lambda b, h, i: (b, h, 0)),
        ],
        out_specs=pl.BlockSpec((1, tq, LANES), lambda b, h, i: (b, i, h)),
        out_shape=jax.ShapeDtypeStruct((B, S, MLA_HEADS * MLA_V), BF16),
        compiler_params=_params(("parallel", "parallel", "arbitrary")),
        name="mla_attention",
    )(qm, km, vm)


def _odd_proj_kernel(x_ref, gmix_ref, w_ref, wvt_ref, ones_ref, gq_ref, gk_ref, q_ref, k_ref, vt_ref):
    x = x_ref[0]
    h = _rms(x, gmix_ref[...]).astype(BF16)
    z = _dot(h, w_ref[...])
    ones = ones_ref[...]
    width = DIFF_HEADS * 2 * DIFF_DIM
    piece = gq_ref.shape[1]
    for c in range(0, width, piece):
        q = _group_norm64(z[:, c:c + piece], ones, gq_ref[...]) * (DIFF_DIM ** -0.5 * LOG2E)
        q_ref[0, :, c:c + piece] = q.astype(BF16)
        k = _group_norm64(z[:, width + c:width + c + piece], ones, gk_ref[...])
        k_ref[0, :, c:c + piece] = k.astype(BF16)
    vt_ref[0] = _dot_nt(wvt_ref[...], h).astype(BF16)


def _odd_proj(x, gmix, w_qk, w_vt, ones, gq, gk, ts):
    B, S, D = x.shape
    tok = pl.BlockSpec((1, ts, D), lambda b, i: (b, i, 0))
    consts = [gmix, w_qk, w_vt, ones, gq, gk]
    return pl.pallas_call(
        _odd_proj_kernel,
        grid=(B, S // ts),
        in_specs=[tok] + [_const_spec(c.shape) for c in consts],
        out_specs=[tok, tok, pl.BlockSpec((1, D, ts), lambda b, i: (b, 0, i))],
        out_shape=[jax.ShapeDtypeStruct((B, S, D), BF16)] * 2 + [jax.ShapeDtypeStruct((B, D, S), BF16)],
        compiler_params=_params(("parallel", "parallel")),
        name="odd_proj",
    )(x, *consts)


def _diff_q_parts(q, lane):
    zero = jnp.zeros_like(q)
    return jnp.where(lane < DIFF_DIM, q, zero), jnp.where(lane >= DIFF_DIM, q, zero)


def _diff_running_max(slope, q_ref, k_ref, vt_ref, qi, tq):
    row = lax.broadcasted_iota(jnp.int32, (tq, tq), 0)
    col = lax.broadcasted_iota(jnp.int32, (tq, tq), 1)
    causal = row >= col
    bias = slope * (col - row).astype(F32)
    lane = lax.broadcasted_iota(jnp.int32, (tq, LANES), 1)
    q_parts = _diff_q_parts(q_ref[0], lane)

    def step(j, carry, masked):
        ks = pl.multiple_of(j * tq, tq)
        k = k_ref[0, pl.ds(ks, tq), :]
        vt = vt_ref[0, :, pl.ds(ks, tq)]
        offset = slope * ((j - qi) * tq).astype(F32)
        new = []
        for c in range(2):
            m, l, acc = carry[c]
            s = _dot_nt(q_parts[c], k) + bias
            if masked:
                s = jnp.where(causal, s, MASKED)
            m_new = jnp.maximum(m, jnp.max(s, axis=-1, keepdims=True) + offset)
            alpha = jnp.exp2(m - m_new)
            p = jnp.exp2(s - (m_new - offset))
            l = alpha * l + jnp.sum(p, axis=-1, keepdims=True)
            acc = alpha * acc + _dot_nt(p.astype(BF16), vt)
            new.append((m_new, l, acc))
        return tuple(new)

    one = (jnp.full((tq, 1), MASKED, F32), jnp.zeros((tq, 1), F32), jnp.zeros((tq, LANES), F32))
    carry = lax.fori_loop(0, qi, lambda j, c: step(j, c, False), (one, one))
    (m1, l1, a1), (m2, l2, a2) = step(qi, carry, True)
    return a1 / l1, a2 / l2


def _diff_kernel(slope_ref, qx_ref, qpos_ref, q_ref, k_ref, kx_ref, vt_ref, lam_ref, subln_ref, o_ref, *, tq,
                 lambda_init):
    hd = pl.program_id(1)
    qi = pl.program_id(2)
    lane = lax.broadcasted_iota(jnp.int32, (tq, LANES), 1)
    q_parts = _diff_q_parts(q_ref[0], lane)
    head_row = qx_ref[0]
    qx = jnp.where(lane < 3 * N_SPLIT, head_row,
                   qpos_ref[...].astype(F32) * head_row[:, SLOPE_LANE:SLOPE_LANE + 1]).astype(BF16)
    q_aug = [jnp.concatenate([part, qx], axis=1) for part in q_parts]
    lf = lam_ref[...]
    lam = (jnp.exp(jnp.sum(lf[0:1] * lf[1:2], axis=-1, keepdims=True))
           - jnp.exp(jnp.sum(lf[2:3] * lf[3:4], axis=-1, keepdims=True)) + lambda_init)
    gain = subln_ref[...] * (1.0 - lambda_init)

    def block(j, carry, key_lo=0, key_n=tq, q_lo=0, masked=False):
        ks = pl.multiple_of(j * tq + key_lo, key_n)
        k = jnp.concatenate([k_ref[0, pl.ds(ks, key_n), :], kx_ref[pl.ds(ks, key_n), :]], axis=1)
        vt = vt_ref[0, :, pl.ds(ks, key_n)]
        n = tq - q_lo
        q_both = jnp.concatenate([q_aug[0][q_lo:, :], q_aug[1][q_lo:, :]], axis=0)
        st = _dot_nt(k, q_both)
        if masked:
            causal = _causal_t(key_lo, key_n, q_lo, tq)
            st = jnp.where(jnp.concatenate([causal, causal], axis=1), st, MASKED)
        pt = jnp.exp2(st)
        part = jnp.sum(pt.reshape(key_n // SUBLANES, SUBLANES, 2 * n), axis=0)
        pv = _dot(vt, pt.astype(BF16))
        return tuple((_add_cols(carry[c][0], q_lo, part[:, n * c:n * (c + 1)]),
                      _add_cols(carry[c][1], q_lo, pv[:, n * c:n * (c + 1)])) for c in range(2))

    zero = (jnp.zeros((SUBLANES, tq), F32), jnp.zeros((LANES, tq), F32))
    carry = _loop_blocks(qi, block, (zero, zero))
    for key_lo, key_n, q_lo in _diagonal_spans(tq):
        carry = block(qi, carry, key_lo, key_n, q_lo, masked=True)
    (ls1, a1), (ls2, a2) = carry
    l1 = jnp.sum(ls1, axis=0, keepdims=True)
    l2 = jnp.sum(ls2, axis=0, keepdims=True)

    ot = a1 / l1 - lam * (a2 / l2)
    ot = ot * lax.rsqrt(jnp.mean(ot * ot, axis=0, keepdims=True) + EPS)
    o_ref[0] = (ot.T * gain).astype(BF16)

    @pl.when(jnp.logical_not(jnp.min(jnp.minimum(l1, l2)) >= UNDERFLOW_GUARD))
    def _():
        o1, o2 = _diff_running_max(slope_ref[hd] * LOG2E, q_ref, k_ref, vt_ref, qi, tq)
        o_ref[0] = _rms(o1 - lam * o2, gain).astype(BF16)


SLOPE_LANE = 5 * N_SPLIT


def _alibi_tables(slopes, bound, S):
    assert all(math.log2(s) == int(math.log2(s)) for s in slopes), "ALiBi slopes must be powers of two"
    n = N_SPLIT
    log2e_terms = jnp.concatenate(_split3(jnp.full((1, 1), LOG2E, F32)), axis=1)
    s = jnp.asarray(slopes, F32).reshape(-1, 1)
    head = jnp.concatenate(
        [s * log2e_terms * float(LANES), s * log2e_terms, jnp.broadcast_to(_shift_row(bound, 0)[:, :n], (len(slopes), n)),
         jnp.zeros((len(slopes), SLOPE_LANE - 3 * n), F32), s, jnp.zeros((len(slopes), LANES - SLOPE_LANE - 1), F32)],
        axis=1)
    idx = np.arange(S)
    hi, lo = (idx // LANES)[:, None].astype(np.float32), (idx % LANES)[:, None].astype(np.float32)
    qpos = np.zeros((S, LANES), np.float32)
    qpos[:, 3 * n:4 * n] = hi * LANES
    qpos[:, 4 * n:5 * n] = lo
    kx = jnp.concatenate(
        [jnp.asarray(np.concatenate([np.repeat(hi, n, 1), np.repeat(lo, n, 1), np.ones((S, n), np.float32)], axis=1)),
         jnp.broadcast_to(-log2e_terms, (S, n)), jnp.broadcast_to(-log2e_terms, (S, n)),
         jnp.zeros((S, LANES - 5 * n), F32)], axis=1)
    return head.reshape(-1, 1, LANES), jnp.asarray(qpos, dtype=BF16), kx.astype(BF16)


def _diff_attention(slopes, bound, q, k, vt, lambdas, subln, tq, lambda_init):
    B, S, D = q.shape
    qx, qpos, kx = _alibi_tables(slopes, bound, S)
    return pl.pallas_call(
        functools.partial(_diff_kernel, tq=tq, lambda_init=lambda_init),
        grid=(B, DIFF_HEADS, S // tq),
        in_specs=[
            pl.BlockSpec(memory_space=pltpu.SMEM),
            pl.BlockSpec((1, 1, LANES), lambda b, h, i: (h, 0, 0)),
            pl.BlockSpec((tq, LANES), lambda b, h, i: (i, 0)),
            pl.BlockSpec((1, tq, LANES), lambda b, h, i: (b, i, h)),
            pl.BlockSpec((1, S, LANES), lambda b, h, i: (b, 0, h)),
            _const_spec(kx.shape),
            pl.BlockSpec((1, LANES, S), lambda b, h, i: (b, h, 0)),
            pl.BlockSpec(lambdas.shape, lambda b, h, i: (0, 0)),
            pl.BlockSpec(subln.shape, lambda b, h, i: (0, 0)),
        ],
        out_specs=pl.BlockSpec((1, tq, LANES), lambda b, h, i: (b, i, h)),
        out_shape=jax.ShapeDtypeStruct((B, S, D), BF16),
        compiler_params=_params(("parallel", "parallel", "arbitrary")),
        name="diff_attention",
    )(jnp.asarray(slopes, F32), qx, qpos, q, k, kx, vt, lambdas, subln)


def _mem_kv_kernel(mem_ref, norm_ref, w_ref, gain_ref, k_ref, v_ref):
    h = _rms(mem_ref[0], norm_ref[0]).astype(BF16)
    z = _dot(h, w_ref[0])
    width = MEM_HEADS * MEM_HEAD_DIM
    gain = gain_ref[0]
    for hd in range(MEM_HEADS):
        sl = slice(MEM_HEAD_DIM * hd, MEM_HEAD_DIM * (hd + 1))
        k_ref[0, 0, :, sl] = _rms(z[:, sl], gain).astype(BF16)
    v_ref[0, 0] = z[:, width:2 * width].astype(BF16)


def _mem_kv(mem, norms, w_kv, gains):
    B, M, D = mem.shape
    L = norms.shape[0]
    width = MEM_HEADS * MEM_HEAD_DIM
    out = pl.BlockSpec((1, 1, M, width), lambda l, b: (l, b, 0, 0))
    return pl.pallas_call(
        _mem_kv_kernel,
        grid=(L, B),
        in_specs=[
            pl.BlockSpec((1, M, D), lambda l, b: (b, 0, 0)),
            pl.BlockSpec((1, 1, D), lambda l, b: (l, 0, 0)),
            pl.BlockSpec((1, D, 2 * width), lambda l, b: (l, 0, 0)),
            pl.BlockSpec((1, 1, MEM_HEAD_DIM), lambda l, b: (l, 0, 0)),
        ],
        out_specs=[out, out],
        out_shape=[jax.ShapeDtypeStruct((L, B, M, width), BF16)] * 2,
        compiler_params=_params(("parallel", "parallel")),
        name="mem_kv",
    )(mem, norms, w_kv, gains)


def _post_kernel(*refs, n_attn):
    x_ref = refs[0]
    attn_refs = refs[1:1 + n_attn]
    wo_refs = refs[1 + n_attn:1 + 2 * n_attn]
    (mqn_ref, wq_ref, qg_ref, mk_ref, mv_ref, wmo_ref, fn_ref, wg_ref, wu_ref, wd_ref, o_ref) = refs[1 + 2 * n_attn:]
    x = x_ref[0]
    for a_ref, w_ref in zip(attn_refs, wo_refs):
        x = x + _dot(a_ref[0], w_ref[...])

    q = _dot(_rms(x, mqn_ref[...]).astype(BF16), wq_ref[...])
    heads = []
    for hd in range(MEM_HEADS):
        sl = slice(MEM_HEAD_DIM * hd, MEM_HEAD_DIM * (hd + 1))
        qh = (_rms(q[:, sl], qg_ref[...]) * (MEM_HEAD_DIM ** -0.5 * LOG2E)).astype(BF16)
        s = _dot_nt(qh, mk_ref[0, :, sl])
        p = jnp.exp2(s - jnp.max(s, axis=-1, keepdims=True))
        l = jnp.sum(p, axis=-1, keepdims=True)
        heads.append((_dot(p.astype(BF16), mv_ref[0, :, sl]) / l).astype(BF16))
    x = x + _dot(jnp.concatenate(heads, axis=-1), wmo_ref[...])

    hf = _rms(x, fn_ref[...]).astype(BF16)
    g = _dot(hf, wg_ref[...])
    u = _dot(hf, wu_ref[...])
    act = (g * jax.nn.sigmoid(g) * u).astype(BF16)
    o_ref[0] = x + _dot(act, wd_ref[...])


def _post_block(x, attns, wos, mqn, wq, qg, mk, mv, wmo, fn, wg, wu, wd, ts):
    B, S, D = x.shape
    n = len(attns)
    tok = lambda c: pl.BlockSpec((1, ts, c), lambda b, i: (b, i, 0))
    memspec = pl.BlockSpec((1,) + mk.shape[1:], lambda b, i: (b, 0, 0))
    consts_a = list(wos) + [mqn, wq, qg]
    consts_b = [wmo, fn, wg, wu, wd]
    return pl.pallas_call(
        functools.partial(_post_kernel, n_attn=n),
        grid=(B, S // ts),
        in_specs=([tok(D)] + [tok(a.shape[-1]) for a in attns] + [_const_spec(c.shape) for c in consts_a]
                  + [memspec, memspec] + [_const_spec(c.shape) for c in consts_b]),
        out_specs=tok(D),
        out_shape=jax.ShapeDtypeStruct((B, S, D), F32),
        compiler_params=_params(("parallel", "parallel")),
        name="post_block",
    )(x, *attns, *consts_a, mk, mv, *consts_b)


def _row(v):
    return v.reshape(1, -1).astype(F32)


def _shift_row(bound, first_lane):
    terms = jnp.stack(_split3(-jnp.asarray(bound, F32))).reshape(1, N_SPLIT)
    return jnp.pad(terms, ((0, 0), (first_lane, LANES - first_lane - N_SPLIT)))


def _mla_lane_perm():
    half = MLA_ROPE // 2
    lanes = np.arange(MLA_QK)
    return np.where(lanes < half, MLA_NOPE + lanes,
                    np.where(lanes < ROPE_HI_LANE, lanes - half,
                             np.where(lanes < ROPE_HI_LANE + half, lanes + half, lanes - MLA_ROPE)))


def _even_weights(w_in, w_uq, w_ukv, q_gain, k_gain):
    perm = _mla_lane_perm()
    is_nope = perm < MLA_NOPE
    pad = LANES - MLA_QK

    def head_cols(w_head, live):
        return jnp.pad(jnp.where(live[None, :], w_head[:, perm], 0.0), ((0, 0), (0, pad)))

    c_kr = EVEN_COLS[-1]
    kr_src = jnp.concatenate([jnp.zeros((w_in.shape[0], MLA_NOPE), w_in.dtype), w_in[:, c_kr:c_kr + MLA_ROPE]], axis=1)
    w1 = jnp.concatenate([w_in[:, :c_kr], head_cols(kr_src, ~is_nope)], axis=1).astype(BF16)
    wuq = jnp.concatenate(
        [head_cols(w_uq[:, MLA_QK * h:MLA_QK * (h + 1)], np.ones_like(is_nope)) for h in range(MLA_HEADS)],
        axis=1).astype(BF16)
    kv_w = MLA_NOPE + MLA_V
    rope_zeros = jnp.zeros((w_ukv.shape[0], MLA_ROPE), w_ukv.dtype)
    wuk = jnp.concatenate(
        [head_cols(jnp.concatenate([w_ukv[:, kv_w * h:kv_w * h + MLA_NOPE], rope_zeros], axis=1), is_nope)
         for h in range(MLA_HEADS)], axis=1).astype(BF16)
    wuv = jnp.concatenate(
        [w_ukv[:, kv_w * h + MLA_NOPE:kv_w * (h + 1)] for h in range(MLA_HEADS)], axis=1).astype(BF16).T
    gmq = _row(jnp.pad(q_gain[perm], (0, pad)))
    gmk = _row(jnp.pad(k_gain[perm], (0, pad)))
    return w1, wuq, wuk, wuv, gmq, gmk


def _rope_freqs():
    half = MLA_ROPE // 2
    freqs = (np.float32(ROPE_THETA) ** (-np.arange(half, dtype=np.float32) / np.float32(half))).astype(np.float32)
    return jnp.asarray(freqs.reshape(half, 1))


def _ones_blocks(n):
    g = np.arange(n) // HEAD_DIM
    return jnp.asarray((g[:, None] == g[None, :]).astype(np.float32), dtype=BF16)


def _tile_sizes(S):
    token_tile = min(2 * MXU_WIDTH, S)
    return token_tile, token_tile, min(2 * MXU_WIDTH, S - SWA_WINDOW), min(4 * MXU_WIDTH, S)


def kernel(x, mem, positions, mix_norm, ev_w_in, ev_swa_q_gain, ev_swa_k_gain, ev_sinks, ev_q_latent_norm,
           ev_kv_latent_norm, ev_w_uq, ev_w_ukv, ev_mla_q_gain, ev_mla_k_gain, ev_w_out, od_w_qkv, od_q_gain,
           od_k_gain, od_lambda, od_subln, od_w_out, mem_q_norm, mem_kv_norm, mem_w_q, mem_w_kv, mem_q_gain,
           mem_k_gain, mem_w_out, ffn_norm, ffn_w_gate, ffn_w_up, ffn_w_down):
    B, S, D = x.shape
    depth = mix_norm.shape[0]
    ts_proj, ts_post, tq_swa, tq_flash = _tile_sizes(S)

    ones = _ones_blocks(MXU_WIDTH)
    pos3 = positions.reshape(B, 1, S)
    inv = _rope_freqs()
    mem_k, mem_v = _mem_kv(mem, mem_kv_norm.reshape(depth, 1, D), mem_w_kv.astype(BF16),
                           mem_k_gain.reshape(depth, 1, MEM_HEAD_DIM))
    diff_slopes = [2.0 ** (-8.0 * (i + 1) / DIFF_HEADS) for i in range(DIFF_HEADS)]

    for l in range(depth):
        if l % 2 == 0:
            e = l // 2
            w1, wuq, wuk, wuv, gmq, gmk = _even_weights(ev_w_in[e], ev_w_uq[e], ev_w_ukv[e], ev_mla_q_gain[e],
                                                        ev_mla_k_gain[e])
            qa, ka, va, qm, km, vm = _even_proj(
                x, pos3, _row(mix_norm[l]), w1, ones, _row(jnp.tile(ev_swa_q_gain[e], SWA_HEADS)),
                _row(jnp.tile(ev_swa_k_gain[e], SWA_KV_HEADS)), _row(ev_q_latent_norm[e]),
                _row(ev_kv_latent_norm[e]), wuq, wuk, wuv, gmq, gmk,
                _shift_row(_score_bound(ev_mla_q_gain[e], ev_mla_k_gain[e], MLA_QK, MLA_QK ** -0.5 * LOG2E), MLA_QK),
                inv, ts_proj)
            swa_bound = _score_bound(ev_swa_q_gain[e], ev_swa_k_gain[e], HEAD_DIM, HEAD_DIM ** -0.5 * LOG2E)
            swa_scalars = jnp.concatenate([ev_sinks[e].astype(F32), swa_bound.reshape(1).astype(F32)])
            out_a = _swa_attention(swa_scalars, qa, ka, va, tq_swa)
            out_b = _mla_attention(qm, km, vm, tq_flash)
            wo = ev_w_out[e].astype(BF16)
            attns = [out_a, out_b]
            wos = [wo[:SWA_Q_W], wo[SWA_Q_W:]]
        else:
            o = l // 2
            lambda_init = 0.8 - 0.6 * math.exp(-0.3 * l)
            n_grp = DIFF_HEADS * 2
            w_qkv = od_w_qkv[o].astype(BF16)
            q, k, vt = _odd_proj(x, _row(mix_norm[l]), w_qkv[:, :2 * D], w_qkv[:, 2 * D:].T, ones,
                                 _row(jnp.tile(od_q_gain[o], n_grp // 2)), _row(jnp.tile(od_k_gain[o], n_grp // 2)),
                                 ts_proj)
            bound = _score_bound(od_q_gain[o], od_k_gain[o], DIFF_DIM, DIFF_DIM ** -0.5 * LOG2E)
            out_d = _diff_attention(diff_slopes, bound, q, k, vt, od_lambda[o].astype(F32), _row(od_subln[o]),
                                    tq_flash, lambda_init)
            attns = [out_d]
            wos = [od_w_out[o].astype(BF16)]
        x = _post_block(x, attns, wos, _row(mem_q_norm[l]), mem_w_q[l].astype(BF16), _row(mem_q_gain[l]),
                        mem_k[l], mem_v[l], mem_w_out[l].astype(BF16), _row(ffn_norm[l]),
                        ffn_w_gate[l].astype(BF16), ffn_w_up[l].astype(BF16), ffn_w_down[l].astype(BF16), ts_post)
    return x
```

```python
import functools
import math

import numpy as np
import jax
import jax.numpy as jnp
from jax import lax
from jax.experimental import pallas as pl
from jax.experimental.pallas import tpu as pltpu

F32 = jnp.float32
BF16 = jnp.bfloat16

EPS = 1e-6
MASKED = -1e30
LOG2E = 1.4426950408889634
ROPE_THETA = 10000.0

LANES = 128
SUBLANES = 8
MXU_WIDTH = 256
HEAD_DIM = 64
SWA_HEADS = 8
SWA_KV_HEADS = 2
SWA_WINDOW = 128
MLA_HEADS = 8
MLA_NOPE = 64
MLA_ROPE = 32
MLA_QK = MLA_NOPE + MLA_ROPE
MLA_V = 64
ROPE_HI_LANE = 64
DIFF_HEADS = 8
DIFF_DIM = 64
MEM_HEADS = 4
MEM_HEAD_DIM = 128
MLA_Q_RANK = 256
MLA_KV_RANK = 128
SWA_Q_W = SWA_HEADS * HEAD_DIM
SWA_KV_W = SWA_KV_HEADS * HEAD_DIM
EVEN_COLS = tuple(np.cumsum([SWA_Q_W, SWA_KV_W, SWA_KV_W, MLA_Q_RANK, MLA_KV_RANK]).tolist())

VMEM_LIMIT = 56 * 1024 * 1024

UNDERFLOW_GUARD = 2.0 ** -80
N_SPLIT = 3


def _params(sem):
    return pltpu.CompilerParams(dimension_semantics=sem, vmem_limit_bytes=VMEM_LIMIT)


def _const_spec(shape):
    nd = len(shape)
    return pl.BlockSpec(shape, lambda *_: (0,) * nd, pipeline_mode=pl.Buffered(1))


def _rms(x, g):
    ms = jnp.mean(x * x, axis=-1, keepdims=True)
    return x * lax.rsqrt(ms + EPS) * g


def _dot(a, b):
    return jnp.dot(a, b, preferred_element_type=F32)


def _dot_nt(a, b):
    return lax.dot_general(a, b, (((1,), (1,)), ((), ())), preferred_element_type=F32)


def _group_sumsq(x, ones_blk):
    x2 = x * x
    hi = x2.astype(BF16)
    lo = (x2 - hi.astype(F32)).astype(BF16)
    return _dot(hi, ones_blk) + _dot(lo, ones_blk)


def _group_norm64(x, ones_blk, gain):
    chunk = min(ones_blk.shape[0], x.shape[1])
    ssq = jnp.concatenate([_group_sumsq(x[:, c:c + chunk], ones_blk[0:chunk, 0:chunk])
                           for c in range(0, x.shape[1], chunk)], axis=1)
    return x * lax.rsqrt(ssq * (1.0 / HEAD_DIM) + EPS) * gain


def _even_proj_kernel(x_ref, pos_ref, gmix_ref, w1_ref, ones_ref, gq_ref, gk_ref, qln_ref, kvln_ref,
                      wuq_ref, wuk_ref, wuv_ref, gmq_ref, gmk_ref, qshift_ref, inv_ref,
                      qa_ref, ka_ref, va_ref, qm_ref, km_ref, vm_ref):
    x = x_ref[0]
    h = _rms(x, gmix_ref[...]).astype(BF16)
    z = _dot(h, w1_ref[...])
    c_ka, c_va, c_cq, c_ckv, c_kr = EVEN_COLS
    ones = ones_ref[...]
    qa = _group_norm64(z[:, 0:c_ka], ones, gq_ref[...]) * (HEAD_DIM ** -0.5 * LOG2E)
    qa_ref[0] = qa.astype(BF16)
    ka = _group_norm64(z[:, c_ka:c_va], ones[0:SWA_KV_W, 0:SWA_KV_W], gk_ref[...])
    ka_ref[0] = ka.astype(BF16)
    va_ref[0] = z[:, c_va:c_cq].astype(BF16)

    cqn = _rms(z[:, c_cq:c_ckv], qln_ref[...]).astype(BF16)
    ckvn = _rms(z[:, c_ckv:c_kr], kvln_ref[...]).astype(BF16)
    kr = z[:, c_kr:c_kr + LANES]
    qf = _dot(cqn, wuq_ref[...])
    kn = _dot(ckvn, wuk_ref[...])
    vm_ref[0] = _dot_nt(wuv_ref[...], ckvn).astype(BF16)

    half = MLA_ROPE // 2
    ang = inv_ref[...] * pos_ref[0].astype(F32)
    pad = jnp.zeros((LANES - half, ang.shape[1]), F32)
    cos = jnp.concatenate([jnp.cos(ang), pad], axis=0).T
    sin = jnp.concatenate([jnp.sin(ang), pad], axis=0).T
    lane = lax.broadcasted_iota(jnp.int32, cos.shape, 1)
    in_lo = lane < half
    in_hi = (lane >= ROPE_HI_LANE) & (lane < ROPE_HI_LANE + half)
    cos_t = jnp.where(in_lo, cos, jnp.where(in_hi, pltpu.roll(cos, ROPE_HI_LANE, 1), 1.0))
    sin_t = jnp.where(in_lo, -sin, jnp.where(in_hi, pltpu.roll(sin, ROPE_HI_LANE, 1), 0.0))

    pair = 2 * LANES
    gi = lax.broadcasted_iota(jnp.int32, (pair, pair), 0) // LANES
    gj = lax.broadcasted_iota(jnp.int32, (pair, pair), 1) // LANES
    head_ones = (gi == gj).astype(BF16)

    def norm_rope(xp, gain):
        r = lax.rsqrt(_group_sumsq(xp, head_ones) * (1.0 / MLA_QK) + EPS)
        xn = xp * r * gain
        halves = []
        for t in range(2):
            xh = xn[:, LANES * t:LANES * (t + 1)]
            halves.append(xh * cos_t + pltpu.roll(xh, ROPE_HI_LANE, 1) * sin_t)
        return halves

    gmq = jnp.concatenate([gmq_ref[...]] * 2, axis=1)
    gmk = jnp.concatenate([gmk_ref[...]] * 2, axis=1)
    kr2 = jnp.concatenate([kr, kr], axis=1)
    shift_lanes = (lane >= MLA_QK) & (lane < MLA_QK + N_SPLIT)
    qshift = qshift_ref[...]
    for hp in range(MLA_HEADS // 2):
        sl = slice(pair * hp, pair * (hp + 1))
        q_heads = norm_rope(qf[:, sl], gmq)
        k_heads = norm_rope(kn[:, sl] + kr2, gmk)
        for t in range(2):
            hl = slice(pair * hp + LANES * t, pair * hp + LANES * (t + 1))
            qm_ref[0, :, hl] = jnp.where(shift_lanes, qshift, q_heads[t] * (MLA_QK ** -0.5 * LOG2E)).astype(BF16)
            km_ref[0, :, hl] = jnp.where(shift_lanes, 1.0, k_heads[t]).astype(BF16)


def _even_proj(x, pos3, gmix, w1, ones, gq, gk, qln, kvln, wuq, wuk, wuv, gmq, gmk, qshift, inv, ts):
    B, S, D = x.shape
    grid = (B, S // ts)
    tok = lambda c: pl.BlockSpec((1, ts, c), lambda b, i: (b, i, 0))
    consts = [gmix, w1, ones, gq, gk, qln, kvln, wuq, wuk, wuv, gmq, gmk, qshift, inv]
    out_cols = [SWA_Q_W, SWA_KV_W, SWA_KV_W, MLA_HEADS * LANES, MLA_HEADS * LANES]
    v_rows = MLA_HEADS * MLA_V
    return pl.pallas_call(
        _even_proj_kernel,
        grid=grid,
        in_specs=([tok(D), pl.BlockSpec((1, 1, ts), lambda b, i: (b, 0, i))]
                  + [_const_spec(c.shape) for c in consts]),
        out_specs=[tok(c) for c in out_cols] + [pl.BlockSpec((1, v_rows, ts), lambda b, i: (b, 0, i))],
        out_shape=([jax.ShapeDtypeStruct((B, S, c), BF16) for c in out_cols]
                   + [jax.ShapeDtypeStruct((B, v_rows, S), BF16)]),
        compiler_params=_params(("parallel", "parallel")),
        name="even_proj",
    )(x, pos3, *consts)


def _swa_place_q(q_ref, hd, lane):
    hk = hd // (SWA_HEADS // SWA_KV_HEADS)
    qq = q_ref[0, :, LANES * (hd // 2):LANES * (hd // 2 + 1)].astype(F32)
    xq = qq if hd % 2 == hk else pltpu.roll(qq, HEAD_DIM, 1)
    return jnp.where((lane >= HEAD_DIM * hk) & (lane < HEAD_DIM * (hk + 1)), xq, 0.0).astype(BF16)


def _swa_place_out(o_even, o_odd, cb, lane):
    placed = []
    for e, o in enumerate((o_even, o_odd)):
        hk = (2 * cb + e) // (SWA_HEADS // SWA_KV_HEADS)
        placed.append(o if e == hk else pltpu.roll(o, HEAD_DIM, 1))
    return jnp.where(lane < HEAD_DIM, placed[0], placed[1]).astype(BF16)


def _swa_running_max(sink_ref, q_ref, k_ref, v_ref, o_ref, qi, tq):
    win = tq + SWA_WINDOW
    wstart = pl.multiple_of(jnp.maximum(qi * tq - SWA_WINDOW, 0), SWA_WINDOW)
    kw = k_ref[0, pl.ds(wstart, win), :]
    vw = v_ref[0, pl.ds(wstart, win), :]
    row = lax.broadcasted_iota(jnp.int32, (tq, win), 0)
    col = lax.broadcasted_iota(jnp.int32, (tq, win), 1)
    dist = (qi * tq + row) - (wstart + col)
    valid = (dist >= 0) & (dist < SWA_WINDOW)
    distf = dist.astype(F32)
    lane = lax.broadcasted_iota(jnp.int32, (tq, LANES), 1)
    for cb in range(SWA_HEADS // 2):
        outs = []
        for e in range(2):
            hd = 2 * cb + e
            s = _dot_nt(_swa_place_q(q_ref, hd, lane), kw) - (2.0 ** (-8.0 * (hd + 1) / SWA_HEADS) * LOG2E) * distf
            s = jnp.where(valid, s, MASKED)
            sink = sink_ref[hd] * LOG2E
            m = jnp.maximum(jnp.max(s, axis=-1, keepdims=True), sink)
            p = jnp.exp2(s - m)
            l = jnp.sum(p, axis=-1, keepdims=True) + jnp.exp2(sink - m)
            outs.append(_dot(p.astype(BF16), vw) / l)
        o_ref[0, :, LANES * cb:LANES * (cb + 1)] = _swa_place_out(outs[0], outs[1], cb, lane)


def _swa_kernel(sink_ref, q_ref, k_ref, v_ref, o_ref, *, tq):
    qi = pl.program_id(1)
    sub = SWA_WINDOW
    win = 2 * sub
    n_sub = tq // sub
    bound = sink_ref[SWA_HEADS]
    lane_q = lax.broadcasted_iota(jnp.int32, (tq, LANES), 1)
    lane = lax.broadcasted_iota(jnp.int32, (sub, LANES), 1)
    row = lax.broadcasted_iota(jnp.int32, (sub, win), 0)
    col = lax.broadcasted_iota(jnp.int32, (sub, win), 1)

    def bias_mask(first_key_offset, slope):
        dist = row + first_key_offset - col
        return jnp.where((dist >= 0) & (dist < SWA_WINDOW), -slope * dist.astype(F32) - bound, MASKED)

    ones_blk = jnp.ones((win, LANES), BF16)
    windows = []
    for r in range(n_sub):
        wstart = qi * tq + sub * (r - 1)
        wstart = pl.multiple_of(jnp.maximum(wstart, 0) if r == 0 else wstart, sub)
        windows.append((k_ref[0, pl.ds(wstart, win), :],
                        jnp.concatenate([v_ref[0, pl.ds(wstart, win), :], ones_blk], axis=1)))

    outs = [[None] * SWA_HEADS for _ in range(n_sub)]
    min_sum = None
    for hd in range(SWA_HEADS):
        slope = 2.0 ** (-8.0 * (hd + 1) / SWA_HEADS) * LOG2E
        bm = bias_mask(sub, slope)
        bm_first = jnp.where(qi == 0, bias_mask(0, slope), bm)
        xq = _swa_place_q(q_ref, hd, lane_q)
        sink_term = jnp.exp2(sink_ref[hd] * LOG2E - bound)
        for r in range(n_sub):
            kw, v_ones = windows[r]
            s = _dot_nt(xq[sub * r:sub * (r + 1)], kw) + (bm_first if r == 0 else bm)
            ov = _dot(jnp.exp2(s).astype(BF16), v_ones)
            l = ov[:, LANES:] + sink_term
            outs[r][hd] = ov[:, :LANES] / l
            min_sum = l if min_sum is None else jnp.minimum(min_sum, l)
    for r in range(n_sub):
        for cb in range(SWA_HEADS // 2):
            o_ref[0, sub * r:sub * (r + 1), LANES * cb:LANES * (cb + 1)] = _swa_place_out(
                outs[r][2 * cb], outs[r][2 * cb + 1], cb, lane)

    @pl.when(jnp.logical_not(jnp.min(min_sum) >= UNDERFLOW_GUARD))
    def _():
        _swa_running_max(sink_ref, q_ref, k_ref, v_ref, o_ref, qi, tq)


def _swa_attention(sinks, qa, ka, va, tq):
    B, S, _ = qa.shape
    return pl.pallas_call(
        functools.partial(_swa_kernel, tq=tq),
        grid=(B, S // tq),
        in_specs=[
            pl.BlockSpec(memory_space=pltpu.SMEM),
            pl.BlockSpec((1, tq, SWA_Q_W), lambda b, i: (b, i, 0)),
            pl.BlockSpec((1, S, SWA_KV_W), lambda b, i: (b, 0, 0)),
            pl.BlockSpec((1, S, SWA_KV_W), lambda b, i: (b, 0, 0)),
        ],
        out_specs=pl.BlockSpec((1, tq, SWA_Q_W), lambda b, i: (b, i, 0)),
        out_shape=jax.ShapeDtypeStruct((B, S, SWA_Q_W), BF16),
        compiler_params=_params(("parallel", "arbitrary")),
        name="swa_attention",
    )(sinks, qa, ka, va)


def _split3(x):
    hi = x.astype(BF16).astype(F32)
    r = x - hi
    mid = r.astype(BF16).astype(F32)
    lo = (r - mid).astype(BF16).astype(F32)
    return hi, mid, lo


def _score_bound(q_gain, k_gain, dim, scale):
    return dim * scale * jnp.max(jnp.abs(q_gain)) * jnp.max(jnp.abs(k_gain))


BLOCKS_PER_TRIP = 4


def _loop_blocks(n, step, init):
    def trip(t, c):
        for u in range(BLOCKS_PER_TRIP):
            c = step(BLOCKS_PER_TRIP * t + u, c)
        return c

    full = n // BLOCKS_PER_TRIP
    carry = lax.fori_loop(0, full, trip, init)
    return lax.fori_loop(full * BLOCKS_PER_TRIP, n, step, carry)


def _add_cols(acc, lo, part):
    if lo == 0:
        return acc + part
    return acc + jnp.concatenate([jnp.zeros((acc.shape[0], lo), acc.dtype), part], axis=1)


def _diagonal_spans(tq):
    h = tq // 2
    return ((0, h, 0), (h, h, h)) if h % MXU_WIDTH == 0 else ((0, tq, 0),)


def _causal_t(key_lo, key_n, q_lo, tq):
    shape = (key_n, tq - q_lo)
    return (key_lo + lax.broadcasted_iota(jnp.int32, shape, 0)) <= (q_lo + lax.broadcasted_iota(jnp.int32, shape, 1))


def _causal_mask(tq):
    row = lax.broadcasted_iota(jnp.int32, (tq, tq), 0)
    col = lax.broadcasted_iota(jnp.int32, (tq, tq), 1)
    return row >= col


def _mla_running_max(q_ref, k_ref, vt_ref, e, qi, tq):
    causal = _causal_mask(tq)
    q = q_ref[0, :, LANES * e:LANES * (e + 1)]

    def step(j, carry, masked):
        m, l, acc = carry
        ks = pl.multiple_of(j * tq, tq)
        k = k_ref[0, pl.ds(ks, tq), LANES * e:LANES * (e + 1)]
        vt = vt_ref[0, :, pl.ds(ks, tq)]
        s = _dot_nt(q, k)
        if masked:
            s = jnp.where(causal, s, MASKED)
        m_new = jnp.maximum(m, jnp.max(s, axis=-1, keepdims=True))
        alpha = jnp.exp2(m - m_new)
        p = jnp.exp2(s - m_new)
        l = alpha * l + jnp.sum(p, axis=-1, keepdims=True)
        acc = alpha * acc + _dot_nt(p.astype(BF16), vt)
        return m_new, l, acc

    init = (jnp.full((tq, 1), MASKED, F32), jnp.zeros((tq, 1), F32), jnp.zeros((tq, LANES), F32))
    carry = lax.fori_loop(0, qi, lambda j, c: step(j, c, False), init)
    m, l, acc = step(qi, carry, True)
    return acc / l


def _mla_kernel(q_ref, k_ref, vt_ref, o_ref, *, tq):
    qi = pl.program_id(2)
    lane = lax.broadcasted_iota(jnp.int32, (tq, LANES), 1)
    q = q_ref[0]
    first = lax.broadcasted_iota(jnp.int32, q.shape, 1) < LANES
    zero_q = jnp.zeros_like(q)
    q_aug = (jnp.where(first, q, zero_q), jnp.where(first, zero_q, q))

    def block(j, carry, key_lo=0, key_n=tq, q_lo=0, masked=False):
        ks = pl.multiple_of(j * tq + key_lo, key_n)
        n = tq - q_lo
        q_both = jnp.concatenate([q_aug[0][q_lo:, :], q_aug[1][q_lo:, :]], axis=0)
        st = _dot_nt(k_ref[0, pl.ds(ks, key_n), :], q_both)
        if masked:
            causal = _causal_t(key_lo, key_n, q_lo, tq)
            st = jnp.where(jnp.concatenate([causal, causal], axis=1), st, MASKED)
        pt = jnp.exp2(st)
        part = jnp.sum(pt.reshape(key_n // SUBLANES, SUBLANES, 2 * n), axis=0)
        pv = _dot(vt_ref[0, :, pl.ds(ks, key_n)], pt.astype(BF16))
        return tuple((_add_cols(carry[e][0], q_lo, part[:, n * e:n * (e + 1)]),
                      _add_cols(carry[e][1], q_lo, pv[MLA_V * e:MLA_V * (e + 1), n * e:n * (e + 1)]))
                     for e in range(2))

    zero = (jnp.zeros((SUBLANES, tq), F32), jnp.zeros((MLA_V, tq), F32))
    carry = _loop_blocks(qi, block, (zero, zero))
    for key_lo, key_n, q_lo in _diagonal_spans(tq):
        carry = block(qi, carry, key_lo, key_n, q_lo, masked=True)
    sums = [jnp.sum(lsum, axis=0, keepdims=True) for lsum, _ in carry]
    outs = [acc / l for (_, acc), l in zip(carry, sums)]
    o_ref[0] = jnp.concatenate(outs, axis=0).T.astype(BF16)

    @pl.when(jnp.logical_not(jnp.min(jnp.minimum(sums[0], sums[1])) >= UNDERFLOW_GUARD))
    def _():
        slow = [_mla_running_max(q_ref, k_ref, vt_ref, e, qi, tq) for e in range(2)]
        o_ref[0] = jnp.where(lane < MLA_V, slow[0], slow[1]).astype(BF16)


def _mla_attention(qm, km, vm, tq):
    B, S, _ = qm.shape
    return pl.pallas_call(
        functools.partial(_mla_kernel, tq=tq),
        grid=(B, MLA_HEADS // 2, S // tq),
        in_specs=[
            pl.BlockSpec((1, tq, 2 * LANES), lambda b, h, i: (b, i, h)),
            pl.BlockSpec((1, S, 2 * LANES), lambda b, h, i: (b, 0, h)),
            pl.BlockSpec((1, 2 * MLA_V, S), lambda b, h, i: (b, h, 0)),
        ],
        out_specs=pl.BlockSpec((1, tq, LANES), lambda b, h, i: (b, i, h)),
        out_shape=jax.ShapeDtypeStruct((B, S, MLA_HEADS * MLA_V), BF16),
        compiler_params=_params(("parallel", "parallel", "arbitrary")),
        name="mla_attention",
    )(qm, km, vm)


def _odd_proj_kernel(x_ref, gmix_ref, w_ref, wvt_ref, ones_ref, gq_ref, gk_ref, q_ref, k_ref, vt_ref):
    x = x_ref[0]
    h = _rms(x, gmix_ref[...]).astype(BF16)
    z = _dot(h, w_ref[...])
    ones = ones_ref[...]
    width = DIFF_HEADS * 2 * DIFF_DIM
    piece = gq_ref.shape[1]
    for c in range(0, width, piece):
        q = _group_norm64(z[:, c:c + piece], ones, gq_ref[...]) * (DIFF_DIM ** -0.5 * LOG2E)
        q_ref[0, :, c:c + piece] = q.astype(BF16)
        k = _group_norm64(z[:, width + c:width + c + piece], ones, gk_ref[...])
        k_ref[0, :, c:c + piece] = k.astype(BF16)
    vt_ref[0] = _dot_nt(wvt_ref[...], h).astype(BF16)


def _odd_proj(x, gmix, w_qk, w_vt, ones, gq, gk, ts):
    B, S, D = x.shape
    tok = pl.BlockSpec((1, ts, D), lambda b, i: (b, i, 0))
    consts = [gmix, w_qk, w_vt, ones, gq, gk]
    return pl.pallas_call(
        _odd_proj_kernel,
        grid=(B, S // ts),
        in_specs=[tok] + [_const_spec(c.shape) for c in consts],
        out_specs=[tok, tok, pl.BlockSpec((1, D, ts), lambda b, i: (b, 0, i))],
        out_shape=[jax.ShapeDtypeStruct((B, S, D), BF16)] * 2 + [jax.ShapeDtypeStruct((B, D, S), BF16)],
        compiler_params=_params(("parallel", "parallel")),
        name="odd_proj",
    )(x, *consts)


def _diff_q_parts(q, lane):
    zero = jnp.zeros_like(q)
    return jnp.where(lane < DIFF_DIM, q, zero), jnp.where(lane >= DIFF_DIM, q, zero)


def _diff_running_max(slope, q_ref, k_ref, vt_ref, qi, tq):
    row = lax.broadcasted_iota(jnp.int32, (tq, tq), 0)
    col = lax.broadcasted_iota(jnp.int32, (tq, tq), 1)
    causal = row >= col
    bias = slope * (col - row).astype(F32)
    lane = lax.broadcasted_iota(jnp.int32, (tq, LANES), 1)
    q_parts = _diff_q_parts(q_ref[0], lane)

    def step(j, carry, masked):
        ks = pl.multiple_of(j * tq, tq)
        k = k_ref[0, pl.ds(ks, tq), :]
        vt = vt_ref[0, :, pl.ds(ks, tq)]
        offset = slope * ((j - qi) * tq).astype(F32)
        new = []
        for c in range(2):
            m, l, acc = carry[c]
            s = _dot_nt(q_parts[c], k) + bias
            if masked:
                s = jnp.where(causal, s, MASKED)
            m_new = jnp.maximum(m, jnp.max(s, axis=-1, keepdims=True) + offset)
            alpha = jnp.exp2(m - m_new)
            p = jnp.exp2(s - (m_new - offset))
            l = alpha * l + jnp.sum(p, axis=-1, keepdims=True)
            acc = alpha * acc + _dot_nt(p.astype(BF16), vt)
            new.append((m_new, l, acc))
        return tuple(new)

    one = (jnp.full((tq, 1), MASKED, F32), jnp.zeros((tq, 1), F32), jnp.zeros((tq, LANES), F32))
    carry = lax.fori_loop(0, qi, lambda j, c: step(j, c, False), (one, one))
    (m1, l1, a1), (m2, l2, a2) = step(qi, carry, True)
    return a1 / l1, a2 / l2


def _diff_kernel(slope_ref, qx_ref, qpos_ref, q_ref, k_ref, kx_ref, vt_ref, lam_ref, subln_ref, o_ref, *, tq,
                 lambda_init):
    hd = pl.program_id(1)
    qi = pl.program_id(2)
    lane = lax.broadcasted_iota(jnp.int32, (tq, LANES), 1)
    q_parts = _diff_q_parts(q_ref[0], lane)
    head_row = qx_ref[0]
    qx = jnp.where(lane < 3 * N_SPLIT, head_row,
                   qpos_ref[...].astype(F32) * head_row[:, SLOPE_LANE:SLOPE_LANE + 1]).astype(BF16)
    q_aug = [jnp.concatenate([part, qx], axis=1) for part in q_parts]
    lf = lam_ref[...]
    lam = (jnp.exp(jnp.sum(lf[0:1] * lf[1:2], axis=-1, keepdims=True))
           - jnp.exp(jnp.sum(lf[2:3] * lf[3:4], axis=-1, keepdims=True)) + lambda_init)
    gain = subln_ref[...] * (1.0 - lambda_init)

    def block(j, carry, key_lo=0, key_n=tq, q_lo=0, masked=False):
        ks = pl.multiple_of(j * tq + key_lo, key_n)
        k = jnp.concatenate([k_ref[0, pl.ds(ks, key_n), :], kx_ref[pl.ds(ks, key_n), :]], axis=1)
        vt = vt_ref[0, :, pl.ds(ks, key_n)]
        n = tq - q_lo
        q_both = jnp.concatenate([q_aug[0][q_lo:, :], q_aug[1][q_lo:, :]], axis=0)
        st = _dot_nt(k, q_both)
        if masked:
            causal = _causal_t(key_lo, key_n, q_lo, tq)
            st = jnp.where(jnp.concatenate([causal, causal], axis=1), st, MASKED)
        pt = jnp.exp2(st)
        part = jnp.sum(pt.reshape(key_n // SUBLANES, SUBLANES, 2 * n), axis=0)
        pv = _dot(vt, pt.astype(BF16))
        return tuple((_add_cols(carry[c][0], q_lo, part[:, n * c:n * (c + 1)]),
                      _add_cols(carry[c][1], q_lo, pv[:, n * c:n * (c + 1)])) for c in range(2))

    zero = (jnp.zeros((SUBLANES, tq), F32), jnp.zeros((LANES, tq), F32))
    carry = _loop_blocks(qi, block, (zero, zero))
    for key_lo, key_n, q_lo in _diagonal_spans(tq):
        carry = block(qi, carry, key_lo, key_n, q_lo, masked=True)
    (ls1, a1), (ls2, a2) = carry
    l1 = jnp.sum(ls1, axis=0, keepdims=True)
    l2 = jnp.sum(ls2, axis=0, keepdims=True)

    ot = a1 / l1 - lam * (a2 / l2)
    ot = ot * lax.rsqrt(jnp.mean(ot * ot, axis=0, keepdims=True) + EPS)
    o_ref[0] = (ot.T * gain).astype(BF16)

    @pl.when(jnp.logical_not(jnp.min(jnp.minimum(l1, l2)) >= UNDERFLOW_GUARD))
    def _():
        o1, o2 = _diff_running_max(slope_ref[hd] * LOG2E, q_ref, k_ref, vt_ref, qi, tq)
        o_ref[0] = _rms(o1 - lam * o2, gain).astype(BF16)


SLOPE_LANE = 5 * N_SPLIT


def _alibi_tables(slopes, bound, S):
    assert all(math.log2(s) == int(math.log2(s)) for s in slopes), "ALiBi slopes must be powers of two"
    n = N_SPLIT
    log2e_terms = jnp.concatenate(_split3(jnp.full((1, 1), LOG2E, F32)), axis=1)
    s = jnp.asarray(slopes, F32).reshape(-1, 1)
    head = jnp.concatenate(
        [s * log2e_terms * float(LANES), s * log2e_terms, jnp.broadcast_to(_shift_row(bound, 0)[:, :n], (len(slopes), n)),
         jnp.zeros((len(slopes), SLOPE_LANE - 3 * n), F32), s, jnp.zeros((len(slopes), LANES - SLOPE_LANE - 1), F32)],
        axis=1)
    idx = np.arange(S)
    hi, lo = (idx // LANES)[:, None].astype(np.float32), (idx % LANES)[:, None].astype(np.float32)
    qpos = np.zeros((S, LANES), np.float32)
    qpos[:, 3 * n:4 * n] = hi * LANES
    qpos[:, 4 * n:5 * n] = lo
    kx = jnp.concatenate(
        [jnp.asarray(np.concatenate([np.repeat(hi, n, 1), np.repeat(lo, n, 1), np.ones((S, n), np.float32)], axis=1)),
         jnp.broadcast_to(-log2e_terms, (S, n)), jnp.broadcast_to(-log2e_terms, (S, n)),
         jnp.zeros((S, LANES - 5 * n), F32)], axis=1)
    return head.reshape(-1, 1, LANES), jnp.asarray(qpos, dtype=BF16), kx.astype(BF16)


def _diff_attention(slopes, bound, q, k, vt, lambdas, subln, tq, lambda_init):
    B, S, D = q.shape
    qx, qpos, kx = _alibi_tables(slopes, bound, S)
    return pl.pallas_call(
        functools.partial(_diff_kernel, tq=tq, lambda_init=lambda_init),
        grid=(B, DIFF_HEADS, S // tq),
        in_specs=[
            pl.BlockSpec(memory_space=pltpu.SMEM),
            pl.BlockSpec((1, 1, LANES), lambda b, h, i: (h, 0, 0)),
            pl.BlockSpec((tq, LANES), lambda b, h, i: (i, 0)),
            pl.BlockSpec((1, tq, LANES), lambda b, h, i: (b, i, h)),
            pl.BlockSpec((1, S, LANES), lambda b, h, i: (b, 0, h)),
            _const_spec(kx.shape),
            pl.BlockSpec((1, LANES, S), lambda b, h, i: (b, h, 0)),
            pl.BlockSpec(lambdas.shape, lambda b, h, i: (0, 0)),
            pl.BlockSpec(subln.shape, lambda b, h, i: (0, 0)),
        ],
        out_specs=pl.BlockSpec((1, tq, LANES), lambda b, h, i: (b, i, h)),
        out_shape=jax.ShapeDtypeStruct((B, S, D), BF16),
        compiler_params=_params(("parallel", "parallel", "arbitrary")),
        name="diff_attention",
    )(jnp.asarray(slopes, F32), qx, qpos, q, k, kx, vt, lambdas, subln)


def _mem_kv_kernel(mem_ref, norm_ref, w_ref, gain_ref, k_ref, v_ref):
    h = _rms(mem_ref[0], norm_ref[0]).astype(BF16)
    z = _dot(h, w_ref[0])
    width = MEM_HEADS * MEM_HEAD_DIM
    gain = gain_ref[0]
    for hd in range(MEM_HEADS):
        sl = slice(MEM_HEAD_DIM * hd, MEM_HEAD_DIM * (hd + 1))
        k_ref[0, 0, :, sl] = _rms(z[:, sl], gain).astype(BF16)
    v_ref[0, 0] = z[:, width:2 * width].astype(BF16)


def _mem_kv(mem, norms, w_kv, gains):
    B, M, D = mem.shape
    L = norms.shape[0]
    width = MEM_HEADS * MEM_HEAD_DIM
    out = pl.BlockSpec((1, 1, M, width), lambda l, b: (l, b, 0, 0))
    return pl.pallas_call(
        _mem_kv_kernel,
        grid=(L, B),
        in_specs=[
            pl.BlockSpec((1, M, D), lambda l, b: (b, 0, 0)),
            pl.BlockSpec((1, 1, D), lambda l, b: (l, 0, 0)),
            pl.BlockSpec((1, D, 2 * width), lambda l, b: (l, 0, 0)),
            pl.BlockSpec((1, 1, MEM_HEAD_DIM), lambda l, b: (l, 0, 0)),
        ],
        out_specs=[out, out],
        out_shape=[jax.ShapeDtypeStruct((L, B, M, width), BF16)] * 2,
        compiler_params=_params(("parallel", "parallel")),
        name="mem_kv",
    )(mem, norms, w_kv, gains)


def _post_kernel(*refs, n_attn):
    x_ref = refs[0]
    attn_refs = refs[1:1 + n_attn]
    wo_refs = refs[1 + n_attn:1 + 2 * n_attn]
    (mqn_ref, wq_ref, qg_ref, mk_ref, mv_ref, wmo_ref, fn_ref, wg_ref, wu_ref, wd_ref, o_ref) = refs[1 + 2 * n_attn:]
    x = x_ref[0]
    for a_ref, w_ref in zip(attn_refs, wo_refs):
        x = x + _dot(a_ref[0], w_ref[...])

    q = _dot(_rms(x, mqn_ref[...]).astype(BF16), wq_ref[...])
    heads = []
    for hd in range(MEM_HEADS):
        sl = slice(MEM_HEAD_DIM * hd, MEM_HEAD_DIM * (hd + 1))
        qh = (_rms(q[:, sl], qg_ref[...]) * (MEM_HEAD_DIM ** -0.5 * LOG2E)).astype(BF16)
        s = _dot_nt(qh, mk_ref[0, :, sl])
        p = jnp.exp2(s - jnp.max(s, axis=-1, keepdims=True))
        l = jnp.sum(p, axis=-1, keepdims=True)
        heads.append((_dot(p.astype(BF16), mv_ref[0, :, sl]) / l).astype(BF16))
    x = x + _dot(jnp.concatenate(heads, axis=-1), wmo_ref[...])

    hf = _rms(x, fn_ref[...]).astype(BF16)
    g = _dot(hf, wg_ref[...])
    u = _dot(hf, wu_ref[...])
    act = (g * jax.nn.sigmoid(g) * u).astype(BF16)
    o_ref[0] = x + _dot(act, wd_ref[...])


def _post_block(x, attns, wos, mqn, wq, qg, mk, mv, wmo, fn, wg, wu, wd, ts):
    B, S, D = x.shape
    n = len(attns)
    tok = lambda c: pl.BlockSpec((1, ts, c), lambda b, i: (b, i, 0))
    memspec = pl.BlockSpec((1,) + mk.shape[1:], lambda b, i: (b, 0, 0))
    consts_a = list(wos) + [mqn, wq, qg]
    consts_b = [wmo, fn, wg, wu, wd]
    return pl.pallas_call(
        functools.partial(_post_kernel, n_attn=n),
        grid=(B, S // ts),
        in_specs=([tok(D)] + [tok(a.shape[-1]) for a in attns] + [_const_spec(c.shape) for c in consts_a]
                  + [memspec, memspec] + [_const_spec(c.shape) for c in consts_b]),
        out_specs=tok(D),
        out_shape=jax.ShapeDtypeStruct((B, S, D), F32),
        compiler_params=_params(("parallel", "parallel")),
        name="post_block",
    )(x, *attns, *consts_a, mk, mv, *consts_b)


def _row(v):
    return v.reshape(1, -1).astype(F32)


def _shift_row(bound, first_lane):
    terms = jnp.stack(_split3(-jnp.asarray(bound, F32))).reshape(1, N_SPLIT)
    return jnp.pad(terms, ((0, 0), (first_lane, LANES - first_lane - N_SPLIT)))


def _mla_lane_perm():
    half = MLA_ROPE // 2
    lanes = np.arange(MLA_QK)
    return np.where(lanes < half, MLA_NOPE + lanes,
                    np.where(lanes < ROPE_HI_LANE, lanes - half,
                             np.where(lanes < ROPE_HI_LANE + half, lanes + half, lanes - MLA_ROPE)))


def _even_weights(w_in, w_uq, w_ukv, q_gain, k_gain):
    perm = _mla_lane_perm()
    is_nope = perm < MLA_NOPE
    pad = LANES - MLA_QK

    def head_cols(w_head, live):
        return jnp.pad(jnp.where(live[None, :], w_head[:, perm], 0.0), ((0, 0), (0, pad)))

    c_kr = EVEN_COLS[-1]
    kr_src = jnp.concatenate([jnp.zeros((w_in.shape[0], MLA_NOPE), w_in.dtype), w_in[:, c_kr:c_kr + MLA_ROPE]], axis=1)
    w1 = jnp.concatenate([w_in[:, :c_kr], head_cols(kr_src, ~is_nope)], axis=1).astype(BF16)
    wuq = jnp.concatenate(
        [head_cols(w_uq[:, MLA_QK * h:MLA_QK * (h + 1)], np.ones_like(is_nope)) for h in range(MLA_HEADS)],
        axis=1).astype(BF16)
    kv_w = MLA_NOPE + MLA_V
    rope_zeros = jnp.zeros((w_ukv.shape[0], MLA_ROPE), w_ukv.dtype)
    wuk = jnp.concatenate(
        [head_cols(jnp.concatenate([w_ukv[:, kv_w * h:kv_w * h + MLA_NOPE], rope_zeros], axis=1), is_nope)
         for h in range(MLA_HEADS)], axis=1).astype(BF16)
    wuv = jnp.concatenate(
        [w_ukv[:, kv_w * h + MLA_NOPE:kv_w * (h + 1)] for h in range(MLA_HEADS)], axis=1).astype(BF16).T
    gmq = _row(jnp.pad(q_gain[perm], (0, pad)))
    gmk = _row(jnp.pad(k_gain[perm], (0, pad)))
    return w1, wuq, wuk, wuv, gmq, gmk


def _rope_freqs():
    half = MLA_ROPE // 2
    freqs = (np.float32(ROPE_THETA) ** (-np.arange(half, dtype=np.float32) / np.float32(half))).astype(np.float32)
    return jnp.asarray(freqs.reshape(half, 1))


def _ones_blocks(n):
    g = np.arange(n) // HEAD_DIM
    return jnp.asarray((g[:, None] == g[None, :]).astype(np.float32), dtype=BF16)


def _tile_sizes(S):
    token_tile = min(2 * MXU_WIDTH, S)
    return token_tile, token_tile, min(2 * MXU_WIDTH, S - SWA_WINDOW), min(4 * MXU_WIDTH, S)


def kernel(x, mem, positions, mix_norm, ev_w_in, ev_swa_q_gain, ev_swa_k_gain, ev_sinks, ev_q_latent_norm,
           ev_kv_latent_norm, ev_w_uq, ev_w_ukv, ev_mla_q_gain, ev_mla_k_gain, ev_w_out, od_w_qkv, od_q_gain,
           od_k_gain, od_lambda, od_subln, od_w_out, mem_q_norm, mem_kv_norm, mem_w_q, mem_w_kv, mem_q_gain,
           mem_k_gain, mem_w_out, ffn_norm, ffn_w_gate, ffn_w_up, ffn_w_down):
    B, S, D = x.shape
    depth = mix_norm.shape[0]
    ts_proj, ts_post, tq_swa, tq_flash = _tile_sizes(S)

    ones = _ones_blocks(MXU_WIDTH)
    pos3 = positions.reshape(B, 1, S)
    inv = _rope_freqs()
    mem_k, mem_v = _mem_kv(mem, mem_kv_norm.reshape(depth, 1, D), mem_w_kv.astype(BF16),
                           mem_k_gain.reshape(depth, 1, MEM_HEAD_DIM))
    diff_slopes = [2.0 ** (-8.0 * (i + 1) / DIFF_HEADS) for i in range(DIFF_HEADS)]

    for l in range(depth):
        if l % 2 == 0:
            e = l // 2
            w1, wuq, wuk, wuv, gmq, gmk = _even_weights(ev_w_in[e], ev_w_uq[e], ev_w_ukv[e], ev_mla_q_gain[e],
                                                        ev_mla_k_gain[e])
            qa, ka, va, qm, km, vm = _even_proj(
                x, pos3, _row(mix_norm[l]), w1, ones, _row(jnp.tile(ev_swa_q_gain[e], SWA_HEADS)),
                _row(jnp.tile(ev_swa_k_gain[e], SWA_KV_HEADS)), _row(ev_q_latent_norm[e]),
                _row(ev_kv_latent_norm[e]), wuq, wuk, wuv, gmq, gmk,
                _shift_row(_score_bound(ev_mla_q_gain[e], ev_mla_k_gain[e], MLA_QK, MLA_QK ** -0.5 * LOG2E), MLA_QK),
                inv, ts_proj)
            swa_bound = _score_bound(ev_swa_q_gain[e], ev_swa_k_gain[e], HEAD_DIM, HEAD_DIM ** -0.5 * LOG2E)
            swa_scalars = jnp.concatenate([ev_sinks[e].astype(F32), swa_bound.reshape(1).astype(F32)])
            out_a = _swa_attention(swa_scalars, qa, ka, va, tq_swa)
            out_b = _mla_attention(qm, km, vm, tq_flash)
            wo = ev_w_out[e].astype(BF16)
            attns = [out_a, out_b]
            wos = [wo[:SWA_Q_W], wo[SWA_Q_W:]]
        else:
            o = l // 2
            lambda_init = 0.8 - 0.6 * math.exp(-0.3 * l)
            n_grp = DIFF_HEADS * 2
            w_qkv = od_w_qkv[o].astype(BF16)
            q, k, vt = _odd_proj(x, _row(mix_norm[l]), w_qkv[:, :2 * D], w_qkv[:, 2 * D:].T, ones,
                                 _row(jnp.tile(od_q_gain[o], n_grp // 2)), _row(jnp.tile(od_k_gain[o], n_grp // 2)),
                                 ts_proj)
            bound = _score_bound(od_q_gain[o], od_k_gain[o], DIFF_DIM, DIFF_DIM ** -0.5 * LOG2E)
            out_d = _diff_attention(diff_slopes, bound, q, k, vt, od_lambda[o].astype(F32), _row(od_subln[o]),
                                    tq_flash, lambda_init)
            attns = [out_d]
            wos = [od_w_out[o].astype(BF16)]
        x = _post_block(x, attns, wos, _row(mem_q_norm[l]), mem_w_q[l].astype(BF16), _row(mem_q_gain[l]),
                        mem_k[l], mem_v[l], mem_w_out[l].astype(BF16), _row(ffn_norm[l]),
                        ffn_w_gate[l].astype(BF16), ffn_w_up[l].astype(BF16), ffn_w_down[l].astype(BF16), ts_post)
    return x
```

```python
import functools
import math

import numpy as np
import jax
import jax.numpy as jnp
from jax import lax
from jax.experimental import pallas as pl
from jax.experimental.pallas import tpu as pltpu

F32 = jnp.float32
BF16 = jnp.bfloat16

EPS = 1e-6
MASKED = -1e30
LOG2E = 1.4426950408889634
ROPE_THETA = 10000.0

LANES = 128
SUBLANES = 8
MXU_WIDTH = 256
HEAD_DIM = 64
SWA_HEADS = 8
SWA_KV_HEADS = 2
SWA_WINDOW = 128
MLA_HEADS = 8
MLA_NOPE = 64
MLA_ROPE = 32
MLA_QK = MLA_NOPE + MLA_ROPE
MLA_V = 64
ROPE_HI_LANE = 64
DIFF_HEADS = 8
DIFF_DIM = 64
MEM_HEADS = 4
MEM_HEAD_DIM = 128
MLA_Q_RANK = 256
MLA_KV_RANK = 128
SWA_Q_W = SWA_HEADS * HEAD_DIM
SWA_KV_W = SWA_KV_HEADS * HEAD_DIM
EVEN_COLS = tuple(np.cumsum([SWA_Q_W, SWA_KV_W, SWA_KV_W, MLA_Q_RANK, MLA_KV_RANK]).tolist())

VMEM_LIMIT = 56 * 1024 * 1024

UNDERFLOW_GUARD = 2.0 ** -80
N_SPLIT = 3


def _params(sem):
    return pltpu.CompilerParams(dimension_semantics=sem, vmem_limit_bytes=VMEM_LIMIT)


def _const_spec(shape):
    nd = len(shape)
    return pl.BlockSpec(shape, lambda *_: (0,) * nd, pipeline_mode=pl.Buffered(1))


def _rms(x, g):
    ms = jnp.mean(x * x, axis=-1, keepdims=True)
    return x * lax.rsqrt(ms + EPS) * g


def _dot(a, b):
    return jnp.dot(a, b, preferred_element_type=F32)


def _dot_nt(a, b):
    return lax.dot_general(a, b, (((1,), (1,)), ((), ())), preferred_element_type=F32)


def _group_sumsq(x, ones_blk):
    x2 = x * x
    hi = x2.astype(BF16)
    lo = (x2 - hi.astype(F32)).astype(BF16)
    return _dot(hi, ones_blk) + _dot(lo, ones_blk)


def _group_norm64(x, ones_blk, gain):
    chunk = min(ones_blk.shape[0], x.shape[1])
    ssq = jnp.concatenate([_group_sumsq(x[:, c:c + chunk], ones_blk[0:chunk, 0:chunk])
                           for c in range(0, x.shape[1], chunk)], axis=1)
    return x * lax.rsqrt(ssq * (1.0 / HEAD_DIM) + EPS) * gain


def _even_proj_kernel(x_ref, pos_ref, gmix_ref, w1_ref, ones_ref, gq_ref, gk_ref, qln_ref, kvln_ref,
                      wuq_ref, wuk_ref, wuv_ref, gmq_ref, gmk_ref, qshift_ref, inv_ref,
                      qa_ref, ka_ref, va_ref, qm_ref, km_ref, vm_ref):
    x = x_ref[0]
    h = _rms(x, gmix_ref[...]).astype(BF16)
    z = _dot(h, w1_ref[...])
    c_ka, c_va, c_cq, c_ckv, c_kr = EVEN_COLS
    ones = ones_ref[...]
    qa = _group_norm64(z[:, 0:c_ka], ones, gq_ref[...]) * (HEAD_DIM ** -0.5 * LOG2E)
    qa_ref[0] = qa.astype(BF16)
    ka = _group_norm64(z[:, c_ka:c_va], ones[0:SWA_KV_W, 0:SWA_KV_W], gk_ref[...])
    ka_ref[0] = ka.astype(BF16)
    va_ref[0] = z[:, c_va:c_cq].astype(BF16)

    cqn = _rms(z[:, c_cq:c_ckv], qln_ref[...]).astype(BF16)
    ckvn = _rms(z[:, c_ckv:c_kr], kvln_ref[...]).astype(BF16)
    kr = z[:, c_kr:c_kr + LANES]
    qf = _dot(cqn, wuq_ref[...])
    kn = _dot(ckvn, wuk_ref[...])
    vm_ref[0] = _dot_nt(wuv_ref[...], ckvn).astype(BF16)

    half = MLA_ROPE // 2
    ang = inv_ref[...] * pos_ref[0].astype(F32)
    pad = jnp.zeros((LANES - half, ang.shape[1]), F32)
    cos = jnp.concatenate([jnp.cos(ang), pad], axis=0).T
    sin = jnp.concatenate([jnp.sin(ang), pad], axis=0).T
    lane = lax.broadcasted_iota(jnp.int32, cos.shape, 1)
    in_lo = lane < half
    in_hi = (lane >= ROPE_HI_LANE) & (lane < ROPE_HI_LANE + half)
    cos_t = jnp.where(in_lo, cos, jnp.where(in_hi, pltpu.roll(cos, ROPE_HI_LANE, 1), 1.0))
    sin_t = jnp.where(in_lo, -sin, jnp.where(in_hi, pltpu.roll(sin, ROPE_HI_LANE, 1), 0.0))

    pair = 2 * LANES
    gi = lax.broadcasted_iota(jnp.int32, (pair, pair), 0) // LANES
    gj = lax.broadcasted_iota(jnp.int32, (pair, pair), 1) // LANES
    head_ones = (gi == gj).astype(BF16)

    def norm_rope(xp, gain):
        r = lax.rsqrt(_group_sumsq(xp, head_ones) * (1.0 / MLA_QK) + EPS)
        xn = xp * r * gain
        halves = []
        for t in range(2):
            xh = xn[:, LANES * t:LANES * (t + 1)]
            halves.append(xh * cos_t + pltpu.roll(xh, ROPE_HI_LANE, 1) * sin_t)
        return halves

    gmq = jnp.concatenate([gmq_ref[...]] * 2, axis=1)
    gmk = jnp.concatenate([gmk_ref[...]] * 2, axis=1)
    kr2 = jnp.concatenate([kr, kr], axis=1)
    shift_lanes = (lane >= MLA_QK) & (lane < MLA_QK + N_SPLIT)
    qshift = qshift_ref[...]
    for hp in range(MLA_HEADS // 2):
        sl = slice(pair * hp, pair * (hp + 1))
        q_heads = norm_rope(qf[:, sl], gmq)
        k_heads = norm_rope(kn[:, sl] + kr2, gmk)
        for t in range(2):
            hl = slice(pair * hp + LANES * t, pair * hp + LANES * (t + 1))
            qm_ref[0, :, hl] = jnp.where(shift_lanes, qshift, q_heads[t] * (MLA_QK ** -0.5 * LOG2E)).astype(BF16)
            km_ref[0, :, hl] = jnp.where(shift_lanes, 1.0, k_heads[t]).astype(BF16)


def _even_proj(x, pos3, gmix, w1, ones, gq, gk, qln, kvln, wuq, wuk, wuv, gmq, gmk, qshift, inv, ts):
    B, S, D = x.shape
    grid = (B, S // ts)
    tok = lambda c: pl.BlockSpec((1, ts, c), lambda b, i: (b, i, 0))
    consts = [gmix, w1, ones, gq, gk, qln, kvln, wuq, wuk, wuv, gmq, gmk, qshift, inv]
    out_cols = [SWA_Q_W, SWA_KV_W, SWA_KV_W, MLA_HEADS * LANES, MLA_HEADS * LANES]
    v_rows = MLA_HEADS * MLA_V
    return pl.pallas_call(
        _even_proj_kernel,
        grid=grid,
        in_specs=([tok(D), pl.BlockSpec((1, 1, ts), lambda b, i: (b, 0, i))]
                  + [_const_spec(c.shape) for c in consts]),
        out_specs=[tok(c) for c in out_cols] + [pl.BlockSpec((1, v_rows, ts), lambda b, i: (b, 0, i))],
        out_shape=([jax.ShapeDtypeStruct((B, S, c), BF16) for c in out_cols]
                   + [jax.ShapeDtypeStruct((B, v_rows, S), BF16)]),
        compiler_params=_params(("parallel", "parallel")),
        name="even_proj",
    )(x, pos3, *consts)


def _swa_place_q(q_ref, hd, lane):
    hk = hd // (SWA_HEADS // SWA_KV_HEADS)
    qq = q_ref[0, :, LANES * (hd // 2):LANES * (hd // 2 + 1)].astype(F32)
    xq = qq if hd % 2 == hk else pltpu.roll(qq, HEAD_DIM, 1)
    return jnp.where((lane >= HEAD_DIM * hk) & (lane < HEAD_DIM * (hk + 1)), xq, 0.0).astype(BF16)


def _swa_place_out(o_even, o_odd, cb, lane):
    placed = []
    for e, o in enumerate((o_even, o_odd)):
        hk = (2 * cb + e) // (SWA_HEADS // SWA_KV_HEADS)
        placed.append(o if e == hk else pltpu.roll(o, HEAD_DIM, 1))
    return jnp.where(lane < HEAD_DIM, placed[0], placed[1]).astype(BF16)


def _swa_running_max(sink_ref, q_ref, k_ref, v_ref, o_ref, qi, tq):
    win = tq + SWA_WINDOW
    wstart = pl.multiple_of(jnp.maximum(qi * tq - SWA_WINDOW, 0), SWA_WINDOW)
    kw = k_ref[0, pl.ds(wstart, win), :]
    vw = v_ref[0, pl.ds(wstart, win), :]
    row = lax.broadcasted_iota(jnp.int32, (tq, win), 0)
    col = lax.broadcasted_iota(jnp.int32, (tq, win), 1)
    dist = (qi * tq + row) - (wstart + col)
    valid = (dist >= 0) & (dist < SWA_WINDOW)
    distf = dist.astype(F32)
    lane = lax.broadcasted_iota(jnp.int32, (tq, LANES), 1)
    for cb in range(SWA_HEADS // 2):
        outs = []
        for e in range(2):
            hd = 2 * cb + e
            s = _dot_nt(_swa_place_q(q_ref, hd, lane), kw) - (2.0 ** (-8.0 * (hd + 1) / SWA_HEADS) * LOG2E) * distf
            s = jnp.where(valid, s, MASKED)
            sink = sink_ref[hd] * LOG2E
            m = jnp.maximum(jnp.max(s, axis=-1, keepdims=True), sink)
            p = jnp.exp2(s - m)
            l = jnp.sum(p, axis=-1, keepdims=True) + jnp.exp2(sink - m)
            outs.append(_dot(p.astype(BF16), vw) / l)
        o_ref[0, :, LANES * cb:LANES * (cb + 1)] = _swa_place_out(outs[0], outs[1], cb, lane)


def _swa_kernel(sink_ref, q_ref, k_ref, v_ref, o_ref, *, tq):
    qi = pl.program_id(1)
    sub = SWA_WINDOW
    win = 2 * sub
    n_sub = tq // sub
    bound = sink_ref[SWA_HEADS]
    lane_q = lax.broadcasted_iota(jnp.int32, (tq, LANES), 1)
    lane = lax.broadcasted_iota(jnp.int32, (sub, LANES), 1)
    row = lax.broadcasted_iota(jnp.int32, (sub, win), 0)
    col = lax.broadcasted_iota(jnp.int32, (sub, win), 1)

    def bias_mask(first_key_offset, slope):
        dist = row + first_key_offset - col
        return jnp.where((dist >= 0) & (dist < SWA_WINDOW), -slope * dist.astype(F32) - bound, MASKED)

    ones_blk = jnp.ones((win, LANES), BF16)
    windows = []
    for r in range(n_sub):
        wstart = qi * tq + sub * (r - 1)
        wstart = pl.multiple_of(jnp.maximum(wstart, 0) if r == 0 else wstart, sub)
        windows.append((k_ref[0, pl.ds(wstart, win), :],
                        jnp.concatenate([v_ref[0, pl.ds(wstart, win), :], ones_blk], axis=1)))

    outs = [[None] * SWA_HEADS for _ in range(n_sub)]
    min_sum = None
    for hd in range(SWA_HEADS):
        slope = 2.0 ** (-8.0 * (hd + 1) / SWA_HEADS) * LOG2E
        bm = bias_mask(sub, slope)
        bm_first = jnp.where(qi == 0, bias_mask(0, slope), bm)
        xq = _swa_place_q(q_ref, hd, lane_q)
        sink_term = jnp.exp2(sink_ref[hd] * LOG2E - bound)
        for r in range(n_sub):
            kw, v_ones = windows[r]
            s = _dot_nt(xq[sub * r:sub * (r + 1)], kw) + (bm_first if r == 0 else bm)
            ov = _dot(jnp.exp2(s).astype(BF16), v_ones)
            l = ov[:, LANES:] + sink_term
            outs[r][hd] = ov[:, :LANES] / l
            min_sum = l if min_sum is None else jnp.minimum(min_sum, l)
    for r in range(n_sub):
        for cb in range(SWA_HEADS // 2):
            o_ref[0, sub * r:sub * (r + 1), LANES * cb:LANES * (cb + 1)] = _swa_place_out(
                outs[r][2 * cb], outs[r][2 * cb + 1], cb, lane)

    @pl.when(jnp.logical_not(jnp.min(min_sum) >= UNDERFLOW_GUARD))
    def _():
        _swa_running_max(sink_ref, q_ref, k_ref, v_ref, o_ref, qi, tq)


def _swa_attention(sinks, qa, ka, va, tq):
    B, S, _ = qa.shape
    return pl.pallas_call(
        functools.partial(_swa_kernel, tq=tq),
        grid=(B, S // tq),
        in_specs=[
            pl.BlockSpec(memory_space=pltpu.SMEM),
            pl.BlockSpec((1, tq, SWA_Q_W), lambda b, i: (b, i, 0)),
            pl.BlockSpec((1, S, SWA_KV_W), lambda b, i: (b, 0, 0)),
            pl.BlockSpec((1, S, SWA_KV_W), lambda b, i: (b, 0, 0)),
        ],
        out_specs=pl.BlockSpec((1, tq, SWA_Q_W), lambda b, i: (b, i, 0)),
        out_shape=jax.ShapeDtypeStruct((B, S, SWA_Q_W), BF16),
        compiler_params=_params(("parallel", "arbitrary")),
        name="swa_attention",
    )(sinks, qa, ka, va)


def _split3(x):
    hi = x.astype(BF16).astype(F32)
    r = x - hi
    mid = r.astype(BF16).astype(F32)
    lo = (r - mid).astype(BF16).astype(F32)
    return hi, mid, lo


def _score_bound(q_gain, k_gain, dim, scale):
    return dim * scale * jnp.max(jnp.abs(q_gain)) * jnp.max(jnp.abs(k_gain))


BLOCKS_PER_MATMUL = 2
BLOCKS_PER_TRIP = 4


def _loop_blocks(n, step, init, tq):
    def trip(t, c):
        for u in range(0, BLOCKS_PER_TRIP, BLOCKS_PER_MATMUL):
            c = step(BLOCKS_PER_TRIP * t + u, c, 0, BLOCKS_PER_MATMUL * tq)
        return c

    full = n // BLOCKS_PER_TRIP
    carry = lax.fori_loop(0, full, trip, init)
    return lax.fori_loop(full * BLOCKS_PER_TRIP, n, step, carry)


def _add_cols(acc, lo, part):
    if lo == 0:
        return acc + part
    return acc + jnp.concatenate([jnp.zeros((acc.shape[0], lo), acc.dtype), part], axis=1)


def _diagonal_spans(tq):
    h = tq // 2
    return ((0, h, 0), (h, h, h)) if h % MXU_WIDTH == 0 else ((0, tq, 0),)


def _causal_t(key_lo, key_n, q_lo, tq):
    shape = (key_n, tq - q_lo)
    return (key_lo + lax.broadcasted_iota(jnp.int32, shape, 0)) <= (q_lo + lax.broadcasted_iota(jnp.int32, shape, 1))


def _causal_mask(tq):
    row = lax.broadcasted_iota(jnp.int32, (tq, tq), 0)
    col = lax.broadcasted_iota(jnp.int32, (tq, tq), 1)
    return row >= col


def _mla_running_max(q_ref, k_ref, vt_ref, e, qi, tq):
    causal = _causal_mask(tq)
    q = q_ref[0, :, LANES * e:LANES * (e + 1)]

    def step(j, carry, masked):
        m, l, acc = carry
        ks = pl.multiple_of(j * tq, tq)
        k = k_ref[0, pl.ds(ks, tq), LANES * e:LANES * (e + 1)]
        vt = vt_ref[0, :, pl.ds(ks, tq)]
        s = _dot_nt(q, k)
        if masked:
            s = jnp.where(causal, s, MASKED)
        m_new = jnp.maximum(m, jnp.max(s, axis=-1, keepdims=True))
        alpha = jnp.exp2(m - m_new)
        p = jnp.exp2(s - m_new)
        l = alpha * l + jnp.sum(p, axis=-1, keepdims=True)
        acc = alpha * acc + _dot_nt(p.astype(BF16), vt)
        return m_new, l, acc

    init = (jnp.full((tq, 1), MASKED, F32), jnp.zeros((tq, 1), F32), jnp.zeros((tq, LANES), F32))
    carry = lax.fori_loop(0, qi, lambda j, c: step(j, c, False), init)
    m, l, acc = step(qi, carry, True)
    return acc / l


def _mla_kernel(q_ref, k_ref, vt_ref, o_ref, *, tq):
    qi = pl.program_id(2)
    lane = lax.broadcasted_iota(jnp.int32, (tq, LANES), 1)
    q = q_ref[0]
    first = lax.broadcasted_iota(jnp.int32, q.shape, 1) < LANES
    zero_q = jnp.zeros_like(q)
    q_aug = (jnp.where(first, q, zero_q), jnp.where(first, zero_q, q))

    def block(j, carry, key_lo=0, key_n=tq, q_lo=0, masked=False):
        ks = pl.multiple_of(j * tq + key_lo, key_n)
        n = tq - q_lo
        q_both = jnp.concatenate([q_aug[0][q_lo:, :], q_aug[1][q_lo:, :]], axis=0)
        st = _dot_nt(k_ref[0, pl.ds(ks, key_n), :], q_both)
        if masked:
            causal = _causal_t(key_lo, key_n, q_lo, tq)
            st = jnp.where(jnp.concatenate([causal, causal], axis=1), st, MASKED)
        pt = jnp.exp2(st)
        part = jnp.sum(pt.reshape(key_n // SUBLANES, SUBLANES, 2 * n), axis=0)
        pv = _dot(vt_ref[0, :, pl.ds(ks, key_n)], pt.astype(BF16))
        return tuple((_add_cols(carry[e][0], q_lo, part[:, n * e:n * (e + 1)]),
                      _add_cols(carry[e][1], q_lo, pv[MLA_V * e:MLA_V * (e + 1), n * e:n * (e + 1)]))
                     for e in range(2))

    zero = (jnp.zeros((SUBLANES, tq), F32), jnp.zeros((MLA_V, tq), F32))
    carry = _loop_blocks(qi, block, (zero, zero), tq)
    for key_lo, key_n, q_lo in _diagonal_spans(tq):
        carry = block(qi, carry, key_lo, key_n, q_lo, masked=True)
    sums = [jnp.sum(lsum, axis=0, keepdims=True) for lsum, _ in carry]
    outs = [acc / l for (_, acc), l in zip(carry, sums)]
    o_ref[0] = jnp.concatenate(outs, axis=0).T.astype(BF16)

    @pl.when(jnp.logical_not(jnp.min(jnp.minimum(sums[0], sums[1])) >= UNDERFLOW_GUARD))
    def _():
        slow = [_mla_running_max(q_ref, k_ref, vt_ref, e, qi, tq) for e in range(2)]
        o_ref[0] = jnp.where(lane < MLA_V, slow[0], slow[1]).astype(BF16)


def _mla_attention(qm, km, vm, tq):
    B, S, _ = qm.shape
    return pl.pallas_call(
        functools.partial(_mla_kernel, tq=tq),
        grid=(B, MLA_HEADS // 2, S // tq),
        in_specs=[
            pl.BlockSpec((1, tq, 2 * LANES), lambda b, h, i: (b, i, h)),
            pl.BlockSpec((1, S, 2 * LANES), lambda b, h, i: (b, 0, h)),
            pl.BlockSpec((1, 2 * MLA_V, S), lambda b, h, i: (b, h, 0)),
        ],
        out_specs=pl.BlockSpec((1, tq, LANES), lambda b, h, i: (b, i, h)),
        out_shape=jax.ShapeDtypeStruct((B, S, MLA_HEADS * MLA_V), BF16),
        compiler_params=_params(("parallel", "parallel", "arbitrary")),
        name="mla_attention",
    )(qm, km, vm)


def _odd_proj_kernel(x_ref, gmix_ref, w_ref, wvt_ref, ones_ref, gq_ref, gk_ref, q_ref, k_ref, vt_ref):
    x = x_ref[0]
    h = _rms(x, gmix_ref[...]).astype(BF16)
    z = _dot(h, w_ref[...])
    ones = ones_ref[...]
    width = DIFF_HEADS * 2 * DIFF_DIM
    piece = gq_ref.shape[1]
    for c in range(0, width, piece):
        q = _group_norm64(z[:, c:c + piece], ones, gq_ref[...]) * (DIFF_DIM ** -0.5 * LOG2E)
        q_ref[0, :, c:c + piece] = q.astype(BF16)
        k = _group_norm64(z[:, width + c:width + c + piece], ones, gk_ref[...])
        k_ref[0, :, c:c + piece] = k.astype(BF16)
    vt_ref[0] = _dot_nt(wvt_ref[...], h).astype(BF16)


def _odd_proj(x, gmix, w_qk, w_vt, ones, gq, gk, ts):
    B, S, D = x.shape
    tok = pl.BlockSpec((1, ts, D), lambda b, i: (b, i, 0))
    consts = [gmix, w_qk, w_vt, ones, gq, gk]
    return pl.pallas_call(
        _odd_proj_kernel,
        grid=(B, S // ts),
        in_specs=[tok] + [_const_spec(c.shape) for c in consts],
        out_specs=[tok, tok, pl.BlockSpec((1, D, ts), lambda b, i: (b, 0, i))],
        out_shape=[jax.ShapeDtypeStruct((B, S, D), BF16)] * 2 + [jax.ShapeDtypeStruct((B, D, S), BF16)],
        compiler_params=_params(("parallel", "parallel")),
        name="odd_proj",
    )(x, *consts)


def _diff_q_parts(q, lane):
    zero = jnp.zeros_like(q)
    return jnp.where(lane < DIFF_DIM, q, zero), jnp.where(lane >= DIFF_DIM, q, zero)


def _diff_running_max(slope, q_ref, k_ref, vt_ref, qi, tq):
    row = lax.broadcasted_iota(jnp.int32, (tq, tq), 0)
    col = lax.broadcasted_iota(jnp.int32, (tq, tq), 1)
    causal = row >= col
    bias = slope * (col - row).astype(F32)
    lane = lax.broadcasted_iota(jnp.int32, (tq, LANES), 1)
    q_parts = _diff_q_parts(q_ref[0], lane)

    def step(j, carry, masked):
        ks = pl.multiple_of(j * tq, tq)
        k = k_ref[0, pl.ds(ks, tq), :]
        vt = vt_ref[0, :, pl.ds(ks, tq)]
        offset = slope * ((j - qi) * tq).astype(F32)
        new = []
        for c in range(2):
            m, l, acc = carry[c]
            s = _dot_nt(q_parts[c], k) + bias
            if masked:
                s = jnp.where(causal, s, MASKED)
            m_new = jnp.maximum(m, jnp.max(s, axis=-1, keepdims=True) + offset)
            alpha = jnp.exp2(m - m_new)
            p = jnp.exp2(s - (m_new - offset))
            l = alpha * l + jnp.sum(p, axis=-1, keepdims=True)
            acc = alpha * acc + _dot_nt(p.astype(BF16), vt)
            new.append((m_new, l, acc))
        return tuple(new)

    one = (jnp.full((tq, 1), MASKED, F32), jnp.zeros((tq, 1), F32), jnp.zeros((tq, LANES), F32))
    carry = lax.fori_loop(0, qi, lambda j, c: step(j, c, False), (one, one))
    (m1, l1, a1), (m2, l2, a2) = step(qi, carry, True)
    return a1 / l1, a2 / l2


def _diff_kernel(slope_ref, qx_ref, qpos_ref, q_ref, k_ref, kx_ref, vt_ref, lam_ref, subln_ref, o_ref, *, tq,
                 lambda_init):
    hd = pl.program_id(1)
    qi = pl.program_id(2)
    lane = lax.broadcasted_iota(jnp.int32, (tq, LANES), 1)
    q_parts = _diff_q_parts(q_ref[0], lane)
    head_row = qx_ref[0]
    qx = jnp.where(lane < 3 * N_SPLIT, head_row,
                   qpos_ref[...].astype(F32) * head_row[:, SLOPE_LANE:SLOPE_LANE + 1]).astype(BF16)
    q_aug = [jnp.concatenate([part, qx], axis=1) for part in q_parts]
    lf = lam_ref[...]
    lam = (jnp.exp(jnp.sum(lf[0:1] * lf[1:2], axis=-1, keepdims=True))
           - jnp.exp(jnp.sum(lf[2:3] * lf[3:4], axis=-1, keepdims=True)) + lambda_init)
    gain = subln_ref[...] * (1.0 - lambda_init)

    def block(j, carry, key_lo=0, key_n=tq, q_lo=0, masked=False):
        ks = pl.multiple_of(j * tq + key_lo, key_n)
        k = jnp.concatenate([k_ref[0, pl.ds(ks, key_n), :], kx_ref[pl.ds(ks, key_n), :]], axis=1)
        vt = vt_ref[0, :, pl.ds(ks, key_n)]
        n = tq - q_lo
        q_both = jnp.concatenate([q_aug[0][q_lo:, :], q_aug[1][q_lo:, :]], axis=0)
        st = _dot_nt(k, q_both)
        if masked:
            causal = _causal_t(key_lo, key_n, q_lo, tq)
            st = jnp.where(jnp.concatenate([causal, causal], axis=1), st, MASKED)
        pt = jnp.exp2(st)
        part = jnp.sum(pt.reshape(key_n // SUBLANES, SUBLANES, 2 * n), axis=0)
        pv = _dot(vt, pt.astype(BF16))
        return tuple((_add_cols(carry[c][0], q_lo, part[:, n * c:n * (c + 1)]),
                      _add_cols(carry[c][1], q_lo, pv[:, n * c:n * (c + 1)])) for c in range(2))

    zero = (jnp.zeros((SUBLANES, tq), F32), jnp.zeros((LANES, tq), F32))
    carry = _loop_blocks(qi, block, (zero, zero), tq)
    for key_lo, key_n, q_lo in _diagonal_spans(tq):
        carry = block(qi, carry, key_lo, key_n, q_lo, masked=True)
    (ls1, a1), (ls2, a2) = carry
    l1 = jnp.sum(ls1, axis=0, keepdims=True)
    l2 = jnp.sum(ls2, axis=0, keepdims=True)

    ot = a1 / l1 - lam * (a2 / l2)
    ot = ot * lax.rsqrt(jnp.mean(ot * ot, axis=0, keepdims=True) + EPS)
    o_ref[0] = (ot.T * gain).astype(BF16)

    @pl.when(jnp.logical_not(jnp.min(jnp.minimum(l1, l2)) >= UNDERFLOW_GUARD))
    def _():
        o1, o2 = _diff_running_max(slope_ref[hd] * LOG2E, q_ref, k_ref, vt_ref, qi, tq)
        o_ref[0] = _rms(o1 - lam * o2, gain).astype(BF16)


SLOPE_LANE = 5 * N_SPLIT


def _alibi_tables(slopes, bound, S):
    assert all(math.log2(s) == int(math.log2(s)) for s in slopes), "ALiBi slopes must be powers of two"
    n = N_SPLIT
    log2e_terms = jnp.concatenate(_split3(jnp.full((1, 1), LOG2E, F32)), axis=1)
    s = jnp.asarray(slopes, F32).reshape(-1, 1)
    head = jnp.concatenate(
        [s * log2e_terms * float(LANES), s * log2e_terms, jnp.broadcast_to(_shift_row(bound, 0)[:, :n], (len(slopes), n)),
         jnp.zeros((len(slopes), SLOPE_LANE - 3 * n), F32), s, jnp.zeros((len(slopes), LANES - SLOPE_LANE - 1), F32)],
        axis=1)
    idx = np.arange(S)
    hi, lo = (idx // LANES)[:, None].astype(np.float32), (idx % LANES)[:, None].astype(np.float32)
    qpos = np.zeros((S, LANES), np.float32)
    qpos[:, 3 * n:4 * n] = hi * LANES
    qpos[:, 4 * n:5 * n] = lo
    kx = jnp.concatenate(
        [jnp.asarray(np.concatenate([np.repeat(hi, n, 1), np.repeat(lo, n, 1), np.ones((S, n), np.float32)], axis=1)),
         jnp.broadcast_to(-log2e_terms, (S, n)), jnp.broadcast_to(-log2e_terms, (S, n)),
         jnp.zeros((S, LANES - 5 * n), F32)], axis=1)
    return head.reshape(-1, 1, LANES), jnp.asarray(qpos, dtype=BF16), kx.astype(BF16)


def _diff_attention(slopes, bound, q, k, vt, lambdas, subln, tq, lambda_init):
    B, S, D = q.shape
    qx, qpos, kx = _alibi_tables(slopes, bound, S)
    return pl.pallas_call(
        functools.partial(_diff_kernel, tq=tq, lambda_init=lambda_init),
        grid=(B, DIFF_HEADS, S // tq),
        in_specs=[
            pl.BlockSpec(memory_space=pltpu.SMEM),
            pl.BlockSpec((1, 1, LANES), lambda b, h, i: (h, 0, 0)),
            pl.BlockSpec((tq, LANES), lambda b, h, i: (i, 0)),
            pl.BlockSpec((1, tq, LANES), lambda b, h, i: (b, i, h)),
            pl.BlockSpec((1, S, LANES), lambda b, h, i: (b, 0, h)),
            _const_spec(kx.shape),
            pl.BlockSpec((1, LANES, S), lambda b, h, i: (b, h, 0)),
            pl.BlockSpec(lambdas.shape, lambda b, h, i: (0, 0)),
            pl.BlockSpec(subln.shape, lambda b, h, i: (0, 0)),
        ],
        out_specs=pl.BlockSpec((1, tq, LANES), lambda b, h, i: (b, i, h)),
        out_shape=jax.ShapeDtypeStruct((B, S, D), BF16),
        compiler_params=_params(("parallel", "parallel", "arbitrary")),
        name="diff_attention",
    )(jnp.asarray(slopes, F32), qx, qpos, q, k, kx, vt, lambdas, subln)


def _mem_kv_kernel(mem_ref, norm_ref, w_ref, gain_ref, k_ref, v_ref):
    h = _rms(mem_ref[0], norm_ref[0]).astype(BF16)
    z = _dot(h, w_ref[0])
    width = MEM_HEADS * MEM_HEAD_DIM
    gain = gain_ref[0]
    for hd in range(MEM_HEADS):
        sl = slice(MEM_HEAD_DIM * hd, MEM_HEAD_DIM * (hd + 1))
        k_ref[0, 0, :, sl] = _rms(z[:, sl], gain).astype(BF16)
    v_ref[0, 0] = z[:, width:2 * width].astype(BF16)


def _mem_kv(mem, norms, w_kv, gains):
    B, M, D = mem.shape
    L = norms.shape[0]
    width = MEM_HEADS * MEM_HEAD_DIM
    out = pl.BlockSpec((1, 1, M, width), lambda l, b: (l, b, 0, 0))
    return pl.pallas_call(
        _mem_kv_kernel,
        grid=(L, B),
        in_specs=[
            pl.BlockSpec((1, M, D), lambda l, b: (b, 0, 0)),
            pl.BlockSpec((1, 1, D), lambda l, b: (l, 0, 0)),
            pl.BlockSpec((1, D, 2 * width), lambda l, b: (l, 0, 0)),
            pl.BlockSpec((1, 1, MEM_HEAD_DIM), lambda l, b: (l, 0, 0)),
        ],
        out_specs=[out, out],
        out_shape=[jax.ShapeDtypeStruct((L, B, M, width), BF16)] * 2,
        compiler_params=_params(("parallel", "parallel")),
        name="mem_kv",
    )(mem, norms, w_kv, gains)


def _post_kernel(*refs, n_attn):
    x_ref = refs[0]
    attn_refs = refs[1:1 + n_attn]
    wo_refs = refs[1 + n_attn:1 + 2 * n_attn]
    (mqn_ref, wq_ref, qg_ref, mk_ref, mv_ref, wmo_ref, fn_ref, wg_ref, wu_ref, wd_ref, o_ref) = refs[1 + 2 * n_attn:]
    x = x_ref[0]
    for a_ref, w_ref in zip(attn_refs, wo_refs):
        x = x + _dot(a_ref[0], w_ref[...])

    q = _dot(_rms(x, mqn_ref[...]).astype(BF16), wq_ref[...])
    heads = []
    for hd in range(MEM_HEADS):
        sl = slice(MEM_HEAD_DIM * hd, MEM_HEAD_DIM * (hd + 1))
        qh = (_rms(q[:, sl], qg_ref[...]) * (MEM_HEAD_DIM ** -0.5 * LOG2E)).astype(BF16)
        s = _dot_nt(qh, mk_ref[0, :, sl])
        p = jnp.exp2(s - jnp.max(s, axis=-1, keepdims=True))
        l = jnp.sum(p, axis=-1, keepdims=True)
        heads.append((_dot(p.astype(BF16), mv_ref[0, :, sl]) / l).astype(BF16))
    x = x + _dot(jnp.concatenate(heads, axis=-1), wmo_ref[...])

    hf = _rms(x, fn_ref[...]).astype(BF16)
    g = _dot(hf, wg_ref[...])
    u = _dot(hf, wu_ref[...])
    act = (g * jax.nn.sigmoid(g) * u).astype(BF16)
    o_ref[0] = x + _dot(act, wd_ref[...])


def _post_block(x, attns, wos, mqn, wq, qg, mk, mv, wmo, fn, wg, wu, wd, ts):
    B, S, D = x.shape
    n = len(attns)
    tok = lambda c: pl.BlockSpec((1, ts, c), lambda b, i: (b, i, 0))
    memspec = pl.BlockSpec((1,) + mk.shape[1:], lambda b, i: (b, 0, 0))
    consts_a = list(wos) + [mqn, wq, qg]
    consts_b = [wmo, fn, wg, wu, wd]
    return pl.pallas_call(
        functools.partial(_post_kernel, n_attn=n),
        grid=(B, S // ts),
        in_specs=([tok(D)] + [tok(a.shape[-1]) for a in attns] + [_const_spec(c.shape) for c in consts_a]
                  + [memspec, memspec] + [_const_spec(c.shape) for c in consts_b]),
        out_specs=tok(D),
        out_shape=jax.ShapeDtypeStruct((B, S, D), F32),
        compiler_params=_params(("parallel", "parallel")),
        name="post_block",
    )(x, *attns, *consts_a, mk, mv, *consts_b)


def _row(v):
    return v.reshape(1, -1).astype(F32)


def _shift_row(bound, first_lane):
    terms = jnp.stack(_split3(-jnp.asarray(bound, F32))).reshape(1, N_SPLIT)
    return jnp.pad(terms, ((0, 0), (first_lane, LANES - first_lane - N_SPLIT)))


def _mla_lane_perm():
    half = MLA_ROPE // 2
    lanes = np.arange(MLA_QK)
    return np.where(lanes < half, MLA_NOPE + lanes,
                    np.where(lanes < ROPE_HI_LANE, lanes - half,
                             np.where(lanes < ROPE_HI_LANE + half, lanes + half, lanes - MLA_ROPE)))


def _even_weights(w_in, w_uq, w_ukv, q_gain, k_gain):
    perm = _mla_lane_perm()
    is_nope = perm < MLA_NOPE
    pad = LANES - MLA_QK

    def head_cols(w_head, live):
        return jnp.pad(jnp.where(live[None, :], w_head[:, perm], 0.0), ((0, 0), (0, pad)))

    c_kr = EVEN_COLS[-1]
    kr_src = jnp.concatenate([jnp.zeros((w_in.shape[0], MLA_NOPE), w_in.dtype), w_in[:, c_kr:c_kr + MLA_ROPE]], axis=1)
    w1 = jnp.concatenate([w_in[:, :c_kr], head_cols(kr_src, ~is_nope)], axis=1).astype(BF16)
    wuq = jnp.concatenate(
        [head_cols(w_uq[:, MLA_QK * h:MLA_QK * (h + 1)], np.ones_like(is_nope)) for h in range(MLA_HEADS)],
        axis=1).astype(BF16)
    kv_w = MLA_NOPE + MLA_V
    rope_zeros = jnp.zeros((w_ukv.shape[0], MLA_ROPE), w_ukv.dtype)
    wuk = jnp.concatenate(
        [head_cols(jnp.concatenate([w_ukv[:, kv_w * h:kv_w * h + MLA_NOPE], rope_zeros], axis=1), is_nope)
         for h in range(MLA_HEADS)], axis=1).astype(BF16)
    wuv = jnp.concatenate(
        [w_ukv[:, kv_w * h + MLA_NOPE:kv_w * (h + 1)] for h in range(MLA_HEADS)], axis=1).astype(BF16).T
    gmq = _row(jnp.pad(q_gain[perm], (0, pad)))
    gmk = _row(jnp.pad(k_gain[perm], (0, pad)))
    return w1, wuq, wuk, wuv, gmq, gmk


def _rope_freqs():
    half = MLA_ROPE // 2
    freqs = (np.float32(ROPE_THETA) ** (-np.arange(half, dtype=np.float32) / np.float32(half))).astype(np.float32)
    return jnp.asarray(freqs.reshape(half, 1))


def _ones_blocks(n):
    g = np.arange(n) // HEAD_DIM
    return jnp.asarray((g[:, None] == g[None, :]).astype(np.float32), dtype=BF16)


def _tile_sizes(S):
    token_tile = min(2 * MXU_WIDTH, S)
    return token_tile, token_tile, min(2 * MXU_WIDTH, S - SWA_WINDOW), min(4 * MXU_WIDTH, S)


def kernel(x, mem, positions, mix_norm, ev_w_in, ev_swa_q_gain, ev_swa_k_gain, ev_sinks, ev_q_latent_norm,
           ev_kv_latent_norm, ev_w_uq, ev_w_ukv, ev_mla_q_gain, ev_mla_k_gain, ev_w_out, od_w_qkv, od_q_gain,
           od_k_gain, od_lambda, od_subln, od_w_out, mem_q_norm, mem_kv_norm, mem_w_q, mem_w_kv, mem_q_gain,
           mem_k_gain, mem_w_out, ffn_norm, ffn_w_gate, ffn_w_up, ffn_w_down):
    B, S, D = x.shape
    depth = mix_norm.shape[0]
    ts_proj, ts_post, tq_swa, tq_flash = _tile_sizes(S)

    ones = _ones_blocks(MXU_WIDTH)
    pos3 = positions.reshape(B, 1, S)
    inv = _rope_freqs()
    mem_k, mem_v = _mem_kv(mem, mem_kv_norm.reshape(depth, 1, D), mem_w_kv.astype(BF16),
                           mem_k_gain.reshape(depth, 1, MEM_HEAD_DIM))
    diff_slopes = [2.0 ** (-8.0 * (i + 1) / DIFF_HEADS) for i in range(DIFF_HEADS)]

    for l in range(depth):
        if l % 2 == 0:
            e = l // 2
            w1, wuq, wuk, wuv, gmq, gmk = _even_weights(ev_w_in[e], ev_w_uq[e], ev_w_ukv[e], ev_mla_q_gain[e],
                                                        ev_mla_k_gain[e])
            qa, ka, va, qm, km, vm = _even_proj(
                x, pos3, _row(mix_norm[l]), w1, ones, _row(jnp.tile(ev_swa_q_gain[e], SWA_HEADS)),
                _row(jnp.tile(ev_swa_k_gain[e], SWA_KV_HEADS)), _row(ev_q_latent_norm[e]),
                _row(ev_kv_latent_norm[e]), wuq, wuk, wuv, gmq, gmk,
                _shift_row(_score_bound(ev_mla_q_gain[e], ev_mla_k_gain[e], MLA_QK, MLA_QK ** -0.5 * LOG2E), MLA_QK),
                inv, ts_proj)
            swa_bound = _score_bound(ev_swa_q_gain[e], ev_swa_k_gain[e], HEAD_DIM, HEAD_DIM ** -0.5 * LOG2E)
            swa_scalars = jnp.concatenate([ev_sinks[e].astype(F32), swa_bound.reshape(1).astype(F32)])
            out_a = _swa_attention(swa_scalars, qa, ka, va, tq_swa)
            out_b = _mla_attention(qm, km, vm, tq_flash)
            wo = ev_w_out[e].astype(BF16)
            attns = [out_a, out_b]
            wos = [wo[:SWA_Q_W], wo[SWA_Q_W:]]
        else:
            o = l // 2
            lambda_init = 0.8 - 0.6 * math.exp(-0.3 * l)
            n_grp = DIFF_HEADS * 2
            w_qkv = od_w_qkv[o].astype(BF16)
            q, k, vt = _odd_proj(x, _row(mix_norm[l]), w_qkv[:, :2 * D], w_qkv[:, 2 * D:].T, ones,
                                 _row(jnp.tile(od_q_gain[o], n_grp // 2)), _row(jnp.tile(od_k_gain[o], n_grp // 2)),
                                 ts_proj)
            bound = _score_bound(od_q_gain[o], od_k_gain[o], DIFF_DIM, DIFF_DIM ** -0.5 * LOG2E)
            out_d = _diff_attention(diff_slopes, bound, q, k, vt, od_lambda[o].astype(F32), _row(od_subln[o]),
                                    tq_flash, lambda_init)
            attns = [out_d]
            wos = [od_w_out[o].astype(BF16)]
        x = _post_block(x, attns, wos, _row(mem_q_norm[l]), mem_w_q[l].astype(BF16), _row(mem_q_gain[l]),
                        mem_k[l], mem_v[l], mem_w_out[l].astype(BF16), _row(ffn_norm[l]),
                        ffn_w_gate[l].astype(BF16), ffn_w_up[l].astype(BF16), ffn_w_down[l].astype(BF16), ts_post)
    return x
```

```python
import functools
import math

import numpy as np
import jax
import jax.numpy as jnp
from jax import lax
from jax.experimental import pallas as pl
from jax.experimental.pallas import tpu as pltpu

F32 = jnp.float32
BF16 = jnp.bfloat16

EPS = 1e-6
MASKED = -1e30
LOG2E = 1.4426950408889634
ROPE_THETA = 10000.0

LANES = 128
SUBLANES = 8
MXU_WIDTH = 256
HEAD_DIM = 64
SWA_HEADS = 8
SWA_KV_HEADS = 2
SWA_WINDOW = 128
MLA_HEADS = 8
MLA_NOPE = 64
MLA_ROPE = 32
MLA_QK = MLA_NOPE + MLA_ROPE
MLA_V = 64
ROPE_HI_LANE = 64
DIFF_HEADS = 8
DIFF_DIM = 64
MEM_HEADS = 4
MEM_HEAD_DIM = 128
MLA_Q_RANK = 256
MLA_KV_RANK = 128
SWA_Q_W = SWA_HEADS * HEAD_DIM
SWA_KV_W = SWA_KV_HEADS * HEAD_DIM
EVEN_COLS = tuple(np.cumsum([SWA_Q_W, SWA_KV_W, SWA_KV_W, MLA_Q_RANK, MLA_KV_RANK]).tolist())

VMEM_LIMIT = 56 * 1024 * 1024

UNDERFLOW_GUARD = 2.0 ** -80
N_SPLIT = 3


def _params(sem):
    return pltpu.CompilerParams(dimension_semantics=sem, vmem_limit_bytes=VMEM_LIMIT)


def _const_spec(shape):
    nd = len(shape)
    return pl.BlockSpec(shape, lambda *_: (0,) * nd, pipeline_mode=pl.Buffered(1))


def _rms(x, g):
    ms = jnp.mean(x * x, axis=-1, keepdims=True)
    return x * lax.rsqrt(ms + EPS) * g


def _dot(a, b):
    return jnp.dot(a, b, preferred_element_type=F32)


def _dot_nt(a, b):
    return lax.dot_general(a, b, (((1,), (1,)), ((), ())), preferred_element_type=F32)


def _group_sumsq(x, ones_blk):
    x2 = x * x
    hi = x2.astype(BF16)
    lo = (x2 - hi.astype(F32)).astype(BF16)
    return _dot(hi, ones_blk) + _dot(lo, ones_blk)


def _group_norm64(x, ones_blk, gain):
    chunk = min(ones_blk.shape[0], x.shape[1])
    ssq = jnp.concatenate([_group_sumsq(x[:, c:c + chunk], ones_blk[0:chunk, 0:chunk])
                           for c in range(0, x.shape[1], chunk)], axis=1)
    return x * lax.rsqrt(ssq * (1.0 / HEAD_DIM) + EPS) * gain


def _even_proj_kernel(x_ref, pos_ref, gmix_ref, w1_ref, ones_ref, gq_ref, gk_ref, qln_ref, kvln_ref,
                      wuq_ref, wuk_ref, wuv_ref, gmq_ref, gmk_ref, qshift_ref, inv_ref,
                      qa_ref, ka_ref, va_ref, qm_ref, km_ref, vm_ref):
    x = x_ref[0]
    h = _rms(x, gmix_ref[...]).astype(BF16)
    z = _dot(h, w1_ref[...])
    c_ka, c_va, c_cq, c_ckv, c_kr = EVEN_COLS
    ones = ones_ref[...]
    qa = _group_norm64(z[:, 0:c_ka], ones, gq_ref[...]) * (HEAD_DIM ** -0.5 * LOG2E)
    qa_ref[0] = qa.astype(BF16)
    ka = _group_norm64(z[:, c_ka:c_va], ones[0:SWA_KV_W, 0:SWA_KV_W], gk_ref[...])
    ka_ref[0] = ka.astype(BF16)
    va_ref[0] = z[:, c_va:c_cq].astype(BF16)

    cqn = _rms(z[:, c_cq:c_ckv], qln_ref[...]).astype(BF16)
    ckvn = _rms(z[:, c_ckv:c_kr], kvln_ref[...]).astype(BF16)
    kr = z[:, c_kr:c_kr + LANES]
    qf = _dot(cqn, wuq_ref[...])
    kn = _dot(ckvn, wuk_ref[...])
    vm_ref[0] = _dot_nt(wuv_ref[...], ckvn).astype(BF16)

    half = MLA_ROPE // 2
    ang = inv_ref[...] * pos_ref[0].astype(F32)
    pad = jnp.zeros((LANES - half, ang.shape[1]), F32)
    cos = jnp.concatenate([jnp.cos(ang), pad], axis=0).T
    sin = jnp.concatenate([jnp.sin(ang), pad], axis=0).T
    lane = lax.broadcasted_iota(jnp.int32, cos.shape, 1)
    in_lo = lane < half
    in_hi = (lane >= ROPE_HI_LANE) & (lane < ROPE_HI_LANE + half)
    cos_t = jnp.where(in_lo, cos, jnp.where(in_hi, pltpu.roll(cos, ROPE_HI_LANE, 1), 1.0))
    sin_t = jnp.where(in_lo, -sin, jnp.where(in_hi, pltpu.roll(sin, ROPE_HI_LANE, 1), 0.0))

    pair = 2 * LANES
    gi = lax.broadcasted_iota(jnp.int32, (pair, pair), 0) // LANES
    gj = lax.broadcasted_iota(jnp.int32, (pair, pair), 1) // LANES
    head_ones = (gi == gj).astype(BF16)

    def norm_rope(xp, gain):
        r = lax.rsqrt(_group_sumsq(xp, head_ones) * (1.0 / MLA_QK) + EPS)
        xn = xp * r * gain
        halves = []
        for t in range(2):
            xh = xn[:, LANES * t:LANES * (t + 1)]
            halves.append(xh * cos_t + pltpu.roll(xh, ROPE_HI_LANE, 1) * sin_t)
        return halves

    gmq = jnp.concatenate([gmq_ref[...]] * 2, axis=1)
    gmk = jnp.concatenate([gmk_ref[...]] * 2, axis=1)
    kr2 = jnp.concatenate([kr, kr], axis=1)
    shift_lanes = (lane >= MLA_QK) & (lane < MLA_QK + N_SPLIT)
    qshift = qshift_ref[...]
    for hp in range(MLA_HEADS // 2):
        sl = slice(pair * hp, pair * (hp + 1))
        q_heads = norm_rope(qf[:, sl], gmq)
        k_heads = norm_rope(kn[:, sl] + kr2, gmk)
        for t in range(2):
            hl = slice(pair * hp + LANES * t, pair * hp + LANES * (t + 1))
            qm_ref[0, :, hl] = jnp.where(shift_lanes, qshift, q_heads[t] * (MLA_QK ** -0.5 * LOG2E)).astype(BF16)
            km_ref[0, :, hl] = jnp.where(shift_lanes, 1.0, k_heads[t]).astype(BF16)


def _even_proj(x, pos3, gmix, w1, ones, gq, gk, qln, kvln, wuq, wuk, wuv, gmq, gmk, qshift, inv, ts):
    B, S, D = x.shape
    grid = (B, S // ts)
    tok = lambda c: pl.BlockSpec((1, ts, c), lambda b, i: (b, i, 0))
    consts = [gmix, w1, ones, gq, gk, qln, kvln, wuq, wuk, wuv, gmq, gmk, qshift, inv]
    out_cols = [SWA_Q_W, SWA_KV_W, SWA_KV_W, MLA_HEADS * LANES, MLA_HEADS * LANES]
    v_rows = MLA_HEADS * MLA_V
    return pl.pallas_call(
        _even_proj_kernel,
        grid=grid,
        in_specs=([tok(D), pl.BlockSpec((1, 1, ts), lambda b, i: (b, 0, i))]
                  + [_const_spec(c.shape) for c in consts]),
        out_specs=[tok(c) for c in out_cols] + [pl.BlockSpec((1, v_rows, ts), lambda b, i: (b, 0, i))],
        out_shape=([jax.ShapeDtypeStruct((B, S, c), BF16) for c in out_cols]
                   + [jax.ShapeDtypeStruct((B, v_rows, S), BF16)]),
        compiler_params=_params(("parallel", "parallel")),
        name="even_proj",
    )(x, pos3, *consts)


def _swa_place_q(q_ref, hd, lane):
    hk = hd // (SWA_HEADS // SWA_KV_HEADS)
    qq = q_ref[0, :, LANES * (hd // 2):LANES * (hd // 2 + 1)].astype(F32)
    xq = qq if hd % 2 == hk else pltpu.roll(qq, HEAD_DIM, 1)
    return jnp.where((lane >= HEAD_DIM * hk) & (lane < HEAD_DIM * (hk + 1)), xq, 0.0).astype(BF16)


def _swa_place_out(o_even, o_odd, cb, lane):
    placed = []
    for e, o in enumerate((o_even, o_odd)):
        hk = (2 * cb + e) // (SWA_HEADS // SWA_KV_HEADS)
        placed.append(o if e == hk else pltpu.roll(o, HEAD_DIM, 1))
    return jnp.where(lane < HEAD_DIM, placed[0], placed[1]).astype(BF16)


def _swa_running_max(sink_ref, q_ref, k_ref, v_ref, o_ref, qi, tq):
    win = tq + SWA_WINDOW
    wstart = pl.multiple_of(jnp.maximum(qi * tq - SWA_WINDOW, 0), SWA_WINDOW)
    kw = k_ref[0, pl.ds(wstart, win), :]
    vw = v_ref[0, pl.ds(wstart, win), :]
    row = lax.broadcasted_iota(jnp.int32, (tq, win), 0)
    col = lax.broadcasted_iota(jnp.int32, (tq, win), 1)
    dist = (qi * tq + row) - (wstart + col)
    valid = (dist >= 0) & (dist < SWA_WINDOW)
    distf = dist.astype(F32)
    lane = lax.broadcasted_iota(jnp.int32, (tq, LANES), 1)
    for cb in range(SWA_HEADS // 2):
        outs = []
        for e in range(2):
            hd = 2 * cb + e
            s = _dot_nt(_swa_place_q(q_ref, hd, lane), kw) - (2.0 ** (-8.0 * (hd + 1) / SWA_HEADS) * LOG2E) * distf
            s = jnp.where(valid, s, MASKED)
            sink = sink_ref[hd] * LOG2E
            m = jnp.maximum(jnp.max(s, axis=-1, keepdims=True), sink)
            p = jnp.exp2(s - m)
            l = jnp.sum(p, axis=-1, keepdims=True) + jnp.exp2(sink - m)
            outs.append(_dot(p.astype(BF16), vw) / l)
        o_ref[0, :, LANES * cb:LANES * (cb + 1)] = _swa_place_out(outs[0], outs[1], cb, lane)


def _swa_kernel(sink_ref, q_ref, k_ref, v_ref, o_ref, *, tq):
    qi = pl.program_id(1)
    sub = SWA_WINDOW
    win = 2 * sub
    n_sub = tq // sub
    bound = sink_ref[SWA_HEADS]
    lane_q = lax.broadcasted_iota(jnp.int32, (tq, LANES), 1)
    lane = lax.broadcasted_iota(jnp.int32, (sub, LANES), 1)
    row = lax.broadcasted_iota(jnp.int32, (sub, win), 0)
    col = lax.broadcasted_iota(jnp.int32, (sub, win), 1)

    def bias_mask(first_key_offset, slope):
        dist = row + first_key_offset - col
        return jnp.where((dist >= 0) & (dist < SWA_WINDOW), -slope * dist.astype(F32) - bound, MASKED)

    ones_blk = jnp.ones((win, LANES), BF16)
    windows = []
    for r in range(n_sub):
        wstart = qi * tq + sub * (r - 1)
        wstart = pl.multiple_of(jnp.maximum(wstart, 0) if r == 0 else wstart, sub)
        windows.append((k_ref[0, pl.ds(wstart, win), :],
                        jnp.concatenate([v_ref[0, pl.ds(wstart, win), :], ones_blk], axis=1)))

    outs = [[None] * SWA_HEADS for _ in range(n_sub)]
    min_sum = None
    for hd in range(SWA_HEADS):
        slope = 2.0 ** (-8.0 * (hd + 1) / SWA_HEADS) * LOG2E
        bm = bias_mask(sub, slope)
        bm_first = jnp.where(qi == 0, bias_mask(0, slope), bm)
        xq = _swa_place_q(q_ref, hd, lane_q)
        sink_term = jnp.exp2(sink_ref[hd] * LOG2E - bound)
        for r in range(n_sub):
            kw, v_ones = windows[r]
            s = _dot_nt(xq[sub * r:sub * (r + 1)], kw) + (bm_first if r == 0 else bm)
            ov = _dot(jnp.exp2(s).astype(BF16), v_ones)
            l = ov[:, LANES:] + sink_term
            outs[r][hd] = ov[:, :LANES] / l
            min_sum = l if min_sum is None else jnp.minimum(min_sum, l)
    for r in range(n_sub):
        for cb in range(SWA_HEADS // 2):
            o_ref[0, sub * r:sub * (r + 1), LANES * cb:LANES * (cb + 1)] = _swa_place_out(
                outs[r][2 * cb], outs[r][2 * cb + 1], cb, lane)

    @pl.when(jnp.logical_not(jnp.min(min_sum) >= UNDERFLOW_GUARD))
    def _():
        _swa_running_max(sink_ref, q_ref, k_ref, v_ref, o_ref, qi, tq)


def _swa_attention(sinks, qa, ka, va, tq):
    B, S, _ = qa.shape
    return pl.pallas_call(
        functools.partial(_swa_kernel, tq=tq),
        grid=(B, S // tq),
        in_specs=[
            pl.BlockSpec(memory_space=pltpu.SMEM),
            pl.BlockSpec((1, tq, SWA_Q_W), lambda b, i: (b, i, 0)),
            pl.BlockSpec((1, S, SWA_KV_W), lambda b, i: (b, 0, 0)),
            pl.BlockSpec((1, S, SWA_KV_W), lambda b, i: (b, 0, 0)),
        ],
        out_specs=pl.BlockSpec((1, tq, SWA_Q_W), lambda b, i: (b, i, 0)),
        out_shape=jax.ShapeDtypeStruct((B, S, SWA_Q_W), BF16),
        compiler_params=_params(("parallel", "arbitrary")),
        name="swa_attention",
    )(sinks, qa, ka, va)


def _split3(x):
    hi = x.astype(BF16).astype(F32)
    r = x - hi
    mid = r.astype(BF16).astype(F32)
    lo = (r - mid).astype(BF16).astype(F32)
    return hi, mid, lo


def _score_bound(q_gain, k_gain, dim, scale):
    return dim * scale * jnp.max(jnp.abs(q_gain)) * jnp.max(jnp.abs(k_gain))


BLOCKS_PER_MATMUL = 2
BLOCKS_PER_TRIP = 4


def _loop_blocks(n, step, init, tq):
    def trip(t, c):
        for u in range(0, BLOCKS_PER_TRIP, BLOCKS_PER_MATMUL):
            c = step(BLOCKS_PER_TRIP * t + u, c, 0, BLOCKS_PER_MATMUL * tq)
        return c

    full = n // BLOCKS_PER_TRIP
    carry = lax.fori_loop(0, full, trip, init)
    return lax.fori_loop(full * BLOCKS_PER_TRIP, n, step, carry)


def _add_cols(acc, lo, part):
    if lo == 0:
        return acc + part
    return acc + jnp.concatenate([jnp.zeros((acc.shape[0], lo), acc.dtype), part], axis=1)


def _diagonal_spans(tq):
    h = tq // 2
    return ((0, h, 0), (h, h, h)) if h % MXU_WIDTH == 0 else ((0, tq, 0),)


def _causal_t(key_lo, key_n, q_lo, tq):
    shape = (key_n, tq - q_lo)
    return (key_lo + lax.broadcasted_iota(jnp.int32, shape, 0)) <= (q_lo + lax.broadcasted_iota(jnp.int32, shape, 1))


def _causal_mask(tq):
    row = lax.broadcasted_iota(jnp.int32, (tq, tq), 0)
    col = lax.broadcasted_iota(jnp.int32, (tq, tq), 1)
    return row >= col


def _mla_running_max(q_ref, k_ref, vt_ref, e, qi, tq):
    causal = _causal_mask(tq)
    q = q_ref[0, :, LANES * e:LANES * (e + 1)]

    def step(j, carry, masked):
        m, l, acc = carry
        ks = pl.multiple_of(j * tq, tq)
        k = k_ref[0, pl.ds(ks, tq), LANES * e:LANES * (e + 1)]
        vt = vt_ref[0, :, pl.ds(ks, tq)]
        s = _dot_nt(q, k)
        if masked:
            s = jnp.where(causal, s, MASKED)
        m_new = jnp.maximum(m, jnp.max(s, axis=-1, keepdims=True))
        alpha = jnp.exp2(m - m_new)
        p = jnp.exp2(s - m_new)
        l = alpha * l + jnp.sum(p, axis=-1, keepdims=True)
        acc = alpha * acc + _dot_nt(p.astype(BF16), vt)
        return m_new, l, acc

    init = (jnp.full((tq, 1), MASKED, F32), jnp.zeros((tq, 1), F32), jnp.zeros((tq, LANES), F32))
    carry = lax.fori_loop(0, qi, lambda j, c: step(j, c, False), init)
    m, l, acc = step(qi, carry, True)
    return acc / l


def _mla_kernel(q_ref, k_ref, vt_ref, o_ref, *, tq):
    qi = pl.program_id(2)
    lane = lax.broadcasted_iota(jnp.int32, (tq, LANES), 1)
    q = q_ref[0]
    first = lax.broadcasted_iota(jnp.int32, q.shape, 1) < LANES
    zero_q = jnp.zeros_like(q)
    q_aug = (jnp.where(first, q, zero_q), jnp.where(first, zero_q, q))

    def block(j, carry, key_lo=0, key_n=tq, q_lo=0, masked=False):
        ks = pl.multiple_of(j * tq + key_lo, key_n)
        n = tq - q_lo
        q_both = jnp.concatenate([q_aug[0][q_lo:, :], q_aug[1][q_lo:, :]], axis=0)
        st = _dot_nt(k_ref[0, pl.ds(ks, key_n), :], q_both)
        if masked:
            causal = _causal_t(key_lo, key_n, q_lo, tq)
            st = jnp.where(jnp.concatenate([causal, causal], axis=1), st, MASKED)
        pt = jnp.exp2(st)
        part = jnp.sum(pt.reshape(key_n // SUBLANES, SUBLANES, 2 * n), axis=0)
        pv = _dot(vt_ref[0, :, pl.ds(ks, key_n)], pt.astype(BF16))
        return tuple((_add_cols(carry[e][0], q_lo, part[:, n * e:n * (e + 1)]),
                      _add_cols(carry[e][1], q_lo, pv[MLA_V * e:MLA_V * (e + 1), n * e:n * (e + 1)]))
                     for e in range(2))

    zero = (jnp.zeros((SUBLANES, tq), F32), jnp.zeros((MLA_V, tq), F32))
    carry = _loop_blocks(qi, block, (zero, zero), tq)
    for key_lo, key_n, q_lo in _diagonal_spans(tq):
        carry = block(qi, carry, key_lo, key_n, q_lo, masked=True)
    sums = [jnp.sum(lsum, axis=0, keepdims=True) for lsum, _ in carry]
    outs = [acc / l for (_, acc), l in zip(carry, sums)]
    o_ref[0] = jnp.concatenate(outs, axis=0).T.astype(BF16)

    @pl.when(jnp.logical_not(jnp.min(jnp.minimum(sums[0], sums[1])) >= UNDERFLOW_GUARD))
    def _():
        slow = [_mla_running_max(q_ref, k_ref, vt_ref, e, qi, tq) for e in range(2)]
        o_ref[0] = jnp.where(lane < MLA_V, slow[0], slow[1]).astype(BF16)


def _mla_attention(qm, km, vm, tq):
    B, S, _ = qm.shape
    return pl.pallas_call(
        functools.partial(_mla_kernel, tq=tq),
        grid=(B, MLA_HEADS // 2, S // tq),
        in_specs=[
            pl.BlockSpec((1, tq, 2 * LANES), lambda b, h, i: (b, i, h)),
            pl.BlockSpec((1, S, 2 * LANES), lambda b, h, i: (b, 0, h)),
            pl.BlockSpec((1, 2 * MLA_V, S), lambda b, h, i: (b, h, 0)),
        ],
        out_specs=pl.BlockSpec((1, tq, LANES), lambda b, h, i: (b, i, h)),
        out_shape=jax.ShapeDtypeStruct((B, S, MLA_HEADS * MLA_V), BF16),
        compiler_params=_params(("parallel", "parallel", "arbitrary")),
        name="mla_attention",
    )(qm, km, vm)


def _odd_proj_kernel(x_ref, gmix_ref, w_ref, wvt_ref, ones_ref, gq_ref, gk_ref, q_ref, k_ref, vt_ref):
    x = x_ref[0]
    h = _rms(x, gmix_ref[...]).astype(BF16)
    z = _dot(h, w_ref[...])
    ones = ones_ref[...]
    width = DIFF_HEADS * 2 * DIFF_DIM
    piece = gq_ref.shape[1]
    for c in range(0, width, piece):
        q = _group_norm64(z[:, c:c + piece], ones, gq_ref[...]) * (DIFF_DIM ** -0.5 * LOG2E)
        q_ref[0, :, c:c + piece] = q.astype(BF16)
        k = _group_norm64(z[:, width + c:width + c + piece], ones, gk_ref[...])
        k_ref[0, :, c:c + piece] = k.astype(BF16)
    vt_ref[0] = _dot_nt(wvt_ref[...], h).astype(BF16)


def _odd_proj(x, gmix, w_qk, w_vt, ones, gq, gk, ts):
    B, S, D = x.shape
    tok = pl.BlockSpec((1, ts, D), lambda b, i: (b, i, 0))
    consts = [gmix, w_qk, w_vt, ones, gq, gk]
    return pl.pallas_call(
        _odd_proj_kernel,
        grid=(B, S // ts),
        in_specs=[tok] + [_const_spec(c.shape) for c in consts],
        out_specs=[tok, tok, pl.BlockSpec((1, D, ts), lambda b, i: (b, 0, i))],
        out_shape=[jax.ShapeDtypeStruct((B, S, D), BF16)] * 2 + [jax.ShapeDtypeStruct((B, D, S), BF16)],
        compiler_params=_params(("parallel", "parallel")),
        name="odd_proj",
    )(x, *consts)


def _diff_q_parts(q, lane):
    zero = jnp.zeros_like(q)
    return jnp.where(lane < DIFF_DIM, q, zero), jnp.where(lane >= DIFF_DIM, q, zero)


def _diff_running_max(slope, q_ref, k_ref, vt_ref, qi, tq):
    row = lax.broadcasted_iota(jnp.int32, (tq, tq), 0)
    col = lax.broadcasted_iota(jnp.int32, (tq, tq), 1)
    causal = row >= col
    bias = slope * (col - row).astype(F32)
    lane = lax.broadcasted_iota(jnp.int32, (tq, LANES), 1)
    q_parts = _diff_q_parts(q_ref[0], lane)

    def step(j, carry, masked):
        ks = pl.multiple_of(j * tq, tq)
        k = k_ref[0, pl.ds(ks, tq), :]
        vt = vt_ref[0, :, pl.ds(ks, tq)]
        offset = slope * ((j - qi) * tq).astype(F32)
        new = []
        for c in range(2):
            m, l, acc = carry[c]
            s = _dot_nt(q_parts[c], k) + bias
            if masked:
                s = jnp.where(causal, s, MASKED)
            m_new = jnp.maximum(m, jnp.max(s, axis=-1, keepdims=True) + offset)
            alpha = jnp.exp2(m - m_new)
            p = jnp.exp2(s - (m_new - offset))
            l = alpha * l + jnp.sum(p, axis=-1, keepdims=True)
            acc = alpha * acc + _dot_nt(p.astype(BF16), vt)
            new.append((m_new, l, acc))
        return tuple(new)

    one = (jnp.full((tq, 1), MASKED, F32), jnp.zeros((tq, 1), F32), jnp.zeros((tq, LANES), F32))
    carry = lax.fori_loop(0, qi, lambda j, c: step(j, c, False), (one, one))
    (m1, l1, a1), (m2, l2, a2) = step(qi, carry, True)
    return a1 / l1, a2 / l2


def _diff_kernel(slope_ref, qx_ref, qpos_ref, q_ref, k_ref, kx_ref, vt_ref, lam_ref, subln_ref, o_ref, *, tq,
                 lambda_init):
    hd = pl.program_id(1)
    qi = pl.program_id(2)
    lane = lax.broadcasted_iota(jnp.int32, (tq, LANES), 1)
    q_parts = _diff_q_parts(q_ref[0], lane)
    head_row = qx_ref[0]
    qx = jnp.where(lane < 3 * N_SPLIT, head_row,
                   qpos_ref[...].astype(F32) * head_row[:, SLOPE_LANE:SLOPE_LANE + 1]).astype(BF16)
    q_aug = [jnp.concatenate([part, qx], axis=1) for part in q_parts]
    lf = lam_ref[...]
    lam = (jnp.exp(jnp.sum(lf[0:1] * lf[1:2], axis=-1, keepdims=True))
           - jnp.exp(jnp.sum(lf[2:3] * lf[3:4], axis=-1, keepdims=True)) + lambda_init)
    gain = subln_ref[...] * (1.0 - lambda_init)

    def block(j, carry, key_lo=0, key_n=tq, q_lo=0, masked=False):
        ks = pl.multiple_of(j * tq + key_lo, key_n)
        k = jnp.concatenate([k_ref[0, pl.ds(ks, key_n), :], kx_ref[pl.ds(ks, key_n), :]], axis=1)
        vt = vt_ref[0, :, pl.ds(ks, key_n)]
        n = tq - q_lo
        q_both = jnp.concatenate([q_aug[0][q_lo:, :], q_aug[1][q_lo:, :]], axis=0)
        st = _dot_nt(k, q_both)
        if masked:
            causal = _causal_t(key_lo, key_n, q_lo, tq)
            st = jnp.where(jnp.concatenate([causal, causal], axis=1), st, MASKED)
        pt = jnp.exp2(st)
        part = jnp.sum(pt.reshape(key_n // SUBLANES, SUBLANES, 2 * n), axis=0)
        pv = _dot(vt, pt.astype(BF16))
        return tuple((_add_cols(carry[c][0], q_lo, part[:, n * c:n * (c + 1)]),
                      _add_cols(carry[c][1], q_lo, pv[:, n * c:n * (c + 1)])) for c in range(2))

    zero = (jnp.zeros((SUBLANES, tq), F32), jnp.zeros((LANES, tq), F32))
    carry = _loop_blocks(qi, block, (zero, zero), tq)
    for key_lo, key_n, q_lo in _diagonal_spans(tq):
        carry = block(qi, carry, key_lo, key_n, q_lo, masked=True)
    (ls1, a1), (ls2, a2) = carry
    l1 = jnp.sum(ls1, axis=0, keepdims=True)
    l2 = jnp.sum(ls2, axis=0, keepdims=True)

    ot = a1 / l1 - lam * (a2 / l2)
    ot = ot * lax.rsqrt(jnp.mean(ot * ot, axis=0, keepdims=True) + EPS)
    o_ref[0] = (ot.T * gain).astype(BF16)

    @pl.when(jnp.logical_not(jnp.min(jnp.minimum(l1, l2)) >= UNDERFLOW_GUARD))
    def _():
        o1, o2 = _diff_running_max(slope_ref[hd] * LOG2E, q_ref, k_ref, vt_ref, qi, tq)
        o_ref[0] = _rms(o1 - lam * o2, gain).astype(BF16)


SLOPE_LANE = 5 * N_SPLIT


def _alibi_tables(slopes, bound, S):
    assert all(math.log2(s) == int(math.log2(s)) for s in slopes), "ALiBi slopes must be powers of two"
    n = N_SPLIT
    log2e_terms = jnp.concatenate(_split3(jnp.full((1, 1), LOG2E, F32)), axis=1)
    s = jnp.asarray(slopes, F32).reshape(-1, 1)
    head = jnp.concatenate(
        [s * log2e_terms * float(LANES), s * log2e_terms, jnp.broadcast_to(_shift_row(bound, 0)[:, :n], (len(slopes), n)),
         jnp.zeros((len(slopes), SLOPE_LANE - 3 * n), F32), s, jnp.zeros((len(slopes), LANES - SLOPE_LANE - 1), F32)],
        axis=1)
    idx = np.arange(S)
    hi, lo = (idx // LANES)[:, None].astype(np.float32), (idx % LANES)[:, None].astype(np.float32)
    qpos = np.zeros((S, LANES), np.float32)
    qpos[:, 3 * n:4 * n] = hi * LANES
    qpos[:, 4 * n:5 * n] = lo
    kx = jnp.concatenate(
        [jnp.asarray(np.concatenate([np.repeat(hi, n, 1), np.repeat(lo, n, 1), np.ones((S, n), np.float32)], axis=1)),
         jnp.broadcast_to(-log2e_terms, (S, n)), jnp.broadcast_to(-log2e_terms, (S, n)),
         jnp.zeros((S, LANES - 5 * n), F32)], axis=1)
    return head.reshape(-1, 1, LANES), jnp.asarray(qpos, dtype=BF16), kx.astype(BF16)


def _diff_attention(slopes, bound, q, k, vt, lambdas, subln, tq, lambda_init):
    B, S, D = q.shape
    qx, qpos, kx = _alibi_tables(slopes, bound, S)
    return pl.pallas_call(
        functools.partial(_diff_kernel, tq=tq, lambda_init=lambda_init),
        grid=(B, DIFF_HEADS, S // tq),
        in_specs=[
            pl.BlockSpec(memory_space=pltpu.SMEM),
            pl.BlockSpec((1, 1, LANES), lambda b, h, i: (h, 0, 0)),
            pl.BlockSpec((tq, LANES), lambda b, h, i: (i, 0)),
            pl.BlockSpec((1, tq, LANES), lambda b, h, i: (b, i, h)),
            pl.BlockSpec((1, S, LANES), lambda b, h, i: (b, 0, h)),
            _const_spec(kx.shape),
            pl.BlockSpec((1, LANES, S), lambda b, h, i: (b, h, 0)),
            pl.BlockSpec(lambdas.shape, lambda b, h, i: (0, 0)),
            pl.BlockSpec(subln.shape, lambda b, h, i: (0, 0)),
        ],
        out_specs=pl.BlockSpec((1, tq, LANES), lambda b, h, i: (b, i, h)),
        out_shape=jax.ShapeDtypeStruct((B, S, D), BF16),
        compiler_params=_params(("parallel", "parallel", "arbitrary")),
        name="diff_attention",
    )(jnp.asarray(slopes, F32), qx, qpos, q, k, kx, vt, lambdas, subln)


def _mem_kv_kernel(mem_ref, norm_ref, w_ref, gain_ref, k_ref, v_ref):
    h = _rms(mem_ref[0], norm_ref[0]).astype(BF16)
    z = _dot(h, w_ref[0])
    width = MEM_HEADS * MEM_HEAD_DIM
    gain = gain_ref[0]
    for hd in range(MEM_HEADS):
        sl = slice(MEM_HEAD_DIM * hd, MEM_HEAD_DIM * (hd + 1))
        k_ref[0, 0, :, sl] = _rms(z[:, sl], gain).astype(BF16)
    v_ref[0, 0] = z[:, width:2 * width].astype(BF16)


def _mem_kv(mem, norms, w_kv, gains):
    B, M, D = mem.shape
    L = norms.shape[0]
    width = MEM_HEADS * MEM_HEAD_DIM
    out = pl.BlockSpec((1, 1, M, width), lambda l, b: (l, b, 0, 0))
    return pl.pallas_call(
        _mem_kv_kernel,
        grid=(L, B),
        in_specs=[
            pl.BlockSpec((1, M, D), lambda l, b: (b, 0, 0)),
            pl.BlockSpec((1, 1, D), lambda l, b: (l, 0, 0)),
            pl.BlockSpec((1, D, 2 * width), lambda l, b: (l, 0, 0)),
            pl.BlockSpec((1, 1, MEM_HEAD_DIM), lambda l, b: (l, 0, 0)),
        ],
        out_specs=[out, out],
        out_shape=[jax.ShapeDtypeStruct((L, B, M, width), BF16)] * 2,
        compiler_params=_params(("parallel", "parallel")),
        name="mem_kv",
    )(mem, norms, w_kv, gains)


def _post_kernel(*refs, n_attn):
    x_ref = refs[0]
    attn_refs = refs[1:1 + n_attn]
    wo_refs = refs[1 + n_attn:1 + 2 * n_attn]
    (mqn_ref, wq_ref, qg_ref, mk_ref, mv_ref, wmo_ref, fn_ref, wg_ref, wu_ref, wd_ref, o_ref) = refs[1 + 2 * n_attn:]
    x = x_ref[0]
    for a_ref, w_ref in zip(attn_refs, wo_refs):
        x = x + _dot(a_ref[0], w_ref[...])

    q = _dot(_rms(x, mqn_ref[...]).astype(BF16), wq_ref[...])
    heads = []
    for hd in range(MEM_HEADS):
        sl = slice(MEM_HEAD_DIM * hd, MEM_HEAD_DIM * (hd + 1))
        qh = (_rms(q[:, sl], qg_ref[...]) * (MEM_HEAD_DIM ** -0.5 * LOG2E)).astype(BF16)
        s = _dot_nt(qh, mk_ref[0, :, sl])
        p = jnp.exp2(s - jnp.max(s, axis=-1, keepdims=True))
        l = jnp.sum(p, axis=-1, keepdims=True)
        heads.append((_dot(p.astype(BF16), mv_ref[0, :, sl]) / l).astype(BF16))
    x = x + _dot(jnp.concatenate(heads, axis=-1), wmo_ref[...])

    hf = _rms(x, fn_ref[...]).astype(BF16)
    g = _dot(hf, wg_ref[...])
    u = _dot(hf, wu_ref[...])
    act = (g * jax.nn.sigmoid(g) * u).astype(BF16)
    o_ref[0] = x + _dot(act, wd_ref[...])


def _post_block(x, attns, wos, mqn, wq, qg, mk, mv, wmo, fn, wg, wu, wd, ts):
    B, S, D = x.shape
    n = len(attns)
    tok = lambda c: pl.BlockSpec((1, ts, c), lambda b, i: (b, i, 0))
    memspec = pl.BlockSpec((1,) + mk.shape[1:], lambda b, i: (b, 0, 0))
    consts_a = list(wos) + [mqn, wq, qg]
    consts_b = [wmo, fn, wg, wu, wd]
    return pl.pallas_call(
        functools.partial(_post_kernel, n_attn=n),
        grid=(B, S // ts),
        in_specs=([tok(D)] + [tok(a.shape[-1]) for a in attns] + [_const_spec(c.shape) for c in consts_a]
                  + [memspec, memspec] + [_const_spec(c.shape) for c in consts_b]),
        out_specs=tok(D),
        out_shape=jax.ShapeDtypeStruct((B, S, D), F32),
        compiler_params=_params(("parallel", "parallel")),
        name="post_block",
    )(x, *attns, *consts_a, mk, mv, *consts_b)


def _row(v):
    return v.reshape(1, -1).astype(F32)


def _shift_row(bound, first_lane):
    terms = jnp.stack(_split3(-jnp.asarray(bound, F32))).reshape(1, N_SPLIT)
    return jnp.pad(terms, ((0, 0), (first_lane, LANES - first_lane - N_SPLIT)))


def _mla_lane_perm():
    half = MLA_ROPE // 2
    lanes = np.arange(MLA_QK)
    return np.where(lanes < half, MLA_NOPE + lanes,
                    np.where(lanes < ROPE_HI_LANE, lanes - half,
                             np.where(lanes < ROPE_HI_LANE + half, lanes + half, lanes - MLA_ROPE)))


def _even_weights(w_in, w_uq, w_ukv, q_gain, k_gain):
    perm = _mla_lane_perm()
    is_nope = perm < MLA_NOPE
    pad = LANES - MLA_QK

    def head_cols(w_head, live):
        return jnp.pad(jnp.where(live[None, :], w_head[:, perm], 0.0), ((0, 0), (0, pad)))

    c_kr = EVEN_COLS[-1]
    kr_src = jnp.concatenate([jnp.zeros((w_in.shape[0], MLA_NOPE), w_in.dtype), w_in[:, c_kr:c_kr + MLA_ROPE]], axis=1)
    w1 = jnp.concatenate([w_in[:, :c_kr], head_cols(kr_src, ~is_nope)], axis=1).astype(BF16)
    wuq = jnp.concatenate(
        [head_cols(w_uq[:, MLA_QK * h:MLA_QK * (h + 1)], np.ones_like(is_nope)) for h in range(MLA_HEADS)],
        axis=1).astype(BF16)
    kv_w = MLA_NOPE + MLA_V
    rope_zeros = jnp.zeros((w_ukv.shape[0], MLA_ROPE), w_ukv.dtype)
    wuk = jnp.concatenate(
        [head_cols(jnp.concatenate([w_ukv[:, kv_w * h:kv_w * h + MLA_NOPE], rope_zeros], axis=1), is_nope)
         for h in range(MLA_HEADS)], axis=1).astype(BF16)
    wuv = jnp.concatenate(
        [w_ukv[:, kv_w * h + MLA_NOPE:kv_w * (h + 1)] for h in range(MLA_HEADS)], axis=1).astype(BF16).T
    gmq = _row(jnp.pad(q_gain[perm], (0, pad)))
    gmk = _row(jnp.pad(k_gain[perm], (0, pad)))
    return w1, wuq, wuk, wuv, gmq, gmk


def _rope_freqs():
    half = MLA_ROPE // 2
    freqs = (np.float32(ROPE_THETA) ** (-np.arange(half, dtype=np.float32) / np.float32(half))).astype(np.float32)
    return jnp.asarray(freqs.reshape(half, 1))


def _ones_blocks(n):
    g = np.arange(n) // HEAD_DIM
    return jnp.asarray((g[:, None] == g[None, :]).astype(np.float32), dtype=BF16)


def _tile_sizes(S):
    return min(4 * MXU_WIDTH, S), min(2 * MXU_WIDTH, S), min(2 * MXU_WIDTH, S - SWA_WINDOW), min(4 * MXU_WIDTH, S)


def kernel(x, mem, positions, mix_norm, ev_w_in, ev_swa_q_gain, ev_swa_k_gain, ev_sinks, ev_q_latent_norm,
           ev_kv_latent_norm, ev_w_uq, ev_w_ukv, ev_mla_q_gain, ev_mla_k_gain, ev_w_out, od_w_qkv, od_q_gain,
           od_k_gain, od_lambda, od_subln, od_w_out, mem_q_norm, mem_kv_norm, mem_w_q, mem_w_kv, mem_q_gain,
           mem_k_gain, mem_w_out, ffn_norm, ffn_w_gate, ffn_w_up, ffn_w_down):
    B, S, D = x.shape
    depth = mix_norm.shape[0]
    ts_proj, ts_post, tq_swa, tq_flash = _tile_sizes(S)

    ones = _ones_blocks(MXU_WIDTH)
    pos3 = positions.reshape(B, 1, S)
    inv = _rope_freqs()
    mem_k, mem_v = _mem_kv(mem, mem_kv_norm.reshape(depth, 1, D), mem_w_kv.astype(BF16),
                           mem_k_gain.reshape(depth, 1, MEM_HEAD_DIM))
    diff_slopes = [2.0 ** (-8.0 * (i + 1) / DIFF_HEADS) for i in range(DIFF_HEADS)]

    for l in range(depth):
        if l % 2 == 0:
            e = l // 2
            w1, wuq, wuk, wuv, gmq, gmk = _even_weights(ev_w_in[e], ev_w_uq[e], ev_w_ukv[e], ev_mla_q_gain[e],
                                                        ev_mla_k_gain[e])
            qa, ka, va, qm, km, vm = _even_proj(
                x, pos3, _row(mix_norm[l]), w1, ones, _row(jnp.tile(ev_swa_q_gain[e], SWA_HEADS)),
                _row(jnp.tile(ev_swa_k_gain[e], SWA_KV_HEADS)), _row(ev_q_latent_norm[e]),
                _row(ev_kv_latent_norm[e]), wuq, wuk, wuv, gmq, gmk,
                _shift_row(_score_bound(ev_mla_q_gain[e], ev_mla_k_gain[e], MLA_QK, MLA_QK ** -0.5 * LOG2E), MLA_QK),
                inv, ts_proj)
            swa_bound = _score_bound(ev_swa_q_gain[e], ev_swa_k_gain[e], HEAD_DIM, HEAD_DIM ** -0.5 * LOG2E)
            swa_scalars = jnp.concatenate([ev_sinks[e].astype(F32), swa_bound.reshape(1).astype(F32)])
            out_a = _swa_attention(swa_scalars, qa, ka, va, tq_swa)
            out_b = _mla_attention(qm, km, vm, tq_flash)
            wo = ev_w_out[e].astype(BF16)
            attns = [out_a, out_b]
            wos = [wo[:SWA_Q_W], wo[SWA_Q_W:]]
        else:
            o = l // 2
            lambda_init = 0.8 - 0.6 * math.exp(-0.3 * l)
            n_grp = DIFF_HEADS * 2
            w_qkv = od_w_qkv[o].astype(BF16)
            q, k, vt = _odd_proj(x, _row(mix_norm[l]), w_qkv[:, :2 * D], w_qkv[:, 2 * D:].T, ones,
                                 _row(jnp.tile(od_q_gain[o], n_grp // 2)), _row(jnp.tile(od_k_gain[o], n_grp // 2)),
                                 ts_proj)
            bound = _score_bound(od_q_gain[o], od_k_gain[o], DIFF_DIM, DIFF_DIM ** -0.5 * LOG2E)
            out_d = _diff_attention(diff_slopes, bound, q, k, vt, od_lambda[o].astype(F32), _row(od_subln[o]),
                                    tq_flash, lambda_init)
            attns = [out_d]
            wos = [od_w_out[o].astype(BF16)]
        x = _post_block(x, attns, wos, _row(mem_q_norm[l]), mem_w_q[l].astype(BF16), _row(mem_q_gain[l]),
                        mem_k[l], mem_v[l], mem_w_out[l].astype(BF16), _row(ffn_norm[l]),
                        ffn_w_gate[l].astype(BF16), ffn_w_up[l].astype(BF16), ffn_w_down[l].astype(BF16), ts_post)
    return x
```

```python
import functools
import math

import numpy as np
import jax
import jax.numpy as jnp
from jax import lax
from jax.experimental import pallas as pl
from jax.experimental.pallas import tpu as pltpu

F32 = jnp.float32
BF16 = jnp.bfloat16

EPS = 1e-6
MASKED = -1e30
LOG2E = 1.4426950408889634
ROPE_THETA = 10000.0

LANES = 128
SUBLANES = 8
MXU_WIDTH = 256
HEAD_DIM = 64
SWA_HEADS = 8
SWA_KV_HEADS = 2
SWA_WINDOW = 128
MLA_HEADS = 8
MLA_NOPE = 64
MLA_ROPE = 32
MLA_QK = MLA_NOPE + MLA_ROPE
MLA_V = 64
ROPE_HI_LANE = 64
DIFF_HEADS = 8
DIFF_DIM = 64
MEM_HEADS = 4
MEM_HEAD_DIM = 128
MLA_Q_RANK = 256
MLA_KV_RANK = 128
SWA_Q_W = SWA_HEADS * HEAD_DIM
SWA_KV_W = SWA_KV_HEADS * HEAD_DIM
EVEN_COLS = tuple(np.cumsum([SWA_Q_W, SWA_KV_W, SWA_KV_W, MLA_Q_RANK, MLA_KV_RANK]).tolist())

VMEM_LIMIT = 56 * 1024 * 1024

UNDERFLOW_GUARD = 2.0 ** -80
N_SPLIT = 3


def _params(sem):
    return pltpu.CompilerParams(dimension_semantics=sem, vmem_limit_bytes=VMEM_LIMIT)


def _const_spec(shape):
    nd = len(shape)
    return pl.BlockSpec(shape, lambda *_: (0,) * nd, pipeline_mode=pl.Buffered(1))


def _rms(x, g):
    ms = jnp.mean(x * x, axis=-1, keepdims=True)
    return x * lax.rsqrt(ms + EPS) * g


def _dot(a, b):
    return jnp.dot(a, b, preferred_element_type=F32)


def _dot_nt(a, b):
    return lax.dot_general(a, b, (((1,), (1,)), ((), ())), preferred_element_type=F32)


def _group_sumsq(x, ones_blk):
    x2 = x * x
    hi = x2.astype(BF16)
    lo = (x2 - hi.astype(F32)).astype(BF16)
    return _dot(hi, ones_blk) + _dot(lo, ones_blk)


def _group_norm64(x, ones_blk, gain):
    chunk = min(ones_blk.shape[0], x.shape[1])
    ssq = jnp.concatenate([_group_sumsq(x[:, c:c + chunk], ones_blk[0:chunk, 0:chunk])
                           for c in range(0, x.shape[1], chunk)], axis=1)
    return x * lax.rsqrt(ssq * (1.0 / HEAD_DIM) + EPS) * gain


def _even_proj_kernel(x_ref, pos_ref, gmix_ref, w1_ref, ones_ref, gq_ref, gk_ref, qln_ref, kvln_ref,
                      wuq_ref, wuk_ref, wuv_ref, gmq_ref, gmk_ref, qshift_ref, inv_ref,
                      qa_ref, ka_ref, va_ref, qm_ref, km_ref, vm_ref):
    x = x_ref[0]
    h = _rms(x, gmix_ref[...]).astype(BF16)
    z = _dot(h, w1_ref[...])
    c_ka, c_va, c_cq, c_ckv, c_kr = EVEN_COLS
    ones = ones_ref[...]
    qa = _group_norm64(z[:, 0:c_ka], ones, gq_ref[...]) * (HEAD_DIM ** -0.5 * LOG2E)
    qa_ref[0] = qa.astype(BF16)
    ka = _group_norm64(z[:, c_ka:c_va], ones[0:SWA_KV_W, 0:SWA_KV_W], gk_ref[...])
    ka_ref[0] = ka.astype(BF16)
    va_ref[0] = z[:, c_va:c_cq].astype(BF16)

    cqn = _rms(z[:, c_cq:c_ckv], qln_ref[...]).astype(BF16)
    ckvn = _rms(z[:, c_ckv:c_kr], kvln_ref[...]).astype(BF16)
    kr = z[:, c_kr:c_kr + LANES]
    qf = _dot(cqn, wuq_ref[...])
    kn = _dot(ckvn, wuk_ref[...])
    vm_ref[0] = _dot_nt(wuv_ref[...], ckvn).astype(BF16)

    half = MLA_ROPE // 2
    ang = inv_ref[...] * pos_ref[0].astype(F32)
    pad = jnp.zeros((LANES - half, ang.shape[1]), F32)
    cos = jnp.concatenate([jnp.cos(ang), pad], axis=0).T
    sin = jnp.concatenate([jnp.sin(ang), pad], axis=0).T
    lane = lax.broadcasted_iota(jnp.int32, cos.shape, 1)
    in_lo = lane < half
    in_hi = (lane >= ROPE_HI_LANE) & (lane < ROPE_HI_LANE + half)
    cos_t = jnp.where(in_lo, cos, jnp.where(in_hi, pltpu.roll(cos, ROPE_HI_LANE, 1), 1.0))
    sin_t = jnp.where(in_lo, -sin, jnp.where(in_hi, pltpu.roll(sin, ROPE_HI_LANE, 1), 0.0))

    pair = 2 * LANES
    gi = lax.broadcasted_iota(jnp.int32, (pair, pair), 0) // LANES
    gj = lax.broadcasted_iota(jnp.int32, (pair, pair), 1) // LANES
    head_ones = (gi == gj).astype(BF16)

    def norm_rope(xp, gain):
        r = lax.rsqrt(_group_sumsq(xp, head_ones) * (1.0 / MLA_QK) + EPS)
        xn = xp * r * gain
        halves = []
        for t in range(2):
            xh = xn[:, LANES * t:LANES * (t + 1)]
            halves.append(xh * cos_t + pltpu.roll(xh, ROPE_HI_LANE, 1) * sin_t)
        return halves

    gmq = jnp.concatenate([gmq_ref[...]] * 2, axis=1)
    gmk = jnp.concatenate([gmk_ref[...]] * 2, axis=1)
    kr2 = jnp.concatenate([kr, kr], axis=1)
    shift_lanes = (lane >= MLA_QK) & (lane < MLA_QK + N_SPLIT)
    qshift = qshift_ref[...]
    for hp in range(MLA_HEADS // 2):
        sl = slice(pair * hp, pair * (hp + 1))
        q_heads = norm_rope(qf[:, sl], gmq)
        k_heads = norm_rope(kn[:, sl] + kr2, gmk)
        for t in range(2):
            hl = slice(pair * hp + LANES * t, pair * hp + LANES * (t + 1))
            qm_ref[0, :, hl] = jnp.where(shift_lanes, qshift, q_heads[t] * (MLA_QK ** -0.5 * LOG2E)).astype(BF16)
            km_ref[0, :, hl] = jnp.where(shift_lanes, 1.0, k_heads[t]).astype(BF16)


def _even_proj(x, pos3, gmix, w1, ones, gq, gk, qln, kvln, wuq, wuk, wuv, gmq, gmk, qshift, inv, ts):
    B, S, D = x.shape
    grid = (B, S // ts)
    tok = lambda c: pl.BlockSpec((1, ts, c), lambda b, i: (b, i, 0))
    consts = [gmix, w1, ones, gq, gk, qln, kvln, wuq, wuk, wuv, gmq, gmk, qshift, inv]
    out_cols = [SWA_Q_W, SWA_KV_W, SWA_KV_W, MLA_HEADS * LANES, MLA_HEADS * LANES]
    v_rows = MLA_HEADS * MLA_V
    return pl.pallas_call(
        _even_proj_kernel,
        grid=grid,
        in_specs=([tok(D), pl.BlockSpec((1, 1, ts), lambda b, i: (b, 0, i))]
                  + [_const_spec(c.shape) for c in consts]),
        out_specs=[tok(c) for c in out_cols] + [pl.BlockSpec((1, v_rows, ts), lambda b, i: (b, 0, i))],
        out_shape=([jax.ShapeDtypeStruct((B, S, c), BF16) for c in out_cols]
                   + [jax.ShapeDtypeStruct((B, v_rows, S), BF16)]),
        compiler_params=_params(("parallel", "parallel")),
        name="even_proj",
    )(x, pos3, *consts)


def _swa_place_q(q_ref, hd, lane):
    hk = hd // (SWA_HEADS // SWA_KV_HEADS)
    qq = q_ref[0, :, LANES * (hd // 2):LANES * (hd // 2 + 1)].astype(F32)
    xq = qq if hd % 2 == hk else pltpu.roll(qq, HEAD_DIM, 1)
    return jnp.where((lane >= HEAD_DIM * hk) & (lane < HEAD_DIM * (hk + 1)), xq, 0.0).astype(BF16)


def _swa_place_out(o_even, o_odd, cb, lane):
    placed = []
    for e, o in enumerate((o_even, o_odd)):
        hk = (2 * cb + e) // (SWA_HEADS // SWA_KV_HEADS)
        placed.append(o if e == hk else pltpu.roll(o, HEAD_DIM, 1))
    return jnp.where(lane < HEAD_DIM, placed[0], placed[1]).astype(BF16)


def _swa_running_max(sink_ref, q_ref, k_ref, v_ref, o_ref, qi, tq):
    win = tq + SWA_WINDOW
    wstart = pl.multiple_of(jnp.maximum(qi * tq - SWA_WINDOW, 0), SWA_WINDOW)
    kw = k_ref[0, pl.ds(wstart, win), :]
    vw = v_ref[0, pl.ds(wstart, win), :]
    row = lax.broadcasted_iota(jnp.int32, (tq, win), 0)
    col = lax.broadcasted_iota(jnp.int32, (tq, win), 1)
    dist = (qi * tq + row) - (wstart + col)
    valid = (dist >= 0) & (dist < SWA_WINDOW)
    distf = dist.astype(F32)
    lane = lax.broadcasted_iota(jnp.int32, (tq, LANES), 1)
    for cb in range(SWA_HEADS // 2):
        outs = []
        for e in range(2):
            hd = 2 * cb + e
            s = _dot_nt(_swa_place_q(q_ref, hd, lane), kw) - (2.0 ** (-8.0 * (hd + 1) / SWA_HEADS) * LOG2E) * distf
            s = jnp.where(valid, s, MASKED)
            sink = sink_ref[hd] * LOG2E
            m = jnp.maximum(jnp.max(s, axis=-1, keepdims=True), sink)
            p = jnp.exp2(s - m)
            l = jnp.sum(p, axis=-1, keepdims=True) + jnp.exp2(sink - m)
            outs.append(_dot(p.astype(BF16), vw) / l)
        o_ref[0, :, LANES * cb:LANES * (cb + 1)] = _swa_place_out(outs[0], outs[1], cb, lane)


def _swa_kernel(sink_ref, q_ref, k_ref, v_ref, o_ref, *, tq):
    qi = pl.program_id(1)
    sub = SWA_WINDOW
    win = 2 * sub
    n_sub = tq // sub
    bound = sink_ref[SWA_HEADS]
    lane_q = lax.broadcasted_iota(jnp.int32, (tq, LANES), 1)
    lane = lax.broadcasted_iota(jnp.int32, (sub, LANES), 1)
    row = lax.broadcasted_iota(jnp.int32, (sub, win), 0)
    col = lax.broadcasted_iota(jnp.int32, (sub, win), 1)

    def bias_mask(first_key_offset, slope):
        dist = row + first_key_offset - col
        return jnp.where((dist >= 0) & (dist < SWA_WINDOW), -slope * dist.astype(F32) - bound, MASKED)

    ones_blk = jnp.ones((win, LANES), BF16)
    windows = []
    for r in range(n_sub):
        wstart = qi * tq + sub * (r - 1)
        wstart = pl.multiple_of(jnp.maximum(wstart, 0) if r == 0 else wstart, sub)
        windows.append((k_ref[0, pl.ds(wstart, win), :],
                        jnp.concatenate([v_ref[0, pl.ds(wstart, win), :], ones_blk], axis=1)))

    outs = [[None] * SWA_HEADS for _ in range(n_sub)]
    min_sum = None
    for hd in range(SWA_HEADS):
        slope = 2.0 ** (-8.0 * (hd + 1) / SWA_HEADS) * LOG2E
        bm = bias_mask(sub, slope)
        bm_first = jnp.where(qi == 0, bias_mask(0, slope), bm)
        xq = _swa_place_q(q_ref, hd, lane_q)
        sink_term = jnp.exp2(sink_ref[hd] * LOG2E - bound)
        for r in range(n_sub):
            kw, v_ones = windows[r]
            s = _dot_nt(xq[sub * r:sub * (r + 1)], kw) + (bm_first if r == 0 else bm)
            ov = _dot(jnp.exp2(s).astype(BF16), v_ones)
            l = ov[:, LANES:] + sink_term
            outs[r][hd] = ov[:, :LANES] / l
            min_sum = l if min_sum is None else jnp.minimum(min_sum, l)
    for r in range(n_sub):
        for cb in range(SWA_HEADS // 2):
            o_ref[0, sub * r:sub * (r + 1), LANES * cb:LANES * (cb + 1)] = _swa_place_out(
                outs[r][2 * cb], outs[r][2 * cb + 1], cb, lane)

    @pl.when(jnp.logical_not(jnp.min(min_sum) >= UNDERFLOW_GUARD))
    def _():
        _swa_running_max(sink_ref, q_ref, k_ref, v_ref, o_ref, qi, tq)


def _swa_attention(sinks, qa, ka, va, tq):
    B, S, _ = qa.shape
    return pl.pallas_call(
        functools.partial(_swa_kernel, tq=tq),
        grid=(B, S // tq),
        in_specs=[
            pl.BlockSpec(memory_space=pltpu.SMEM),
            pl.BlockSpec((1, tq, SWA_Q_W), lambda b, i: (b, i, 0)),
            pl.BlockSpec((1, S, SWA_KV_W), lambda b, i: (b, 0, 0)),
            pl.BlockSpec((1, S, SWA_KV_W), lambda b, i: (b, 0, 0)),
        ],
        out_specs=pl.BlockSpec((1, tq, SWA_Q_W), lambda b, i: (b, i, 0)),
        out_shape=jax.ShapeDtypeStruct((B, S, SWA_Q_W), BF16),
        compiler_params=_params(("parallel", "arbitrary")),
        name="swa_attention",
    )(sinks, qa, ka, va)


def _split3(x):
    hi = x.astype(BF16).astype(F32)
    r = x - hi
    mid = r.astype(BF16).astype(F32)
    lo = (r - mid).astype(BF16).astype(F32)
    return hi, mid, lo


def _score_bound(q_gain, k_gain, dim, scale):
    return dim * scale * jnp.max(jnp.abs(q_gain)) * jnp.max(jnp.abs(k_gain))


BLOCKS_PER_MATMUL = 2
BLOCKS_PER_TRIP = 4


def _loop_blocks(n, step, init, tq):
    def trip(t, c):
        for u in range(0, BLOCKS_PER_TRIP, BLOCKS_PER_MATMUL):
            c = step(BLOCKS_PER_TRIP * t + u, c, 0, BLOCKS_PER_MATMUL * tq)
        return c

    full = n // BLOCKS_PER_TRIP
    carry = lax.fori_loop(0, full, trip, init)
    done = full * BLOCKS_PER_TRIP
    pair = (n - done) >= BLOCKS_PER_MATMUL
    carry = lax.cond(pair, lambda c: step(done, c, 0, BLOCKS_PER_MATMUL * tq), lambda c: c, carry)
    return lax.fori_loop(done + jnp.where(pair, BLOCKS_PER_MATMUL, 0), n, step, carry)


def _add_cols(acc, lo, part):
    if lo == 0:
        return acc + part
    return acc + jnp.concatenate([jnp.zeros((acc.shape[0], lo), acc.dtype), part], axis=1)


def _diagonal_spans(tq):
    h = tq // 2
    return ((0, h, 0), (h, h, h)) if h % MXU_WIDTH == 0 else ((0, tq, 0),)


def _causal_t(key_lo, key_n, q_lo, tq):
    shape = (key_n, tq - q_lo)
    return (key_lo + lax.broadcasted_iota(jnp.int32, shape, 0)) <= (q_lo + lax.broadcasted_iota(jnp.int32, shape, 1))


def _causal_mask(tq):
    row = lax.broadcasted_iota(jnp.int32, (tq, tq), 0)
    col = lax.broadcasted_iota(jnp.int32, (tq, tq), 1)
    return row >= col


def _mla_running_max(q_ref, k_ref, vt_ref, e, qi, tq):
    causal = _causal_mask(tq)
    q = q_ref[0, :, LANES * e:LANES * (e + 1)]

    def step(j, carry, masked):
        m, l, acc = carry
        ks = pl.multiple_of(j * tq, tq)
        k = k_ref[0, pl.ds(ks, tq), LANES * e:LANES * (e + 1)]
        vt = vt_ref[0, :, pl.ds(ks, tq)]
        s = _dot_nt(q, k)
        if masked:
            s = jnp.where(causal, s, MASKED)
        m_new = jnp.maximum(m, jnp.max(s, axis=-1, keepdims=True))
        alpha = jnp.exp2(m - m_new)
        p = jnp.exp2(s - m_new)
        l = alpha * l + jnp.sum(p, axis=-1, keepdims=True)
        acc = alpha * acc + _dot_nt(p.astype(BF16), vt)
        return m_new, l, acc

    init = (jnp.full((tq, 1), MASKED, F32), jnp.zeros((tq, 1), F32), jnp.zeros((tq, LANES), F32))
    carry = lax.fori_loop(0, qi, lambda j, c: step(j, c, False), init)
    m, l, acc = step(qi, carry, True)
    return acc / l


def _mla_kernel(q_ref, k_ref, vt_ref, o_ref, *, tq):
    qi = pl.program_id(2)
    lane = lax.broadcasted_iota(jnp.int32, (tq, LANES), 1)
    q = q_ref[0]
    first = lax.broadcasted_iota(jnp.int32, q.shape, 1) < LANES
    zero_q = jnp.zeros_like(q)
    q_aug = (jnp.where(first, q, zero_q), jnp.where(first, zero_q, q))

    def block(j, carry, key_lo=0, key_n=tq, q_lo=0, masked=False):
        ks = pl.multiple_of(j * tq + key_lo, key_n)
        n = tq - q_lo
        q_both = jnp.concatenate([q_aug[0][q_lo:, :], q_aug[1][q_lo:, :]], axis=0)
        st = _dot_nt(k_ref[0, pl.ds(ks, key_n), :], q_both)
        if masked:
            causal = _causal_t(key_lo, key_n, q_lo, tq)
            st = jnp.where(jnp.concatenate([causal, causal], axis=1), st, MASKED)
        pt = jnp.exp2(st)
        part = jnp.sum(pt.reshape(key_n // SUBLANES, SUBLANES, 2 * n), axis=0)
        pv = _dot(vt_ref[0, :, pl.ds(ks, key_n)], pt.astype(BF16))
        return tuple((_add_cols(carry[e][0], q_lo, part[:, n * e:n * (e + 1)]),
                      _add_cols(carry[e][1], q_lo, pv[MLA_V * e:MLA_V * (e + 1), n * e:n * (e + 1)]))
                     for e in range(2))

    zero = (jnp.zeros((SUBLANES, tq), F32), jnp.zeros((MLA_V, tq), F32))
    carry = _loop_blocks(qi, block, (zero, zero), tq)
    for key_lo, key_n, q_lo in _diagonal_spans(tq):
        carry = block(qi, carry, key_lo, key_n, q_lo, masked=True)
    sums = [jnp.sum(lsum, axis=0, keepdims=True) for lsum, _ in carry]
    outs = [acc / l for (_, acc), l in zip(carry, sums)]
    o_ref[0] = jnp.concatenate(outs, axis=0).T.astype(BF16)

    @pl.when(jnp.logical_not(jnp.min(jnp.minimum(sums[0], sums[1])) >= UNDERFLOW_GUARD))
    def _():
        slow = [_mla_running_max(q_ref, k_ref, vt_ref, e, qi, tq) for e in range(2)]
        o_ref[0] = jnp.where(lane < MLA_V, slow[0], slow[1]).astype(BF16)


def _mla_attention(qm, km, vm, tq):
    B, S, _ = qm.shape
    return pl.pallas_call(
        functools.partial(_mla_kernel, tq=tq),
        grid=(B, MLA_HEADS // 2, S // tq),
        in_specs=[
            pl.BlockSpec((1, tq, 2 * LANES), lambda b, h, i: (b, i, h)),
            pl.BlockSpec((1, S, 2 * LANES), lambda b, h, i: (b, 0, h)),
            pl.BlockSpec((1, 2 * MLA_V, S), lambda b, h, i: (b, h, 0)),
        ],
        out_specs=pl.BlockSpec((1, tq, LANES), lambda b, h, i: (b, i, h)),
        out_shape=jax.ShapeDtypeStruct((B, S, MLA_HEADS * MLA_V), BF16),
        compiler_params=_params(("parallel", "parallel", "arbitrary")),
        name="mla_attention",
    )(qm, km, vm)


def _odd_proj_kernel(x_ref, gmix_ref, w_ref, wvt_ref, ones_ref, gq_ref, gk_ref, q_ref, k_ref, vt_ref):
    x = x_ref[0]
    h = _rms(x, gmix_ref[...]).astype(BF16)
    z = _dot(h, w_ref[...])
    ones = ones_ref[...]
    width = DIFF_HEADS * 2 * DIFF_DIM
    piece = gq_ref.shape[1]
    for c in range(0, width, piece):
        q = _group_norm64(z[:, c:c + piece], ones, gq_ref[...]) * (DIFF_DIM ** -0.5 * LOG2E)
        q_ref[0, :, c:c + piece] = q.astype(BF16)
        k = _group_norm64(z[:, width + c:width + c + piece], ones, gk_ref[...])
        k_ref[0, :, c:c + piece] = k.astype(BF16)
    vt_ref[0] = _dot_nt(wvt_ref[...], h).astype(BF16)


def _odd_proj(x, gmix, w_qk, w_vt, ones, gq, gk, ts):
    B, S, D = x.shape
    tok = pl.BlockSpec((1, ts, D), lambda b, i: (b, i, 0))
    consts = [gmix, w_qk, w_vt, ones, gq, gk]
    return pl.pallas_call(
        _odd_proj_kernel,
        grid=(B, S // ts),
        in_specs=[tok] + [_const_spec(c.shape) for c in consts],
        out_specs=[tok, tok, pl.BlockSpec((1, D, ts), lambda b, i: (b, 0, i))],
        out_shape=[jax.ShapeDtypeStruct((B, S, D), BF16)] * 2 + [jax.ShapeDtypeStruct((B, D, S), BF16)],
        compiler_params=_params(("parallel", "parallel")),
        name="odd_proj",
    )(x, *consts)


def _diff_q_parts(q, lane):
    zero = jnp.zeros_like(q)
    return jnp.where(lane < DIFF_DIM, q, zero), jnp.where(lane >= DIFF_DIM, q, zero)


def _diff_running_max(slope, q_ref, k_ref, vt_ref, qi, tq):
    row = lax.broadcasted_iota(jnp.int32, (tq, tq), 0)
    col = lax.broadcasted_iota(jnp.int32, (tq, tq), 1)
    causal = row >= col
    bias = slope * (col - row).astype(F32)
    lane = lax.broadcasted_iota(jnp.int32, (tq, LANES), 1)
    q_parts = _diff_q_parts(q_ref[0], lane)

    def step(j, carry, masked):
        ks = pl.multiple_of(j * tq, tq)
        k = k_ref[0, pl.ds(ks, tq), :]
        vt = vt_ref[0, :, pl.ds(ks, tq)]
        offset = slope * ((j - qi) * tq).astype(F32)
        new = []
        for c in range(2):
            m, l, acc = carry[c]
            s = _dot_nt(q_parts[c], k) + bias
            if masked:
                s = jnp.where(causal, s, MASKED)
            m_new = jnp.maximum(m, jnp.max(s, axis=-1, keepdims=True) + offset)
            alpha = jnp.exp2(m - m_new)
            p = jnp.exp2(s - (m_new - offset))
            l = alpha * l + jnp.sum(p, axis=-1, keepdims=True)
            acc = alpha * acc + _dot_nt(p.astype(BF16), vt)
            new.append((m_new, l, acc))
        return tuple(new)

    one = (jnp.full((tq, 1), MASKED, F32), jnp.zeros((tq, 1), F32), jnp.zeros((tq, LANES), F32))
    carry = lax.fori_loop(0, qi, lambda j, c: step(j, c, False), (one, one))
    (m1, l1, a1), (m2, l2, a2) = step(qi, carry, True)
    return a1 / l1, a2 / l2


def _diff_kernel(slope_ref, qx_ref, qpos_ref, q_ref, k_ref, kx_ref, vt_ref, lam_ref, subln_ref, o_ref, *, tq,
                 lambda_init):
    hd = pl.program_id(1)
    qi = pl.program_id(2)
    lane = lax.broadcasted_iota(jnp.int32, (tq, LANES), 1)
    q_parts = _diff_q_parts(q_ref[0], lane)
    head_row = qx_ref[0]
    qx = jnp.where(lane < 3 * N_SPLIT, head_row,
                   qpos_ref[...].astype(F32) * head_row[:, SLOPE_LANE:SLOPE_LANE + 1]).astype(BF16)
    q_aug = [jnp.concatenate([part, qx], axis=1) for part in q_parts]
    lf = lam_ref[...]
    lam = (jnp.exp(jnp.sum(lf[0:1] * lf[1:2], axis=-1, keepdims=True))
           - jnp.exp(jnp.sum(lf[2:3] * lf[3:4], axis=-1, keepdims=True)) + lambda_init)
    gain = subln_ref[...] * (1.0 - lambda_init)

    def block(j, carry, key_lo=0, key_n=tq, q_lo=0, masked=False):
        ks = pl.multiple_of(j * tq + key_lo, key_n)
        k = jnp.concatenate([k_ref[0, pl.ds(ks, key_n), :], kx_ref[pl.ds(ks, key_n), :]], axis=1)
        vt = vt_ref[0, :, pl.ds(ks, key_n)]
        n = tq - q_lo
        q_both = jnp.concatenate([q_aug[0][q_lo:, :], q_aug[1][q_lo:, :]], axis=0)
        st = _dot_nt(k, q_both)
        if masked:
            causal = _causal_t(key_lo, key_n, q_lo, tq)
            st = jnp.where(jnp.concatenate([causal, causal], axis=1), st, MASKED)
        pt = jnp.exp2(st)
        part = jnp.sum(pt.reshape(key_n // SUBLANES, SUBLANES, 2 * n), axis=0)
        pv = _dot(vt, pt.astype(BF16))
        return tuple((_add_cols(carry[c][0], q_lo, part[:, n * c:n * (c + 1)]),
                      _add_cols(carry[c][1], q_lo, pv[:, n * c:n * (c + 1)])) for c in range(2))

    zero = (jnp.zeros((SUBLANES, tq), F32), jnp.zeros((LANES, tq), F32))
    carry = _loop_blocks(qi, block, (zero, zero), tq)
    for key_lo, key_n, q_lo in _diagonal_spans(tq):
        carry = block(qi, carry, key_lo, key_n, q_lo, masked=True)
    (ls1, a1), (ls2, a2) = carry
    l1 = jnp.sum(ls1, axis=0, keepdims=True)
    l2 = jnp.sum(ls2, axis=0, keepdims=True)

    ot = a1 / l1 - lam * (a2 / l2)
    ot = ot * lax.rsqrt(jnp.mean(ot * ot, axis=0, keepdims=True) + EPS)
    o_ref[0] = (ot.T * gain).astype(BF16)

    @pl.when(jnp.logical_not(jnp.min(jnp.minimum(l1, l2)) >= UNDERFLOW_GUARD))
    def _():
        o1, o2 = _diff_running_max(slope_ref[hd] * LOG2E, q_ref, k_ref, vt_ref, qi, tq)
        o_ref[0] = _rms(o1 - lam * o2, gain).astype(BF16)


SLOPE_LANE = 5 * N_SPLIT


def _alibi_tables(slopes, bound, S):
    assert all(math.log2(s) == int(math.log2(s)) for s in slopes), "ALiBi slopes must be powers of two"
    n = N_SPLIT
    log2e_terms = jnp.concatenate(_split3(jnp.full((1, 1), LOG2E, F32)), axis=1)
    s = jnp.asarray(slopes, F32).reshape(-1, 1)
    head = jnp.concatenate(
        [s * log2e_terms * float(LANES), s * log2e_terms, jnp.broadcast_to(_shift_row(bound, 0)[:, :n], (len(slopes), n)),
         jnp.zeros((len(slopes), SLOPE_LANE - 3 * n), F32), s, jnp.zeros((len(slopes), LANES - SLOPE_LANE - 1), F32)],
        axis=1)
    idx = np.arange(S)
    hi, lo = (idx // LANES)[:, None].astype(np.float32), (idx % LANES)[:, None].astype(np.float32)
    qpos = np.zeros((S, LANES), np.float32)
    qpos[:, 3 * n:4 * n] = hi * LANES
    qpos[:, 4 * n:5 * n] = lo
    kx = jnp.concatenate(
        [jnp.asarray(np.concatenate([np.repeat(hi, n, 1), np.repeat(lo, n, 1), np.ones((S, n), np.float32)], axis=1)),
         jnp.broadcast_to(-log2e_terms, (S, n)), jnp.broadcast_to(-log2e_terms, (S, n)),
         jnp.zeros((S, LANES - 5 * n), F32)], axis=1)
    return head.reshape(-1, 1, LANES), jnp.asarray(qpos, dtype=BF16), kx.astype(BF16)


def _diff_attention(slopes, bound, q, k, vt, lambdas, subln, tq, lambda_init):
    B, S, D = q.shape
    qx, qpos, kx = _alibi_tables(slopes, bound, S)
    return pl.pallas_call(
        functools.partial(_diff_kernel, tq=tq, lambda_init=lambda_init),
        grid=(B, DIFF_HEADS, S // tq),
        in_specs=[
            pl.BlockSpec(memory_space=pltpu.SMEM),
            pl.BlockSpec((1, 1, LANES), lambda b, h, i: (h, 0, 0)),
            pl.BlockSpec((tq, LANES), lambda b, h, i: (i, 0)),
            pl.BlockSpec((1, tq, LANES), lambda b, h, i: (b, i, h)),
            pl.BlockSpec((1, S, LANES), lambda b, h, i: (b, 0, h)),
            _const_spec(kx.shape),
            pl.BlockSpec((1, LANES, S), lambda b, h, i: (b, h, 0)),
            pl.BlockSpec(lambdas.shape, lambda b, h, i: (0, 0)),
            pl.BlockSpec(subln.shape, lambda b, h, i: (0, 0)),
        ],
        out_specs=pl.BlockSpec((1, tq, LANES), lambda b, h, i: (b, i, h)),
        out_shape=jax.ShapeDtypeStruct((B, S, D), BF16),
        compiler_params=_params(("parallel", "parallel", "arbitrary")),
        name="diff_attention",
    )(jnp.asarray(slopes, F32), qx, qpos, q, k, kx, vt, lambdas, subln)


def _mem_kv_kernel(mem_ref, norm_ref, w_ref, gain_ref, k_ref, v_ref):
    h = _rms(mem_ref[0], norm_ref[0]).astype(BF16)
    z = _dot(h, w_ref[0])
    width = MEM_HEADS * MEM_HEAD_DIM
    gain = gain_ref[0]
    for hd in range(MEM_HEADS):
        sl = slice(MEM_HEAD_DIM * hd, MEM_HEAD_DIM * (hd + 1))
        k_ref[0, 0, :, sl] = _rms(z[:, sl], gain).astype(BF16)
    v_ref[0, 0] = z[:, width:2 * width].astype(BF16)


def _mem_kv(mem, norms, w_kv, gains):
    B, M, D = mem.shape
    L = norms.shape[0]
    width = MEM_HEADS * MEM_HEAD_DIM
    out = pl.BlockSpec((1, 1, M, width), lambda l, b: (l, b, 0, 0))
    return pl.pallas_call(
        _mem_kv_kernel,
        grid=(L, B),
        in_specs=[
            pl.BlockSpec((1, M, D), lambda l, b: (b, 0, 0)),
            pl.BlockSpec((1, 1, D), lambda l, b: (l, 0, 0)),
            pl.BlockSpec((1, D, 2 * width), lambda l, b: (l, 0, 0)),
            pl.BlockSpec((1, 1, MEM_HEAD_DIM), lambda l, b: (l, 0, 0)),
        ],
        out_specs=[out, out],
        out_shape=[jax.ShapeDtypeStruct((L, B, M, width), BF16)] * 2,
        compiler_params=_params(("parallel", "parallel")),
        name="mem_kv",
    )(mem, norms, w_kv, gains)


def _post_kernel(*refs, n_attn):
    x_ref = refs[0]
    attn_refs = refs[1:1 + n_attn]
    wo_refs = refs[1 + n_attn:1 + 2 * n_attn]
    (mqn_ref, wq_ref, qg_ref, mk_ref, mv_ref, wmo_ref, fn_ref, wg_ref, wu_ref, wd_ref, o_ref) = refs[1 + 2 * n_attn:]
    x = x_ref[0]
    for a_ref, w_ref in zip(attn_refs, wo_refs):
        x = x + _dot(a_ref[0], w_ref[...])

    q = _dot(_rms(x, mqn_ref[...]).astype(BF16), wq_ref[...])
    heads = []
    for hd in range(MEM_HEADS):
        sl = slice(MEM_HEAD_DIM * hd, MEM_HEAD_DIM * (hd + 1))
        qh = (_rms(q[:, sl], qg_ref[...]) * (MEM_HEAD_DIM ** -0.5 * LOG2E)).astype(BF16)
        s = _dot_nt(qh, mk_ref[0, :, sl])
        p = jnp.exp2(s - jnp.max(s, axis=-1, keepdims=True))
        l = jnp.sum(p, axis=-1, keepdims=True)
        heads.append((_dot(p.astype(BF16), mv_ref[0, :, sl]) / l).astype(BF16))
    x = x + _dot(jnp.concatenate(heads, axis=-1), wmo_ref[...])

    hf = _rms(x, fn_ref[...]).astype(BF16)
    g = _dot(hf, wg_ref[...])
    u = _dot(hf, wu_ref[...])
    act = (g * jax.nn.sigmoid(g) * u).astype(BF16)
    o_ref[0] = x + _dot(act, wd_ref[...])


def _post_block(x, attns, wos, mqn, wq, qg, mk, mv, wmo, fn, wg, wu, wd, ts):
    B, S, D = x.shape
    n = len(attns)
    tok = lambda c: pl.BlockSpec((1, ts, c), lambda b, i: (b, i, 0))
    memspec = pl.BlockSpec((1,) + mk.shape[1:], lambda b, i: (b, 0, 0))
    consts_a = list(wos) + [mqn, wq, qg]
    consts_b = [wmo, fn, wg, wu, wd]
    return pl.pallas_call(
        functools.partial(_post_kernel, n_attn=n),
        grid=(B, S // ts),
        in_specs=([tok(D)] + [tok(a.shape[-1]) for a in attns] + [_const_spec(c.shape) for c in consts_a]
                  + [memspec, memspec] + [_const_spec(c.shape) for c in consts_b]),
        out_specs=tok(D),
        out_shape=jax.ShapeDtypeStruct((B, S, D), F32),
        compiler_params=_params(("parallel", "parallel")),
        name="post_block",
    )(x, *attns, *consts_a, mk, mv, *consts_b)


def _row(v):
    return v.reshape(1, -1).astype(F32)


def _shift_row(bound, first_lane):
    terms = jnp.stack(_split3(-jnp.asarray(bound, F32))).reshape(1, N_SPLIT)
    return jnp.pad(terms, ((0, 0), (first_lane, LANES - first_lane - N_SPLIT)))


def _mla_lane_perm():
    half = MLA_ROPE // 2
    lanes = np.arange(MLA_QK)
    return np.where(lanes < half, MLA_NOPE + lanes,
                    np.where(lanes < ROPE_HI_LANE, lanes - half,
                             np.where(lanes < ROPE_HI_LANE + half, lanes + half, lanes - MLA_ROPE)))


def _even_weights(w_in, w_uq, w_ukv, q_gain, k_gain):
    perm = _mla_lane_perm()
    is_nope = perm < MLA_NOPE
    pad = LANES - MLA_QK

    def head_cols(w_head, live):
        return jnp.pad(jnp.where(live[None, :], w_head[:, perm], 0.0), ((0, 0), (0, pad)))

    c_kr = EVEN_COLS[-1]
    kr_src = jnp.concatenate([jnp.zeros((w_in.shape[0], MLA_NOPE), w_in.dtype), w_in[:, c_kr:c_kr + MLA_ROPE]], axis=1)
    w1 = jnp.concatenate([w_in[:, :c_kr], head_cols(kr_src, ~is_nope)], axis=1).astype(BF16)
    wuq = jnp.concatenate(
        [head_cols(w_uq[:, MLA_QK * h:MLA_QK * (h + 1)], np.ones_like(is_nope)) for h in range(MLA_HEADS)],
        axis=1).astype(BF16)
    kv_w = MLA_NOPE + MLA_V
    rope_zeros = jnp.zeros((w_ukv.shape[0], MLA_ROPE), w_ukv.dtype)
    wuk = jnp.concatenate(
        [head_cols(jnp.concatenate([w_ukv[:, kv_w * h:kv_w * h + MLA_NOPE], rope_zeros], axis=1), is_nope)
         for h in range(MLA_HEADS)], axis=1).astype(BF16)
    wuv = jnp.concatenate(
        [w_ukv[:, kv_w * h + MLA_NOPE:kv_w * (h + 1)] for h in range(MLA_HEADS)], axis=1).astype(BF16).T
    gmq = _row(jnp.pad(q_gain[perm], (0, pad)))
    gmk = _row(jnp.pad(k_gain[perm], (0, pad)))
    return w1, wuq, wuk, wuv, gmq, gmk


def _rope_freqs():
    half = MLA_ROPE // 2
    freqs = (np.float32(ROPE_THETA) ** (-np.arange(half, dtype=np.float32) / np.float32(half))).astype(np.float32)
    return jnp.asarray(freqs.reshape(half, 1))


def _ones_blocks(n):
    g = np.arange(n) // HEAD_DIM
    return jnp.asarray((g[:, None] == g[None, :]).astype(np.float32), dtype=BF16)


def _tile_sizes(S):
    return min(4 * MXU_WIDTH, S), min(2 * MXU_WIDTH, S), min(2 * MXU_WIDTH, S - SWA_WINDOW), min(4 * MXU_WIDTH, S)


def kernel(x, mem, positions, mix_norm, ev_w_in, ev_swa_q_gain, ev_swa_k_gain, ev_sinks, ev_q_latent_norm,
           ev_kv_latent_norm, ev_w_uq, ev_w_ukv, ev_mla_q_gain, ev_mla_k_gain, ev_w_out, od_w_qkv, od_q_gain,
           od_k_gain, od_lambda, od_subln, od_w_out, mem_q_norm, mem_kv_norm, mem_w_q, mem_w_kv, mem_q_gain,
           mem_k_gain, mem_w_out, ffn_norm, ffn_w_gate, ffn_w_up, ffn_w_down):
    B, S, D = x.shape
    depth = mix_norm.shape[0]
    ts_proj, ts_post, tq_swa, tq_flash = _tile_sizes(S)

    ones = _ones_blocks(MXU_WIDTH)
    pos3 = positions.reshape(B, 1, S)
    inv = _rope_freqs()
    mem_k, mem_v = _mem_kv(mem, mem_kv_norm.reshape(depth, 1, D), mem_w_kv.astype(BF16),
                           mem_k_gain.reshape(depth, 1, MEM_HEAD_DIM))
    diff_slopes = [2.0 ** (-8.0 * (i + 1) / DIFF_HEADS) for i in range(DIFF_HEADS)]

    for l in range(depth):
        if l % 2 == 0:
            e = l // 2
            w1, wuq, wuk, wuv, gmq, gmk = _even_weights(ev_w_in[e], ev_w_uq[e], ev_w_ukv[e], ev_mla_q_gain[e],
                                                        ev_mla_k_gain[e])
            qa, ka, va, qm, km, vm = _even_proj(
                x, pos3, _row(mix_norm[l]), w1, ones, _row(jnp.tile(ev_swa_q_gain[e], SWA_HEADS)),
                _row(jnp.tile(ev_swa_k_gain[e], SWA_KV_HEADS)), _row(ev_q_latent_norm[e]),
                _row(ev_kv_latent_norm[e]), wuq, wuk, wuv, gmq, gmk,
                _shift_row(_score_bound(ev_mla_q_gain[e], ev_mla_k_gain[e], MLA_QK, MLA_QK ** -0.5 * LOG2E), MLA_QK),
                inv, ts_proj)
            swa_bound = _score_bound(ev_swa_q_gain[e], ev_swa_k_gain[e], HEAD_DIM, HEAD_DIM ** -0.5 * LOG2E)
            swa_scalars = jnp.concatenate([ev_sinks[e].astype(F32), swa_bound.reshape(1).astype(F32)])
            out_a = _swa_attention(swa_scalars, qa, ka, va, tq_swa)
            out_b = _mla_attention(qm, km, vm, tq_flash)
            wo = ev_w_out[e].astype(BF16)
            attns = [out_a, out_b]
            wos = [wo[:SWA_Q_W], wo[SWA_Q_W:]]
        else:
            o = l // 2
            lambda_init = 0.8 - 0.6 * math.exp(-0.3 * l)
            n_grp = DIFF_HEADS * 2
            w_qkv = od_w_qkv[o].astype(BF16)
            q, k, vt = _odd_proj(x, _row(mix_norm[l]), w_qkv[:, :2 * D], w_qkv[:, 2 * D:].T, ones,
                                 _row(jnp.tile(od_q_gain[o], n_grp // 2)), _row(jnp.tile(od_k_gain[o], n_grp // 2)),
                                 ts_proj)
            bound = _score_bound(od_q_gain[o], od_k_gain[o], DIFF_DIM, DIFF_DIM ** -0.5 * LOG2E)
            out_d = _diff_attention(diff_slopes, bound, q, k, vt, od_lambda[o].astype(F32), _row(od_subln[o]),
                                    tq_flash, lambda_init)
            attns = [out_d]
            wos = [od_w_out[o].astype(BF16)]
        x = _post_block(x, attns, wos, _row(mem_q_norm[l]), mem_w_q[l].astype(BF16), _row(mem_q_gain[l]),
                        mem_k[l], mem_v[l], mem_w_out[l].astype(BF16), _row(ffn_norm[l]),
                        ffn_w_gate[l].astype(BF16), ffn_w_up[l].astype(BF16), ffn_w_down[l].astype(BF16), ts_post)
    return x
```

```python
import functools
import math

import numpy as np
import jax
import jax.numpy as jnp
from jax import lax
from jax.experimental import pallas as pl
from jax.experimental.pallas import tpu as pltpu

F32 = jnp.float32
BF16 = jnp.bfloat16

EPS = 1e-6
MASKED = -1e30
LOG2E = 1.4426950408889634
ROPE_THETA = 10000.0

LANES = 128
SUBLANES = 8
MXU_WIDTH = 256
HEAD_DIM = 64
SWA_HEADS = 8
SWA_KV_HEADS = 2
SWA_WINDOW = 128
MLA_HEADS = 8
MLA_NOPE = 64
MLA_ROPE = 32
MLA_QK = MLA_NOPE + MLA_ROPE
MLA_V = 64
ROPE_HI_LANE = 64
DIFF_HEADS = 8
DIFF_DIM = 64
MEM_HEADS = 4
MEM_HEAD_DIM = 128
MLA_Q_RANK = 256
MLA_KV_RANK = 128
SWA_Q_W = SWA_HEADS * HEAD_DIM
SWA_KV_W = SWA_KV_HEADS * HEAD_DIM
EVEN_COLS = tuple(np.cumsum([SWA_Q_W, SWA_KV_W, SWA_KV_W, MLA_Q_RANK, MLA_KV_RANK]).tolist())

VMEM_LIMIT = 56 * 1024 * 1024

UNDERFLOW_GUARD = 2.0 ** -80
N_SPLIT = 3


def _params(sem):
    return pltpu.CompilerParams(dimension_semantics=sem, vmem_limit_bytes=VMEM_LIMIT)


def _const_spec(shape):
    nd = len(shape)
    return pl.BlockSpec(shape, lambda *_: (0,) * nd, pipeline_mode=pl.Buffered(1))


def _rms(x, g):
    ms = jnp.mean(x * x, axis=-1, keepdims=True)
    return x * lax.rsqrt(ms + EPS) * g


def _dot(a, b):
    return jnp.dot(a, b, preferred_element_type=F32)


def _dot_nt(a, b):
    return lax.dot_general(a, b, (((1,), (1,)), ((), ())), preferred_element_type=F32)


def _group_sumsq(x, ones_blk):
    x2 = x * x
    hi = x2.astype(BF16)
    lo = (x2 - hi.astype(F32)).astype(BF16)
    both = _dot(jnp.concatenate([hi, lo], axis=0), ones_blk)
    return both[:x.shape[0]] + both[x.shape[0]:]


def _group_norm64(x, ones_blk, gain):
    chunk = min(ones_blk.shape[0], x.shape[1])
    ssq = jnp.concatenate([_group_sumsq(x[:, c:c + chunk], ones_blk[0:chunk, 0:chunk])
                           for c in range(0, x.shape[1], chunk)], axis=1)
    return x * lax.rsqrt(ssq * (1.0 / HEAD_DIM) + EPS) * gain


def _even_proj_kernel(x_ref, pos_ref, gmix_ref, w1_ref, ones_ref, gq_ref, gk_ref, qln_ref, kvln_ref,
                      wuq_ref, wuk_ref, wuv_ref, gmq_ref, gmk_ref, qshift_ref, inv_ref,
                      qa_ref, ka_ref, va_ref, qm_ref, km_ref, vm_ref):
    x = x_ref[0]
    h = _rms(x, gmix_ref[...]).astype(BF16)
    z = _dot(h, w1_ref[...])
    c_ka, c_va, c_cq, c_ckv, c_kr = EVEN_COLS
    ones = ones_ref[...]
    qa = _group_norm64(z[:, 0:c_ka], ones, gq_ref[...]) * (HEAD_DIM ** -0.5 * LOG2E)
    qa_ref[0] = qa.astype(BF16)
    ka = _group_norm64(z[:, c_ka:c_va], ones[0:SWA_KV_W, 0:SWA_KV_W], gk_ref[...])
    ka_ref[0] = ka.astype(BF16)
    va_ref[0] = z[:, c_va:c_cq].astype(BF16)

    cqn = _rms(z[:, c_cq:c_ckv], qln_ref[...]).astype(BF16)
    ckvn = _rms(z[:, c_ckv:c_kr], kvln_ref[...]).astype(BF16)
    kr = z[:, c_kr:c_kr + LANES]
    qf = _dot(cqn, wuq_ref[...])
    kn = _dot(ckvn, wuk_ref[...])
    vm_ref[0] = _dot_nt(wuv_ref[...], ckvn).astype(BF16)

    half = MLA_ROPE // 2
    ang = inv_ref[...] * pos_ref[0].astype(F32)
    pad = jnp.zeros((LANES - half, ang.shape[1]), F32)
    cos = jnp.concatenate([jnp.cos(ang), pad], axis=0).T
    sin = jnp.concatenate([jnp.sin(ang), pad], axis=0).T
    lane = lax.broadcasted_iota(jnp.int32, cos.shape, 1)
    in_lo = lane < half
    in_hi = (lane >= ROPE_HI_LANE) & (lane < ROPE_HI_LANE + half)
    cos_t = jnp.where(in_lo, cos, jnp.where(in_hi, pltpu.roll(cos, ROPE_HI_LANE, 1), 1.0))
    sin_t = jnp.where(in_lo, -sin, jnp.where(in_hi, pltpu.roll(sin, ROPE_HI_LANE, 1), 0.0))

    pair = 2 * LANES
    gi = lax.broadcasted_iota(jnp.int32, (pair, pair), 0) // LANES
    gj = lax.broadcasted_iota(jnp.int32, (pair, pair), 1) // LANES
    head_ones = (gi == gj).astype(BF16)

    def norm_rope(xp, gain):
        r = lax.rsqrt(_group_sumsq(xp, head_ones) * (1.0 / MLA_QK) + EPS)
        xn = xp * r * gain
        halves = []
        for t in range(2):
            xh = xn[:, LANES * t:LANES * (t + 1)]
            halves.append(xh * cos_t + pltpu.roll(xh, ROPE_HI_LANE, 1) * sin_t)
        return halves

    gmq = jnp.concatenate([gmq_ref[...]] * 2, axis=1)
    gmk = jnp.concatenate([gmk_ref[...]] * 2, axis=1)
    kr2 = jnp.concatenate([kr, kr], axis=1)
    shift_lanes = (lane >= MLA_QK) & (lane < MLA_QK + N_SPLIT)
    qshift = qshift_ref[...]
    for hp in range(MLA_HEADS // 2):
        sl = slice(pair * hp, pair * (hp + 1))
        q_heads = norm_rope(qf[:, sl], gmq)
        k_heads = norm_rope(kn[:, sl] + kr2, gmk)
        for t in range(2):
            hl = slice(pair * hp + LANES * t, pair * hp + LANES * (t + 1))
            qm_ref[0, :, hl] = jnp.where(shift_lanes, qshift, q_heads[t] * (MLA_QK ** -0.5 * LOG2E)).astype(BF16)
            km_ref[0, :, hl] = jnp.where(shift_lanes, 1.0, k_heads[t]).astype(BF16)


def _even_proj(x, pos3, gmix, w1, ones, gq, gk, qln, kvln, wuq, wuk, wuv, gmq, gmk, qshift, inv, ts):
    B, S, D = x.shape
    grid = (B, S // ts)
    tok = lambda c: pl.BlockSpec((1, ts, c), lambda b, i: (b, i, 0))
    consts = [gmix, w1, ones, gq, gk, qln, kvln, wuq, wuk, wuv, gmq, gmk, qshift, inv]
    out_cols = [SWA_Q_W, SWA_KV_W, SWA_KV_W, MLA_HEADS * LANES, MLA_HEADS * LANES]
    v_rows = MLA_HEADS * MLA_V
    return pl.pallas_call(
        _even_proj_kernel,
        grid=grid,
        in_specs=([tok(D), pl.BlockSpec((1, 1, ts), lambda b, i: (b, 0, i))]
                  + [_const_spec(c.shape) for c in consts]),
        out_specs=[tok(c) for c in out_cols] + [pl.BlockSpec((1, v_rows, ts), lambda b, i: (b, 0, i))],
        out_shape=([jax.ShapeDtypeStruct((B, S, c), BF16) for c in out_cols]
                   + [jax.ShapeDtypeStruct((B, v_rows, S), BF16)]),
        compiler_params=_params(("parallel", "parallel")),
        name="even_proj",
    )(x, pos3, *consts)


def _swa_place_q(q_ref, hd, lane):
    hk = hd // (SWA_HEADS // SWA_KV_HEADS)
    qq = q_ref[0, :, LANES * (hd // 2):LANES * (hd // 2 + 1)].astype(F32)
    xq = qq if hd % 2 == hk else pltpu.roll(qq, HEAD_DIM, 1)
    return jnp.where((lane >= HEAD_DIM * hk) & (lane < HEAD_DIM * (hk + 1)), xq, 0.0).astype(BF16)


def _swa_place_out(o_even, o_odd, cb, lane):
    placed = []
    for e, o in enumerate((o_even, o_odd)):
        hk = (2 * cb + e) // (SWA_HEADS // SWA_KV_HEADS)
        placed.append(o if e == hk else pltpu.roll(o, HEAD_DIM, 1))
    return jnp.where(lane < HEAD_DIM, placed[0], placed[1]).astype(BF16)


def _swa_running_max(sink_ref, q_ref, k_ref, v_ref, o_ref, qi, tq):
    win = tq + SWA_WINDOW
    wstart = pl.multiple_of(jnp.maximum(qi * tq - SWA_WINDOW, 0), SWA_WINDOW)
    kw = k_ref[0, pl.ds(wstart, win), :]
    vw = v_ref[0, pl.ds(wstart, win), :]
    row = lax.broadcasted_iota(jnp.int32, (tq, win), 0)
    col = lax.broadcasted_iota(jnp.int32, (tq, win), 1)
    dist = (qi * tq + row) - (wstart + col)
    valid = (dist >= 0) & (dist < SWA_WINDOW)
    distf = dist.astype(F32)
    lane = lax.broadcasted_iota(jnp.int32, (tq, LANES), 1)
    for cb in range(SWA_HEADS // 2):
        outs = []
        for e in range(2):
            hd = 2 * cb + e
            s = _dot_nt(_swa_place_q(q_ref, hd, lane), kw) - (2.0 ** (-8.0 * (hd + 1) / SWA_HEADS) * LOG2E) * distf
            s = jnp.where(valid, s, MASKED)
            sink = sink_ref[hd] * LOG2E
            m = jnp.maximum(jnp.max(s, axis=-1, keepdims=True), sink)
            p = jnp.exp2(s - m)
            l = jnp.sum(p, axis=-1, keepdims=True) + jnp.exp2(sink - m)
            outs.append(_dot(p.astype(BF16), vw) / l)
        o_ref[0, :, LANES * cb:LANES * (cb + 1)] = _swa_place_out(outs[0], outs[1], cb, lane)


def _swa_kernel(sink_ref, q_ref, k_ref, v_ref, o_ref, *, tq):
    qi = pl.program_id(1)
    sub = SWA_WINDOW
    win = 2 * sub
    n_sub = tq // sub
    bound = sink_ref[SWA_HEADS]
    lane_q = lax.broadcasted_iota(jnp.int32, (tq, LANES), 1)
    lane = lax.broadcasted_iota(jnp.int32, (sub, LANES), 1)
    row = lax.broadcasted_iota(jnp.int32, (sub, win), 0)
    col = lax.broadcasted_iota(jnp.int32, (sub, win), 1)

    def bias_mask(first_key_offset, slope):
        dist = row + first_key_offset - col
        return jnp.where((dist >= 0) & (dist < SWA_WINDOW), -slope * dist.astype(F32) - bound, MASKED)

    ones_blk = jnp.ones((win, LANES), BF16)
    windows = []
    for r in range(n_sub):
        wstart = qi * tq + sub * (r - 1)
        wstart = pl.multiple_of(jnp.maximum(wstart, 0) if r == 0 else wstart, sub)
        windows.append((k_ref[0, pl.ds(wstart, win), :],
                        jnp.concatenate([v_ref[0, pl.ds(wstart, win), :], ones_blk], axis=1)))

    outs = [[None] * SWA_HEADS for _ in range(n_sub)]
    min_sum = None
    for hd in range(SWA_HEADS):
        slope = 2.0 ** (-8.0 * (hd + 1) / SWA_HEADS) * LOG2E
        bm = bias_mask(sub, slope)
        bm_first = jnp.where(qi == 0, bias_mask(0, slope), bm)
        xq = _swa_place_q(q_ref, hd, lane_q)
        sink_term = jnp.exp2(sink_ref[hd] * LOG2E - bound)
        for r in range(n_sub):
            kw, v_ones = windows[r]
            s = _dot_nt(xq[sub * r:sub * (r + 1)], kw) + (bm_first if r == 0 else bm)
            ov = _dot(jnp.exp2(s).astype(BF16), v_ones)
            l = ov[:, LANES:] + sink_term
            outs[r][hd] = ov[:, :LANES] / l
            min_sum = l if min_sum is None else jnp.minimum(min_sum, l)
    for r in range(n_sub):
        for cb in range(SWA_HEADS // 2):
            o_ref[0, sub * r:sub * (r + 1), LANES * cb:LANES * (cb + 1)] = _swa_place_out(
                outs[r][2 * cb], outs[r][2 * cb + 1], cb, lane)

    @pl.when(jnp.logical_not(jnp.min(min_sum) >= UNDERFLOW_GUARD))
    def _():
        _swa_running_max(sink_ref, q_ref, k_ref, v_ref, o_ref, qi, tq)


def _swa_attention(sinks, qa, ka, va, tq):
    B, S, _ = qa.shape
    return pl.pallas_call(
        functools.partial(_swa_kernel, tq=tq),
        grid=(B, S // tq),
        in_specs=[
            pl.BlockSpec(memory_space=pltpu.SMEM),
            pl.BlockSpec((1, tq, SWA_Q_W), lambda b, i: (b, i, 0)),
            pl.BlockSpec((1, S, SWA_KV_W), lambda b, i: (b, 0, 0)),
            pl.BlockSpec((1, S, SWA_KV_W), lambda b, i: (b, 0, 0)),
        ],
        out_specs=pl.BlockSpec((1, tq, SWA_Q_W), lambda b, i: (b, i, 0)),
        out_shape=jax.ShapeDtypeStruct((B, S, SWA_Q_W), BF16),
        compiler_params=_params(("parallel", "arbitrary")),
        name="swa_attention",
    )(sinks, qa, ka, va)


def _split3(x):
    hi = x.astype(BF16).astype(F32)
    r = x - hi
    mid = r.astype(BF16).astype(F32)
    lo = (r - mid).astype(BF16).astype(F32)
    return hi, mid, lo


def _score_bound(q_gain, k_gain, dim, scale):
    return dim * scale * jnp.max(jnp.abs(q_gain)) * jnp.max(jnp.abs(k_gain))


BLOCKS_PER_MATMUL = 2
BLOCKS_PER_TRIP = 4


def _loop_blocks(n, step, init, tq):
    def trip(t, c):
        for u in range(0, BLOCKS_PER_TRIP, BLOCKS_PER_MATMUL):
            c = step(BLOCKS_PER_TRIP * t + u, c, 0, BLOCKS_PER_MATMUL * tq)
        return c

    full = n // BLOCKS_PER_TRIP
    carry = lax.fori_loop(0, full, trip, init)
    return lax.fori_loop(full * BLOCKS_PER_TRIP, n, step, carry)


def _add_cols(acc, lo, part):
    if lo == 0:
        return acc + part
    return acc + jnp.concatenate([jnp.zeros((acc.shape[0], lo), acc.dtype), part], axis=1)


def _diagonal_spans(tq):
    h = tq // 2
    return ((0, h, 0), (h, h, h)) if h % MXU_WIDTH == 0 else ((0, tq, 0),)


def _causal_t(key_lo, key_n, q_lo, tq):
    shape = (key_n, tq - q_lo)
    return (key_lo + lax.broadcasted_iota(jnp.int32, shape, 0)) <= (q_lo + lax.broadcasted_iota(jnp.int32, shape, 1))


def _causal_mask(tq):
    row = lax.broadcasted_iota(jnp.int32, (tq, tq), 0)
    col = lax.broadcasted_iota(jnp.int32, (tq, tq), 1)
    return row >= col


def _mla_running_max(q_ref, k_ref, vt_ref, e, qi, tq):
    causal = _causal_mask(tq)
    q = q_ref[0, :, LANES * e:LANES * (e + 1)]

    def step(j, carry, masked):
        m, l, acc = carry
        ks = pl.multiple_of(j * tq, tq)
        k = k_ref[0, pl.ds(ks, tq), LANES * e:LANES * (e + 1)]
        vt = vt_ref[0, :, pl.ds(ks, tq)]
        s = _dot_nt(q, k)
        if masked:
            s = jnp.where(causal, s, MASKED)
        m_new = jnp.maximum(m, jnp.max(s, axis=-1, keepdims=True))
        alpha = jnp.exp2(m - m_new)
        p = jnp.exp2(s - m_new)
        l = alpha * l + jnp.sum(p, axis=-1, keepdims=True)
        acc = alpha * acc + _dot_nt(p.astype(BF16), vt)
        return m_new, l, acc

    init = (jnp.full((tq, 1), MASKED, F32), jnp.zeros((tq, 1), F32), jnp.zeros((tq, LANES), F32))
    carry = lax.fori_loop(0, qi, lambda j, c: step(j, c, False), init)
    m, l, acc = step(qi, carry, True)
    return acc / l


def _mla_kernel(q_ref, k_ref, vt_ref, o_ref, *, tq):
    qi = pl.program_id(2)
    lane = lax.broadcasted_iota(jnp.int32, (tq, LANES), 1)
    q = q_ref[0]
    first = lax.broadcasted_iota(jnp.int32, q.shape, 1) < LANES
    zero_q = jnp.zeros_like(q)
    q_aug = (jnp.where(first, q, zero_q), jnp.where(first, zero_q, q))

    def block(j, carry, key_lo=0, key_n=tq, q_lo=0, masked=False):
        ks = pl.multiple_of(j * tq + key_lo, key_n)
        n = tq - q_lo
        q_both = jnp.concatenate([q_aug[0][q_lo:, :], q_aug[1][q_lo:, :]], axis=0)
        st = _dot_nt(k_ref[0, pl.ds(ks, key_n), :], q_both)
        if masked:
            causal = _causal_t(key_lo, key_n, q_lo, tq)
            st = jnp.where(jnp.concatenate([causal, causal], axis=1), st, MASKED)
        pt = jnp.exp2(st)
        part = jnp.sum(pt.reshape(key_n // SUBLANES, SUBLANES, 2 * n), axis=0)
        pv = _dot(vt_ref[0, :, pl.ds(ks, key_n)], pt.astype(BF16))
        return tuple((_add_cols(carry[e][0], q_lo, part[:, n * e:n * (e + 1)]),
                      _add_cols(carry[e][1], q_lo, pv[MLA_V * e:MLA_V * (e + 1), n * e:n * (e + 1)]))
                     for e in range(2))

    zero = (jnp.zeros((SUBLANES, tq), F32), jnp.zeros((MLA_V, tq), F32))
    carry = _loop_blocks(qi, block, (zero, zero), tq)
    for key_lo, key_n, q_lo in _diagonal_spans(tq):
        carry = block(qi, carry, key_lo, key_n, q_lo, masked=True)
    sums = [jnp.sum(lsum, axis=0, keepdims=True) for lsum, _ in carry]
    outs = [acc / l for (_, acc), l in zip(carry, sums)]
    o_ref[0] = jnp.concatenate(outs, axis=0).T.astype(BF16)

    @pl.when(jnp.logical_not(jnp.min(jnp.minimum(sums[0], sums[1])) >= UNDERFLOW_GUARD))
    def _():
        slow = [_mla_running_max(q_ref, k_ref, vt_ref, e, qi, tq) for e in range(2)]
        o_ref[0] = jnp.where(lane < MLA_V, slow[0], slow[1]).astype(BF16)


def _mla_attention(qm, km, vm, tq):
    B, S, _ = qm.shape
    return pl.pallas_call(
        functools.partial(_mla_kernel, tq=tq),
        grid=(B, MLA_HEADS // 2, S // tq),
        in_specs=[
            pl.BlockSpec((1, tq, 2 * LANES), lambda b, h, i: (b, i, h)),
            pl.BlockSpec((1, S, 2 * LANES), lambda b, h, i: (b, 0, h)),
            pl.BlockSpec((1, 2 * MLA_V, S), lambda b, h, i: (b, h, 0)),
        ],
        out_specs=pl.BlockSpec((1, tq, LANES), lambda b, h, i: (b, i, h)),
        out_shape=jax.ShapeDtypeStruct((B, S, MLA_HEADS * MLA_V), BF16),
        compiler_params=_params(("parallel", "parallel", "arbitrary")),
        name="mla_attention",
    )(qm, km, vm)


def _odd_proj_kernel(x_ref, gmix_ref, w_ref, wvt_ref, ones_ref, gq_ref, gk_ref, q_ref, k_ref, vt_ref):
    x = x_ref[0]
    h = _rms(x, gmix_ref[...]).astype(BF16)
    z = _dot(h, w_ref[...])
    ones = ones_ref[...]
    width = DIFF_HEADS * 2 * DIFF_DIM
    piece = gq_ref.shape[1]
    for c in range(0, width, piece):
        q = _group_norm64(z[:, c:c + piece], ones, gq_ref[...]) * (DIFF_DIM ** -0.5 * LOG2E)
        q_ref[0, :, c:c + piece] = q.astype(BF16)
        k = _group_norm64(z[:, width + c:width + c + piece], ones, gk_ref[...])
        k_ref[0, :, c:c + piece] = k.astype(BF16)
    vt_ref[0] = _dot_nt(wvt_ref[...], h).astype(BF16)


def _odd_proj(x, gmix, w_qk, w_vt, ones, gq, gk, ts):
    B, S, D = x.shape
    tok = pl.BlockSpec((1, ts, D), lambda b, i: (b, i, 0))
    consts = [gmix, w_qk, w_vt, ones, gq, gk]
    return pl.pallas_call(
        _odd_proj_kernel,
        grid=(B, S // ts),
        in_specs=[tok] + [_const_spec(c.shape) for c in consts],
        out_specs=[tok, tok, pl.BlockSpec((1, D, ts), lambda b, i: (b, 0, i))],
        out_shape=[jax.ShapeDtypeStruct((B, S, D), BF16)] * 2 + [jax.ShapeDtypeStruct((B, D, S), BF16)],
        compiler_params=_params(("parallel", "parallel")),
        name="odd_proj",
    )(x, *consts)


def _diff_q_parts(q, lane):
    zero = jnp.zeros_like(q)
    return jnp.where(lane < DIFF_DIM, q, zero), jnp.where(lane >= DIFF_DIM, q, zero)


def _diff_running_max(slope, q_ref, k_ref, vt_ref, qi, tq):
    row = lax.broadcasted_iota(jnp.int32, (tq, tq), 0)
    col = lax.broadcasted_iota(jnp.int32, (tq, tq), 1)
    causal = row >= col
    bias = slope * (col - row).astype(F32)
    lane = lax.broadcasted_iota(jnp.int32, (tq, LANES), 1)
    q_parts = _diff_q_parts(q_ref[0], lane)

    def step(j, carry, masked):
        ks = pl.multiple_of(j * tq, tq)
        k = k_ref[0, pl.ds(ks, tq), :]
        vt = vt_ref[0, :, pl.ds(ks, tq)]
        offset = slope * ((j - qi) * tq).astype(F32)
        new = []
        for c in range(2):
            m, l, acc = carry[c]
            s = _dot_nt(q_parts[c], k) + bias
            if masked:
                s = jnp.where(causal, s, MASKED)
            m_new = jnp.maximum(m, jnp.max(s, axis=-1, keepdims=True) + offset)
            alpha = jnp.exp2(m - m_new)
            p = jnp.exp2(s - (m_new - offset))
            l = alpha * l + jnp.sum(p, axis=-1, keepdims=True)
            acc = alpha * acc + _dot_nt(p.astype(BF16), vt)
            new.append((m_new, l, acc))
        return tuple(new)

    one = (jnp.full((tq, 1), MASKED, F32), jnp.zeros((tq, 1), F32), jnp.zeros((tq, LANES), F32))
    carry = lax.fori_loop(0, qi, lambda j, c: step(j, c, False), (one, one))
    (m1, l1, a1), (m2, l2, a2) = step(qi, carry, True)
    return a1 / l1, a2 / l2


def _diff_kernel(slope_ref, qx_ref, qpos_ref, q_ref, k_ref, kx_ref, vt_ref, lam_ref, subln_ref, o_ref, *, tq,
                 lambda_init):
    hd = pl.program_id(1)
    qi = pl.program_id(2)
    lane = lax.broadcasted_iota(jnp.int32, (tq, LANES), 1)
    q_parts = _diff_q_parts(q_ref[0], lane)
    head_row = qx_ref[0]
    qx = jnp.where(lane < 3 * N_SPLIT, head_row,
                   qpos_ref[...].astype(F32) * head_row[:, SLOPE_LANE:SLOPE_LANE + 1]).astype(BF16)
    q_aug = [jnp.concatenate([part, qx], axis=1) for part in q_parts]
    lf = lam_ref[...]
    lam = (jnp.exp(jnp.sum(lf[0:1] * lf[1:2], axis=-1, keepdims=True))
           - jnp.exp(jnp.sum(lf[2:3] * lf[3:4], axis=-1, keepdims=True)) + lambda_init)
    gain = subln_ref[...] * (1.0 - lambda_init)

    def block(j, carry, key_lo=0, key_n=tq, q_lo=0, masked=False):
        ks = pl.multiple_of(j * tq + key_lo, key_n)
        k = jnp.concatenate([k_ref[0, pl.ds(ks, key_n), :], kx_ref[pl.ds(ks, key_n), :]], axis=1)
        vt = vt_ref[0, :, pl.ds(ks, key_n)]
        n = tq - q_lo
        q_both = jnp.concatenate([q_aug[0][q_lo:, :], q_aug[1][q_lo:, :]], axis=0)
        st = _dot_nt(k, q_both)
        if masked:
            causal = _causal_t(key_lo, key_n, q_lo, tq)
            st = jnp.where(jnp.concatenate([causal, causal], axis=1), st, MASKED)
        pt = jnp.exp2(st)
        part = jnp.sum(pt.reshape(key_n // SUBLANES, SUBLANES, 2 * n), axis=0)
        pv = _dot(vt, pt.astype(BF16))
        return tuple((_add_cols(carry[c][0], q_lo, part[:, n * c:n * (c + 1)]),
                      _add_cols(carry[c][1], q_lo, pv[:, n * c:n * (c + 1)])) for c in range(2))

    zero = (jnp.zeros((SUBLANES, tq), F32), jnp.zeros((LANES, tq), F32))
    carry = _loop_blocks(qi, block, (zero, zero), tq)
    for key_lo, key_n, q_lo in _diagonal_spans(tq):
        carry = block(qi, carry, key_lo, key_n, q_lo, masked=True)
    (ls1, a1), (ls2, a2) = carry
    l1 = jnp.sum(ls1, axis=0, keepdims=True)
    l2 = jnp.sum(ls2, axis=0, keepdims=True)

    ot = a1 / l1 - lam * (a2 / l2)
    ot = ot * lax.rsqrt(jnp.mean(ot * ot, axis=0, keepdims=True) + EPS)
    o_ref[0] = (ot.T * gain).astype(BF16)

    @pl.when(jnp.logical_not(jnp.min(jnp.minimum(l1, l2)) >= UNDERFLOW_GUARD))
    def _():
        o1, o2 = _diff_running_max(slope_ref[hd] * LOG2E, q_ref, k_ref, vt_ref, qi, tq)
        o_ref[0] = _rms(o1 - lam * o2, gain).astype(BF16)


SLOPE_LANE = 5 * N_SPLIT


def _alibi_tables(slopes, bound, S):
    assert all(math.log2(s) == int(math.log2(s)) for s in slopes), "ALiBi slopes must be powers of two"
    n = N_SPLIT
    log2e_terms = jnp.concatenate(_split3(jnp.full((1, 1), LOG2E, F32)), axis=1)
    s = jnp.asarray(slopes, F32).reshape(-1, 1)
    head = jnp.concatenate(
        [s * log2e_terms * float(LANES), s * log2e_terms, jnp.broadcast_to(_shift_row(bound, 0)[:, :n], (len(slopes), n)),
         jnp.zeros((len(slopes), SLOPE_LANE - 3 * n), F32), s, jnp.zeros((len(slopes), LANES - SLOPE_LANE - 1), F32)],
        axis=1)
    idx = np.arange(S)
    hi, lo = (idx // LANES)[:, None].astype(np.float32), (idx % LANES)[:, None].astype(np.float32)
    qpos = np.zeros((S, LANES), np.float32)
    qpos[:, 3 * n:4 * n] = hi * LANES
    qpos[:, 4 * n:5 * n] = lo
    kx = jnp.concatenate(
        [jnp.asarray(np.concatenate([np.repeat(hi, n, 1), np.repeat(lo, n, 1), np.ones((S, n), np.float32)], axis=1)),
         jnp.broadcast_to(-log2e_terms, (S, n)), jnp.broadcast_to(-log2e_terms, (S, n)),
         jnp.zeros((S, LANES - 5 * n), F32)], axis=1)
    return head.reshape(-1, 1, LANES), jnp.asarray(qpos, dtype=BF16), kx.astype(BF16)


def _diff_attention(slopes, bound, q, k, vt, lambdas, subln, tq, lambda_init):
    B, S, D = q.shape
    qx, qpos, kx = _alibi_tables(slopes, bound, S)
    return pl.pallas_call(
        functools.partial(_diff_kernel, tq=tq, lambda_init=lambda_init),
        grid=(B, DIFF_HEADS, S // tq),
        in_specs=[
            pl.BlockSpec(memory_space=pltpu.SMEM),
            pl.BlockSpec((1, 1, LANES), lambda b, h, i: (h, 0, 0)),
            pl.BlockSpec((tq, LANES), lambda b, h, i: (i, 0)),
            pl.BlockSpec((1, tq, LANES), lambda b, h, i: (b, i, h)),
            pl.BlockSpec((1, S, LANES), lambda b, h, i: (b, 0, h)),
            _const_spec(kx.shape),
            pl.BlockSpec((1, LANES, S), lambda b, h, i: (b, h, 0)),
            pl.BlockSpec(lambdas.shape, lambda b, h, i: (0, 0)),
            pl.BlockSpec(subln.shape, lambda b, h, i: (0, 0)),
        ],
        out_specs=pl.BlockSpec((1, tq, LANES), lambda b, h, i: (b, i, h)),
        out_shape=jax.ShapeDtypeStruct((B, S, D), BF16),
        compiler_params=_params(("parallel", "parallel", "arbitrary")),
        name="diff_attention",
    )(jnp.asarray(slopes, F32), qx, qpos, q, k, kx, vt, lambdas, subln)


def _mem_kv_kernel(mem_ref, norm_ref, w_ref, gain_ref, k_ref, v_ref):
    h = _rms(mem_ref[0], norm_ref[0]).astype(BF16)
    z = _dot(h, w_ref[0])
    width = MEM_HEADS * MEM_HEAD_DIM
    gain = gain_ref[0]
    for hd in range(MEM_HEADS):
        sl = slice(MEM_HEAD_DIM * hd, MEM_HEAD_DIM * (hd + 1))
        k_ref[0, 0, :, sl] = _rms(z[:, sl], gain).astype(BF16)
    v_ref[0, 0] = z[:, width:2 * width].astype(BF16)


def _mem_kv(mem, norms, w_kv, gains):
    B, M, D = mem.shape
    L = norms.shape[0]
    width = MEM_HEADS * MEM_HEAD_DIM
    out = pl.BlockSpec((1, 1, M, width), lambda l, b: (l, b, 0, 0))
    return pl.pallas_call(
        _mem_kv_kernel,
        grid=(L, B),
        in_specs=[
            pl.BlockSpec((1, M, D), lambda l, b: (b, 0, 0)),
            pl.BlockSpec((1, 1, D), lambda l, b: (l, 0, 0)),
            pl.BlockSpec((1, D, 2 * width), lambda l, b: (l, 0, 0)),
            pl.BlockSpec((1, 1, MEM_HEAD_DIM), lambda l, b: (l, 0, 0)),
        ],
        out_specs=[out, out],
        out_shape=[jax.ShapeDtypeStruct((L, B, M, width), BF16)] * 2,
        compiler_params=_params(("parallel", "parallel")),
        name="mem_kv",
    )(mem, norms, w_kv, gains)


def _post_kernel(*refs, n_attn):
    x_ref = refs[0]
    attn_refs = refs[1:1 + n_attn]
    wo_refs = refs[1 + n_attn:1 + 2 * n_attn]
    (mqn_ref, wq_ref, qg_ref, mk_ref, mv_ref, wmo_ref, fn_ref, wg_ref, wu_ref, wd_ref, o_ref) = refs[1 + 2 * n_attn:]
    x = x_ref[0]
    for a_ref, w_ref in zip(attn_refs, wo_refs):
        x = x + _dot(a_ref[0], w_ref[...])

    q = _dot(_rms(x, mqn_ref[...]).astype(BF16), wq_ref[...])
    heads = []
    for hd in range(MEM_HEADS):
        sl = slice(MEM_HEAD_DIM * hd, MEM_HEAD_DIM * (hd + 1))
        qh = (_rms(q[:, sl], qg_ref[...]) * (MEM_HEAD_DIM ** -0.5 * LOG2E)).astype(BF16)
        s = _dot_nt(qh, mk_ref[0, :, sl])
        p = jnp.exp2(s - jnp.max(s, axis=-1, keepdims=True))
        l = jnp.sum(p, axis=-1, keepdims=True)
        heads.append((_dot(p.astype(BF16), mv_ref[0, :, sl]) / l).astype(BF16))
    x = x + _dot(jnp.concatenate(heads, axis=-1), wmo_ref[...])

    hf = _rms(x, fn_ref[...]).astype(BF16)
    g = _dot(hf, wg_ref[...])
    u = _dot(hf, wu_ref[...])
    act = (g * jax.nn.sigmoid(g) * u).astype(BF16)
    o_ref[0] = x + _dot(act, wd_ref[...])


def _post_block(x, attns, wos, mqn, wq, qg, mk, mv, wmo, fn, wg, wu, wd, ts):
    B, S, D = x.shape
    n = len(attns)
    tok = lambda c: pl.BlockSpec((1, ts, c), lambda b, i: (b, i, 0))
    memspec = pl.BlockSpec((1,) + mk.shape[1:], lambda b, i: (b, 0, 0))
    consts_a = list(wos) + [mqn, wq, qg]
    consts_b = [wmo, fn, wg, wu, wd]
    return pl.pallas_call(
        functools.partial(_post_kernel, n_attn=n),
        grid=(B, S // ts),
        in_specs=([tok(D)] + [tok(a.shape[-1]) for a in attns] + [_const_spec(c.shape) for c in consts_a]
                  + [memspec, memspec] + [_const_spec(c.shape) for c in consts_b]),
        out_specs=tok(D),
        out_shape=jax.ShapeDtypeStruct((B, S, D), F32),
        compiler_params=_params(("parallel", "parallel")),
        name="post_block",
    )(x, *attns, *consts_a, mk, mv, *consts_b)


def _row(v):
    return v.reshape(1, -1).astype(F32)


def _shift_row(bound, first_lane):
    terms = jnp.stack(_split3(-jnp.asarray(bound, F32))).reshape(1, N_SPLIT)
    return jnp.pad(terms, ((0, 0), (first_lane, LANES - first_lane - N_SPLIT)))


def _mla_lane_perm():
    half = MLA_ROPE // 2
    lanes = np.arange(MLA_QK)
    return np.where(lanes < half, MLA_NOPE + lanes,
                    np.where(lanes < ROPE_HI_LANE, lanes - half,
                             np.where(lanes < ROPE_HI_LANE + half, lanes + half, lanes - MLA_ROPE)))


def _even_weights(w_in, w_uq, w_ukv, q_gain, k_gain):
    perm = _mla_lane_perm()
    is_nope = perm < MLA_NOPE
    pad = LANES - MLA_QK

    def head_cols(w_head, live):
        return jnp.pad(jnp.where(live[None, :], w_head[:, perm], 0.0), ((0, 0), (0, pad)))

    c_kr = EVEN_COLS[-1]
    kr_src = jnp.concatenate([jnp.zeros((w_in.shape[0], MLA_NOPE), w_in.dtype), w_in[:, c_kr:c_kr + MLA_ROPE]], axis=1)
    w1 = jnp.concatenate([w_in[:, :c_kr], head_cols(kr_src, ~is_nope)], axis=1).astype(BF16)
    wuq = jnp.concatenate(
        [head_cols(w_uq[:, MLA_QK * h:MLA_QK * (h + 1)], np.ones_like(is_nope)) for h in range(MLA_HEADS)],
        axis=1).astype(BF16)
    kv_w = MLA_NOPE + MLA_V
    rope_zeros = jnp.zeros((w_ukv.shape[0], MLA_ROPE), w_ukv.dtype)
    wuk = jnp.concatenate(
        [head_cols(jnp.concatenate([w_ukv[:, kv_w * h:kv_w * h + MLA_NOPE], rope_zeros], axis=1), is_nope)
         for h in range(MLA_HEADS)], axis=1).astype(BF16)
    wuv = jnp.concatenate(
        [w_ukv[:, kv_w * h + MLA_NOPE:kv_w * (h + 1)] for h in range(MLA_HEADS)], axis=1).astype(BF16).T
    gmq = _row(jnp.pad(q_gain[perm], (0, pad)))
    gmk = _row(jnp.pad(k_gain[perm], (0, pad)))
    return w1, wuq, wuk, wuv, gmq, gmk


def _rope_freqs():
    half = MLA_ROPE // 2
    freqs = (np.float32(ROPE_THETA) ** (-np.arange(half, dtype=np.float32) / np.float32(half))).astype(np.float32)
    return jnp.asarray(freqs.reshape(half, 1))


def _ones_blocks(n):
    g = np.arange(n) // HEAD_DIM
    return jnp.asarray((g[:, None] == g[None, :]).astype(np.float32), dtype=BF16)


def _tile_sizes(S):
    return min(4 * MXU_WIDTH, S), min(2 * MXU_WIDTH, S), min(2 * MXU_WIDTH, S - SWA_WINDOW), min(4 * MXU_WIDTH, S)


def kernel(x, mem, positions, mix_norm, ev_w_in, ev_swa_q_gain, ev_swa_k_gain, ev_sinks, ev_q_latent_norm,
           ev_kv_latent_norm, ev_w_uq, ev_w_ukv, ev_mla_q_gain, ev_mla_k_gain, ev_w_out, od_w_qkv, od_q_gain,
           od_k_gain, od_lambda, od_subln, od_w_out, mem_q_norm, mem_kv_norm, mem_w_q, mem_w_kv, mem_q_gain,
           mem_k_gain, mem_w_out, ffn_norm, ffn_w_gate, ffn_w_up, ffn_w_down):
    B, S, D = x.shape
    depth = mix_norm.shape[0]
    ts_proj, ts_post, tq_swa, tq_flash = _tile_sizes(S)

    ones = _ones_blocks(MXU_WIDTH)
    pos3 = positions.reshape(B, 1, S)
    inv = _rope_freqs()
    mem_k, mem_v = _mem_kv(mem, mem_kv_norm.reshape(depth, 1, D), mem_w_kv.astype(BF16),
                           mem_k_gain.reshape(depth, 1, MEM_HEAD_DIM))
    diff_slopes = [2.0 ** (-8.0 * (i + 1) / DIFF_HEADS) for i in range(DIFF_HEADS)]

    for l in range(depth):
        if l % 2 == 0:
            e = l // 2
            w1, wuq, wuk, wuv, gmq, gmk = _even_weights(ev_w_in[e], ev_w_uq[e], ev_w_ukv[e], ev_mla_q_gain[e],
                                                        ev_mla_k_gain[e])
            qa, ka, va, qm, km, vm = _even_proj(
                x, pos3, _row(mix_norm[l]), w1, ones, _row(jnp.tile(ev_swa_q_gain[e], SWA_HEADS)),
                _row(jnp.tile(ev_swa_k_gain[e], SWA_KV_HEADS)), _row(ev_q_latent_norm[e]),
                _row(ev_kv_latent_norm[e]), wuq, wuk, wuv, gmq, gmk,
                _shift_row(_score_bound(ev_mla_q_gain[e], ev_mla_k_gain[e], MLA_QK, MLA_QK ** -0.5 * LOG2E), MLA_QK),
                inv, ts_proj)
            swa_bound = _score_bound(ev_swa_q_gain[e], ev_swa_k_gain[e], HEAD_DIM, HEAD_DIM ** -0.5 * LOG2E)
            swa_scalars = jnp.concatenate([ev_sinks[e].astype(F32), swa_bound.reshape(1).astype(F32)])
            out_a = _swa_attention(swa_scalars, qa, ka, va, tq_swa)
            out_b = _mla_attention(qm, km, vm, tq_flash)
            wo = ev_w_out[e].astype(BF16)
            attns = [out_a, out_b]
            wos = [wo[:SWA_Q_W], wo[SWA_Q_W:]]
        else:
            o = l // 2
            lambda_init = 0.8 - 0.6 * math.exp(-0.3 * l)
            n_grp = DIFF_HEADS * 2
            w_qkv = od_w_qkv[o].astype(BF16)
            q, k, vt = _odd_proj(x, _row(mix_norm[l]), w_qkv[:, :2 * D], w_qkv[:, 2 * D:].T, ones,
                                 _row(jnp.tile(od_q_gain[o], n_grp // 2)), _row(jnp.tile(od_k_gain[o], n_grp // 2)),
                                 ts_proj)
            bound = _score_bound(od_q_gain[o], od_k_gain[o], DIFF_DIM, DIFF_DIM ** -0.5 * LOG2E)
            out_d = _diff_attention(diff_slopes, bound, q, k, vt, od_lambda[o].astype(F32), _row(od_subln[o]),
                                    tq_flash, lambda_init)
            attns = [out_d]
            wos = [od_w_out[o].astype(BF16)]
        x = _post_block(x, attns, wos, _row(mem_q_norm[l]), mem_w_q[l].astype(BF16), _row(mem_q_gain[l]),
                        mem_k[l], mem_v[l], mem_w_out[l].astype(BF16), _row(ffn_norm[l]),
                        ffn_w_gate[l].astype(BF16), ffn_w_up[l].astype(BF16), ffn_w_down[l].astype(BF16), ts_post)
    return x
```
